```python
import math
import jax, jax.numpy as jnp
from jax import lax
import numpy as np

D_MODEL = 1024
BATCH = 8
SEQ = 2048
DEPTH = 1

N_MEM = 256
EPS = 1e-6
DSA_HEADS = 8
DSA_LATENT = 128
DSA_VDIM = 64
IDX_HEADS = 8
IDX_DIM = 64
TOPK_MAX = 256
Q_BLOCK = 128
REL_BUCKETS = 32
REL_MAX_DIST = 128
S5_WIDTH = 512
S5_GROUP = 16
S5_GROUPS = S5_WIDTH // S5_GROUP
S5_STATE = 64
DT_MIN = 0.001
DT_MAX = 0.1
X_HEADS = 4
X_HEAD_DIM = 128
DSA_WIDTH = DSA_HEADS * DSA_VDIM
X_WIDTH = X_HEADS * X_HEAD_DIM
N_BRANCH = 3
D_FF = -(-(8 * D_MODEL) // (3 * 256)) * 256
IN_SPLITS = (DSA_HEADS * DSA_LATENT,
             DSA_LATENT,
             IDX_HEADS * IDX_DIM,
             IDX_DIM,
             IDX_HEADS,
             S5_WIDTH,
             X_WIDTH,
             N_BRANCH * D_MODEL)
D_IN = sum(IN_SPLITS)

kernel_name = 'hybrid_dsa_s5_memx_gated_block'


def rms_norm(x, g):
    x32 = x.astype(jnp.float32)
    y = x32 * lax.rsqrt(jnp.mean(x32 * x32, axis=-1, keepdims=True) + EPS)
    return (y * g.astype(jnp.float32)).astype(x.dtype)


def t5_bucket(n):
    max_exact = REL_BUCKETS // 2
    nf = jnp.maximum(n, 1).astype(jnp.float32)
    large = max_exact + (jnp.log(nf / max_exact) / math.log(REL_MAX_DIST / max_exact)
                         * (REL_BUCKETS - max_exact)).astype(jnp.int32)
    large = jnp.minimum(large, REL_BUCKETS - 1)
    return jnp.where(n < max_exact, n, large)


def dsa_attention(q, c, q_idx, k_idx, w_idx, rel_bias, w_uv):
    B, S = q.shape[0], q.shape[1]
    topk = min(TOPK_MAX, S // 4)
    n_blk = S // Q_BLOCK
    spos = jnp.arange(S)
    scale = DSA_LATENT ** -0.5

    def block(i):
        t0 = i * Q_BLOCK
        tpos = t0 + jnp.arange(Q_BLOCK)
        qb = lax.dynamic_slice_in_dim(q, t0, Q_BLOCK, axis=1)
        qib = lax.dynamic_slice_in_dim(q_idx, t0, Q_BLOCK, axis=1)
        wib = lax.dynamic_slice_in_dim(w_idx, t0, Q_BLOCK, axis=1)
        dots = jnp.einsum('bthd,bsd->bhts', qib, k_idx).astype(jnp.float32)
        score = jnp.einsum('bhts,bth->bts', jax.nn.relu(dots), wib.astype(jnp.float32))
        visible = spos[None, :] <= tpos[:, None]
        score = jnp.where(visible[None], score, -jnp.inf)
        _, idx = lax.top_k(score, topk)
        valid = idx <= tpos[None, :, None]
        c_sel = jax.vmap(lambda cb, ib: cb[ib])(c, idx)
        bucket = t5_bucket(jnp.maximum(tpos[None, :, None] - idx, 0))
        bias = jnp.moveaxis(rel_bias[bucket], -1, 1).astype(jnp.float32)
        logits = jnp.einsum('bthd,btkd->bhtk', qb, c_sel).astype(jnp.float32) * scale + bias
        logits = jnp.where(valid[:, None], logits, -jnp.inf)
        p = jax.nn.softmax(logits, axis=-1).astype(c.dtype)
        o = jnp.einsum('bhtk,btkd->bthd', p, c_sel)
        return jnp.einsum('bthd,hde->bthe', o, w_uv).reshape(B, Q_BLOCK, DSA_WIDTH)

    out = lax.map(block, jnp.arange(n_blk))
    return jnp.moveaxis(out, 0, 1).reshape(B, S, DSA_WIDTH)


def s5_mixer(u, a_re, a_im, log_dt, b_re, b_im, c_re, c_im, d_skip, w_glu):
    B, S = u.shape[0], u.shape[1]
    f32 = jnp.float32
    u32 = u.astype(f32).reshape(B, S, S5_GROUPS, S5_GROUP)
    lam = lax.complex(a_re.astype(f32), a_im.astype(f32))
    dt = jnp.exp(log_dt.astype(f32))[:, None]
    lam_bar = jnp.exp(lam * dt)
    b_bar = ((lam_bar - 1.0) / lam)[..., None] * lax.complex(b_re.astype(f32), b_im.astype(f32))
    bu = jnp.einsum('blgc,gnc->blgn', u32.astype(jnp.complex64), b_bar)
    a = jnp.broadcast_to(lam_bar, bu.shape)

    def combine(l, r):
        return (l[0] * r[0], r[0] * l[1] + r[1])

    _, states = lax.associative_scan(combine, (a, bu), axis=1)
    cmat = lax.complex(c_re.astype(f32), c_im.astype(f32))
    y = jnp.einsum('blgn,gcn->blgc', states, cmat).real + d_skip.astype(f32) * u32
    y = jax.nn.gelu(y.reshape(B, S, S5_WIDTH))
    y = y * jax.nn.sigmoid(y @ w_glu.astype(f32))
    return y.astype(u.dtype)


def cross_attention(q, mem_n, w_mem_kv, g_q, g_k):
    B, S = q.shape[0], q.shape[1]
    M = mem_n.shape[1]
    k, v = jnp.split(mem_n @ w_mem_kv, 2, axis=-1)
    qh = rms_norm(q.reshape(B, S, X_HEADS, X_HEAD_DIM), g_q)
    kh = rms_norm(k.reshape(B, M, X_HEADS, X_HEAD_DIM), g_k)
    vh = v.reshape(B, M, X_HEADS, X_HEAD_DIM)
    logits = jnp.einsum('bthd,bmhd->bhtm', qh, kh).astype(jnp.float32) * (X_HEAD_DIM ** -0.5)
    p = jax.nn.softmax(logits, axis=-1).astype(vh.dtype)
    return jnp.einsum('bhtm,bmhd->bthd', p, vh).reshape(B, S, X_WIDTH)


def hybrid_layer(x, mem, rel_bias, w_in, g_mix_norm, g_q_dsa, g_kv_dsa, w_uv_dsa,
                 a_re, a_im, log_dt, b_re, b_im, c_re, c_im, d_skip, w_glu,
                 g_mem_norm, w_mem_kv, g_q_cross, g_k_cross,
                 w_br_dsa, w_br_s5, w_br_cross, w_out,
                 g_ffn_norm, w_ffn_gate, w_ffn_up, w_ffn_down):
    B, S = x.shape[0], x.shape[1]
    h = rms_norm(x, g_mix_norm)
    offs = [int(o) for o in np.cumsum(IN_SPLITS)[:-1]]
    q_dsa, c_kv, q_idx, k_idx, w_idx, u_s5, q_x, gates = jnp.split(h @ w_in, offs, axis=-1)
    q_dsa = rms_norm(q_dsa.reshape(B, S, DSA_HEADS, DSA_LATENT), g_q_dsa)
    c_kv = rms_norm(c_kv, g_kv_dsa)
    o_dsa = dsa_attention(q_dsa, c_kv, q_idx.reshape(B, S, IDX_HEADS, IDX_DIM),
                          k_idx, w_idx, rel_bias, w_uv_dsa)
    o_s5 = s5_mixer(u_s5, a_re, a_im, log_dt, b_re, b_im, c_re, c_im, d_skip, w_glu)
    o_x = cross_attention(q_x, rms_norm(mem, g_mem_norm), w_mem_kv, g_q_cross, g_k_cross)
    g_dsa, g_s5, g_x = jnp.split(jax.nn.sigmoid(gates), N_BRANCH, axis=-1)
    merged = g_dsa * (o_dsa @ w_br_dsa) + g_s5 * (o_s5 @ w_br_s5) + g_x * (o_x @ w_br_cross)
    x = x + merged @ w_out
    hf = rms_norm(x, g_ffn_norm)
    return x + (jax.nn.silu(hf @ w_ffn_gate) * (hf @ w_ffn_up)) @ w_ffn_down


def setup_inputs(seed: int = 0) -> dict:
    key = jax.random.key(seed)
    ks = jax.random.split(key, 32)
    f32 = jnp.float32
    L = DEPTH

    def nrm(k, shape, fan):
        return jax.random.normal(k, shape, f32) * (fan ** -0.5)

    def gain(k, shape):
        return 1.0 + 0.01 * jax.random.normal(k, shape, f32)

    a_im = jnp.broadcast_to(math.pi * jnp.arange(S5_STATE, dtype=f32), (L, S5_GROUPS, S5_STATE))
    return {
        'x': jax.random.normal(ks[0], (BATCH, SEQ, D_MODEL), f32),
        'mem': jax.random.normal(ks[1], (BATCH, N_MEM, D_MODEL), f32),
        'rel_bias': 0.1 * jax.random.normal(ks[2], (REL_BUCKETS, DSA_HEADS), f32),
        'w_in': nrm(ks[3], (L, D_MODEL, D_IN), D_MODEL),
        'g_mix_norm': gain(ks[4], (L, D_MODEL)),
        'g_q_dsa': gain(ks[5], (L, DSA_LATENT)),
        'g_kv_dsa': gain(ks[6], (L, DSA_LATENT)),
        'w_uv_dsa': nrm(ks[7], (L, DSA_HEADS, DSA_LATENT, DSA_VDIM), DSA_LATENT),
        'a_re': -0.5 + 0.005 * jax.random.normal(ks[8], (L, S5_GROUPS, S5_STATE), f32),
        'a_im': a_im + 0.001 * jax.random.normal(ks[9], (L, S5_GROUPS, S5_STATE), f32),
        'log_dt': jax.random.uniform(ks[10], (L, S5_GROUPS), f32, math.log(DT_MIN), math.log(DT_MAX)),
        'b_re': nrm(ks[11], (L, S5_GROUPS, S5_STATE, S5_GROUP), 2 * S5_GROUP),
        'b_im': nrm(ks[12], (L, S5_GROUPS, S5_STATE, S5_GROUP), 2 * S5_GROUP),
        'c_re': nrm(ks[13], (L, S5_GROUPS, S5_GROUP, S5_STATE), 2 * S5_STATE),
        'c_im': nrm(ks[14], (L, S5_GROUPS, S5_GROUP, S5_STATE), 2 * S5_STATE),
        'd_skip': jax.random.normal(ks[15], (L, S5_GROUPS, S5_GROUP), f32),
        'w_glu': nrm(ks[16], (L, S5_WIDTH, S5_WIDTH), S5_WIDTH),
        'g_mem_norm': gain(ks[17], (L, D_MODEL)),
        'w_mem_kv': nrm(ks[18], (L, D_MODEL, 2 * X_WIDTH), D_MODEL),
        'g_q_cross': gain(ks[19], (L, X_HEAD_DIM)),
        'g_k_cross': gain(ks[20], (L, X_HEAD_DIM)),
        'w_br_dsa': nrm(ks[21], (L, DSA_WIDTH, D_MODEL), DSA_WIDTH),
        'w_br_s5': nrm(ks[22], (L, S5_WIDTH, D_MODEL), S5_WIDTH),
        'w_br_cross': nrm(ks[23], (L, X_WIDTH, D_MODEL), X_WIDTH),
        'w_out': nrm(ks[24], (L, D_MODEL, D_MODEL), D_MODEL),
        'g_ffn_norm': gain(ks[25], (L, D_MODEL)),
        'w_ffn_gate': nrm(ks[26], (L, D_MODEL, D_FF), D_MODEL),
        'w_ffn_up': nrm(ks[27], (L, D_MODEL, D_FF), D_MODEL),
        'w_ffn_down': nrm(ks[28], (L, D_FF, D_MODEL), D_FF),
    }


def reference(x, mem, rel_bias, w_in, g_mix_norm, g_q_dsa, g_kv_dsa, w_uv_dsa,
              a_re, a_im, log_dt, b_re, b_im, c_re, c_im, d_skip, w_glu,
              g_mem_norm, w_mem_kv, g_q_cross, g_k_cross,
              w_br_dsa, w_br_s5, w_br_cross, w_out,
              g_ffn_norm, w_ffn_gate, w_ffn_up, w_ffn_down):
    for l in range(DEPTH):
        x = hybrid_layer(x, mem, rel_bias, w_in[l], g_mix_norm[l], g_q_dsa[l], g_kv_dsa[l], w_uv_dsa[l],
                         a_re[l], a_im[l], log_dt[l], b_re[l], b_im[l], c_re[l], c_im[l], d_skip[l], w_glu[l],
                         g_mem_norm[l], w_mem_kv[l], g_q_cross[l], g_k_cross[l],
                         w_br_dsa[l], w_br_s5[l], w_br_cross[l], w_out[l],
                         g_ffn_norm[l], w_ffn_gate[l], w_ffn_up[l], w_ffn_down[l])
    return x
```

```python
import functools
import math

import jax
import jax.numpy as jnp
import numpy as np
from jax import lax
from jax.experimental import pallas as pl
from jax.experimental.pallas import tpu as pltpu

F32 = jnp.float32
BF16 = jnp.bfloat16

D_MODEL = 1024
N_MEM = 256
EPS = 1e-6
DSA_HEADS = 8
DSA_LATENT = 128
DSA_VDIM = 64
IDX_HEADS = 8
IDX_DIM = 64
TOPK_MAX = 256
REL_BUCKETS = 32
REL_MAX_DIST = 128
S5_WIDTH = 512
S5_GROUP = 16
S5_GROUPS = S5_WIDTH // S5_GROUP
S5_STATE = 64
X_HEADS = 4
X_HEAD_DIM = 128
DSA_WIDTH = DSA_HEADS * DSA_VDIM
X_WIDTH = X_HEADS * X_HEAD_DIM
N_BRANCH = 3
IN_SPLITS = (DSA_HEADS * DSA_LATENT, DSA_LATENT, IDX_HEADS * IDX_DIM, IDX_DIM,
             IDX_HEADS, S5_WIDTH, X_WIDTH, N_BRANCH * D_MODEL)

LANES = 128
VMEM_LIMIT = 56 * 1024 * 1024

TS_IN = 512
TQ = 256
TK = 256
RC = 128
N_BISECT = 12
T_S5 = 64
TS_MERGE = 512
TS_FFN = 512
NEG = -1e30


def _rms(x, g):
    ms = jnp.mean(x * x, axis=-1, keepdims=True)
    return x * lax.rsqrt(ms + EPS) * g


def _const_spec(shape):
    nd = len(shape)
    return pl.BlockSpec(shape, lambda *_: (0,) * nd, pipeline_mode=pl.Buffered(1))


def _params(sem):
    return pltpu.CompilerParams(dimension_semantics=sem, vmem_limit_bytes=VMEM_LIMIT)


def _inproj_kernel(x_ref, gmix_ref, wq_ref, wc_ref, wqi_ref, wkw_ref, wu_ref, gq_ref, gkv_ref,
                   q_ref, c_ref, qi_ref, kidx_ref, kw_ref, u_ref):
    hb = _rms(x_ref[...], gmix_ref[...]).astype(BF16)
    q = jnp.dot(hb, wq_ref[...], preferred_element_type=F32)
    gq = gq_ref[...] * (DSA_LATENT ** -0.5)
    for h in range(DSA_HEADS):
        sl = slice(h * DSA_LATENT, (h + 1) * DSA_LATENT)
        q_ref[:, sl] = _rms(q[:, sl], gq).astype(BF16)
    c = jnp.dot(hb, wc_ref[...], preferred_element_type=F32)
    c_ref[...] = _rms(c, gkv_ref[...]).astype(BF16)
    qi_ref[...] = jnp.dot(hb, wqi_ref[...], preferred_element_type=F32).astype(BF16)
    kw = jnp.dot(hb, wkw_ref[...], preferred_element_type=F32)
    kw_ref[...] = kw
    kidx_ref[...] = kw.astype(BF16)
    u_ref[...] = jnp.dot(hb, wu_ref[...], preferred_element_type=F32)


def _inproj(x, gmix, wq, wc, wqi, wkw, wu, gq, gkv):
    B, S, D = x.shape
    grid = (B, S // TS_IN)
    tok = lambda w: pl.BlockSpec((None, TS_IN, w), lambda b, s: (b, s, 0))
    out_shape = (
        jax.ShapeDtypeStruct((B, S, DSA_HEADS * DSA_LATENT), BF16),
        jax.ShapeDtypeStruct((B, S, DSA_LATENT), BF16),
        jax.ShapeDtypeStruct((B, S, IDX_HEADS * LANES), BF16),
        jax.ShapeDtypeStruct((B, S, LANES), BF16),
        jax.ShapeDtypeStruct((B, S, LANES), F32),
        jax.ShapeDtypeStruct((S, B * S5_WIDTH), F32),
    )
    out_specs = (tok(DSA_HEADS * DSA_LATENT), tok(DSA_LATENT), tok(IDX_HEADS * LANES),
                 tok(LANES), tok(LANES),
                 pl.BlockSpec((TS_IN, S5_WIDTH), lambda b, s: (s, b)))
    in_specs = [tok(D), _const_spec(gmix.shape), _const_spec(wq.shape), _const_spec(wc.shape),
                _const_spec(wqi.shape), _const_spec(wkw.shape), _const_spec(wu.shape),
                _const_spec(gq.shape), _const_spec(gkv.shape)]
    return pl.pallas_call(
        _inproj_kernel, grid=grid, in_specs=in_specs, out_specs=out_specs, out_shape=out_shape,
        compiler_params=_params(("parallel", "parallel")), name="inproj",
    )(x, gmix, wq, wc, wqi, wkw, wu, gq, gkv)


def _dsa_kernel(q_ref, qi_ref, kw_ref, kidx_ref, c_ref, nb_ref, wuv_ref, o_ref,
                sc_ref, qs_ref, qis_ref, m_ref, l_ref, acc_ref):
    i = pl.program_id(1)
    nk = sc_ref.shape[0]
    kf = float(TOPK_MAX)
    nt_dims = (((1,), (1,)), ((), ()))

    for h in range(DSA_HEADS):
        sl = slice(h * LANES, (h + 1) * LANES)
        qs_ref[h * TQ:(h + 1) * TQ, :] = q_ref[:, sl]
        qis_ref[h * TQ:(h + 1) * TQ, :] = qi_ref[:, sl]

    kw = kw_ref[...]
    w3 = jnp.stack([kw[:, IDX_DIM + h:IDX_DIM + h + 1] for h in range(IDX_HEADS)], axis=0)

    def score(j):
        ks = kidx_ref[pl.ds(pl.multiple_of(j * TK, TK), TK), :]
        d = lax.dot_general(qis_ref[...], ks, nt_dims, preferred_element_type=F32)
        d3 = jnp.maximum(d, 0.0).reshape(IDX_HEADS, TQ, TK) * w3
        return jnp.sum(d3, axis=0)

    def score_body(j, carry):
        sc_ref[j] = score(j)
        return carry

    lax.fori_loop(0, i, score_body, 0)
    row_t = lax.broadcasted_iota(jnp.int32, (TQ, TK), 0)
    col_t = lax.broadcasted_iota(jnp.int32, (TQ, TK), 1)
    causal = col_t <= row_t
    sc_ref[i] = jnp.where(causal, score(i), -jnp.inf)

    @pl.when(i == 0)
    def _():
        sc_ref[0] = jnp.where(causal, 0.0, NEG)

    @pl.when(i > 0)
    def _():
        nt = i + 1
        lane_i = lax.broadcasted_iota(jnp.int32, (RC, LANES), 1)
        for rc in range(TQ // RC):
            rows = slice(rc * RC, (rc + 1) * RC)

            def tile(j):
                return sc_ref[j, rows, :]

            def full(v, dt=F32):
                return jnp.full((RC, LANES), v, dt)

            def rep(x):
                return jnp.broadcast_to(x, (RC, LANES))

            def rsum(a):
                return rep(jnp.sum(a, axis=1, keepdims=True))

            def rmin(a):
                return rep(jnp.min(a, axis=1, keepdims=True))

            def rmax(a):
                return rep(jnp.max(a, axis=1, keepdims=True))

            def minmax_body(j, carry):
                mn, mx = carry
                s = tile(j)
                for sl in (s[:, :LANES], s[:, LANES:]):
                    mx = jnp.maximum(mx, sl)
                    mn = jnp.minimum(mn, jnp.where(sl == -jnp.inf, jnp.inf, sl))
                return mn, mx

            mn, mx = lax.fori_loop(0, nt, minmax_body, (full(jnp.inf), full(-jnp.inf)))
            lo, hi = rmin(mn), rmax(mx)

            def count_ge(thr):
                def body(j, acc):
                    s = tile(j)
                    return (acc + jnp.where(s[:, :LANES] >= thr, 1.0, 0.0)
                            + jnp.where(s[:, LANES:] >= thr, 1.0, 0.0))
                return rsum(lax.fori_loop(0, nt, body, full(0.0)))

            def bisect(_, carry):
                lo, hi = carry
                mid = 0.5 * (lo + hi)
                ge = count_ge(mid) >= kf
                return jnp.where(ge, mid, lo), jnp.where(ge, hi, mid)

            lo, hi = lax.fori_loop(0, N_BISECT, bisect, (lo, hi))

            def walk(cur):
                def body(j, carry):
                    ac, am = carry
                    s = tile(j)
                    for sl in (s[:, :LANES], s[:, LANES:]):
                        g = sl > cur
                        ac = ac + jnp.where(g, 1.0, 0.0)
                        am = jnp.minimum(am, jnp.where(g, sl, jnp.inf))
                    return ac, am
                ac, am = lax.fori_loop(0, nt, body, (full(0.0), full(jnp.inf)))
                return rsum(ac), rmin(am)

            def walk_cond(carry):
                _, _, go, it = carry
                return jnp.logical_and(go > 0, it < nk * TK + 2)

            def walk_body(carry):
                cur, _, _, it = carry
                c, nxt = walk(cur)
                move = c >= kf
                cur = jnp.where(move, nxt, cur)
                go = (jnp.max(jnp.where(move, 1.0, 0.0)) > 0.5).astype(jnp.int32)
                return cur, c, go, it + 1

            kth, cgt, _, _ = lax.while_loop(
                walk_cond, walk_body, (lo, full(0.0), jnp.int32(1), jnp.int32(0)))
            need = kf - cgt

            def count_eq_upto(cmax):
                def body(j, acc):
                    s = tile(j)
                    col = j * TK + lane_i
                    a0 = jnp.where(s[:, :LANES] == kth, jnp.where(col <= cmax, 1.0, 0.0), 0.0)
                    a1 = jnp.where(s[:, LANES:] == kth, jnp.where(col + LANES <= cmax, 1.0, 0.0), 0.0)
                    return acc + a0 + a1
                return rsum(lax.fori_loop(0, nt, body, full(0.0)))

            big = nk * TK
            ceq = count_eq_upto(full(big, jnp.int32))
            has_ties = jnp.max(jnp.where(ceq > need, 1.0, 0.0)) > 0.5

            def tie_cut():
                def body(_, carry):
                    lo_c, hi_c = carry
                    mid = lax.shift_right_arithmetic(lo_c + hi_c, 1)
                    ok = count_eq_upto(mid) >= need
                    return jnp.where(ok, lo_c, mid), jnp.where(ok, mid, hi_c)
                _, hi_c = lax.fori_loop(0, 12, body, (full(-1, jnp.int32), full(big - 1, jnp.int32)))
                return hi_c

            cut = lax.cond(has_ties, tie_cut, lambda: full(big, jnp.int32))
            kth2 = jnp.concatenate([kth, kth], axis=1)
            cut2 = jnp.concatenate([cut, cut], axis=1)
            col2 = lax.broadcasted_iota(jnp.int32, (RC, TK), 1)

            def mask_body(j, carry):
                s = tile(j)
                tie_ok = jnp.where(col2 + j * TK <= cut2, 0.0, NEG)
                sc_ref[j, rows, :] = jnp.where(s > kth2, 0.0, jnp.where(s == kth2, tie_ok, NEG))
                return carry

            lax.fori_loop(0, nt, mask_body, 0)

    m_ref[...] = jnp.full(m_ref.shape, NEG, F32)
    l_ref[...] = jnp.zeros(l_ref.shape, F32)
    acc_ref[...] = jnp.zeros(acc_ref.shape, F32)
    hq = DSA_HEADS * TQ

    def attend(j, bias):
        ct = c_ref[pl.ds(pl.multiple_of(j * TK, TK), TK), :]
        lg = lax.dot_general(qs_ref[...], ct, nt_dims, preferred_element_type=F32)
        lg = lg.reshape(DSA_HEADS, TQ, TK) + sc_ref[j][None]
        if bias is not None:
            lg = lg + bias
        m_old = m_ref[...].reshape(DSA_HEADS, TQ, LANES)
        m_new = jnp.maximum(m_old, jnp.max(lg, axis=2, keepdims=True))
        alpha = jnp.exp(m_old - m_new)
        p = jnp.exp(lg - jnp.concatenate([m_new, m_new], axis=2))
        l_ref[...] = (alpha.reshape(hq, LANES) * l_ref[...]
                      + (p[:, :, :LANES] + p[:, :, LANES:]).reshape(hq, LANES))
        pv = jnp.dot(p.reshape(hq, TK).astype(BF16), ct, preferred_element_type=F32)
        acc_ref[...] = alpha.reshape(hq, LANES) * acc_ref[...] + pv
        m_ref[...] = m_new.reshape(hq, LANES)

    def far_body(j, carry):
        attend(j, None)
        return carry

    lax.fori_loop(0, jnp.maximum(i - 1, 0), far_body, 0)

    @pl.when(i > 0)
    def _():
        attend(i - 1, nb_ref[1])

    attend(i, nb_ref[0])

    l = jnp.sum(l_ref[...], axis=1, keepdims=True)
    o = (acc_ref[...] / l).astype(BF16)
    out = jnp.zeros((TQ, DSA_WIDTH), F32)
    for h in range(DSA_HEADS):
        out = out + jnp.dot(o[h * TQ:(h + 1) * TQ, :], wuv_ref[h], preferred_element_type=F32)
    o_ref[...] = out.astype(BF16)


def _dsa(q, qi, kw, kidx, c, nb, wuv):
    B, S, _ = q.shape
    nk = S // TK
    grid = (B, S // TQ)
    tile = lambda w: pl.BlockSpec((None, TQ, w), lambda b, i: (b, i, 0))
    seq = lambda w: pl.BlockSpec((None, S, w), lambda b, i: (b, 0, 0))
    in_specs = [tile(DSA_HEADS * DSA_LATENT), tile(IDX_HEADS * LANES), tile(LANES),
                seq(LANES), seq(DSA_LATENT), _const_spec(nb.shape), _const_spec(wuv.shape)]
    scratch = [
        pltpu.VMEM((nk, TQ, TK), F32),
        pltpu.VMEM((DSA_HEADS * TQ, DSA_LATENT), BF16),
        pltpu.VMEM((IDX_HEADS * TQ, LANES), BF16),
        pltpu.VMEM((DSA_HEADS * TQ, LANES), F32),
        pltpu.VMEM((DSA_HEADS * TQ, LANES), F32),
        pltpu.VMEM((DSA_HEADS * TQ, DSA_LATENT), F32),
    ]
    return pl.pallas_call(
        _dsa_kernel, grid=grid, in_specs=in_specs, out_specs=tile(DSA_WIDTH),
        out_shape=jax.ShapeDtypeStruct((B, S, DSA_WIDTH), BF16), scratch_shapes=scratch,
        compiler_params=_params(("parallel", "arbitrary")), name="dsa",
    )(q, qi, kw, kidx, c, nb, wuv)


def _s5_kernel(u_ref, bm_ref, cm_ref, lre_ref, lim_ref, dsk_ref, wglu_ref, o_ref,
               st_ref, hre_ref, him_ref):
    nstate = S5_GROUPS * S5_STATE
    nb = hre_ref.shape[0]

    @pl.when(pl.program_id(0) == 0)
    def _():
        hre_ref[...] = jnp.zeros(hre_ref.shape, F32)
        him_ref[...] = jnp.zeros(him_ref.shape, F32)

    u = u_ref[...]
    st_ref[...] = jnp.dot(u.astype(BF16), bm_ref[...], preferred_element_type=F32)

    half = nstate // 2
    for part in range(2):
        re_sl = slice(part * half, (part + 1) * half)
        im_sl = slice(nstate + part * half, nstate + (part + 1) * half)
        lre = lre_ref[:, re_sl]
        lim = lim_ref[:, re_sl]

        def step(t, carry):
            hr, hi = carry
            r = pl.ds(pl.multiple_of(t * nb, nb), nb)
            nr = lre * hr - lim * hi + st_ref[r, re_sl]
            ni = lre * hi + lim * hr + st_ref[r, im_sl]
            st_ref[r, re_sl] = nr
            st_ref[r, im_sl] = ni
            return nr, ni

        hr, hi = lax.fori_loop(0, T_S5, step, (hre_ref[:, re_sl], him_ref[:, re_sl]), unroll=4)
        hre_ref[:, re_sl] = hr
        him_ref[:, re_sl] = hi

    y = jnp.dot(st_ref[...].astype(BF16), cm_ref[...], preferred_element_type=F32)
    y = jax.nn.gelu(y + dsk_ref[...] * u)
    z = jnp.dot(y.astype(BF16), wglu_ref[...], preferred_element_type=F32)
    o_ref[...] = (y * jax.nn.sigmoid(z)).astype(BF16)


def _s5(u2, nbatch, bm, cm, lre, lim, dsk, wglu):
    rows = u2.shape[0]
    tb = T_S5 * nbatch
    nstate = S5_GROUPS * S5_STATE
    grid = (rows // tb,)
    tok = pl.BlockSpec((tb, S5_WIDTH), lambda t: (t, 0))
    in_specs = [tok, _const_spec(bm.shape), _const_spec(cm.shape), _const_spec(lre.shape),
                _const_spec(lim.shape), _const_spec(dsk.shape), _const_spec(wglu.shape)]
    scratch = [pltpu.VMEM((tb, 2 * nstate), F32),
               pltpu.VMEM((nbatch, nstate), F32), pltpu.VMEM((nbatch, nstate), F32)]
    return pl.pallas_call(
        _s5_kernel, grid=grid, in_specs=in_specs, out_specs=tok,
        out_shape=jax.ShapeDtypeStruct((rows, S5_WIDTH), BF16), scratch_shapes=scratch,
        compiler_params=_params(("arbitrary",)), name="s5",
    )(u2, bm, cm, lre, lim, dsk, wglu)


def _memkv_kernel(mem_ref, gmem_ref, wkv_ref, gk_ref, k_ref, v_ref):
    mb = _rms(mem_ref[...], gmem_ref[...]).astype(BF16)
    kv = jnp.dot(mb, wkv_ref[...], preferred_element_type=F32)
    for h in range(X_HEADS):
        sl = slice(h * X_HEAD_DIM, (h + 1) * X_HEAD_DIM)
        k_ref[:, sl] = _rms(kv[:, sl], gk_ref[...]).astype(BF16)
    v_ref[...] = kv[:, X_WIDTH:].astype(BF16)


def _memkv(mem, gmem, wkv, gk):
    B, M, D = mem.shape
    blk = lambda w: pl.BlockSpec((None, M, w), lambda b: (b, 0, 0))
    return pl.pallas_call(
        _memkv_kernel, grid=(B,),
        in_specs=[blk(D), _const_spec(gmem.shape), _const_spec(wkv.shape), _const_spec(gk.shape)],
        out_specs=(blk(X_WIDTH), blk(X_WIDTH)),
        out_shape=(jax.ShapeDtypeStruct((B, M, X_WIDTH), BF16),) * 2,
        compiler_params=_params(("parallel",)), name="memkv",
    )(mem, gmem, wkv, gk)


def _merge_kernel(x_ref, odsa_ref, os5_ref, k_ref, v_ref, gmix_ref, wg_ref, wqx_ref, gqx_ref,
                  wb1_ref, wb2_ref, wb3_ref, wout_ref, y_ref):
    x = x_ref[...]
    hb = _rms(x, gmix_ref[...]).astype(BF16)
    nt_dims = (((1,), (1,)), ((), ()))

    qx = jnp.dot(hb, wqx_ref[...], preferred_element_type=F32)
    gqx = gqx_ref[...] * (X_HEAD_DIM ** -0.5)
    ox = []
    for h in range(X_HEADS):
        sl = slice(h * X_HEAD_DIM, (h + 1) * X_HEAD_DIM)
        qh = _rms(qx[:, sl], gqx).astype(BF16)
        lg = lax.dot_general(qh, k_ref[:, sl], nt_dims, preferred_element_type=F32)
        p = jnp.exp(lg - jnp.max(lg, axis=-1, keepdims=True))
        pv = jnp.dot(p.astype(BF16), v_ref[:, sl], preferred_element_type=F32)
        ox.append((pv / jnp.sum(p, axis=-1, keepdims=True)).astype(BF16))
    ox = jnp.concatenate(ox, axis=1)

    merged = None
    for br, (o, wb) in enumerate(((odsa_ref[...], wb1_ref), (os5_ref[...], wb2_ref), (ox, wb3_ref))):
        gate = jax.nn.sigmoid(jnp.dot(hb, wg_ref[:, br * D_MODEL:(br + 1) * D_MODEL],
                                      preferred_element_type=F32))
        term = gate * jnp.dot(o, wb[...], preferred_element_type=F32)
        merged = term if merged is None else merged + term
    y_ref[...] = x + jnp.dot(merged.astype(BF16), wout_ref[...], preferred_element_type=F32)


def _merge(x, odsa, os5, k, v, gmix, wg, wqx, gqx, wb1, wb2, wb3, wout):
    B, S, D = x.shape
    ts = TS_MERGE
    tok = lambda w: pl.BlockSpec((None, ts, w), lambda b, s: (b, s, 0))
    memspec = pl.BlockSpec((None, N_MEM, X_WIDTH), lambda b, s: (b, 0, 0))
    in_specs = [tok(D), tok(DSA_WIDTH),
                pl.BlockSpec((ts, S5_WIDTH), lambda b, s: (s, b)),
                memspec, memspec] + [_const_spec(a.shape) for a in
                                     (gmix, wg, wqx, gqx, wb1, wb2, wb3, wout)]
    return pl.pallas_call(
        _merge_kernel, grid=(B, S // ts), in_specs=in_specs, out_specs=tok(D),
        out_shape=jax.ShapeDtypeStruct((B, S, D), F32),
        compiler_params=_params(("parallel", "parallel")), name="merge",
    )(x, odsa, os5, k, v, gmix, wg, wqx, gqx, wb1, wb2, wb3, wout)


def _ffn_kernel(x_ref, g_ref, wg_ref, wu_ref, wd_ref, y_ref):
    x = x_ref[...]
    hb = _rms(x, g_ref[...]).astype(BF16)
    a = jnp.dot(hb, wg_ref[...], preferred_element_type=F32)
    b = jnp.dot(hb, wu_ref[...], preferred_element_type=F32)
    act = (jax.nn.silu(a) * b).astype(BF16)
    y_ref[...] = x + jnp.dot(act, wd_ref[...], preferred_element_type=F32)


def _ffn(x2, g, wg, wu, wd):
    n, D = x2.shape
    tok = pl.BlockSpec((TS_FFN, D), lambda t: (t, 0))
    return pl.pallas_call(
        _ffn_kernel, grid=(n // TS_FFN,),
        in_specs=[tok] + [_const_spec(a.shape) for a in (g, wg, wu, wd)],
        out_specs=tok, out_shape=jax.ShapeDtypeStruct((n, D), F32),
        compiler_params=_params(("parallel",)), name="ffn",
    )(x2, g, wg, wu, wd)


def _t5_bucket(n):
    max_exact = REL_BUCKETS // 2
    nf = jnp.maximum(n, 1).astype(F32)
    large = max_exact + (jnp.log(nf / max_exact) / math.log(REL_MAX_DIST / max_exact)
                         * (REL_BUCKETS - max_exact)).astype(jnp.int32)
    large = jnp.minimum(large, REL_BUCKETS - 1)
    return jnp.where(n < max_exact, n, large)


def _near_bias(rel_bias):
    n = jnp.arange(2 * TQ, dtype=jnp.int32)
    f = rel_bias[_t5_bucket(n)] - rel_bias[REL_BUCKETS - 1][None, :]
    tt = jnp.arange(TQ, dtype=jnp.int32)[:, None]
    ss = jnp.arange(TK, dtype=jnp.int32)[None, :]
    diag = f[jnp.maximum(tt - ss, 0)]
    prev = f[TQ + tt - ss]
    return jnp.stack([jnp.moveaxis(diag, -1, 0), jnp.moveaxis(prev, -1, 0)], axis=0).astype(F32)


def _s5_mats(a_re, a_im, log_dt, b_re, b_im, c_re, c_im):
    lam = lax.complex(a_re.astype(F32), a_im.astype(F32))
    dt = jnp.exp(log_dt.astype(F32))[:, None]
    lam_bar = jnp.exp(lam * dt)
    b_bar = ((lam_bar - 1.0) / lam)[..., None] * lax.complex(b_re.astype(F32), b_im.astype(F32))
    eye = jnp.eye(S5_GROUPS, dtype=F32)

    def blockdiag_in(w):
        return jnp.einsum('gnc,gh->gchn', w, eye).reshape(S5_WIDTH, S5_GROUPS * S5_STATE)

    def blockdiag_out(w):
        return jnp.einsum('gcn,gh->gnhc', w, eye).reshape(S5_GROUPS * S5_STATE, S5_WIDTH)

    bm = jnp.concatenate([blockdiag_in(jnp.real(b_bar)), blockdiag_in(jnp.imag(b_bar))], axis=1)
    cm = jnp.concatenate([blockdiag_out(c_re.astype(F32)), blockdiag_out(-c_im.astype(F32))], axis=0)
    return bm.astype(BF16), cm.astype(BF16), jnp.real(lam_bar).reshape(1, -1), jnp.imag(lam_bar).reshape(1, -1)


def kernel(x, mem, rel_bias, w_in, g_mix_norm, g_q_dsa, g_kv_dsa, w_uv_dsa, a_re, a_im, log_dt, b_re, b_im, c_re, c_im, d_skip, w_glu, g_mem_norm, w_mem_kv, g_q_cross, g_k_cross, w_br_dsa, w_br_s5, w_br_cross, w_out, g_ffn_norm, w_ffn_gate, w_ffn_up, w_ffn_down):
    B, S, D = x.shape
    depth = w_in.shape[0]
    offs = [0] + [int(o) for o in np.cumsum(IN_SPLITS)]
    nb = _near_bias(rel_bias)
    row = lambda v: v.reshape(1, -1).astype(F32)
    for l in range(depth):
        w = w_in[l]
        cols = [w[:, offs[k]:offs[k + 1]] for k in range(len(IN_SPLITS))]
        wq, wc, wqi, wk, ww, wu, wqx, wg = cols
        wqi = jnp.pad(wqi.reshape(D, IDX_HEADS, IDX_DIM), ((0, 0), (0, 0), (0, LANES - IDX_DIM)))
        wqi = wqi.reshape(D, IDX_HEADS * LANES)
        wkw = jnp.pad(jnp.concatenate([wk, ww], axis=1), ((0, 0), (0, LANES - IDX_DIM - IDX_HEADS)))
        bf = lambda a: a.astype(BF16)

        q, c, qi, kidx, kw, u = _inproj(x, row(g_mix_norm[l]), bf(wq), bf(wc), bf(wqi), bf(wkw), bf(wu),
                                        row(g_q_dsa[l]), row(g_kv_dsa[l]))

        wuv = jnp.einsum('hde,hg->hdge', w_uv_dsa[l], jnp.eye(DSA_HEADS, dtype=F32))
        wuv = bf(wuv.reshape(DSA_HEADS, DSA_LATENT, DSA_WIDTH))
        o_dsa = _dsa(q, qi, kw, kidx, c, nb, wuv)

        bm, cm, lre, lim = _s5_mats(a_re[l], a_im[l], log_dt[l], b_re[l], b_im[l], c_re[l], c_im[l])
        lre = jnp.broadcast_to(lre, (B, lre.shape[1]))
        lim = jnp.broadcast_to(lim, (B, lim.shape[1]))
        o_s5 = _s5(u.reshape(S * B, S5_WIDTH), B, bm, cm, lre, lim, row(d_skip[l]), bf(w_glu[l]))
        o_s5 = o_s5.reshape(S, B * S5_WIDTH)

        k, v = _memkv(mem, row(g_mem_norm[l]), bf(w_mem_kv[l]), row(g_k_cross[l]))

        x1 = _merge(x, o_dsa, o_s5, k, v, row(g_mix_norm[l]), bf(wg), bf(wqx), row(g_q_cross[l]),
                    bf(w_br_dsa[l]), bf(w_br_s5[l]), bf(w_br_cross[l]), bf(w_out[l]))

        x = _ffn(x1.reshape(B * S, D), row(g_ffn_norm[l]), bf(w_ffn_gate[l]), bf(w_ffn_up[l]),
                 bf(w_ffn_down[l])).reshape(B, S, D)
    return x
```

```python
import math

import jax
import jax.numpy as jnp
import numpy as np
from jax import lax
from jax.experimental import pallas as pl
from jax.experimental.pallas import tpu as pltpu

F32 = jnp.float32
BF16 = jnp.bfloat16

D_MODEL = 1024
N_MEM = 256
EPS = 1e-6
DSA_HEADS = 8
DSA_LATENT = 128
DSA_VDIM = 64
IDX_HEADS = 8
IDX_DIM = 64
TOPK_MAX = 256
REL_BUCKETS = 32
REL_MAX_DIST = 128
S5_WIDTH = 512
S5_GROUP = 16
S5_GROUPS = S5_WIDTH // S5_GROUP
S5_STATE = 64
X_HEADS = 4
X_HEAD_DIM = 128
DSA_WIDTH = DSA_HEADS * DSA_VDIM
X_WIDTH = X_HEADS * X_HEAD_DIM
N_BRANCH = 3
IN_SPLITS = (DSA_HEADS * DSA_LATENT, DSA_LATENT, IDX_HEADS * IDX_DIM, IDX_DIM,
             IDX_HEADS, S5_WIDTH, X_WIDTH, N_BRANCH * D_MODEL)

LANES = 128
SUBLANES = 8
BF16_ROWS = 16
VMEM_LIMIT = 56 * 1024 * 1024
LOG2E = math.log2(math.e)

TS_IN = 512
TQ = 256
TK = 256
N_PROBE_FIXED = 10
N_PROBE = 26
T_S5 = 64
TS_MERGE = 512
TS_FFN = 512
NEG = -1e30

NT_DIMS = (((1,), (1,)), ((), ()))


def _rms(x, g):
    ms = jnp.mean(x * x, axis=-1, keepdims=True)
    return x * lax.rsqrt(ms + EPS) * g


def _tree(fn, x):
    while x.shape[0] > 1:
        half = x.shape[0] // 2
        x = fn(x[:half], x[half:])
    return x[0]


def _const_spec(shape):
    nd = len(shape)
    return pl.BlockSpec(shape, lambda *_: (0,) * nd, pipeline_mode=pl.Buffered(1))


def _params(sem):
    return pltpu.CompilerParams(dimension_semantics=sem, vmem_limit_bytes=VMEM_LIMIT)


def _inproj_kernel(x_ref, gmix_ref, wqT_ref, wcT_ref, wc_ref, wqiT_ref, wwT_ref, wk_ref, wu_ref,
                   gqc_ref, gkvc_ref, gkv_ref,
                   qT_ref, cT_ref, c_ref, qiT_ref, wT_ref, kidx_ref, u_ref):
    ts = x_ref.shape[0]
    hb = _rms(x_ref[...], gmix_ref[...]).astype(BF16)

    qT = lax.dot_general(wqT_ref[...], hb, NT_DIMS, preferred_element_type=F32)
    q3 = qT.reshape(DSA_HEADS, DSA_LATENT, ts)
    ms = jnp.mean(q3 * q3, axis=1, keepdims=True)
    qT_ref[...] = (q3 * lax.rsqrt(ms + EPS) * gqc_ref[...][None]).reshape(qT.shape).astype(BF16)

    cT = lax.dot_general(wcT_ref[...], hb, NT_DIMS, preferred_element_type=F32)
    msT = jnp.mean(cT * cT, axis=0, keepdims=True)
    cTn = (cT * lax.rsqrt(msT + EPS) * gkvc_ref[...]).astype(BF16)
    for k in range(ts // TK):
        cT_ref[k] = cTn[:, k * TK:(k + 1) * TK]
    c = jnp.dot(hb, wc_ref[...], preferred_element_type=F32)
    c_ref[...] = _rms(c, gkv_ref[...]).astype(BF16)

    qiT_ref[...] = lax.dot_general(wqiT_ref[...], hb, NT_DIMS, preferred_element_type=F32).astype(BF16)
    wT_ref[...] = lax.dot_general(wwT_ref[...], hb, NT_DIMS, preferred_element_type=F32)
    kidx_ref[...] = jnp.dot(hb, wk_ref[...], preferred_element_type=F32).astype(BF16)

    u_ref[...] = jnp.dot(hb, wu_ref[...], preferred_element_type=F32)


def _inproj(x, gmix, wqT, wcT, wc, wqiT, wwT, wk, wu, gqc, gkvc, gkv):
    B, S, D = x.shape
    ts = TS_IN
    grid = (B, S // ts)
    tok = lambda w: pl.BlockSpec((None, ts, w), lambda b, s: (b, s, 0))
    tokT = lambda r: pl.BlockSpec((None, r, ts), lambda b, s: (b, 0, s))
    hq = DSA_HEADS * DSA_LATENT
    hi = IDX_HEADS * LANES
    out_shape = (
        jax.ShapeDtypeStruct((B, hq, S), BF16),
        jax.ShapeDtypeStruct((B, S // TK, DSA_LATENT, TK), BF16),
        jax.ShapeDtypeStruct((B, S, DSA_LATENT), BF16),
        jax.ShapeDtypeStruct((B, hi, S), BF16),
        jax.ShapeDtypeStruct((B, BF16_ROWS, S), F32),
        jax.ShapeDtypeStruct((B, S, LANES), BF16),
        jax.ShapeDtypeStruct((S, B * S5_WIDTH), F32),
    )
    out_specs = (tokT(hq),
                 pl.BlockSpec((None, ts // TK, DSA_LATENT, TK), lambda b, s: (b, s, 0, 0)),
                 tok(DSA_LATENT), tokT(hi), tokT(BF16_ROWS), tok(LANES),
                 pl.BlockSpec((ts, S5_WIDTH), lambda b, s: (s, b)))
    consts = (gmix, wqT, wcT, wc, wqiT, wwT, wk, wu, gqc, gkvc, gkv)
    in_specs = [tok(D)] + [_const_spec(a.shape) for a in consts]
    return pl.pallas_call(
        _inproj_kernel, grid=grid, in_specs=in_specs, out_specs=out_specs, out_shape=out_shape,
        compiler_params=_params(("parallel", "parallel")), name="inproj",
    )(x, *consts)


def _dsa_kernel(qT_ref, qiT_ref, wT_ref, kidx_ref, c_ref, cT_ref, nb_ref, wuvT_ref, o_ref,
                sc_ref, lg_ref, m_ref, l_ref, acc_ref):
    i = pl.program_id(1)
    nk = sc_ref.shape[0]
    kf = float(TOPK_MAX)
    G = TK // SUBLANES

    def rep(fn, a):
        return jnp.broadcast_to(fn(a, axis=0, keepdims=True), (SUBLANES, TQ))

    def full(v, dt=F32):
        return jnp.full((SUBLANES, TQ), v, dt)

    def key_rows(j):
        return pl.ds(pl.multiple_of(j * TK, TK), TK)

    def score(j):
        ks = kidx_ref[key_rows(j), :]
        acc = None
        for h in range(IDX_HEADS):
            d = jnp.dot(ks, qiT_ref[h * LANES:(h + 1) * LANES, :], preferred_element_type=F32)
            t = jnp.maximum(d, 0.0) * wT_ref[h:h + 1, :]
            acc = t if acc is None else acc + t
        return acc

    def score_body(j, carry):
        sc_ref[j] = score(j)
        return carry

    lax.fori_loop(0, i, score_body, 0)
    key_t = lax.broadcasted_iota(jnp.int32, (TK, TQ), 0)
    qry_t = lax.broadcasted_iota(jnp.int32, (TK, TQ), 1)
    causal = key_t <= qry_t
    sc_ref[i] = jnp.where(causal, score(i), -jnp.inf)

    @pl.when(i == 0)
    def _():
        sc_ref[0] = jnp.where(causal, 0.0, NEG)

    @pl.when(i > 0)
    def _():
        nt = i + 1

        def tile3(j):
            return sc_ref[j].reshape(G, SUBLANES, TQ)

        def minmax_body(j, carry):
            mn, mx = carry
            s = tile3(j)
            mx = jnp.maximum(mx, _tree(jnp.maximum, s))
            mn = jnp.minimum(mn, _tree(jnp.minimum, jnp.where(s == -jnp.inf, jnp.inf, s)))
            return mn, mx

        mn, mx = lax.fori_loop(0, nt, minmax_body, (full(jnp.inf), full(-jnp.inf)))
        lo, hi = rep(jnp.min, mn), rep(jnp.max, mx)

        def count_ge(thr):
            def body(j, acc):
                return acc + _tree(jnp.add, jnp.where(tile3(j) >= thr[None], 1.0, 0.0))
            return rep(jnp.sum, lax.fori_loop(0, nt, body, full(0.0)))

        def probe(t, lo, hi, clo, chi):
            secant = jnp.clip((clo - kf) / (clo - chi), 1.0 / 16, 15.0 / 16)
            frac = jnp.where((t & 1) == 1, secant, 0.5)
            x = lo + (hi - lo) * frac
            cx = count_ge(x)
            done = clo == kf
            up = jnp.where(done, 0.0, jnp.where(cx >= kf, 1.0, 0.0)) > 0.5
            dn = jnp.where(done, 0.0, jnp.where(cx >= kf, 0.0, 1.0)) > 0.5
            return (jnp.where(up, x, lo), jnp.where(dn, x, hi),
                    jnp.where(up, cx, clo), jnp.where(dn, cx, chi))

        def fixed_body(t, carry):
            return probe(t, *carry)

        def probe_cond(carry):
            _, _, _, _, pending, t = carry
            return jnp.logical_and(pending > 0, t < N_PROBE)

        def any_pending(clo):
            return (jnp.max(jnp.where(clo == kf, 0.0, 1.0)) > 0.5).astype(jnp.int32)

        def probe_body(carry):
            lo, hi, clo, chi, _, t = carry
            lo, hi, clo, chi = probe(t, lo, hi, clo, chi)
            return lo, hi, clo, chi, any_pending(clo), t + 1

        nvis = (i * TQ + 1 + lax.broadcasted_iota(jnp.int32, (SUBLANES, TQ), 1)).astype(F32)
        carry = lax.fori_loop(0, N_PROBE_FIXED, fixed_body, (lo, hi, nvis, full(0.0)))
        lo, hi, _, _, pending, _ = lax.while_loop(
            probe_cond, probe_body, (*carry, any_pending(carry[2]), jnp.int32(N_PROBE_FIXED)))

        @pl.when(pending == 0)
        def _():
            def body(j, carry):
                sc_ref[j] = jnp.where(tile3(j) >= lo[None], 0.0, NEG).reshape(TK, TQ)
                return carry
            lax.fori_loop(0, nt, body, 0)

        @pl.when(pending > 0)
        def _():
            _exact_select(sc_ref, nt, lo)

    m_ref[...] = jnp.full(m_ref.shape, NEG, F32)
    l_ref[...] = jnp.zeros(l_ref.shape, F32)
    acc_ref[...] = jnp.zeros(acc_ref.shape, F32)
    LG = DSA_LATENT // SUBLANES

    def attend(j, near):
        ct = c_ref[key_rows(j), :]
        ctT = cT_ref[j]
        mb = sc_ref[j]

        stats = []
        for h in range(DSA_HEADS):
            lg = jnp.dot(ct, qT_ref[h * DSA_LATENT:(h + 1) * DSA_LATENT, :],
                         preferred_element_type=F32) + mb
            if near is not None:
                lg = lg + nb_ref[near, h]
            lg_ref[h] = lg
            m_old = m_ref[h]
            m_new = jnp.maximum(m_old, rep(jnp.max, _tree(jnp.maximum, lg.reshape(G, SUBLANES, TQ))))
            stats.append((m_old, m_new))
        for h in range(DSA_HEADS):
            m_old, m_new = stats[h]
            alpha = jnp.exp2(m_old - m_new)
            p3 = jnp.exp2(lg_ref[h].reshape(G, SUBLANES, TQ) - m_new[None])
            l_ref[h] = alpha * l_ref[h] + _tree(jnp.add, p3)
            pv = jnp.dot(ctT, p3.reshape(TK, TQ).astype(BF16), preferred_element_type=F32)
            acc3 = acc_ref[h].reshape(LG, SUBLANES, TQ) * alpha[None]
            acc_ref[h] = acc3.reshape(DSA_LATENT, TQ) + pv
            m_ref[h] = m_new

    def far_body(j, carry):
        attend(j, None)
        return carry

    lax.fori_loop(0, jnp.maximum(i - 1, 0), far_body, 0)

    @pl.when(i > 0)
    def _():
        attend(i - 1, 1)

    attend(i, 0)

    outs = []
    for h in range(DSA_HEADS):
        rl = 1.0 / rep(jnp.sum, l_ref[h])
        o = (acc_ref[h].reshape(LG, SUBLANES, TQ) * rl[None]).reshape(DSA_LATENT, TQ).astype(BF16)
        outs.append(jnp.dot(wuvT_ref[h], o, preferred_element_type=F32))
    o_ref[...] = jnp.concatenate(outs, axis=0).T.astype(BF16)


def _exact_select(sc_ref, nt, lo):
    nk = sc_ref.shape[0]
    kf = float(TOPK_MAX)
    G = TK // SUBLANES

    def rep(fn, a):
        return jnp.broadcast_to(fn(a, axis=0, keepdims=True), (SUBLANES, TQ))

    def full(v, dt=F32):
        return jnp.full((SUBLANES, TQ), v, dt)

    def tile3(j):
        return sc_ref[j].reshape(G, SUBLANES, TQ)

    def walk(cur):
        def body(j, carry):
            ac, am = carry
            s = tile3(j)
            g = s > cur[None]
            ac = ac + jnp.sum(jnp.where(g, 1.0, 0.0), axis=0)
            am = jnp.minimum(am, jnp.min(jnp.where(g, s, jnp.inf), axis=0))
            return ac, am
        ac, am = lax.fori_loop(0, nt, body, (full(0.0), full(jnp.inf)))
        return rep(jnp.sum, ac), rep(jnp.min, am)

    def walk_cond(carry):
        _, _, go, it = carry
        return jnp.logical_and(go > 0, it < nk * TK + 2)

    def walk_body(carry):
        cur, _, _, it = carry
        c, nxt = walk(cur)
        move = c >= kf
        cur = jnp.where(move, nxt, cur)
        go = (jnp.max(jnp.where(move, 1.0, 0.0)) > 0.5).astype(jnp.int32)
        return cur, c, go, it + 1

    kth, cgt, _, _ = lax.while_loop(
        walk_cond, walk_body, (lo, full(0.0), jnp.int32(1), jnp.int32(0)))
    need = kf - cgt

    key3 = (lax.broadcasted_iota(jnp.int32, (G, SUBLANES, TQ), 0) * SUBLANES
            + lax.broadcasted_iota(jnp.int32, (G, SUBLANES, TQ), 1))

    def count_eq_upto(cmax):
        def body(j, acc):
            ok = jnp.where(key3 + j * TK <= cmax[None], 1.0, 0.0)
            return acc + jnp.sum(jnp.where(tile3(j) == kth[None], ok, 0.0), axis=0)
        return rep(jnp.sum, lax.fori_loop(0, nt, body, full(0.0)))

    big = nk * TK
    ceq = count_eq_upto(full(big, jnp.int32))
    has_ties = jnp.max(jnp.where(ceq > need, 1.0, 0.0)) > 0.5

    def tie_cut():
        def body(_, carry):
            lo_c, hi_c = carry
            mid = lax.shift_right_arithmetic(lo_c + hi_c, 1)
            ok = count_eq_upto(mid) >= need
            return jnp.where(ok, lo_c, mid), jnp.where(ok, mid, hi_c)
        _, hi_c = lax.fori_loop(0, 12, body, (full(-1, jnp.int32), full(big - 1, jnp.int32)))
        return hi_c

    cut = lax.cond(has_ties, tie_cut, lambda: full(big, jnp.int32))

    def mask_body(j, carry):
        s = tile3(j)
        tie_ok = jnp.where(key3 + j * TK <= cut[None], 0.0, NEG)
        mb = jnp.where(s > kth[None], 0.0, jnp.where(s == kth[None], tie_ok, NEG))
        sc_ref[j] = mb.reshape(TK, TQ)
        return carry

    lax.fori_loop(0, nt, mask_body, 0)


def _dsa(qT, qiT, wT, kidx, c, cT, nb, wuvT):
    B, S, _ = c.shape
    nk = S // TK
    grid = (B, S // TQ)
    tileT = lambda r: pl.BlockSpec((None, r, TQ), lambda b, i: (b, 0, i))
    seq = lambda w: pl.BlockSpec((None, S, w), lambda b, i: (b, 0, 0))
    in_specs = [tileT(DSA_HEADS * DSA_LATENT), tileT(IDX_HEADS * LANES), tileT(BF16_ROWS),
                seq(LANES), seq(DSA_LATENT),
                pl.BlockSpec((None, nk, DSA_LATENT, TK), lambda b, i: (b, 0, 0, 0)),
                _const_spec(nb.shape), _const_spec(wuvT.shape)]
    scratch = [
        pltpu.VMEM((nk, TK, TQ), F32),
        pltpu.VMEM((DSA_HEADS, TK, TQ), F32),
        pltpu.VMEM((DSA_HEADS, SUBLANES, TQ), F32),
        pltpu.VMEM((DSA_HEADS, SUBLANES, TQ), F32),
        pltpu.VMEM((DSA_HEADS, DSA_LATENT, TQ), F32),
    ]
    return pl.pallas_call(
        _dsa_kernel, grid=grid, in_specs=in_specs,
        out_specs=pl.BlockSpec((None, TQ, DSA_WIDTH), lambda b, i: (b, i, 0)),
        out_shape=jax.ShapeDtypeStruct((B, S, DSA_WIDTH), BF16), scratch_shapes=scratch,
        compiler_params=_params(("parallel", "arbitrary")), name="dsa",
    )(qT, qiT, wT, kidx, c, cT, nb, wuvT)


def _s5_kernel(u_ref, bm_ref, cm_ref, lre_ref, lim_ref, dsk_ref, wglu_ref, o_ref,
               st_ref, hre_ref, him_ref):
    nstate = S5_GROUPS * S5_STATE
    nb = hre_ref.shape[0]

    @pl.when(pl.program_id(0) == 0)
    def _():
        hre_ref[...] = jnp.zeros(hre_ref.shape, F32)
        him_ref[...] = jnp.zeros(him_ref.shape, F32)

    u = u_ref[...]
    st_ref[...] = jnp.dot(u.astype(BF16), bm_ref[...], preferred_element_type=F32)

    half = nstate // 2
    for part in range(2):
        re_sl = slice(part * half, (part + 1) * half)
        im_sl = slice(nstate + part * half, nstate + (part + 1) * half)
        lre = lre_ref[:, re_sl]
        lim = lim_ref[:, re_sl]

        def step(t, carry):
            hr, hi = carry
            r = pl.ds(pl.multiple_of(t * nb, nb), nb)
            nr = lre * hr - lim * hi + st_ref[r, re_sl]
            ni = lre * hi + lim * hr + st_ref[r, im_sl]
            st_ref[r, re_sl] = nr
            st_ref[r, im_sl] = ni
            return nr, ni

        hr, hi = lax.fori_loop(0, T_S5, step, (hre_ref[:, re_sl], him_ref[:, re_sl]), unroll=4)
        hre_ref[:, re_sl] = hr
        him_ref[:, re_sl] = hi

    y = jnp.dot(st_ref[...].astype(BF16), cm_ref[...], preferred_element_type=F32)
    y = jax.nn.gelu(y + dsk_ref[...] * u)
    z = jnp.dot(y.astype(BF16), wglu_ref[...], preferred_element_type=F32)
    o_ref[...] = (y * jax.nn.sigmoid(z)).astype(BF16)


def _s5(u2, nbatch, bm, cm, lre, lim, dsk, wglu):
    rows = u2.shape[0]
    tb = T_S5 * nbatch
    nstate = S5_GROUPS * S5_STATE
    grid = (rows // tb,)
    tok = pl.BlockSpec((tb, S5_WIDTH), lambda t: (t, 0))
    in_specs = [tok, _const_spec(bm.shape), _const_spec(cm.shape), _const_spec(lre.shape),
                _const_spec(lim.shape), _const_spec(dsk.shape), _const_spec(wglu.shape)]
    scratch = [pltpu.VMEM((tb, 2 * nstate), F32),
               pltpu.VMEM((nbatch, nstate), F32), pltpu.VMEM((nbatch, nstate), F32)]
    return pl.pallas_call(
        _s5_kernel, grid=grid, in_specs=in_specs, out_specs=tok,
        out_shape=jax.ShapeDtypeStruct((rows, S5_WIDTH), BF16), scratch_shapes=scratch,
        compiler_params=_params(("arbitrary",)), name="s5",
    )(u2, bm, cm, lre, lim, dsk, wglu)


def _memkv_kernel(mem_ref, gmem_ref, wkv_ref, gk_ref, k_ref, v_ref):
    mb = _rms(mem_ref[...], gmem_ref[...]).astype(BF16)
    kv = jnp.dot(mb, wkv_ref[...], preferred_element_type=F32)
    for h in range(X_HEADS):
        sl = slice(h * X_HEAD_DIM, (h + 1) * X_HEAD_DIM)
        k_ref[:, sl] = _rms(kv[:, sl], gk_ref[...]).astype(BF16)
    v_ref[...] = kv[:, X_WIDTH:].astype(BF16)


def _memkv(mem, gmem, wkv, gk):
    B, M, D = mem.shape
    blk = lambda w: pl.BlockSpec((None, M, w), lambda b: (b, 0, 0))
    return pl.pallas_call(
        _memkv_kernel, grid=(B,),
        in_specs=[blk(D), _const_spec(gmem.shape), _const_spec(wkv.shape), _const_spec(gk.shape)],
        out_specs=(blk(X_WIDTH), blk(X_WIDTH)),
        out_shape=(jax.ShapeDtypeStruct((B, M, X_WIDTH), BF16),) * 2,
        compiler_params=_params(("parallel",)), name="memkv",
    )(mem, gmem, wkv, gk)


def _merge_kernel(x_ref, odsa_ref, os5_ref, k_ref, v_ref, gmix_ref, wg_ref, wqx_ref, gqx_ref,
                  wb1_ref, wb2_ref, wb3_ref, wout_ref, y_ref):
    x = x_ref[...]
    hb = _rms(x, gmix_ref[...]).astype(BF16)

    qx = jnp.dot(hb, wqx_ref[...], preferred_element_type=F32)
    gqx = gqx_ref[...] * (X_HEAD_DIM ** -0.5)
    ox = []
    for h in range(X_HEADS):
        sl = slice(h * X_HEAD_DIM, (h + 1) * X_HEAD_DIM)
        qh = _rms(qx[:, sl], gqx).astype(BF16)
        lg = lax.dot_general(qh, k_ref[:, sl], NT_DIMS, preferred_element_type=F32)
        p = jnp.exp(lg - jnp.max(lg, axis=-1, keepdims=True))
        pv = jnp.dot(p.astype(BF16), v_ref[:, sl], preferred_element_type=F32)
        ox.append((pv / jnp.sum(p, axis=-1, keepdims=True)).astype(BF16))
    ox = jnp.concatenate(ox, axis=1)

    merged = None
    for br, (o, wb) in enumerate(((odsa_ref[...], wb1_ref), (os5_ref[...], wb2_ref), (ox, wb3_ref))):
        gate = jax.nn.sigmoid(jnp.dot(hb, wg_ref[:, br * D_MODEL:(br + 1) * D_MODEL],
                                      preferred_element_type=F32))
        term = gate * jnp.dot(o, wb[...], preferred_element_type=F32)
        merged = term if merged is None else merged + term
    y_ref[...] = x + jnp.dot(merged.astype(BF16), wout_ref[...], preferred_element_type=F32)


def _merge(x, odsa, os5, k, v, gmix, wg, wqx, gqx, wb1, wb2, wb3, wout):
    B, S, D = x.shape
    ts = TS_MERGE
    tok = lambda w: pl.BlockSpec((None, ts, w), lambda b, s: (b, s, 0))
    memspec = pl.BlockSpec((None, N_MEM, X_WIDTH), lambda b, s: (b, 0, 0))
    in_specs = [tok(D), tok(DSA_WIDTH),
                pl.BlockSpec((ts, S5_WIDTH), lambda b, s: (s, b)),
                memspec, memspec] + [_const_spec(a.shape) for a in
                                     (gmix, wg, wqx, gqx, wb1, wb2, wb3, wout)]
    return pl.pallas_call(
        _merge_kernel, grid=(B, S // ts), in_specs=in_specs, out_specs=tok(D),
        out_shape=jax.ShapeDtypeStruct((B, S, D), F32),
        compiler_params=_params(("parallel", "parallel")), name="merge",
    )(x, odsa, os5, k, v, gmix, wg, wqx, gqx, wb1, wb2, wb3, wout)


def _ffn_kernel(x_ref, g_ref, wg_ref, wu_ref, wd_ref, y_ref):
    x = x_ref[...]
    hb = _rms(x, g_ref[...]).astype(BF16)
    a = jnp.dot(hb, wg_ref[...], preferred_element_type=F32)
    b = jnp.dot(hb, wu_ref[...], preferred_element_type=F32)
    act = (jax.nn.silu(a) * b).astype(BF16)
    y_ref[...] = x + jnp.dot(act, wd_ref[...], preferred_element_type=F32)


def _ffn(x2, g, wg, wu, wd):
    n, D = x2.shape
    tok = pl.BlockSpec((TS_FFN, D), lambda t: (t, 0))
    return pl.pallas_call(
        _ffn_kernel, grid=(n // TS_FFN,),
        in_specs=[tok] + [_const_spec(a.shape) for a in (g, wg, wu, wd)],
        out_specs=tok, out_shape=jax.ShapeDtypeStruct((n, D), F32),
        compiler_params=_params(("parallel",)), name="ffn",
    )(x2, g, wg, wu, wd)


def _t5_bucket(n):
    max_exact = REL_BUCKETS // 2
    nf = jnp.maximum(n, 1).astype(F32)
    large = max_exact + (jnp.log(nf / max_exact) / math.log(REL_MAX_DIST / max_exact)
                         * (REL_BUCKETS - max_exact)).astype(jnp.int32)
    large = jnp.minimum(large, REL_BUCKETS - 1)
    return jnp.where(n < max_exact, n, large)


def _hankel(w, rows, cols):
    H, L = w.shape
    flat = jnp.tile(w, (1, rows + 1))[:, :rows * (L + 1)]
    return flat.reshape(H, rows, L + 1)[:, :, :cols]


def _near_bias(rel_bias):
    n = jnp.arange(2 * TQ, dtype=jnp.int32)
    f = (rel_bias[_t5_bucket(n)] - rel_bias[REL_BUCKETS - 1][None, :]).T * LOG2E
    w_diag = jnp.concatenate([jnp.broadcast_to(f[:, :1], (f.shape[0], TK - 1)), f[:, :TQ]], axis=1)
    w_prev = f[:, 1:TQ + TK]
    diag = _hankel(w_diag, TK, TQ)[:, ::-1, :]
    prev = _hankel(w_prev, TK, TQ)[:, ::-1, :]
    return jnp.stack([diag, prev], axis=0).astype(F32)


def _s5_mats(a_re, a_im, log_dt, b_re, b_im, c_re, c_im):
    lam = lax.complex(a_re.astype(F32), a_im.astype(F32))
    dt = jnp.exp(log_dt.astype(F32))[:, None]
    lam_bar = jnp.exp(lam * dt)
    b_bar = ((lam_bar - 1.0) / lam)[..., None] * lax.complex(b_re.astype(F32), b_im.astype(F32))
    nstate = S5_GROUPS * S5_STATE
    in_mask = (jnp.arange(S5_WIDTH)[:, None] // S5_GROUP) == (jnp.arange(nstate)[None, :] // S5_STATE)

    def blockdiag_in(w):
        t = jnp.transpose(w, (0, 2, 1)).reshape(S5_WIDTH, S5_STATE)
        return jnp.where(in_mask, jnp.tile(t, (1, S5_GROUPS)), 0.0)

    def blockdiag_out(w):
        t = jnp.transpose(w, (0, 2, 1)).reshape(nstate, S5_GROUP)
        return jnp.where(in_mask.T, jnp.tile(t, (1, S5_GROUPS)), 0.0)

    bm = jnp.concatenate([blockdiag_in(jnp.real(b_bar)), blockdiag_in(jnp.imag(b_bar))], axis=1)
    cm = jnp.concatenate([blockdiag_out(c_re.astype(F32)), blockdiag_out(-c_im.astype(F32))], axis=0)
    return bm.astype(BF16), cm.astype(BF16), jnp.real(lam_bar).reshape(1, -1), jnp.imag(lam_bar).reshape(1, -1)


def kernel(x, mem, rel_bias, w_in, g_mix_norm, g_q_dsa, g_kv_dsa, w_uv_dsa, a_re, a_im, log_dt, b_re, b_im, c_re, c_im, d_skip, w_glu, g_mem_norm, w_mem_kv, g_q_cross, g_k_cross, w_br_dsa, w_br_s5, w_br_cross, w_out, g_ffn_norm, w_ffn_gate, w_ffn_up, w_ffn_down):
    B, S, D = x.shape
    depth = w_in.shape[0]
    offs = [0] + [int(o) for o in np.cumsum(IN_SPLITS)]
    nb = _near_bias(rel_bias)
    row = lambda v: v.reshape(1, -1).astype(F32)
    col = lambda v: v.reshape(-1, 1).astype(F32)
    bf = lambda a: a.astype(BF16)
    for l in range(depth):
        w = bf(w_in[l])
        wq, wc, wqi, wk, ww, wu, wqx, wg = [w[:, offs[k]:offs[k + 1]] for k in range(len(IN_SPLITS))]
        wqiT = jnp.pad(wqi.T.reshape(IDX_HEADS, IDX_DIM, D), ((0, 0), (0, LANES - IDX_DIM), (0, 0)))
        wqiT = wqiT.reshape(IDX_HEADS * LANES, D)
        wwT = jnp.pad(ww.T, ((0, BF16_ROWS - IDX_HEADS), (0, 0)))
        wkp = jnp.pad(wk, ((0, 0), (0, LANES - IDX_DIM)))
        gqc = col(g_q_dsa[l]) * (DSA_LATENT ** -0.5 * LOG2E)

        qT, cT, c, qiT, wT, kidx, u = _inproj(
            x, row(g_mix_norm[l]), wq.T, wc.T, wc, wqiT, wwT, wkp, wu,
            gqc, col(g_kv_dsa[l]), row(g_kv_dsa[l]))

        wuvT = bf(jnp.transpose(w_uv_dsa[l], (0, 2, 1)))
        o_dsa = _dsa(qT, qiT, wT, kidx, c, cT, nb, wuvT)

        bm, cm, lre, lim = _s5_mats(a_re[l], a_im[l], log_dt[l], b_re[l], b_im[l], c_re[l], c_im[l])
        lre = jnp.broadcast_to(lre, (B, lre.shape[1]))
        lim = jnp.broadcast_to(lim, (B, lim.shape[1]))
        o_s5 = _s5(u.reshape(S * B, S5_WIDTH), B, bm, cm, lre, lim, row(d_skip[l]), bf(w_glu[l]))
        o_s5 = o_s5.reshape(S, B * S5_WIDTH)

        k, v = _memkv(mem, row(g_mem_norm[l]), bf(w_mem_kv[l]), row(g_k_cross[l]))

        x1 = _merge(x, o_dsa, o_s5, k, v, row(g_mix_norm[l]), wg, wqx, row(g_q_cross[l]),
                    bf(w_br_dsa[l]), bf(w_br_s5[l]), bf(w_br_cross[l]), bf(w_out[l]))

        x = _ffn(x1.reshape(B * S, D), row(g_ffn_norm[l]), bf(w_ffn_gate[l]), bf(w_ffn_up[l]),
                 bf(w_ffn_down[l])).reshape(B, S, D)
    return x
```

```python
import math

import jax
import jax.numpy as jnp
import numpy as np
from jax import lax
from jax.experimental import pallas as pl
from jax.experimental.pallas import tpu as pltpu

F32 = jnp.float32
BF16 = jnp.bfloat16

D_MODEL = 1024
N_MEM = 256
EPS = 1e-6
DSA_HEADS = 8
DSA_LATENT = 128
DSA_VDIM = 64
IDX_HEADS = 8
IDX_DIM = 64
TOPK_MAX = 256
REL_BUCKETS = 32
REL_MAX_DIST = 128
S5_WIDTH = 512
S5_GROUP = 16
S5_GROUPS = S5_WIDTH // S5_GROUP
S5_STATE = 64
X_HEADS = 4
X_HEAD_DIM = 128
DSA_WIDTH = DSA_HEADS * DSA_VDIM
X_WIDTH = X_HEADS * X_HEAD_DIM
N_BRANCH = 3
IN_SPLITS = (DSA_HEADS * DSA_LATENT, DSA_LATENT, IDX_HEADS * IDX_DIM, IDX_DIM,
             IDX_HEADS, S5_WIDTH, X_WIDTH, N_BRANCH * D_MODEL)

LANES = 128
SUBLANES = 8
BF16_ROWS = 16
CT_ROWS = DSA_LATENT + BF16_ROWS
VMEM_LIMIT = 56 * 1024 * 1024
LOG2E = math.log2(math.e)

TS_IN = 512
TQ = 256
TK = 256
N_PROBE_FIXED = 16
N_PROBE = 28
T_S5 = 64
S5_PITCH = T_S5 + SUBLANES
TS_MERGE = 512
TS_FFN = 512
NEG = -1e30

NT_DIMS = (((1,), (1,)), ((), ()))


def _rms(x, g):
    ms = jnp.mean(x * x, axis=-1, keepdims=True)
    return x * lax.rsqrt(ms + EPS) * g


def _tree(fn, x):
    while x.shape[0] > 1:
        half = x.shape[0] // 2
        x = fn(x[:half], x[half:])
    return x[0]


def _const_spec(shape):
    nd = len(shape)
    return pl.BlockSpec(shape, lambda *_: (0,) * nd, pipeline_mode=pl.Buffered(1))


def _params(sem):
    return pltpu.CompilerParams(dimension_semantics=sem, vmem_limit_bytes=VMEM_LIMIT)


def _inproj_kernel(x_ref, gmix_ref, wqT_ref, wcT_ref, wc_ref, wqiT_ref, wwT_ref, wk_ref, wu_ref,
                   gqc_ref, gkvc_ref, gkv_ref,
                   qT_ref, cT_ref, c_ref, qiT_ref, wT_ref, kidx_ref, u_ref):
    ts = x_ref.shape[0]
    hb = _rms(x_ref[...], gmix_ref[...]).astype(BF16)

    qT = lax.dot_general(wqT_ref[...], hb, NT_DIMS, preferred_element_type=F32)
    q3 = qT.reshape(DSA_HEADS, DSA_LATENT, ts)
    ms = jnp.mean(q3 * q3, axis=1, keepdims=True)
    qT_ref[...] = (q3 * lax.rsqrt(ms + EPS) * gqc_ref[...][None]).reshape(qT.shape).astype(BF16)

    cT = lax.dot_general(wcT_ref[...], hb, NT_DIMS, preferred_element_type=F32)
    msT = jnp.mean(cT * cT, axis=0, keepdims=True)
    cTn = (cT * lax.rsqrt(msT + EPS) * gkvc_ref[...]).astype(BF16)
    for k in range(ts // TK):
        cT_ref[k, :DSA_LATENT, :] = cTn[:, k * TK:(k + 1) * TK]
        cT_ref[k, DSA_LATENT:, :] = jnp.ones((BF16_ROWS, TK), BF16)
    c = jnp.dot(hb, wc_ref[...], preferred_element_type=F32)
    c_ref[...] = _rms(c, gkv_ref[...]).astype(BF16)

    qiT_ref[...] = lax.dot_general(wqiT_ref[...], hb, NT_DIMS, preferred_element_type=F32).astype(BF16)
    wT_ref[...] = lax.dot_general(wwT_ref[...], hb, NT_DIMS, preferred_element_type=F32)
    kidx_ref[...] = jnp.dot(hb, wk_ref[...], preferred_element_type=F32).astype(BF16)

    u_ref[...] = jnp.dot(hb, wu_ref[...], preferred_element_type=F32)


def _inproj(x, gmix, wqT, wcT, wc, wqiT, wwT, wk, wu, gqc, gkvc, gkv):
    B, S, D = x.shape
    ts = TS_IN
    grid = (B, S // ts)
    tok = lambda w: pl.BlockSpec((None, ts, w), lambda b, s: (b, s, 0))
    tokT = lambda r: pl.BlockSpec((None, r, ts), lambda b, s: (b, 0, s))
    hq = DSA_HEADS * DSA_LATENT
    hi = IDX_HEADS * LANES
    out_shape = (
        jax.ShapeDtypeStruct((B, hq, S), BF16),
        jax.ShapeDtypeStruct((B, S // TK, CT_ROWS, TK), BF16),
        jax.ShapeDtypeStruct((B, S, DSA_LATENT), BF16),
        jax.ShapeDtypeStruct((B, hi, S), BF16),
        jax.ShapeDtypeStruct((B, BF16_ROWS, S), F32),
        jax.ShapeDtypeStruct((B, S, LANES), BF16),
        jax.ShapeDtypeStruct((B, S, S5_WIDTH), F32),
    )
    out_specs = (tokT(hq),
                 pl.BlockSpec((None, ts // TK, CT_ROWS, TK), lambda b, s: (b, s, 0, 0)),
                 tok(DSA_LATENT), tokT(hi), tokT(BF16_ROWS), tok(LANES),
                 tok(S5_WIDTH))
    consts = (gmix, wqT, wcT, wc, wqiT, wwT, wk, wu, gqc, gkvc, gkv)
    in_specs = [tok(D)] + [_const_spec(a.shape) for a in consts]
    return pl.pallas_call(
        _inproj_kernel, grid=grid, in_specs=in_specs, out_specs=out_specs, out_shape=out_shape,
        compiler_params=_params(("parallel", "parallel")), name="inproj",
    )(x, *consts)


def _dsa_kernel(qT_ref, qiT_ref, wT_ref, kidx_ref, c_ref, cT_ref, nb_ref, wuvT_ref, o_ref,
                sc_ref, lg_ref, m_ref, acc_ref):
    i = pl.program_id(1)
    nk = sc_ref.shape[0]
    kf = float(TOPK_MAX)
    G = TK // SUBLANES

    def rep(fn, a):
        return jnp.broadcast_to(fn(a, axis=0, keepdims=True), (SUBLANES, TQ))

    def full(v, dt=F32):
        return jnp.full((SUBLANES, TQ), v, dt)

    def key_rows(j):
        return pl.ds(pl.multiple_of(j * TK, TK), TK)

    def score(j):
        ks = kidx_ref[key_rows(j), :]
        acc = None
        for h in range(IDX_HEADS):
            d = jnp.dot(ks, qiT_ref[h * LANES:(h + 1) * LANES, :], preferred_element_type=F32)
            t = jnp.maximum(d, 0.0) * wT_ref[h:h + 1, :]
            acc = t if acc is None else acc + t
        return acc

    def score_body(j, carry):
        sc_ref[j] = score(j)
        return carry

    lax.fori_loop(0, i, score_body, 0)
    key_t = lax.broadcasted_iota(jnp.int32, (TK, TQ), 0)
    qry_t = lax.broadcasted_iota(jnp.int32, (TK, TQ), 1)
    causal = key_t <= qry_t
    sc_ref[i] = jnp.where(causal, score(i), -jnp.inf)

    @pl.when(i == 0)
    def _():
        sc_ref[0] = jnp.where(causal, 0.0, NEG)

    @pl.when(i > 0)
    def _():
        nt = i + 1

        def tile3(j):
            return sc_ref[j].reshape(G, SUBLANES, TQ)

        def minmax_body(j, carry):
            mn, mx = carry
            s = tile3(j)
            mx = jnp.maximum(mx, _tree(jnp.maximum, s))
            mn = jnp.minimum(mn, _tree(jnp.minimum, jnp.where(s == -jnp.inf, jnp.inf, s)))
            return mn, mx

        mn, mx = lax.fori_loop(0, nt, minmax_body, (full(jnp.inf), full(-jnp.inf)))
        lo, hi = rep(jnp.min, mn), rep(jnp.max, mx)

        def count_ge(thr):
            def body(j, acc):
                return acc + _tree(jnp.add, jnp.where(tile3(j) >= thr[None], 1.0, 0.0))
            return rep(jnp.sum, lax.fori_loop(0, nt, body, full(0.0)))

        def probe(t, lo, hi, clo, chi):
            secant = jnp.clip((clo - kf) / (clo - chi), 1.0 / 16, 15.0 / 16)
            frac = jnp.where((t & 1) == 1, secant, 0.5)
            x = lo + (hi - lo) * frac
            cx = count_ge(x)
            done = clo == kf
            up = jnp.where(done, 0.0, jnp.where(cx >= kf, 1.0, 0.0)) > 0.5
            dn = jnp.where(done, 0.0, jnp.where(cx >= kf, 0.0, 1.0)) > 0.5
            return (jnp.where(up, x, lo), jnp.where(dn, x, hi),
                    jnp.where(up, cx, clo), jnp.where(dn, cx, chi))

        def fixed_body(t, carry):
            return probe(t, *carry)

        def probe_cond(carry):
            _, _, _, _, pending, t = carry
            return jnp.logical_and(pending > 0, t < N_PROBE)

        def any_pending(clo):
            return (jnp.max(jnp.where(clo == kf, 0.0, 1.0)) > 0.5).astype(jnp.int32)

        def probe_body(carry):
            lo, hi, clo, chi, _, t = carry
            lo, hi, clo, chi = probe(t + 1, *probe(t, lo, hi, clo, chi))
            return lo, hi, clo, chi, any_pending(clo), t + 2

        nvis = (i * TQ + 1 + lax.broadcasted_iota(jnp.int32, (SUBLANES, TQ), 1)).astype(F32)
        carry = lax.fori_loop(0, N_PROBE_FIXED, fixed_body, (lo, hi, nvis, full(0.0)))
        lo, hi, _, _, pending, _ = lax.while_loop(
            probe_cond, probe_body, (*carry, any_pending(carry[2]), jnp.int32(N_PROBE_FIXED)))

        @pl.when(pending == 0)
        def _():
            def body(j, carry):
                sc_ref[j] = jnp.where(tile3(j) >= lo[None], 0.0, NEG).reshape(TK, TQ)
                return carry
            lax.fori_loop(0, nt, body, 0)

        @pl.when(pending > 0)
        def _():
            _exact_select(sc_ref, nt, lo)

    m_ref[...] = jnp.full(m_ref.shape, NEG, F32)
    acc_ref[...] = jnp.zeros(acc_ref.shape, F32)
    LG = CT_ROWS // SUBLANES

    def attend(j, near):
        ct = c_ref[key_rows(j), :]
        ctT = cT_ref[j]
        mb = sc_ref[j]

        stats = []
        for h in range(DSA_HEADS):
            lg = jnp.dot(ct, qT_ref[h * DSA_LATENT:(h + 1) * DSA_LATENT, :],
                         preferred_element_type=F32) + mb
            if near is not None:
                lg = lg + nb_ref[near, h]
            lg_ref[h] = lg
            m_old = m_ref[h]
            m_new = jnp.maximum(m_old, rep(jnp.max, _tree(jnp.maximum, lg.reshape(G, SUBLANES, TQ))))
            stats.append((m_old, m_new))
        for h in range(DSA_HEADS):
            m_old, m_new = stats[h]
            alpha = jnp.exp2(m_old - m_new)
            x3 = lg_ref[h].reshape(G, SUBLANES, TQ) - m_new[None]
            p = jnp.exp2(x3.reshape(TK, TQ).astype(BF16))
            pv = jnp.dot(ctT, p, preferred_element_type=F32)
            acc3 = acc_ref[h].reshape(LG, SUBLANES, TQ) * alpha[None]
            acc_ref[h] = acc3.reshape(CT_ROWS, TQ) + pv
            m_ref[h] = m_new

    def far_body(j, carry):
        attend(j, None)
        return carry

    lax.fori_loop(0, jnp.maximum(i - 1, 0), far_body, 0)

    @pl.when(i > 0)
    def _():
        attend(i - 1, 1)

    attend(i, 0)

    outs = []
    for h in range(DSA_HEADS):
        rl = 1.0 / acc_ref[h, DSA_LATENT:DSA_LATENT + SUBLANES, :]
        o3 = acc_ref[h, :DSA_LATENT, :].reshape(DSA_LATENT // SUBLANES, SUBLANES, TQ) * rl[None]
        o = o3.reshape(DSA_LATENT, TQ).astype(BF16)
        outs.append(jnp.dot(wuvT_ref[h], o, preferred_element_type=F32))
    o_ref[...] = jnp.concatenate(outs, axis=0).T.astype(BF16)


def _exact_select(sc_ref, nt, lo):
    nk = sc_ref.shape[0]
    kf = float(TOPK_MAX)
    G = TK // SUBLANES

    def rep(fn, a):
        return jnp.broadcast_to(fn(a, axis=0, keepdims=True), (SUBLANES, TQ))

    def full(v, dt=F32):
        return jnp.full((SUBLANES, TQ), v, dt)

    def tile3(j):
        return sc_ref[j].reshape(G, SUBLANES, TQ)

    def walk(cur):
        def body(j, carry):
            ac, am = carry
            s = tile3(j)
            g = s > cur[None]
            ac = ac + jnp.sum(jnp.where(g, 1.0, 0.0), axis=0)
            am = jnp.minimum(am, jnp.min(jnp.where(g, s, jnp.inf), axis=0))
            return ac, am
        ac, am = lax.fori_loop(0, nt, body, (full(0.0), full(jnp.inf)))
        return rep(jnp.sum, ac), rep(jnp.min, am)

    def walk_cond(carry):
        _, _, go, it = carry
        return jnp.logical_and(go > 0, it < nk * TK + 2)

    def walk_body(carry):
        cur, _, _, it = carry
        c, nxt = walk(cur)
        move = c >= kf
        cur = jnp.where(move, nxt, cur)
        go = (jnp.max(jnp.where(move, 1.0, 0.0)) > 0.5).astype(jnp.int32)
        return cur, c, go, it + 1

    kth, cgt, _, _ = lax.while_loop(
        walk_cond, walk_body, (lo, full(0.0), jnp.int32(1), jnp.int32(0)))
    need = kf - cgt

    key3 = (lax.broadcasted_iota(jnp.int32, (G, SUBLANES, TQ), 0) * SUBLANES
            + lax.broadcasted_iota(jnp.int32, (G, SUBLANES, TQ), 1))

    def count_eq_upto(cmax):
        def body(j, acc):
            ok = jnp.where(key3 + j * TK <= cmax[None], 1.0, 0.0)
            return acc + jnp.sum(jnp.where(tile3(j) == kth[None], ok, 0.0), axis=0)
        return rep(jnp.sum, lax.fori_loop(0, nt, body, full(0.0)))

    big = nk * TK
    ceq = count_eq_upto(full(big, jnp.int32))
    has_ties = jnp.max(jnp.where(ceq > need, 1.0, 0.0)) > 0.5

    def tie_cut():
        def body(_, carry):
            lo_c, hi_c = carry
            mid = lax.shift_right_arithmetic(lo_c + hi_c, 1)
            ok = count_eq_upto(mid) >= need
            return jnp.where(ok, lo_c, mid), jnp.where(ok, mid, hi_c)
        _, hi_c = lax.fori_loop(0, 12, body, (full(-1, jnp.int32), full(big - 1, jnp.int32)))
        return hi_c

    cut = lax.cond(has_ties, tie_cut, lambda: full(big, jnp.int32))

    def mask_body(j, carry):
        s = tile3(j)
        tie_ok = jnp.where(key3 + j * TK <= cut[None], 0.0, NEG)
        mb = jnp.where(s > kth[None], 0.0, jnp.where(s == kth[None], tie_ok, NEG))
        sc_ref[j] = mb.reshape(TK, TQ)
        return carry

    lax.fori_loop(0, nt, mask_body, 0)


def _dsa(qT, qiT, wT, kidx, c, cT, nb, wuvT):
    B, S, _ = c.shape
    nk = S // TK
    grid = (B, S // TQ)
    tileT = lambda r: pl.BlockSpec((None, r, TQ), lambda b, i: (b, 0, i))
    seq = lambda w: pl.BlockSpec((None, S, w), lambda b, i: (b, 0, 0))
    in_specs = [tileT(DSA_HEADS * DSA_LATENT), tileT(IDX_HEADS * LANES), tileT(BF16_ROWS),
                seq(LANES), seq(DSA_LATENT),
                pl.BlockSpec((None, nk, CT_ROWS, TK), lambda b, i: (b, 0, 0, 0)),
                _const_spec(nb.shape), _const_spec(wuvT.shape)]
    scratch = [
        pltpu.VMEM((nk, TK, TQ), F32),
        pltpu.VMEM((DSA_HEADS, TK, TQ), F32),
        pltpu.VMEM((DSA_HEADS, SUBLANES, TQ), F32),
        pltpu.VMEM((DSA_HEADS, CT_ROWS, TQ), F32),
    ]
    return pl.pallas_call(
        _dsa_kernel, grid=grid, in_specs=in_specs,
        out_specs=pl.BlockSpec((None, TQ, DSA_WIDTH), lambda b, i: (b, i, 0)),
        out_shape=jax.ShapeDtypeStruct((B, S, DSA_WIDTH), BF16), scratch_shapes=scratch,
        compiler_params=_params(("parallel", "arbitrary")), name="dsa",
    )(qT, qiT, wT, kidx, c, cT, nb, wuvT)


def _s5_kernel(u_ref, bm_ref, cm_ref, lre_ref, lim_ref, dsk_ref, wglu_ref, o_ref,
               uslab_ref, uil_ref, st_ref, oslab_ref, hre_ref, him_ref):
    nstate = S5_GROUPS * S5_STATE
    nb = hre_ref.shape[0]
    nslab = S5_WIDTH // LANES

    @pl.when(pl.program_id(0) == 0)
    def _():
        hre_ref[...] = jnp.zeros(hre_ref.shape, F32)
        him_ref[...] = jnp.zeros(him_ref.shape, F32)

    for b in range(nb):
        for k in range(nslab):
            uslab_ref[k, b * S5_PITCH:b * S5_PITCH + T_S5, :] = u_ref[b, :, k * LANES:(k + 1) * LANES]

    def gather_step(t, carry):
        for k in range(nslab):
            uil_ref[pl.ds(pl.multiple_of(t * nb, nb), nb), k * LANES:(k + 1) * LANES] = (
                uslab_ref[k, pl.ds(t, nb, stride=S5_PITCH), :])
        return carry

    lax.fori_loop(0, T_S5, gather_step, 0, unroll=4)
    u = uil_ref[...]
    ub = u.astype(BF16)

    kin = 4
    cin = S5_WIDTH // kin
    sin = nstate // kin
    for k in range(kin):
        uk = ub[:, k * cin:(k + 1) * cin]
        for off in (0, nstate):
            cols = slice(off + k * sin, off + (k + 1) * sin)
            st_ref[:, cols] = jnp.dot(uk, bm_ref[k * cin:(k + 1) * cin, cols],
                                      preferred_element_type=F32)

    half = nstate // 2
    for part in range(2):
        re_sl = slice(part * half, (part + 1) * half)
        im_sl = slice(nstate + part * half, nstate + (part + 1) * half)
        lre = lre_ref[:, re_sl]
        lim = lim_ref[:, re_sl]

        def step(t, carry):
            hr, hi = carry
            r = pl.ds(pl.multiple_of(t * nb, nb), nb)
            nr = lre * hr - lim * hi + st_ref[r, re_sl]
            ni = lre * hi + lim * hr + st_ref[r, im_sl]
            st_ref[r, re_sl] = nr
            st_ref[r, im_sl] = ni
            return nr, ni

        hr, hi = lax.fori_loop(0, T_S5, step, (hre_ref[:, re_sl], him_ref[:, re_sl]), unroll=4)
        hre_ref[:, re_sl] = hr
        him_ref[:, re_sl] = hi

    kout = 2
    cout = S5_WIDTH // kout
    sout = nstate // kout
    ys = []
    for k in range(kout):
        acc = None
        for off in (0, nstate):
            rows = slice(off + k * sout, off + (k + 1) * sout)
            part = jnp.dot(st_ref[:, rows].astype(BF16), cm_ref[rows, k * cout:(k + 1) * cout],
                           preferred_element_type=F32)
            acc = part if acc is None else acc + part
        ys.append(acc)
    y = jnp.concatenate(ys, axis=1)
    y = jax.nn.gelu(y + dsk_ref[...] * u)
    z = jnp.dot(y.astype(BF16), wglu_ref[...], preferred_element_type=F32)
    o = y * jax.nn.sigmoid(z)

    for k in range(nslab):
        oslab_ref[k] = o[:, k * LANES:(k + 1) * LANES]
    for b in range(nb):
        for k in range(nslab):
            o_ref[b, :, k * LANES:(k + 1) * LANES] = (
                oslab_ref[k, pl.ds(b, T_S5, stride=nb), :].astype(BF16))


def _s5(u, bm, cm, lre, lim, dsk, wglu):
    B, S, W = u.shape
    tb = T_S5 * B
    nstate = S5_GROUPS * S5_STATE
    tok = pl.BlockSpec((B, T_S5, W), lambda t: (0, t, 0))
    in_specs = [tok, _const_spec(bm.shape), _const_spec(cm.shape), _const_spec(lre.shape),
                _const_spec(lim.shape), _const_spec(dsk.shape), _const_spec(wglu.shape)]
    scratch = [pltpu.VMEM((W // LANES, B * S5_PITCH, LANES), F32),
               pltpu.VMEM((tb, W), F32),
               pltpu.VMEM((tb, 2 * nstate), F32),
               pltpu.VMEM((W // LANES, tb, LANES), F32),
               pltpu.VMEM((B, nstate), F32), pltpu.VMEM((B, nstate), F32)]
    return pl.pallas_call(
        _s5_kernel, grid=(S // T_S5,), in_specs=in_specs, out_specs=tok,
        out_shape=jax.ShapeDtypeStruct((B, S, W), BF16), scratch_shapes=scratch,
        compiler_params=_params(("arbitrary",)), name="s5",
    )(u, bm, cm, lre, lim, dsk, wglu)


def _memkv_kernel(mem_ref, gmem_ref, wkv_ref, gk_ref, k_ref, v_ref):
    mb = _rms(mem_ref[...], gmem_ref[...]).astype(BF16)
    kv = jnp.dot(mb, wkv_ref[...], preferred_element_type=F32)
    for h in range(X_HEADS):
        sl = slice(h * X_HEAD_DIM, (h + 1) * X_HEAD_DIM)
        k_ref[:, sl] = _rms(kv[:, sl], gk_ref[...]).astype(BF16)
    v_ref[...] = kv[:, X_WIDTH:].astype(BF16)


def _memkv(mem, gmem, wkv, gk):
    B, M, D = mem.shape
    blk = lambda w: pl.BlockSpec((None, M, w), lambda b: (b, 0, 0))
    return pl.pallas_call(
        _memkv_kernel, grid=(B,),
        in_specs=[blk(D), _const_spec(gmem.shape), _const_spec(wkv.shape), _const_spec(gk.shape)],
        out_specs=(blk(X_WIDTH), blk(X_WIDTH)),
        out_shape=(jax.ShapeDtypeStruct((B, M, X_WIDTH), BF16),) * 2,
        compiler_params=_params(("parallel",)), name="memkv",
    )(mem, gmem, wkv, gk)


def _merge_kernel(x_ref, odsa_ref, os5_ref, k_ref, v_ref, gmix_ref, wg_ref, wqx_ref, gqx_ref,
                  wb1_ref, wb2_ref, wb3_ref, wout_ref, y_ref):
    x = x_ref[...]
    hb = _rms(x, gmix_ref[...]).astype(BF16)

    qx = jnp.dot(hb, wqx_ref[...], preferred_element_type=F32)
    gqx = gqx_ref[...] * (X_HEAD_DIM ** -0.5)
    ox = []
    for h in range(X_HEADS):
        sl = slice(h * X_HEAD_DIM, (h + 1) * X_HEAD_DIM)
        qh = _rms(qx[:, sl], gqx).astype(BF16)
        lg = lax.dot_general(qh, k_ref[:, sl], NT_DIMS, preferred_element_type=F32)
        p = jnp.exp(lg - jnp.max(lg, axis=-1, keepdims=True))
        pv = jnp.dot(p.astype(BF16), v_ref[:, sl], preferred_element_type=F32)
        ox.append((pv / jnp.sum(p, axis=-1, keepdims=True)).astype(BF16))
    ox = jnp.concatenate(ox, axis=1)

    merged = None
    for br, (o, wb) in enumerate(((odsa_ref[...], wb1_ref), (os5_ref[...], wb2_ref), (ox, wb3_ref))):
        gate = jax.nn.sigmoid(jnp.dot(hb, wg_ref[:, br * D_MODEL:(br + 1) * D_MODEL],
                                      preferred_element_type=F32))
        term = gate * jnp.dot(o, wb[...], preferred_element_type=F32)
        merged = term if merged is None else merged + term
    y_ref[...] = x + jnp.dot(merged.astype(BF16), wout_ref[...], preferred_element_type=F32)


def _merge(x, odsa, os5, k, v, gmix, wg, wqx, gqx, wb1, wb2, wb3, wout):
    B, S, D = x.shape
    ts = TS_MERGE
    tok = lambda w: pl.BlockSpec((None, ts, w), lambda b, s: (b, s, 0))
    memspec = pl.BlockSpec((None, N_MEM, X_WIDTH), lambda b, s: (b, 0, 0))
    in_specs = [tok(D), tok(DSA_WIDTH),
                tok(S5_WIDTH),
                memspec, memspec] + [_const_spec(a.shape) for a in
                                     (gmix, wg, wqx, gqx, wb1, wb2, wb3, wout)]
    return pl.pallas_call(
        _merge_kernel, grid=(B, S // ts), in_specs=in_specs, out_specs=tok(D),
        out_shape=jax.ShapeDtypeStruct((B, S, D), F32),
        compiler_params=_params(("parallel", "parallel")), name="merge",
    )(x, odsa, os5, k, v, gmix, wg, wqx, gqx, wb1, wb2, wb3, wout)


def _ffn_kernel(x_ref, g_ref, wg_ref, wu_ref, wd_ref, y_ref):
    x = x_ref[...]
    hb = _rms(x, g_ref[...]).astype(BF16)
    a = jnp.dot(hb, wg_ref[...], preferred_element_type=F32)
    b = jnp.dot(hb, wu_ref[...], preferred_element_type=F32)
    act = (jax.nn.silu(a) * b).astype(BF16)
    y_ref[...] = x + jnp.dot(act, wd_ref[...], preferred_element_type=F32)


def _ffn(x2, g, wg, wu, wd):
    n, D = x2.shape
    tok = pl.BlockSpec((TS_FFN, D), lambda t: (t, 0))
    return pl.pallas_call(
        _ffn_kernel, grid=(n // TS_FFN,),
        in_specs=[tok] + [_const_spec(a.shape) for a in (g, wg, wu, wd)],
        out_specs=tok, out_shape=jax.ShapeDtypeStruct((n, D), F32),
        compiler_params=_params(("parallel",)), name="ffn",
    )(x2, g, wg, wu, wd)


def _t5_bucket(n):
    max_exact = REL_BUCKETS // 2
    nf = jnp.maximum(n, 1).astype(F32)
    large = max_exact + (jnp.log(nf / max_exact) / math.log(REL_MAX_DIST / max_exact)
                         * (REL_BUCKETS - max_exact)).astype(jnp.int32)
    large = jnp.minimum(large, REL_BUCKETS - 1)
    return jnp.where(n < max_exact, n, large)


def _hankel(w, rows, cols):
    H, L = w.shape
    flat = jnp.tile(w, (1, rows + 1))[:, :rows * (L + 1)]
    return flat.reshape(H, rows, L + 1)[:, :, :cols]


def _near_bias(rel_bias):
    n = jnp.arange(2 * TQ, dtype=jnp.int32)
    f = (rel_bias[_t5_bucket(n)] - rel_bias[REL_BUCKETS - 1][None, :]).T * LOG2E
    w_diag = jnp.concatenate([jnp.broadcast_to(f[:, :1], (f.shape[0], TK - 1)), f[:, :TQ]], axis=1)
    w_prev = f[:, 1:TQ + TK]
    diag = _hankel(w_diag, TK, TQ)[:, ::-1, :]
    prev = _hankel(w_prev, TK, TQ)[:, ::-1, :]
    return jnp.stack([diag, prev], axis=0).astype(F32)


def _s5_mats(a_re, a_im, log_dt, b_re, b_im, c_re, c_im):
    lam = lax.complex(a_re.astype(F32), a_im.astype(F32))
    dt = jnp.exp(log_dt.astype(F32))[:, None]
    lam_bar = jnp.exp(lam * dt)
    b_bar = ((lam_bar - 1.0) / lam)[..., None] * lax.complex(b_re.astype(F32), b_im.astype(F32))
    nstate = S5_GROUPS * S5_STATE
    in_mask = (jnp.arange(S5_WIDTH)[:, None] // S5_GROUP) == (jnp.arange(nstate)[None, :] // S5_STATE)

    def blockdiag_in(w):
        t = jnp.transpose(w, (0, 2, 1)).reshape(S5_WIDTH, S5_STATE)
        return jnp.where(in_mask, jnp.tile(t, (1, S5_GROUPS)), 0.0)

    def blockdiag_out(w):
        t = jnp.transpose(w, (0, 2, 1)).reshape(nstate, S5_GROUP)
        return jnp.where(in_mask.T, jnp.tile(t, (1, S5_GROUPS)), 0.0)

    bm = jnp.concatenate([blockdiag_in(jnp.real(b_bar)), blockdiag_in(jnp.imag(b_bar))], axis=1)
    cm = jnp.concatenate([blockdiag_out(c_re.astype(F32)), blockdiag_out(-c_im.astype(F32))], axis=0)
    return bm.astype(BF16), cm.astype(BF16), jnp.real(lam_bar).reshape(1, -1), jnp.imag(lam_bar).reshape(1, -1)


def kernel(x, mem, rel_bias, w_in, g_mix_norm, g_q_dsa, g_kv_dsa, w_uv_dsa, a_re, a_im, log_dt, b_re, b_im, c_re, c_im, d_skip, w_glu, g_mem_norm, w_mem_kv, g_q_cross, g_k_cross, w_br_dsa, w_br_s5, w_br_cross, w_out, g_ffn_norm, w_ffn_gate, w_ffn_up, w_ffn_down):
    B, S, D = x.shape
    depth = w_in.shape[0]
    offs = [0] + [int(o) for o in np.cumsum(IN_SPLITS)]
    nb = _near_bias(rel_bias)
    row = lambda v: v.reshape(1, -1).astype(F32)
    col = lambda v: v.reshape(-1, 1).astype(F32)
    bf = lambda a: a.astype(BF16)
    for l in range(depth):
        w = bf(w_in[l])
        wq, wc, wqi, wk, ww, wu, wqx, wg = [w[:, offs[k]:offs[k + 1]] for k in range(len(IN_SPLITS))]
        wqiT = jnp.pad(wqi.T.reshape(IDX_HEADS, IDX_DIM, D), ((0, 0), (0, LANES - IDX_DIM), (0, 0)))
        wqiT = wqiT.reshape(IDX_HEADS * LANES, D)
        wwT = jnp.pad(ww.T, ((0, BF16_ROWS - IDX_HEADS), (0, 0)))
        wkp = jnp.pad(wk, ((0, 0), (0, LANES - IDX_DIM)))
        gqc = col(g_q_dsa[l]) * (DSA_LATENT ** -0.5 * LOG2E)

        qT, cT, c, qiT, wT, kidx, u = _inproj(
            x, row(g_mix_norm[l]), wq.T, wc.T, wc, wqiT, wwT, wkp, wu,
            gqc, col(g_kv_dsa[l]), row(g_kv_dsa[l]))

        wuvT = bf(jnp.transpose(w_uv_dsa[l], (0, 2, 1)))
        o_dsa = _dsa(qT, qiT, wT, kidx, c, cT, nb, wuvT)

        bm, cm, lre, lim = _s5_mats(a_re[l], a_im[l], log_dt[l], b_re[l], b_im[l], c_re[l], c_im[l])
        lre = jnp.broadcast_to(lre, (B, lre.shape[1]))
        lim = jnp.broadcast_to(lim, (B, lim.shape[1]))
        o_s5 = _s5(u, bm, cm, lre, lim, row(d_skip[l]), bf(w_glu[l]))

        k, v = _memkv(mem, row(g_mem_norm[l]), bf(w_mem_kv[l]), row(g_k_cross[l]))

        x1 = _merge(x, o_dsa, o_s5, k, v, row(g_mix_norm[l]), wg, wqx, row(g_q_cross[l]),
                    bf(w_br_dsa[l]), bf(w_br_s5[l]), bf(w_br_cross[l]), bf(w_out[l]))

        x = _ffn(x1.reshape(B * S, D), row(g_ffn_norm[l]), bf(w_ffn_gate[l]), bf(w_ffn_up[l]),
                 bf(w_ffn_down[l])).reshape(B, S, D)
    return x
```

```python
import math

import jax
import jax.numpy as jnp
import numpy as np
from jax import lax
from jax.experimental import pallas as pl
from jax.experimental.pallas import tpu as pltpu

F32 = jnp.float32
BF16 = jnp.bfloat16

D_MODEL = 1024
N_MEM = 256
EPS = 1e-6
DSA_HEADS = 8
DSA_LATENT = 128
DSA_VDIM = 64
IDX_HEADS = 8
IDX_DIM = 64
TOPK_MAX = 256
REL_BUCKETS = 32
REL_MAX_DIST = 128
S5_WIDTH = 512
S5_GROUP = 16
S5_GROUPS = S5_WIDTH // S5_GROUP
S5_STATE = 64
X_HEADS = 4
X_HEAD_DIM = 128
DSA_WIDTH = DSA_HEADS * DSA_VDIM
X_WIDTH = X_HEADS * X_HEAD_DIM
N_BRANCH = 3
IN_SPLITS = (DSA_HEADS * DSA_LATENT, DSA_LATENT, IDX_HEADS * IDX_DIM, IDX_DIM,
             IDX_HEADS, S5_WIDTH, X_WIDTH, N_BRANCH * D_MODEL)

LANES = 128
SUBLANES = 8
BF16_ROWS = 16
CT_ROWS = DSA_LATENT + BF16_ROWS
VMEM_LIMIT = 56 * 1024 * 1024
LOG2E = math.log2(math.e)

TS_IN = 512
TQ = 256
TK = 256
N_BISECT = 13
T_S5 = 64
S5_PITCH = T_S5 + SUBLANES
TS_MERGE = 512
TS_FFN = 512
NEG = -1e30

NT_DIMS = (((1,), (1,)), ((), ()))


def _rms(x, g):
    ms = jnp.mean(x * x, axis=-1, keepdims=True)
    return x * lax.rsqrt(ms + EPS) * g


def _tree(fn, x):
    while x.shape[0] > 1:
        half = x.shape[0] // 2
        x = fn(x[:half], x[half:])
    return x[0]


def _const_spec(shape):
    nd = len(shape)
    return pl.BlockSpec(shape, lambda *_: (0,) * nd, pipeline_mode=pl.Buffered(1))


def _params(sem):
    return pltpu.CompilerParams(dimension_semantics=sem, vmem_limit_bytes=VMEM_LIMIT)


def _inproj_kernel(x_ref, gmix_ref, wqT_ref, wcT_ref, wc_ref, wqiT_ref, wwT_ref, wk_ref, wu_ref,
                   gqc_ref, gkvc_ref, gkv_ref,
                   qT_ref, cT_ref, c_ref, qiT_ref, wT_ref, kidx_ref, u_ref):
    ts = x_ref.shape[0]
    hb = _rms(x_ref[...], gmix_ref[...]).astype(BF16)

    qT = lax.dot_general(wqT_ref[...], hb, NT_DIMS, preferred_element_type=F32)
    q3 = qT.reshape(DSA_HEADS, DSA_LATENT, ts)
    ms = jnp.mean(q3 * q3, axis=1, keepdims=True)
    qT_ref[...] = (q3 * lax.rsqrt(ms + EPS) * gqc_ref[...][None]).reshape(qT.shape).astype(BF16)

    cT = lax.dot_general(wcT_ref[...], hb, NT_DIMS, preferred_element_type=F32)
    msT = jnp.mean(cT * cT, axis=0, keepdims=True)
    cTn = (cT * lax.rsqrt(msT + EPS) * gkvc_ref[...]).astype(BF16)
    for k in range(ts // TK):
        cT_ref[k, :DSA_LATENT, :] = cTn[:, k * TK:(k + 1) * TK]
        cT_ref[k, DSA_LATENT:, :] = jnp.ones((BF16_ROWS, TK), BF16)
    c = jnp.dot(hb, wc_ref[...], preferred_element_type=F32)
    c_ref[...] = _rms(c, gkv_ref[...]).astype(BF16)

    qiT_ref[...] = lax.dot_general(wqiT_ref[...], hb, NT_DIMS, preferred_element_type=F32).astype(BF16)
    wT_ref[...] = lax.dot_general(wwT_ref[...], hb, NT_DIMS, preferred_element_type=F32)
    kidx_ref[...] = jnp.dot(hb, wk_ref[...], preferred_element_type=F32).astype(BF16)

    u_ref[...] = jnp.dot(hb, wu_ref[...], preferred_element_type=F32)


def _inproj(x, gmix, wqT, wcT, wc, wqiT, wwT, wk, wu, gqc, gkvc, gkv):
    B, S, D = x.shape
    ts = TS_IN
    grid = (B, S // ts)
    tok = lambda w: pl.BlockSpec((None, ts, w), lambda b, s: (b, s, 0))
    tokT = lambda r: pl.BlockSpec((None, r, ts), lambda b, s: (b, 0, s))
    hq = DSA_HEADS * DSA_LATENT
    hi = IDX_HEADS * LANES
    out_shape = (
        jax.ShapeDtypeStruct((B, hq, S), BF16),
        jax.ShapeDtypeStruct((B, S // TK, CT_ROWS, TK), BF16),
        jax.ShapeDtypeStruct((B, S, DSA_LATENT), BF16),
        jax.ShapeDtypeStruct((B, hi, S), BF16),
        jax.ShapeDtypeStruct((B, BF16_ROWS, S), F32),
        jax.ShapeDtypeStruct((B, S, LANES), BF16),
        jax.ShapeDtypeStruct((B, S, S5_WIDTH), F32),
    )
    out_specs = (tokT(hq),
                 pl.BlockSpec((None, ts // TK, CT_ROWS, TK), lambda b, s: (b, s, 0, 0)),
                 tok(DSA_LATENT), tokT(hi), tokT(BF16_ROWS), tok(LANES),
                 tok(S5_WIDTH))
    consts = (gmix, wqT, wcT, wc, wqiT, wwT, wk, wu, gqc, gkvc, gkv)
    in_specs = [tok(D)] + [_const_spec(a.shape) for a in consts]
    return pl.pallas_call(
        _inproj_kernel, grid=grid, in_specs=in_specs, out_specs=out_specs, out_shape=out_shape,
        compiler_params=_params(("parallel", "parallel")), name="inproj",
    )(x, *consts)


def _dsa_kernel(qT_ref, qiT_ref, wT_ref, kidx_ref, c_ref, cT_ref, nb_ref, wuvT_ref, o_ref,
                sc_ref, lg_ref, m_ref, acc_ref):
    i = pl.program_id(1)
    nk = sc_ref.shape[0]
    kf = float(TOPK_MAX)
    G = TK // SUBLANES

    def rep(fn, a):
        return jnp.broadcast_to(fn(a, axis=0, keepdims=True), (SUBLANES, TQ))

    def full(v, dt=F32):
        return jnp.full((SUBLANES, TQ), v, dt)

    def key_rows(j):
        return pl.ds(pl.multiple_of(j * TK, TK), TK)

    def score(j):
        ks = kidx_ref[key_rows(j), :]
        acc = None
        for h in range(IDX_HEADS):
            d = jnp.dot(ks, qiT_ref[h * LANES:(h + 1) * LANES, :], preferred_element_type=F32)
            t = jnp.maximum(d, 0.0) * wT_ref[h:h + 1, :]
            acc = t if acc is None else acc + t
        return acc

    def score_body(j, carry):
        sc_ref[j] = score(j)
        return carry

    lax.fori_loop(0, i, score_body, 0)
    key_t = lax.broadcasted_iota(jnp.int32, (TK, TQ), 0)
    qry_t = lax.broadcasted_iota(jnp.int32, (TK, TQ), 1)
    causal = key_t <= qry_t
    sc_ref[i] = jnp.where(causal, score(i), -jnp.inf)

    @pl.when(i == 0)
    def _():
        sc_ref[0] = jnp.where(causal, 0.0, NEG)

    @pl.when(i > 0)
    def _():
        nt = i + 1

        def tile3(j):
            return sc_ref[j].reshape(G, SUBLANES, TQ)

        def minmax_body(j, carry):
            mn, mx = carry
            s = tile3(j)
            mx = jnp.maximum(mx, _tree(jnp.maximum, s))
            mn = jnp.minimum(mn, _tree(jnp.minimum, jnp.where(s == -jnp.inf, jnp.inf, s)))
            return mn, mx

        mn, mx = lax.fori_loop(0, nt, minmax_body, (full(jnp.inf), full(-jnp.inf)))
        lo, hi = rep(jnp.min, mn), rep(jnp.max, mx)

        def count_ge(thr):
            def body(j, acc):
                return acc + _tree(jnp.add, jnp.where(tile3(j) >= thr[None], 1.0, 0.0))
            return rep(jnp.sum, lax.fori_loop(0, nt, body, full(0.0)))

        def bisect(_, carry):
            lo, hi = carry
            mid = 0.5 * (lo + hi)
            ge = count_ge(mid) >= kf
            return jnp.where(ge, mid, lo), jnp.where(ge, hi, mid)

        lo, hi = lax.fori_loop(0, N_BISECT, bisect, (lo, hi))

        def snap_body(j, am):
            s = tile3(j)
            return jnp.minimum(am, _tree(jnp.minimum, jnp.where(s >= lo[None], s, jnp.inf)))

        cur = rep(jnp.min, lax.fori_loop(0, nt, snap_body, full(jnp.inf)))

        def walk(cur):
            def body(j, carry):
                ac, am = carry
                s = tile3(j)
                g = s > cur[None]
                ac = ac + _tree(jnp.add, jnp.where(g, 1.0, 0.0))
                am = jnp.minimum(am, _tree(jnp.minimum, jnp.where(g, s, jnp.inf)))
                return ac, am
            ac, am = lax.fori_loop(0, nt, body, (full(0.0), full(jnp.inf)))
            return rep(jnp.sum, ac), rep(jnp.min, am)

        def walk_cond(carry):
            _, _, go, it = carry
            return jnp.logical_and(go > 0, it < nk * TK + 2)

        def walk_body(carry):
            cur, _, _, it = carry
            c, nxt = walk(cur)
            move = c >= kf
            go = (jnp.max(jnp.where(move, 1.0, 0.0)) > 0.5).astype(jnp.int32)
            return jnp.where(move, nxt, cur), c, go, it + 1

        kth, cgt, _, _ = lax.while_loop(
            walk_cond, walk_body, (cur, full(0.0), jnp.int32(1), jnp.int32(0)))
        need = kf - cgt

        tril = jnp.where(key_t >= qry_t, 1.0, 0.0).astype(BF16)

        def mask_body(j, offset):
            s = tile3(j)
            eq = s == kth[None]
            e01 = jnp.where(eq, 1.0, 0.0).reshape(TK, TQ).astype(BF16)
            pref = jnp.dot(tril, e01, preferred_element_type=F32)
            rank = pref.reshape(G, SUBLANES, TQ) + offset[None]
            tie = jnp.where(rank <= need[None], 0.0, NEG)
            mb = jnp.where(s > kth[None], 0.0, jnp.where(eq, tie, NEG))
            sc_ref[j] = mb.reshape(TK, TQ)
            return offset + jnp.broadcast_to(pref[TK - 1:TK, :], (SUBLANES, TQ))

        lax.fori_loop(0, nt, mask_body, full(0.0))

    m_ref[...] = jnp.full(m_ref.shape, NEG, F32)
    acc_ref[...] = jnp.zeros(acc_ref.shape, F32)
    LG = CT_ROWS // SUBLANES

    def attend(j, near):
        ct = c_ref[key_rows(j), :]
        ctT = cT_ref[j]
        mb = sc_ref[j]

        stats = []
        for h in range(DSA_HEADS):
            lg = jnp.dot(ct, qT_ref[h * DSA_LATENT:(h + 1) * DSA_LATENT, :],
                         preferred_element_type=F32) + mb
            if near is not None:
                lg = lg + nb_ref[near, h]
            lg_ref[h] = lg
            m_old = m_ref[h]
            m_new = jnp.maximum(m_old, rep(jnp.max, _tree(jnp.maximum, lg.reshape(G, SUBLANES, TQ))))
            stats.append((m_old, m_new))
        for h in range(DSA_HEADS):
            m_old, m_new = stats[h]
            alpha = jnp.exp2(m_old - m_new)
            x3 = lg_ref[h].reshape(G, SUBLANES, TQ) - m_new[None]
            p = jnp.exp2(x3.reshape(TK, TQ).astype(BF16))
            pv = jnp.dot(ctT, p, preferred_element_type=F32)
            acc3 = acc_ref[h].reshape(LG, SUBLANES, TQ) * alpha[None]
            acc_ref[h] = acc3.reshape(CT_ROWS, TQ) + pv
            m_ref[h] = m_new

    def far_body(j, carry):
        attend(j, None)
        return carry

    lax.fori_loop(0, jnp.maximum(i - 1, 0), far_body, 0)

    @pl.when(i > 0)
    def _():
        attend(i - 1, 1)

    attend(i, 0)

    outs = []
    for h in range(DSA_HEADS):
        rl = 1.0 / acc_ref[h, DSA_LATENT:DSA_LATENT + SUBLANES, :]
        o3 = acc_ref[h, :DSA_LATENT, :].reshape(DSA_LATENT // SUBLANES, SUBLANES, TQ) * rl[None]
        o = o3.reshape(DSA_LATENT, TQ).astype(BF16)
        outs.append(jnp.dot(wuvT_ref[h], o, preferred_element_type=F32))
    o_ref[...] = jnp.concatenate(outs, axis=0).T.astype(BF16)


def _dsa(qT, qiT, wT, kidx, c, cT, nb, wuvT):
    B, S, _ = c.shape
    nk = S // TK
    grid = (B, S // TQ)
    tileT = lambda r: pl.BlockSpec((None, r, TQ), lambda b, i: (b, 0, i))
    seq = lambda w: pl.BlockSpec((None, S, w), lambda b, i: (b, 0, 0))
    in_specs = [tileT(DSA_HEADS * DSA_LATENT), tileT(IDX_HEADS * LANES), tileT(BF16_ROWS),
                seq(LANES), seq(DSA_LATENT),
                pl.BlockSpec((None, nk, CT_ROWS, TK), lambda b, i: (b, 0, 0, 0)),
                _const_spec(nb.shape), _const_spec(wuvT.shape)]
    scratch = [
        pltpu.VMEM((nk, TK, TQ), F32),
        pltpu.VMEM((DSA_HEADS, TK, TQ), F32),
        pltpu.VMEM((DSA_HEADS, SUBLANES, TQ), F32),
        pltpu.VMEM((DSA_HEADS, CT_ROWS, TQ), F32),
    ]
    return pl.pallas_call(
        _dsa_kernel, grid=grid, in_specs=in_specs,
        out_specs=pl.BlockSpec((None, TQ, DSA_WIDTH), lambda b, i: (b, i, 0)),
        out_shape=jax.ShapeDtypeStruct((B, S, DSA_WIDTH), BF16), scratch_shapes=scratch,
        compiler_params=_params(("parallel", "arbitrary")), name="dsa",
    )(qT, qiT, wT, kidx, c, cT, nb, wuvT)


def _s5_kernel(u_ref, bm_ref, cm_ref, lre_ref, lim_ref, dsk_ref, wglu_ref, o_ref,
               uslab_ref, uil_ref, st_ref, oslab_ref, hre_ref, him_ref):
    nstate = S5_GROUPS * S5_STATE
    nb = hre_ref.shape[0]
    nslab = S5_WIDTH // LANES

    @pl.when(pl.program_id(0) == 0)
    def _():
        hre_ref[...] = jnp.zeros(hre_ref.shape, F32)
        him_ref[...] = jnp.zeros(him_ref.shape, F32)

    for b in range(nb):
        for k in range(nslab):
            uslab_ref[k, b * S5_PITCH:b * S5_PITCH + T_S5, :] = u_ref[b, :, k * LANES:(k + 1) * LANES]

    def gather_step(t, carry):
        for k in range(nslab):
            uil_ref[pl.ds(pl.multiple_of(t * nb, nb), nb), k * LANES:(k + 1) * LANES] = (
                uslab_ref[k, pl.ds(t, nb, stride=S5_PITCH), :])
        return carry

    lax.fori_loop(0, T_S5, gather_step, 0, unroll=4)
    u = uil_ref[...]
    ub = u.astype(BF16)

    kin = 4
    cin = S5_WIDTH // kin
    sin = nstate // kin
    for k in range(kin):
        uk = ub[:, k * cin:(k + 1) * cin]
        for off in (0, nstate):
            cols = slice(off + k * sin, off + (k + 1) * sin)
            st_ref[:, cols] = jnp.dot(uk, bm_ref[k * cin:(k + 1) * cin, cols],
                                      preferred_element_type=F32)

    half = nstate // 2
    for part in range(2):
        re_sl = slice(part * half, (part + 1) * half)
        im_sl = slice(nstate + part * half, nstate + (part + 1) * half)
        lre = lre_ref[:, re_sl]
        lim = lim_ref[:, re_sl]

        def step(t, carry):
            hr, hi = carry
            r = pl.ds(pl.multiple_of(t * nb, nb), nb)
            nr = lre * hr - lim * hi + st_ref[r, re_sl]
            ni = lre * hi + lim * hr + st_ref[r, im_sl]
            st_ref[r, re_sl] = nr
            st_ref[r, im_sl] = ni
            return nr, ni

        hr, hi = lax.fori_loop(0, T_S5, step, (hre_ref[:, re_sl], him_ref[:, re_sl]), unroll=4)
        hre_ref[:, re_sl] = hr
        him_ref[:, re_sl] = hi

    kout = 2
    cout = S5_WIDTH // kout
    sout = nstate // kout
    ys = []
    for k in range(kout):
        acc = None
        for off in (0, nstate):
            rows = slice(off + k * sout, off + (k + 1) * sout)
            part = jnp.dot(st_ref[:, rows].astype(BF16), cm_ref[rows, k * cout:(k + 1) * cout],
                           preferred_element_type=F32)
            acc = part if acc is None else acc + part
        ys.append(acc)
    y = jnp.concatenate(ys, axis=1)
    y = jax.nn.gelu(y + dsk_ref[...] * u)
    z = jnp.dot(y.astype(BF16), wglu_ref[...], preferred_element_type=F32)
    o = y * jax.nn.sigmoid(z)

    for k in range(nslab):
        oslab_ref[k] = o[:, k * LANES:(k + 1) * LANES]
    for b in range(nb):
        for k in range(nslab):
            o_ref[b, :, k * LANES:(k + 1) * LANES] = (
                oslab_ref[k, pl.ds(b, T_S5, stride=nb), :].astype(BF16))


def _s5(u, bm, cm, lre, lim, dsk, wglu):
    B, S, W = u.shape
    tb = T_S5 * B
    nstate = S5_GROUPS * S5_STATE
    tok = pl.BlockSpec((B, T_S5, W), lambda t: (0, t, 0))
    in_specs = [tok, _const_spec(bm.shape), _const_spec(cm.shape), _const_spec(lre.shape),
                _const_spec(lim.shape), _const_spec(dsk.shape), _const_spec(wglu.shape)]
    scratch = [pltpu.VMEM((W // LANES, B * S5_PITCH, LANES), F32),
               pltpu.VMEM((tb, W), F32),
               pltpu.VMEM((tb, 2 * nstate), F32),
               pltpu.VMEM((W // LANES, tb, LANES), F32),
               pltpu.VMEM((B, nstate), F32), pltpu.VMEM((B, nstate), F32)]
    return pl.pallas_call(
        _s5_kernel, grid=(S // T_S5,), in_specs=in_specs, out_specs=tok,
        out_shape=jax.ShapeDtypeStruct((B, S, W), BF16), scratch_shapes=scratch,
        compiler_params=_params(("arbitrary",)), name="s5",
    )(u, bm, cm, lre, lim, dsk, wglu)


def _memkv_kernel(mem_ref, gmem_ref, wkv_ref, gk_ref, k_ref, v_ref):
    mb = _rms(mem_ref[...], gmem_ref[...]).astype(BF16)
    kv = jnp.dot(mb, wkv_ref[...], preferred_element_type=F32)
    for h in range(X_HEADS):
        sl = slice(h * X_HEAD_DIM, (h + 1) * X_HEAD_DIM)
        k_ref[:, sl] = _rms(kv[:, sl], gk_ref[...]).astype(BF16)
    v_ref[...] = kv[:, X_WIDTH:].astype(BF16)


def _memkv(mem, gmem, wkv, gk):
    B, M, D = mem.shape
    blk = lambda w: pl.BlockSpec((None, M, w), lambda b: (b, 0, 0))
    return pl.pallas_call(
        _memkv_kernel, grid=(B,),
        in_specs=[blk(D), _const_spec(gmem.shape), _const_spec(wkv.shape), _const_spec(gk.shape)],
        out_specs=(blk(X_WIDTH), blk(X_WIDTH)),
        out_shape=(jax.ShapeDtypeStruct((B, M, X_WIDTH), BF16),) * 2,
        compiler_params=_params(("parallel",)), name="memkv",
    )(mem, gmem, wkv, gk)


def _merge_kernel(x_ref, odsa_ref, os5_ref, k_ref, v_ref, gmix_ref, wg_ref, wqx_ref, gqx_ref,
                  wb1_ref, wb2_ref, wb3_ref, wout_ref, y_ref):
    x = x_ref[...]
    hb = _rms(x, gmix_ref[...]).astype(BF16)

    qx = jnp.dot(hb, wqx_ref[...], preferred_element_type=F32)
    gqx = gqx_ref[...] * (X_HEAD_DIM ** -0.5)
    ox = []
    for h in range(X_HEADS):
        sl = slice(h * X_HEAD_DIM, (h + 1) * X_HEAD_DIM)
        qh = _rms(qx[:, sl], gqx).astype(BF16)
        lg = lax.dot_general(qh, k_ref[:, sl], NT_DIMS, preferred_element_type=F32)
        p = jnp.exp(lg - jnp.max(lg, axis=-1, keepdims=True))
        pv = jnp.dot(p.astype(BF16), v_ref[:, sl], preferred_element_type=F32)
        ox.append((pv / jnp.sum(p, axis=-1, keepdims=True)).astype(BF16))
    ox = jnp.concatenate(ox, axis=1)

    merged = None
    for br, (o, wb) in enumerate(((odsa_ref[...], wb1_ref), (os5_ref[...], wb2_ref), (ox, wb3_ref))):
        gate = jax.nn.sigmoid(jnp.dot(hb, wg_ref[:, br * D_MODEL:(br + 1) * D_MODEL],
                                      preferred_element_type=F32))
        term = gate * jnp.dot(o, wb[...], preferred_element_type=F32)
        merged = term if merged is None else merged + term
    y_ref[...] = x + jnp.dot(merged.astype(BF16), wout_ref[...], preferred_element_type=F32)


def _merge(x, odsa, os5, k, v, gmix, wg, wqx, gqx, wb1, wb2, wb3, wout):
    B, S, D = x.shape
    ts = TS_MERGE
    tok = lambda w: pl.BlockSpec((None, ts, w), lambda b, s: (b, s, 0))
    memspec = pl.BlockSpec((None, N_MEM, X_WIDTH), lambda b, s: (b, 0, 0))
    in_specs = [tok(D), tok(DSA_WIDTH),
                tok(S5_WIDTH),
                memspec, memspec] + [_const_spec(a.shape) for a in
                                     (gmix, wg, wqx, gqx, wb1, wb2, wb3, wout)]
    return pl.pallas_call(
        _merge_kernel, grid=(B, S // ts), in_specs=in_specs, out_specs=tok(D),
        out_shape=jax.ShapeDtypeStruct((B, S, D), F32),
        compiler_params=_params(("parallel", "parallel")), name="merge",
    )(x, odsa, os5, k, v, gmix, wg, wqx, gqx, wb1, wb2, wb3, wout)


def _ffn_kernel(x_ref, g_ref, wg_ref, wu_ref, wd_ref, y_ref):
    x = x_ref[...]
    hb = _rms(x, g_ref[...]).astype(BF16)
    a = jnp.dot(hb, wg_ref[...], preferred_element_type=F32)
    b = jnp.dot(hb, wu_ref[...], preferred_element_type=F32)
    act = (jax.nn.silu(a) * b).astype(BF16)
    y_ref[...] = x + jnp.dot(act, wd_ref[...], preferred_element_type=F32)


def _ffn(x2, g, wg, wu, wd):
    n, D = x2.shape
    tok = pl.BlockSpec((TS_FFN, D), lambda t: (t, 0))
    return pl.pallas_call(
        _ffn_kernel, grid=(n // TS_FFN,),
        in_specs=[tok] + [_const_spec(a.shape) for a in (g, wg, wu, wd)],
        out_specs=tok, out_shape=jax.ShapeDtypeStruct((n, D), F32),
        compiler_params=_params(("parallel",)), name="ffn",
    )(x2, g, wg, wu, wd)


def _t5_bucket(n):
    max_exact = REL_BUCKETS // 2
    nf = jnp.maximum(n, 1).astype(F32)
    large = max_exact + (jnp.log(nf / max_exact) / math.log(REL_MAX_DIST / max_exact)
                         * (REL_BUCKETS - max_exact)).astype(jnp.int32)
    large = jnp.minimum(large, REL_BUCKETS - 1)
    return jnp.where(n < max_exact, n, large)


def _hankel(w, rows, cols):
    H, L = w.shape
    flat = jnp.tile(w, (1, rows + 1))[:, :rows * (L + 1)]
    return flat.reshape(H, rows, L + 1)[:, :, :cols]


def _near_bias(rel_bias):
    n = jnp.arange(2 * TQ, dtype=jnp.int32)
    f = (rel_bias[_t5_bucket(n)] - rel_bias[REL_BUCKETS - 1][None, :]).T * LOG2E
    w_diag = jnp.concatenate([jnp.broadcast_to(f[:, :1], (f.shape[0], TK - 1)), f[:, :TQ]], axis=1)
    w_prev = f[:, 1:TQ + TK]
    diag = _hankel(w_diag, TK, TQ)[:, ::-1, :]
    prev = _hankel(w_prev, TK, TQ)[:, ::-1, :]
    return jnp.stack([diag, prev], axis=0).astype(F32)


def _s5_mats(a_re, a_im, log_dt, b_re, b_im, c_re, c_im):
    lam = lax.complex(a_re.astype(F32), a_im.astype(F32))
    dt = jnp.exp(log_dt.astype(F32))[:, None]
    lam_bar = jnp.exp(lam * dt)
    b_bar = ((lam_bar - 1.0) / lam)[..., None] * lax.complex(b_re.astype(F32), b_im.astype(F32))
    nstate = S5_GROUPS * S5_STATE
    in_mask = (jnp.arange(S5_WIDTH)[:, None] // S5_GROUP) == (jnp.arange(nstate)[None, :] // S5_STATE)

    def blockdiag_in(w):
        t = jnp.transpose(w, (0, 2, 1)).reshape(S5_WIDTH, S5_STATE)
        return jnp.where(in_mask, jnp.tile(t, (1, S5_GROUPS)), 0.0)

    def blockdiag_out(w):
        t = jnp.transpose(w, (0, 2, 1)).reshape(nstate, S5_GROUP)
        return jnp.where(in_mask.T, jnp.tile(t, (1, S5_GROUPS)), 0.0)

    bm = jnp.concatenate([blockdiag_in(jnp.real(b_bar)), blockdiag_in(jnp.imag(b_bar))], axis=1)
    cm = jnp.concatenate([blockdiag_out(c_re.astype(F32)), blockdiag_out(-c_im.astype(F32))], axis=0)
    return bm.astype(BF16), cm.astype(BF16), jnp.real(lam_bar).reshape(1, -1), jnp.imag(lam_bar).reshape(1, -1)


def kernel(x, mem, rel_bias, w_in, g_mix_norm, g_q_dsa, g_kv_dsa, w_uv_dsa, a_re, a_im, log_dt, b_re, b_im, c_re, c_im, d_skip, w_glu, g_mem_norm, w_mem_kv, g_q_cross, g_k_cross, w_br_dsa, w_br_s5, w_br_cross, w_out, g_ffn_norm, w_ffn_gate, w_ffn_up, w_ffn_down):
    B, S, D = x.shape
    depth = w_in.shape[0]
    offs = [0] + [int(o) for o in np.cumsum(IN_SPLITS)]
    nb = _near_bias(rel_bias)
    row = lambda v: v.reshape(1, -1).astype(F32)
    col = lambda v: v.reshape(-1, 1).astype(F32)
    bf = lambda a: a.astype(BF16)
    for l in range(depth):
        w = bf(w_in[l])
        wq, wc, wqi, wk, ww, wu, wqx, wg = [w[:, offs[k]:offs[k + 1]] for k in range(len(IN_SPLITS))]
        wqiT = jnp.pad(wqi.T.reshape(IDX_HEADS, IDX_DIM, D), ((0, 0), (0, LANES - IDX_DIM), (0, 0)))
        wqiT = wqiT.reshape(IDX_HEADS * LANES, D)
        wwT = jnp.pad(ww.T, ((0, BF16_ROWS - IDX_HEADS), (0, 0)))
        wkp = jnp.pad(wk, ((0, 0), (0, LANES - IDX_DIM)))
        gqc = col(g_q_dsa[l]) * (DSA_LATENT ** -0.5 * LOG2E)

        qT, cT, c, qiT, wT, kidx, u = _inproj(
            x, row(g_mix_norm[l]), wq.T, wc.T, wc, wqiT, wwT, wkp, wu,
            gqc, col(g_kv_dsa[l]), row(g_kv_dsa[l]))

        wuvT = bf(jnp.transpose(w_uv_dsa[l], (0, 2, 1)))
        o_dsa = _dsa(qT, qiT, wT, kidx, c, cT, nb, wuvT)

        bm, cm, lre, lim = _s5_mats(a_re[l], a_im[l], log_dt[l], b_re[l], b_im[l], c_re[l], c_im[l])
        lre = jnp.broadcast_to(lre, (B, lre.shape[1]))
        lim = jnp.broadcast_to(lim, (B, lim.shape[1]))
        o_s5 = _s5(u, bm, cm, lre, lim, row(d_skip[l]), bf(w_glu[l]))

        k, v = _memkv(mem, row(g_mem_norm[l]), bf(w_mem_kv[l]), row(g_k_cross[l]))

        x1 = _merge(x, o_dsa, o_s5, k, v, row(g_mix_norm[l]), wg, wqx, row(g_q_cross[l]),
                    bf(w_br_dsa[l]), bf(w_br_s5[l]), bf(w_br_cross[l]), bf(w_out[l]))

        x = _ffn(x1.reshape(B * S, D), row(g_ffn_norm[l]), bf(w_ffn_gate[l]), bf(w_ffn_up[l]),
                 bf(w_ffn_down[l])).reshape(B, S, D)
    return x
```

```python
import math

import jax
import jax.numpy as jnp
import numpy as np
from jax import lax
from jax.experimental import pallas as pl
from jax.experimental.pallas import tpu as pltpu

F32 = jnp.float32
BF16 = jnp.bfloat16

D_MODEL = 1024
N_MEM = 256
EPS = 1e-6
DSA_HEADS = 8
DSA_LATENT = 128
DSA_VDIM = 64
IDX_HEADS = 8
IDX_DIM = 64
TOPK_MAX = 256
REL_BUCKETS = 32
REL_MAX_DIST = 128
S5_WIDTH = 512
S5_GROUP = 16
S5_GROUPS = S5_WIDTH // S5_GROUP
S5_STATE = 64
X_HEADS = 4
X_HEAD_DIM = 128
DSA_WIDTH = DSA_HEADS * DSA_VDIM
X_WIDTH = X_HEADS * X_HEAD_DIM
N_BRANCH = 3
IN_SPLITS = (DSA_HEADS * DSA_LATENT, DSA_LATENT, IDX_HEADS * IDX_DIM, IDX_DIM,
             IDX_HEADS, S5_WIDTH, X_WIDTH, N_BRANCH * D_MODEL)

LANES = 128
SUBLANES = 8
BF16_ROWS = 16
CT_ROWS = DSA_LATENT + BF16_ROWS
VMEM_LIMIT = 56 * 1024 * 1024
LOG2E = math.log2(math.e)

TS_IN = 512
TQ = 256
TK = 256
N_BISECT = 15
T_S5 = 64
S5_PITCH = T_S5 + SUBLANES
TS_MERGE = 512
TS_FFN = 512
NEG = -1e30

NT_DIMS = (((1,), (1,)), ((), ()))


def _rms(x, g):
    ms = jnp.mean(x * x, axis=-1, keepdims=True)
    return x * lax.rsqrt(ms + EPS) * g


def _tree(fn, x):
    while x.shape[0] > 1:
        half = x.shape[0] // 2
        x = fn(x[:half], x[half:])
    return x[0]


def _const_spec(shape):
    nd = len(shape)
    return pl.BlockSpec(shape, lambda *_: (0,) * nd, pipeline_mode=pl.Buffered(1))


def _params(sem):
    return pltpu.CompilerParams(dimension_semantics=sem, vmem_limit_bytes=VMEM_LIMIT)


def _inproj_kernel(x_ref, gmix_ref, wqT_ref, wc_ref, wqiT_ref, wwT_ref, wk_ref, wu_ref,
                   gqc_ref, gkv_ref,
                   qT_ref, cT_ref, c_ref, qiT_ref, wT_ref, kidx_ref, u_ref):
    ts = x_ref.shape[0]
    hb = _rms(x_ref[...], gmix_ref[...]).astype(BF16)

    qT = lax.dot_general(wqT_ref[...], hb, NT_DIMS, preferred_element_type=F32)
    q3 = qT.reshape(DSA_HEADS, DSA_LATENT, ts)
    ms = jnp.mean(q3 * q3, axis=1, keepdims=True)
    qT_ref[...] = (q3 * lax.rsqrt(ms + EPS) * gqc_ref[...][None]).reshape(qT.shape).astype(BF16)

    cn = _rms(jnp.dot(hb, wc_ref[...], preferred_element_type=F32), gkv_ref[...])
    c_ref[...] = cn.astype(BF16)
    cTn = cn.T.astype(BF16)
    for k in range(ts // TK):
        cT_ref[k, :DSA_LATENT, :] = cTn[:, k * TK:(k + 1) * TK]
        cT_ref[k, DSA_LATENT:, :] = jnp.ones((BF16_ROWS, TK), BF16)

    qiT_ref[...] = lax.dot_general(wqiT_ref[...], hb, NT_DIMS, preferred_element_type=F32).astype(BF16)
    wT_ref[...] = lax.dot_general(wwT_ref[...], hb, NT_DIMS, preferred_element_type=F32)
    kidx_ref[...] = jnp.dot(hb, wk_ref[...], preferred_element_type=F32).astype(BF16)

    u_ref[...] = jnp.dot(hb, wu_ref[...], preferred_element_type=F32)


def _inproj(x, gmix, wqT, wc, wqiT, wwT, wk, wu, gqc, gkv):
    B, S, D = x.shape
    ts = TS_IN
    grid = (B, S // ts)
    tok = lambda w: pl.BlockSpec((None, ts, w), lambda b, s: (b, s, 0))
    tokT = lambda r: pl.BlockSpec((None, r, ts), lambda b, s: (b, 0, s))
    hq = DSA_HEADS * DSA_LATENT
    hi = IDX_HEADS * IDX_DIM
    out_shape = (
        jax.ShapeDtypeStruct((B, hq, S), BF16),
        jax.ShapeDtypeStruct((B, S // TK, CT_ROWS, TK), BF16),
        jax.ShapeDtypeStruct((B, S, DSA_LATENT), BF16),
        jax.ShapeDtypeStruct((B, hi, S), BF16),
        jax.ShapeDtypeStruct((B, BF16_ROWS, S), F32),
        jax.ShapeDtypeStruct((B, S, IDX_DIM), BF16),
        jax.ShapeDtypeStruct((B, S, S5_WIDTH), F32),
    )
    out_specs = (tokT(hq),
                 pl.BlockSpec((None, ts // TK, CT_ROWS, TK), lambda b, s: (b, s, 0, 0)),
                 tok(DSA_LATENT), tokT(hi), tokT(BF16_ROWS), tok(IDX_DIM),
                 tok(S5_WIDTH))
    consts = (gmix, wqT, wc, wqiT, wwT, wk, wu, gqc, gkv)
    in_specs = [tok(D)] + [_const_spec(a.shape) for a in consts]
    return pl.pallas_call(
        _inproj_kernel, grid=grid, in_specs=in_specs, out_specs=out_specs, out_shape=out_shape,
        compiler_params=_params(("parallel", "parallel")), name="inproj",
    )(x, *consts)


def _dsa_kernel(qT_ref, qiT_ref, wT_ref, kidx_ref, c_ref, cT_ref, nb_ref, wuvT_ref, o_ref,
                sc_ref, lg_ref, m_ref, acc_ref):
    i = pl.program_id(1)
    nk = sc_ref.shape[0]
    kf = float(TOPK_MAX)
    G = TK // SUBLANES

    def rep(fn, a):
        return jnp.broadcast_to(fn(a, axis=0, keepdims=True), (SUBLANES, TQ))

    def full(v, dt=F32):
        return jnp.full((SUBLANES, TQ), v, dt)

    def key_rows(j):
        return pl.ds(pl.multiple_of(j * TK, TK), TK)

    def score(j):
        ks = kidx_ref[key_rows(j), :]
        acc = None
        for h in range(IDX_HEADS):
            d = jnp.dot(ks, qiT_ref[h * IDX_DIM:(h + 1) * IDX_DIM, :], preferred_element_type=F32)
            t = jnp.maximum(d, 0.0) * wT_ref[h:h + 1, :]
            acc = t if acc is None else acc + t
        return acc

    def score_body(j, carry):
        sc_ref[j] = score(j)
        return carry

    lax.fori_loop(0, i, score_body, 0)
    key_t = lax.broadcasted_iota(jnp.int32, (TK, TQ), 0)
    qry_t = lax.broadcasted_iota(jnp.int32, (TK, TQ), 1)
    causal = key_t <= qry_t
    sc_ref[i] = jnp.where(causal, score(i), -jnp.inf)

    @pl.when(i == 0)
    def _():
        sc_ref[0] = jnp.where(causal, 0.0, NEG)

    @pl.when(i > 0)
    def _():
        nt = i + 1

        def tile3(j):
            return sc_ref[j].reshape(G, SUBLANES, TQ)

        def minmax_body(j, carry):
            mn, mx = carry
            s = tile3(j)
            mx = jnp.maximum(mx, _tree(jnp.maximum, s))
            mn = jnp.minimum(mn, _tree(jnp.minimum, jnp.where(s == -jnp.inf, jnp.inf, s)))
            return mn, mx

        mn, mx = lax.fori_loop(0, nt, minmax_body, (full(jnp.inf), full(-jnp.inf)))
        lo, hi = rep(jnp.min, mn), rep(jnp.max, mx)

        def count_ge(thr):
            def body(j, acc):
                return acc + _tree(jnp.add, jnp.where(tile3(j) >= thr[None], 1.0, 0.0))
            return rep(jnp.sum, lax.fori_loop(0, nt, body, full(0.0)))

        def bisect(_, carry):
            lo, hi = carry
            mid = 0.5 * (lo + hi)
            ge = count_ge(mid) >= kf
            return jnp.where(ge, mid, lo), jnp.where(ge, hi, mid)

        lo, hi = lax.fori_loop(0, N_BISECT, bisect, (lo, hi))

        def snap_body(j, am):
            s = tile3(j)
            return jnp.minimum(am, _tree(jnp.minimum, jnp.where(s >= lo[None], s, jnp.inf)))

        cur = rep(jnp.min, lax.fori_loop(0, nt, snap_body, full(jnp.inf)))

        def walk(cur):
            def body(j, carry):
                ac, am = carry
                s = tile3(j)
                g = s > cur[None]
                ac = ac + _tree(jnp.add, jnp.where(g, 1.0, 0.0))
                am = jnp.minimum(am, _tree(jnp.minimum, jnp.where(g, s, jnp.inf)))
                return ac, am
            ac, am = lax.fori_loop(0, nt, body, (full(0.0), full(jnp.inf)))
            return rep(jnp.sum, ac), rep(jnp.min, am)

        def walk_cond(carry):
            _, _, go, it = carry
            return jnp.logical_and(go > 0, it < nk * TK + 2)

        def walk_body(carry):
            cur, _, _, it = carry
            c, nxt = walk(cur)
            move = c >= kf
            go = (jnp.max(jnp.where(move, 1.0, 0.0)) > 0.5).astype(jnp.int32)
            return jnp.where(move, nxt, cur), c, go, it + 1

        kth, cgt, _, _ = lax.while_loop(
            walk_cond, walk_body, (cur, full(0.0), jnp.int32(1), jnp.int32(0)))
        need = kf - cgt

        tril = jnp.where(key_t >= qry_t, 1.0, 0.0).astype(BF16)
        pref_ref = lg_ref

        def pref_body(p, carry):
            for j in (2 * p, jnp.minimum(2 * p + 1, nt - 1)):
                e01 = jnp.where(tile3(j) == kth[None], 1.0, 0.0).reshape(TK, TQ).astype(BF16)
                pref_ref[j] = jnp.dot(tril, e01, preferred_element_type=F32)
            return carry

        lax.fori_loop(0, lax.shift_right_logical(nt + 1, 1), pref_body, 0)

        def mask_body(j, offset):
            s = tile3(j)
            pref = pref_ref[j]
            rank = pref.reshape(G, SUBLANES, TQ) + offset[None]
            tie = jnp.where(rank <= need[None], 0.0, NEG)
            mb = jnp.where(s > kth[None], 0.0, jnp.where(s == kth[None], tie, NEG))
            sc_ref[j] = mb.reshape(TK, TQ)
            return offset + jnp.broadcast_to(pref[TK - 1:TK, :], (SUBLANES, TQ))

        lax.fori_loop(0, nt, mask_body, full(0.0))

    m_ref[...] = jnp.full(m_ref.shape, NEG, F32)
    acc_ref[...] = jnp.zeros(acc_ref.shape, F32)
    LG = CT_ROWS // SUBLANES

    def attend(j, near):
        ct = c_ref[key_rows(j), :]
        ctT = cT_ref[j]
        mb = sc_ref[j]

        stats = []
        for h in range(DSA_HEADS):
            lg = jnp.dot(ct, qT_ref[h * DSA_LATENT:(h + 1) * DSA_LATENT, :],
                         preferred_element_type=F32) + mb
            if near is not None:
                lg = lg + nb_ref[near, h]
            lg_ref[h] = lg
            m_old = m_ref[h]
            m_new = jnp.maximum(m_old, rep(jnp.max, _tree(jnp.maximum, lg.reshape(G, SUBLANES, TQ))))
            stats.append((m_old, m_new))
        for h in range(DSA_HEADS):
            m_old, m_new = stats[h]
            alpha = jnp.exp2(m_old - m_new)
            x3 = lg_ref[h].reshape(G, SUBLANES, TQ) - m_new[None]
            p = jnp.exp2(x3.reshape(TK, TQ).astype(BF16))
            pv = jnp.dot(ctT, p, preferred_element_type=F32)
            acc3 = acc_ref[h].reshape(LG, SUBLANES, TQ) * alpha[None]
            acc_ref[h] = acc3.reshape(CT_ROWS, TQ) + pv
            m_ref[h] = m_new

    def far_body(j, carry):
        attend(j, None)
        return carry

    lax.fori_loop(0, jnp.maximum(i - 1, 0), far_body, 0)

    @pl.when(i > 0)
    def _():
        attend(i - 1, 1)

    attend(i, 0)

    outs = []
    for h in range(DSA_HEADS):
        rl = 1.0 / acc_ref[h, DSA_LATENT:DSA_LATENT + SUBLANES, :]
        o3 = acc_ref[h, :DSA_LATENT, :].reshape(DSA_LATENT // SUBLANES, SUBLANES, TQ) * rl[None]
        o = o3.reshape(DSA_LATENT, TQ).astype(BF16)
        outs.append(jnp.dot(wuvT_ref[h], o, preferred_element_type=F32))
    o_ref[...] = jnp.concatenate(outs, axis=0).T.astype(BF16)


def _dsa(qT, qiT, wT, kidx, c, cT, nb, wuvT):
    B, S, _ = c.shape
    nk = S // TK
    grid = (B, S // TQ)
    tileT = lambda r: pl.BlockSpec((None, r, TQ), lambda b, i: (b, 0, i))
    seq = lambda w: pl.BlockSpec((None, S, w), lambda b, i: (b, 0, 0))
    in_specs = [tileT(DSA_HEADS * DSA_LATENT), tileT(IDX_HEADS * IDX_DIM), tileT(BF16_ROWS),
                seq(IDX_DIM), seq(DSA_LATENT),
                pl.BlockSpec((None, nk, CT_ROWS, TK), lambda b, i: (b, 0, 0, 0)),
                _const_spec(nb.shape), _const_spec(wuvT.shape)]
    scratch = [
        pltpu.VMEM((nk, TK, TQ), F32),
        pltpu.VMEM((DSA_HEADS, TK, TQ), F32),
        pltpu.VMEM((DSA_HEADS, SUBLANES, TQ), F32),
        pltpu.VMEM((DSA_HEADS, CT_ROWS, TQ), F32),
    ]
    return pl.pallas_call(
        _dsa_kernel, grid=grid, in_specs=in_specs,
        out_specs=pl.BlockSpec((None, TQ, DSA_WIDTH), lambda b, i: (b, i, 0)),
        out_shape=jax.ShapeDtypeStruct((B, S, DSA_WIDTH), BF16), scratch_shapes=scratch,
        compiler_params=_params(("parallel", "arbitrary")), name="dsa",
    )(qT, qiT, wT, kidx, c, cT, nb, wuvT)


def _s5_kernel(u_ref, bm_ref, cm_ref, lre_ref, lim_ref, dsk_ref, wglu_ref, o_ref,
               uslab_ref, uil_ref, st_ref, oslab_ref, hre_ref, him_ref):
    nstate = S5_GROUPS * S5_STATE
    nb = hre_ref.shape[0]
    nslab = S5_WIDTH // LANES

    @pl.when(pl.program_id(0) == 0)
    def _():
        hre_ref[...] = jnp.zeros(hre_ref.shape, F32)
        him_ref[...] = jnp.zeros(him_ref.shape, F32)

    for b in range(nb):
        for k in range(nslab):
            uslab_ref[k, b * S5_PITCH:b * S5_PITCH + T_S5, :] = u_ref[b, :, k * LANES:(k + 1) * LANES]

    def gather_step(t, carry):
        for k in range(nslab):
            uil_ref[pl.ds(pl.multiple_of(t * nb, nb), nb), k * LANES:(k + 1) * LANES] = (
                uslab_ref[k, pl.ds(t, nb, stride=S5_PITCH), :])
        return carry

    lax.fori_loop(0, T_S5, gather_step, 0, unroll=4)
    u = uil_ref[...]
    ub = u.astype(BF16)

    kin = 4
    cin = S5_WIDTH // kin
    sin = nstate // kin
    for k in range(kin):
        uk = ub[:, k * cin:(k + 1) * cin]
        for off in (0, nstate):
            cols = slice(off + k * sin, off + (k + 1) * sin)
            st_ref[:, cols] = jnp.dot(uk, bm_ref[k * cin:(k + 1) * cin, cols],
                                      preferred_element_type=F32)

    half = nstate // 2
    for part in range(2):
        re_sl = slice(part * half, (part + 1) * half)
        im_sl = slice(nstate + part * half, nstate + (part + 1) * half)
        lre = lre_ref[:, re_sl]
        lim = lim_ref[:, re_sl]

        def step(t, carry):
            hr, hi = carry
            r = pl.ds(pl.multiple_of(t * nb, nb), nb)
            nr = lre * hr - lim * hi + st_ref[r, re_sl]
            ni = lre * hi + lim * hr + st_ref[r, im_sl]
            st_ref[r, re_sl] = nr
            st_ref[r, im_sl] = ni
            return nr, ni

        hr, hi = lax.fori_loop(0, T_S5, step, (hre_ref[:, re_sl], him_ref[:, re_sl]), unroll=4)
        hre_ref[:, re_sl] = hr
        him_ref[:, re_sl] = hi

    kout = 2
    cout = S5_WIDTH // kout
    sout = nstate // kout
    ys = []
    for k in range(kout):
        acc = None
        for off in (0, nstate):
            rows = slice(off + k * sout, off + (k + 1) * sout)
            part = jnp.dot(st_ref[:, rows].astype(BF16), cm_ref[rows, k * cout:(k + 1) * cout],
                           preferred_element_type=F32)
            acc = part if acc is None else acc + part
        ys.append(acc)
    y = jnp.concatenate(ys, axis=1)
    y = jax.nn.gelu(y + dsk_ref[...] * u)
    z = jnp.dot(y.astype(BF16), wglu_ref[...], preferred_element_type=F32)
    o = y * jax.nn.sigmoid(z)

    for k in range(nslab):
        oslab_ref[k] = o[:, k * LANES:(k + 1) * LANES]
    for b in range(nb):
        for k in range(nslab):
            o_ref[b, :, k * LANES:(k + 1) * LANES] = (
                oslab_ref[k, pl.ds(b, T_S5, stride=nb), :].astype(BF16))


def _s5(u, bm, cm, lre, lim, dsk, wglu):
    B, S, W = u.shape
    tb = T_S5 * B
    nstate = S5_GROUPS * S5_STATE
    tok = pl.BlockSpec((B, T_S5, W), lambda t: (0, t, 0))
    in_specs = [tok, _const_spec(bm.shape), _const_spec(cm.shape), _const_spec(lre.shape),
                _const_spec(lim.shape), _const_spec(dsk.shape), _const_spec(wglu.shape)]
    scratch = [pltpu.VMEM((W // LANES, B * S5_PITCH, LANES), F32),
               pltpu.VMEM((tb, W), F32),
               pltpu.VMEM((tb, 2 * nstate), F32),
               pltpu.VMEM((W // LANES, tb, LANES), F32),
               pltpu.VMEM((B, nstate), F32), pltpu.VMEM((B, nstate), F32)]
    return pl.pallas_call(
        _s5_kernel, grid=(S // T_S5,), in_specs=in_specs, out_specs=tok,
        out_shape=jax.ShapeDtypeStruct((B, S, W), BF16), scratch_shapes=scratch,
        compiler_params=_params(("arbitrary",)), name="s5",
    )(u, bm, cm, lre, lim, dsk, wglu)


def _memkv_kernel(mem_ref, gmem_ref, wkv_ref, gk_ref, k_ref, v_ref):
    mb = _rms(mem_ref[...], gmem_ref[...]).astype(BF16)
    kv = jnp.dot(mb, wkv_ref[...], preferred_element_type=F32)
    for h in range(X_HEADS):
        sl = slice(h * X_HEAD_DIM, (h + 1) * X_HEAD_DIM)
        k_ref[:, sl] = _rms(kv[:, sl], gk_ref[...]).astype(BF16)
    v_ref[...] = kv[:, X_WIDTH:].astype(BF16)


def _memkv(mem, gmem, wkv, gk):
    B, M, D = mem.shape
    blk = lambda w: pl.BlockSpec((None, M, w), lambda b: (b, 0, 0))
    return pl.pallas_call(
        _memkv_kernel, grid=(B,),
        in_specs=[blk(D), _const_spec(gmem.shape), _const_spec(wkv.shape), _const_spec(gk.shape)],
        out_specs=(blk(X_WIDTH), blk(X_WIDTH)),
        out_shape=(jax.ShapeDtypeStruct((B, M, X_WIDTH), BF16),) * 2,
        compiler_params=_params(("parallel",)), name="memkv",
    )(mem, gmem, wkv, gk)


def _merge_kernel(x_ref, odsa_ref, os5_ref, k_ref, v_ref, gmix_ref, wg_ref, wqx_ref, gqx_ref,
                  wb1_ref, wb2_ref, wb3_ref, wout_ref, y_ref):
    x = x_ref[...]
    hb = _rms(x, gmix_ref[...]).astype(BF16)

    qx = jnp.dot(hb, wqx_ref[...], preferred_element_type=F32)
    gqx = gqx_ref[...] * (X_HEAD_DIM ** -0.5)
    ox = []
    for h in range(X_HEADS):
        sl = slice(h * X_HEAD_DIM, (h + 1) * X_HEAD_DIM)
        qh = _rms(qx[:, sl], gqx).astype(BF16)
        lg = lax.dot_general(qh, k_ref[:, sl], NT_DIMS, preferred_element_type=F32)
        p = jnp.exp(lg - jnp.max(lg, axis=-1, keepdims=True))
        pv = jnp.dot(p.astype(BF16), v_ref[:, sl], preferred_element_type=F32)
        ox.append((pv / jnp.sum(p, axis=-1, keepdims=True)).astype(BF16))
    ox = jnp.concatenate(ox, axis=1)

    merged = None
    for br, (o, wb) in enumerate(((odsa_ref[...], wb1_ref), (os5_ref[...], wb2_ref), (ox, wb3_ref))):
        gate = jax.nn.sigmoid(jnp.dot(hb, wg_ref[:, br * D_MODEL:(br + 1) * D_MODEL],
                                      preferred_element_type=F32))
        term = gate * jnp.dot(o, wb[...], preferred_element_type=F32)
        merged = term if merged is None else merged + term
    y_ref[...] = x + jnp.dot(merged.astype(BF16), wout_ref[...], preferred_element_type=F32)


def _merge(x, odsa, os5, k, v, gmix, wg, wqx, gqx, wb1, wb2, wb3, wout):
    B, S, D = x.shape
    ts = TS_MERGE
    tok = lambda w: pl.BlockSpec((None, ts, w), lambda b, s: (b, s, 0))
    memspec = pl.BlockSpec((None, N_MEM, X_WIDTH), lambda b, s: (b, 0, 0))
    in_specs = [tok(D), tok(DSA_WIDTH),
                tok(S5_WIDTH),
                memspec, memspec] + [_const_spec(a.shape) for a in
                                     (gmix, wg, wqx, gqx, wb1, wb2, wb3, wout)]
    return pl.pallas_call(
        _merge_kernel, grid=(B, S // ts), in_specs=in_specs, out_specs=tok(D),
        out_shape=jax.ShapeDtypeStruct((B, S, D), F32),
        compiler_params=_params(("parallel", "parallel")), name="merge",
    )(x, odsa, os5, k, v, gmix, wg, wqx, gqx, wb1, wb2, wb3, wout)


def _ffn_kernel(x_ref, g_ref, wg_ref, wu_ref, wd_ref, y_ref):
    x = x_ref[...]
    hb = _rms(x, g_ref[...]).astype(BF16)
    a = jnp.dot(hb, wg_ref[...], preferred_element_type=F32)
    b = jnp.dot(hb, wu_ref[...], preferred_element_type=F32)
    act = (jax.nn.silu(a) * b).astype(BF16)
    y_ref[...] = x + jnp.dot(act, wd_ref[...], preferred_element_type=F32)


def _ffn(x2, g, wg, wu, wd):
    n, D = x2.shape
    tok = pl.BlockSpec((TS_FFN, D), lambda t: (t, 0))
    return pl.pallas_call(
        _ffn_kernel, grid=(n // TS_FFN,),
        in_specs=[tok] + [_const_spec(a.shape) for a in (g, wg, wu, wd)],
        out_specs=tok, out_shape=jax.ShapeDtypeStruct((n, D), F32),
        compiler_params=_params(("parallel",)), name="ffn",
    )(x2, g, wg, wu, wd)


def _t5_bucket(n):
    max_exact = REL_BUCKETS // 2
    nf = jnp.maximum(n, 1).astype(F32)
    large = max_exact + (jnp.log(nf / max_exact) / math.log(REL_MAX_DIST / max_exact)
                         * (REL_BUCKETS - max_exact)).astype(jnp.int32)
    large = jnp.minimum(large, REL_BUCKETS - 1)
    return jnp.where(n < max_exact, n, large)


def _toeplitz(w, rows, cols):
    H, L = w.shape
    flat = jnp.tile(w, (1, rows))[:, :rows * (L - 1)]
    return flat.reshape(H, rows, L - 1)[:, :, :cols]


def _near_bias(rel_bias):
    n = jnp.arange(2 * TQ, dtype=jnp.int32)
    f = (rel_bias[_t5_bucket(n)] - rel_bias[REL_BUCKETS - 1][None, :]).T * LOG2E
    w_diag = jnp.concatenate([f[:, :TQ], jnp.broadcast_to(f[:, :1], (f.shape[0], TK - 1))], axis=1)
    w_prev = jnp.concatenate([f[:, TQ:2 * TQ], f[:, 1:TK]], axis=1)
    return jnp.stack([_toeplitz(w_diag, TK, TQ), _toeplitz(w_prev, TK, TQ)], axis=0).astype(F32)


def _s5_mats(a_re, a_im, log_dt, b_re, b_im, c_re, c_im):
    lam = lax.complex(a_re.astype(F32), a_im.astype(F32))
    dt = jnp.exp(log_dt.astype(F32))[:, None]
    lam_bar = jnp.exp(lam * dt)
    b_bar = ((lam_bar - 1.0) / lam)[..., None] * lax.complex(b_re.astype(F32), b_im.astype(F32))
    nstate = S5_GROUPS * S5_STATE
    in_mask = (jnp.arange(S5_WIDTH)[:, None] // S5_GROUP) == (jnp.arange(nstate)[None, :] // S5_STATE)

    def blockdiag_in(w):
        t = jnp.transpose(w, (0, 2, 1)).reshape(S5_WIDTH, S5_STATE)
        return jnp.where(in_mask, jnp.tile(t, (1, S5_GROUPS)), 0.0)

    def blockdiag_out(w):
        t = jnp.transpose(w, (0, 2, 1)).reshape(nstate, S5_GROUP)
        return jnp.where(in_mask.T, jnp.tile(t, (1, S5_GROUPS)), 0.0)

    bm = jnp.concatenate([blockdiag_in(jnp.real(b_bar)), blockdiag_in(jnp.imag(b_bar))], axis=1)
    cm = jnp.concatenate([blockdiag_out(c_re.astype(F32)), blockdiag_out(-c_im.astype(F32))], axis=0)
    return bm.astype(BF16), cm.astype(BF16), jnp.real(lam_bar).reshape(1, -1), jnp.imag(lam_bar).reshape(1, -1)


def kernel(x, mem, rel_bias, w_in, g_mix_norm, g_q_dsa, g_kv_dsa, w_uv_dsa, a_re, a_im, log_dt, b_re, b_im, c_re, c_im, d_skip, w_glu, g_mem_norm, w_mem_kv, g_q_cross, g_k_cross, w_br_dsa, w_br_s5, w_br_cross, w_out, g_ffn_norm, w_ffn_gate, w_ffn_up, w_ffn_down):
    B, S, D = x.shape
    depth = w_in.shape[0]
    offs = [0] + [int(o) for o in np.cumsum(IN_SPLITS)]
    nb = _near_bias(rel_bias)
    row = lambda v: v.reshape(1, -1).astype(F32)
    col = lambda v: v.reshape(-1, 1).astype(F32)
    bf = lambda a: a.astype(BF16)
    for l in range(depth):
        w = bf(w_in[l])
        wq, wc, wqi, wk, ww, wu, wqx, wg = [w[:, offs[k]:offs[k + 1]] for k in range(len(IN_SPLITS))]
        wwT = jnp.pad(ww.T, ((0, BF16_ROWS - IDX_HEADS), (0, 0)))
        gqc = col(g_q_dsa[l]) * (DSA_LATENT ** -0.5 * LOG2E)

        qT, cT, c, qiT, wT, kidx, u = _inproj(
            x, row(g_mix_norm[l]), wq.T, wc, wqi.T, wwT, wk, wu, gqc, row(g_kv_dsa[l]))

        wuvT = bf(jnp.transpose(w_uv_dsa[l], (0, 2, 1)))
        o_dsa = _dsa(qT, qiT, wT, kidx, c, cT, nb, wuvT)

        bm, cm, lre, lim = _s5_mats(a_re[l], a_im[l], log_dt[l], b_re[l], b_im[l], c_re[l], c_im[l])
        lre = jnp.broadcast_to(lre, (B, lre.shape[1]))
        lim = jnp.broadcast_to(lim, (B, lim.shape[1]))
        o_s5 = _s5(u, bm, cm, lre, lim, row(d_skip[l]), bf(w_glu[l]))

        k, v = _memkv(mem, row(g_mem_norm[l]), bf(w_mem_kv[l]), row(g_k_cross[l]))

        x1 = _merge(x, o_dsa, o_s5, k, v, row(g_mix_norm[l]), wg, wqx, row(g_q_cross[l]),
                    bf(w_br_dsa[l]), bf(w_br_s5[l]), bf(w_br_cross[l]), bf(w_out[l]))

        x = _ffn(x1.reshape(B * S, D), row(g_ffn_norm[l]), bf(w_ffn_gate[l]), bf(w_ffn_up[l]),
                 bf(w_ffn_down[l])).reshape(B, S, D)
    return x
```

```python
import math

import jax
import jax.numpy as jnp
import numpy as np
from jax import lax
from jax.experimental import pallas as pl
from jax.experimental.pallas import tpu as pltpu

F32 = jnp.float32
BF16 = jnp.bfloat16

D_MODEL = 1024
N_MEM = 256
EPS = 1e-6
DSA_HEADS = 8
DSA_LATENT = 128
DSA_VDIM = 64
IDX_HEADS = 8
IDX_DIM = 64
TOPK_MAX = 256
REL_BUCKETS = 32
REL_MAX_DIST = 128
S5_WIDTH = 512
S5_GROUP = 16
S5_GROUPS = S5_WIDTH // S5_GROUP
S5_STATE = 64
X_HEADS = 4
X_HEAD_DIM = 128
DSA_WIDTH = DSA_HEADS * DSA_VDIM
X_WIDTH = X_HEADS * X_HEAD_DIM
N_BRANCH = 3
IN_SPLITS = (DSA_HEADS * DSA_LATENT, DSA_LATENT, IDX_HEADS * IDX_DIM, IDX_DIM,
             IDX_HEADS, S5_WIDTH, X_WIDTH, N_BRANCH * D_MODEL)

LANES = 128
SUBLANES = 8
BF16_ROWS = 16
CT_ROWS = DSA_LATENT + BF16_ROWS
VMEM_LIMIT = 56 * 1024 * 1024
LOG2E = math.log2(math.e)

TS_IN = 512
TQ = 256
TK = 256
N_BISECT = 15
T_S5 = 64
S5_PITCH = T_S5 + SUBLANES
TS_MERGE = 512
TS_FFN = 512
NEG = -(2.0 ** 100)

NT_DIMS = (((1,), (1,)), ((), ()))


def _rms(x, g):
    ms = jnp.mean(x * x, axis=-1, keepdims=True)
    return x * lax.rsqrt(ms + EPS) * g


def _tree(fn, x):
    while x.shape[0] > 1:
        half = x.shape[0] // 2
        x = fn(x[:half], x[half:])
    return x[0]


def _const_spec(shape):
    nd = len(shape)
    return pl.BlockSpec(shape, lambda *_: (0,) * nd, pipeline_mode=pl.Buffered(1))


def _params(sem):
    return pltpu.CompilerParams(dimension_semantics=sem, vmem_limit_bytes=VMEM_LIMIT)


def _inproj_kernel(x_ref, gmix_ref, wqT_ref, wc_ref, wqiT_ref, wwT_ref, wk_ref, wu_ref,
                   gqc_ref, gkv_ref,
                   qT_ref, cT_ref, c_ref, qiT_ref, wT_ref, kidx_ref, u_ref):
    ts = x_ref.shape[0]
    hb = _rms(x_ref[...], gmix_ref[...]).astype(BF16)

    qT = lax.dot_general(wqT_ref[...], hb, NT_DIMS, preferred_element_type=F32)
    q3 = qT.reshape(DSA_HEADS, DSA_LATENT, ts)
    ms = jnp.mean(q3 * q3, axis=1, keepdims=True)
    qT_ref[...] = (q3 * lax.rsqrt(ms + EPS) * gqc_ref[...][None]).reshape(qT.shape).astype(BF16)

    cn = _rms(jnp.dot(hb, wc_ref[...], preferred_element_type=F32), gkv_ref[...])
    c_ref[...] = cn.astype(BF16)
    cTn = cn.T.astype(BF16)
    for k in range(ts // TK):
        cT_ref[k, :DSA_LATENT, :] = cTn[:, k * TK:(k + 1) * TK]
        cT_ref[k, DSA_LATENT:, :] = jnp.ones((BF16_ROWS, TK), BF16)

    qiT_ref[...] = lax.dot_general(wqiT_ref[...], hb, NT_DIMS, preferred_element_type=F32).astype(BF16)
    wT_ref[...] = lax.dot_general(wwT_ref[...], hb, NT_DIMS, preferred_element_type=F32)
    kidx_ref[...] = jnp.dot(hb, wk_ref[...], preferred_element_type=F32).astype(BF16)

    u_ref[...] = jnp.dot(hb, wu_ref[...], preferred_element_type=F32)


def _inproj(x, gmix, wqT, wc, wqiT, wwT, wk, wu, gqc, gkv):
    B, S, D = x.shape
    ts = TS_IN
    grid = (B, S // ts)
    tok = lambda w: pl.BlockSpec((None, ts, w), lambda b, s: (b, s, 0))
    tokT = lambda r: pl.BlockSpec((None, r, ts), lambda b, s: (b, 0, s))
    hq = DSA_HEADS * DSA_LATENT
    hi = IDX_HEADS * IDX_DIM
    out_shape = (
        jax.ShapeDtypeStruct((B, hq, S), BF16),
        jax.ShapeDtypeStruct((B, S // TK, CT_ROWS, TK), BF16),
        jax.ShapeDtypeStruct((B, S, DSA_LATENT), BF16),
        jax.ShapeDtypeStruct((B, hi, S), BF16),
        jax.ShapeDtypeStruct((B, BF16_ROWS, S), F32),
        jax.ShapeDtypeStruct((B, S, IDX_DIM), BF16),
        jax.ShapeDtypeStruct((B, S, S5_WIDTH), F32),
    )
    out_specs = (tokT(hq),
                 pl.BlockSpec((None, ts // TK, CT_ROWS, TK), lambda b, s: (b, s, 0, 0)),
                 tok(DSA_LATENT), tokT(hi), tokT(BF16_ROWS), tok(IDX_DIM),
                 tok(S5_WIDTH))
    consts = (gmix, wqT, wc, wqiT, wwT, wk, wu, gqc, gkv)
    in_specs = [tok(D)] + [_const_spec(a.shape) for a in consts]
    return pl.pallas_call(
        _inproj_kernel, grid=grid, in_specs=in_specs, out_specs=out_specs, out_shape=out_shape,
        compiler_params=_params(("parallel", "parallel")), name="inproj",
    )(x, *consts)


def _dsa_kernel(qT_ref, qiT_ref, wT_ref, kidx_ref, c_ref, cT_ref, nb_ref, wuvT_ref, o_ref,
                sc_ref, pref_ref, lg_ref, m_ref, acc_ref):
    i = pl.program_id(1)
    nk = sc_ref.shape[0]
    kf = float(TOPK_MAX)
    G = TK // SUBLANES

    def rep(fn, a):
        return jnp.broadcast_to(fn(a, axis=0, keepdims=True), (SUBLANES, TQ))

    def full(v, dt=F32):
        return jnp.full((SUBLANES, TQ), v, dt)

    def key_rows(j):
        return pl.ds(pl.multiple_of(j * TK, TK), TK)

    def score(j):
        ks = kidx_ref[key_rows(j), :]
        acc = None
        for h in range(IDX_HEADS):
            d = jnp.dot(ks, qiT_ref[h * IDX_DIM:(h + 1) * IDX_DIM, :], preferred_element_type=F32)
            t = jnp.maximum(d, 0.0) * wT_ref[h:h + 1, :]
            acc = t if acc is None else acc + t
        return acc

    def score_body(j, carry):
        sc_ref[j] = score(j)
        return carry

    lax.fori_loop(0, i, score_body, 0)
    key_t = lax.broadcasted_iota(jnp.int32, (TK, TQ), 0)
    qry_t = lax.broadcasted_iota(jnp.int32, (TK, TQ), 1)
    causal = key_t <= qry_t
    sc_ref[i] = jnp.where(causal, score(i), -jnp.inf)

    @pl.when(i == 0)
    def _():
        sc_ref[0] = jnp.where(causal, 0.0, NEG)

    @pl.when(i > 0)
    def _():
        nt = i + 1

        def tile3(j):
            return sc_ref[j].reshape(G, SUBLANES, TQ)

        def minmax_body(j, carry):
            mn, mx = carry
            s = tile3(j)
            mx = jnp.maximum(mx, _tree(jnp.maximum, s))
            mn = jnp.minimum(mn, _tree(jnp.minimum, jnp.where(s == -jnp.inf, jnp.inf, s)))
            return mn, mx

        mn, mx = lax.fori_loop(0, nt, minmax_body, (full(jnp.inf), full(-jnp.inf)))
        lo, hi = rep(jnp.min, mn), rep(jnp.max, mx)

        def count_ge(thr):
            def body(j, acc):
                return acc + _tree(jnp.add, jnp.where(tile3(j) >= thr[None], 1.0, 0.0))
            return rep(jnp.sum, lax.fori_loop(0, nt, body, full(0.0)))

        def bisect(_, carry):
            lo, hi = carry
            mid = 0.5 * (lo + hi)
            ge = count_ge(mid) >= kf
            return jnp.where(ge, mid, lo), jnp.where(ge, hi, mid)

        lo, hi = lax.fori_loop(0, N_BISECT, bisect, (lo, hi))

        def snap_body(j, am):
            s = tile3(j)
            return jnp.minimum(am, _tree(jnp.minimum, jnp.where(s >= lo[None], s, jnp.inf)))

        cur = rep(jnp.min, lax.fori_loop(0, nt, snap_body, full(jnp.inf)))

        def walk(cur):
            def body(j, carry):
                ac, am = carry
                s = tile3(j)
                g = s > cur[None]
                ac = ac + _tree(jnp.add, jnp.where(g, 1.0, 0.0))
                am = jnp.minimum(am, _tree(jnp.minimum, jnp.where(g, s, jnp.inf)))
                return ac, am
            ac, am = lax.fori_loop(0, nt, body, (full(0.0), full(jnp.inf)))
            return rep(jnp.sum, ac), rep(jnp.min, am)

        def walk_cond(carry):
            _, _, go, it = carry
            return jnp.logical_and(go > 0, it < nk * TK + 2)

        def walk_body(carry):
            cur, _, _, it = carry
            c, nxt = walk(cur)
            move = c >= kf
            go = (jnp.max(jnp.where(move, 1.0, 0.0)) > 0.5).astype(jnp.int32)
            return jnp.where(move, nxt, cur), c, go, it + 1

        kth, cgt, _, _ = lax.while_loop(
            walk_cond, walk_body, (cur, full(0.0), jnp.int32(1), jnp.int32(0)))
        need = kf - cgt

        tril = jnp.where(key_t >= qry_t, 1.0, 0.0).astype(BF16)

        def pref_body(p, carry):
            for j in (2 * p, jnp.minimum(2 * p + 1, nt - 1)):
                e01 = jnp.where(tile3(j) == kth[None], 1.0, 0.0).reshape(TK, TQ).astype(BF16)
                pref_ref[j] = jnp.dot(tril, e01, preferred_element_type=F32)
            return carry

        lax.fori_loop(0, lax.shift_right_logical(nt + 1, 1), pref_body, 0)

        def mask_body(j, offset):
            s = tile3(j)
            pref = pref_ref[j]
            rank = pref.reshape(G, SUBLANES, TQ) + offset[None]
            tie = jnp.where(rank <= need[None], 0.0, NEG)
            mb = jnp.where(s > kth[None], 0.0, jnp.where(s == kth[None], tie, NEG))
            sc_ref[j] = mb.reshape(TK, TQ)
            return offset + jnp.broadcast_to(pref[TK - 1:TK, :], (SUBLANES, TQ))

        lax.fori_loop(0, nt, mask_body, full(0.0))

    m_ref[...] = jnp.full(m_ref.shape, NEG, F32)
    acc_ref[...] = jnp.zeros(acc_ref.shape, F32)
    LG = CT_ROWS // SUBLANES

    RB = TK // BF16_ROWS

    def attend(tiles):
        m_run = [m_ref[h] for h in range(DSA_HEADS)]
        stats = []
        for slot, (j, near) in enumerate(tiles):
            ct = c_ref[key_rows(j), :]
            mbb = sc_ref[j].astype(BF16)
            per_head = []
            for h in range(DSA_HEADS):
                lg = jnp.dot(ct, qT_ref[h * DSA_LATENT:(h + 1) * DSA_LATENT, :],
                             preferred_element_type=F32)
                if near is not None:
                    lg = lg + nb_ref[near, h]
                lgb = lg.astype(BF16) + mbb
                lg_ref[slot, h] = lgb
                tmax = _tree(jnp.maximum, lgb.reshape(RB, BF16_ROWS, TQ)).astype(F32)
                m_new = jnp.maximum(m_run[h], rep(jnp.max, tmax))
                per_head.append((m_run[h], m_new))
                m_run[h] = m_new
            stats.append(per_head)
        for slot, (j, near) in enumerate(tiles):
            ctT = cT_ref[j]
            for h in range(DSA_HEADS):
                m_old, m_new = stats[slot][h]
                alpha = jnp.exp2(m_old - m_new)
                m16 = jnp.concatenate([m_new, m_new], axis=0).astype(BF16)
                x = lg_ref[slot, h].reshape(RB, BF16_ROWS, TQ) - m16[None]
                pv = jnp.dot(ctT, jnp.exp2(x).reshape(TK, TQ), preferred_element_type=F32)
                acc3 = acc_ref[h].reshape(LG, SUBLANES, TQ) * alpha[None]
                acc_ref[h] = acc3.reshape(CT_ROWS, TQ) + pv
        for h in range(DSA_HEADS):
            m_ref[h] = m_run[h]

    nfar = jnp.maximum(i - 1, 0)

    def far_pair(p, carry):
        attend([(2 * p, None), (2 * p + 1, None)])
        return carry

    lax.fori_loop(0, lax.shift_right_logical(nfar, 1), far_pair, 0)

    @pl.when((nfar & 1) == 1)
    def _():
        attend([(nfar - 1, None)])

    @pl.when(i > 0)
    def _():
        attend([(i - 1, 1), (i, 0)])

    @pl.when(i == 0)
    def _():
        attend([(0, 0)])

    outs = []
    for h in range(DSA_HEADS):
        rl = 1.0 / acc_ref[h, DSA_LATENT:DSA_LATENT + SUBLANES, :]
        o3 = acc_ref[h, :DSA_LATENT, :].reshape(DSA_LATENT // SUBLANES, SUBLANES, TQ) * rl[None]
        o = o3.reshape(DSA_LATENT, TQ).astype(BF16)
        outs.append(jnp.dot(wuvT_ref[h], o, preferred_element_type=F32))
    o_ref[...] = jnp.concatenate(outs, axis=0).T.astype(BF16)


def _dsa(qT, qiT, wT, kidx, c, cT, nb, wuvT):
    B, S, _ = c.shape
    nk = S // TK
    grid = (B, S // TQ)
    tileT = lambda r: pl.BlockSpec((None, r, TQ), lambda b, i: (b, 0, i))
    seq = lambda w: pl.BlockSpec((None, S, w), lambda b, i: (b, 0, 0))
    in_specs = [tileT(DSA_HEADS * DSA_LATENT), tileT(IDX_HEADS * IDX_DIM), tileT(BF16_ROWS),
                seq(IDX_DIM), seq(DSA_LATENT),
                pl.BlockSpec((None, nk, CT_ROWS, TK), lambda b, i: (b, 0, 0, 0)),
                _const_spec(nb.shape), _const_spec(wuvT.shape)]
    scratch = [
        pltpu.VMEM((nk, TK, TQ), F32),
        pltpu.VMEM((nk, TK, TQ), F32),
        pltpu.VMEM((2, DSA_HEADS, TK, TQ), BF16),
        pltpu.VMEM((DSA_HEADS, SUBLANES, TQ), F32),
        pltpu.VMEM((DSA_HEADS, CT_ROWS, TQ), F32),
    ]
    return pl.pallas_call(
        _dsa_kernel, grid=grid, in_specs=in_specs,
        out_specs=pl.BlockSpec((None, TQ, DSA_WIDTH), lambda b, i: (b, i, 0)),
        out_shape=jax.ShapeDtypeStruct((B, S, DSA_WIDTH), BF16), scratch_shapes=scratch,
        compiler_params=_params(("parallel", "arbitrary")), name="dsa",
    )(qT, qiT, wT, kidx, c, cT, nb, wuvT)


def _s5_kernel(u_ref, bm_ref, cm_ref, lre_ref, lim_ref, dsk_ref, wglu_ref, o_ref,
               uslab_ref, uil_ref, st_ref, oslab_ref, hre_ref, him_ref):
    nstate = S5_GROUPS * S5_STATE
    nb = hre_ref.shape[0]
    nslab = S5_WIDTH // LANES

    @pl.when(pl.program_id(0) == 0)
    def _():
        hre_ref[...] = jnp.zeros(hre_ref.shape, F32)
        him_ref[...] = jnp.zeros(him_ref.shape, F32)

    for b in range(nb):
        for k in range(nslab):
            uslab_ref[k, b * S5_PITCH:b * S5_PITCH + T_S5, :] = u_ref[b, :, k * LANES:(k + 1) * LANES]

    def gather_step(t, carry):
        for k in range(nslab):
            uil_ref[pl.ds(pl.multiple_of(t * nb, nb), nb), k * LANES:(k + 1) * LANES] = (
                uslab_ref[k, pl.ds(t, nb, stride=S5_PITCH), :])
        return carry

    lax.fori_loop(0, T_S5, gather_step, 0, unroll=4)
    u = uil_ref[...]
    ub = u.astype(BF16)

    kin = 4
    cin = S5_WIDTH // kin
    sin = nstate // kin
    for k in range(kin):
        uk = ub[:, k * cin:(k + 1) * cin]
        for off in (0, nstate):
            cols = slice(off + k * sin, off + (k + 1) * sin)
            st_ref[:, cols] = jnp.dot(uk, bm_ref[k * cin:(k + 1) * cin, cols],
                                      preferred_element_type=F32)

    half = nstate // 2
    for part in range(2):
        re_sl = slice(part * half, (part + 1) * half)
        im_sl = slice(nstate + part * half, nstate + (part + 1) * half)
        lre = lre_ref[:, re_sl]
        lim = lim_ref[:, re_sl]

        def step(t, carry):
            hr, hi = carry
            r = pl.ds(pl.multiple_of(t * nb, nb), nb)
            nr = lre * hr - lim * hi + st_ref[r, re_sl]
            ni = lre * hi + lim * hr + st_ref[r, im_sl]
            st_ref[r, re_sl] = nr
            st_ref[r, im_sl] = ni
            return nr, ni

        hr, hi = lax.fori_loop(0, T_S5, step, (hre_ref[:, re_sl], him_ref[:, re_sl]), unroll=4)
        hre_ref[:, re_sl] = hr
        him_ref[:, re_sl] = hi

    kout = 2
    cout = S5_WIDTH // kout
    sout = nstate // kout
    ys = []
    for k in range(kout):
        acc = None
        for off in (0, nstate):
            rows = slice(off + k * sout, off + (k + 1) * sout)
            part = jnp.dot(st_ref[:, rows].astype(BF16), cm_ref[rows, k * cout:(k + 1) * cout],
                           preferred_element_type=F32)
            acc = part if acc is None else acc + part
        ys.append(acc)
    y = jnp.concatenate(ys, axis=1)
    y = jax.nn.gelu(y + dsk_ref[...] * u)
    z = jnp.dot(y.astype(BF16), wglu_ref[...], preferred_element_type=F32)
    o = y * jax.nn.sigmoid(z)

    for k in range(nslab):
        oslab_ref[k] = o[:, k * LANES:(k + 1) * LANES]
    for b in range(nb):
        for k in range(nslab):
            o_ref[b, :, k * LANES:(k + 1) * LANES] = (
                oslab_ref[k, pl.ds(b, T_S5, stride=nb), :].astype(BF16))


def _s5(u, bm, cm, lre, lim, dsk, wglu):
    B, S, W = u.shape
    tb = T_S5 * B
    nstate = S5_GROUPS * S5_STATE
    tok = pl.BlockSpec((B, T_S5, W), lambda t: (0, t, 0))
    in_specs = [tok, _const_spec(bm.shape), _const_spec(cm.shape), _const_spec(lre.shape),
                _const_spec(lim.shape), _const_spec(dsk.shape), _const_spec(wglu.shape)]
    scratch = [pltpu.VMEM((W // LANES, B * S5_PITCH, LANES), F32),
               pltpu.VMEM((tb, W), F32),
               pltpu.VMEM((tb, 2 * nstate), F32),
               pltpu.VMEM((W // LANES, tb, LANES), F32),
               pltpu.VMEM((B, nstate), F32), pltpu.VMEM((B, nstate), F32)]
    return pl.pallas_call(
        _s5_kernel, grid=(S // T_S5,), in_specs=in_specs, out_specs=tok,
        out_shape=jax.ShapeDtypeStruct((B, S, W), BF16), scratch_shapes=scratch,
        compiler_params=_params(("arbitrary",)), name="s5",
    )(u, bm, cm, lre, lim, dsk, wglu)


def _memkv_kernel(mem_ref, gmem_ref, wkv_ref, gk_ref, k_ref, v_ref):
    mb = _rms(mem_ref[...], gmem_ref[...]).astype(BF16)
    kv = jnp.dot(mb, wkv_ref[...], preferred_element_type=F32)
    for h in range(X_HEADS):
        sl = slice(h * X_HEAD_DIM, (h + 1) * X_HEAD_DIM)
        k_ref[:, sl] = _rms(kv[:, sl], gk_ref[...]).astype(BF16)
    v_ref[...] = kv[:, X_WIDTH:].astype(BF16)


def _memkv(mem, gmem, wkv, gk):
    B, M, D = mem.shape
    blk = lambda w: pl.BlockSpec((None, M, w), lambda b: (b, 0, 0))
    return pl.pallas_call(
        _memkv_kernel, grid=(B,),
        in_specs=[blk(D), _const_spec(gmem.shape), _const_spec(wkv.shape), _const_spec(gk.shape)],
        out_specs=(blk(X_WIDTH), blk(X_WIDTH)),
        out_shape=(jax.ShapeDtypeStruct((B, M, X_WIDTH), BF16),) * 2,
        compiler_params=_params(("parallel",)), name="memkv",
    )(mem, gmem, wkv, gk)


def _merge_kernel(x_ref, odsa_ref, os5_ref, k_ref, v_ref, gmix_ref, wg_ref, wqx_ref, gqx_ref,
                  wb1_ref, wb2_ref, wb3_ref, wout_ref, y_ref):
    x = x_ref[...]
    hb = _rms(x, gmix_ref[...]).astype(BF16)

    qx = jnp.dot(hb, wqx_ref[...], preferred_element_type=F32)
    gqx = gqx_ref[...] * (X_HEAD_DIM ** -0.5)
    ox = []
    for h in range(X_HEADS):
        sl = slice(h * X_HEAD_DIM, (h + 1) * X_HEAD_DIM)
        qh = _rms(qx[:, sl], gqx).astype(BF16)
        lg = lax.dot_general(qh, k_ref[:, sl], NT_DIMS, preferred_element_type=F32)
        p = jnp.exp(lg - jnp.max(lg, axis=-1, keepdims=True))
        pv = jnp.dot(p.astype(BF16), v_ref[:, sl], preferred_element_type=F32)
        ox.append((pv / jnp.sum(p, axis=-1, keepdims=True)).astype(BF16))
    ox = jnp.concatenate(ox, axis=1)

    merged = None
    for br, (o, wb) in enumerate(((odsa_ref[...], wb1_ref), (os5_ref[...], wb2_ref), (ox, wb3_ref))):
        gate = jax.nn.sigmoid(jnp.dot(hb, wg_ref[:, br * D_MODEL:(br + 1) * D_MODEL],
                                      preferred_element_type=F32))
        term = gate * jnp.dot(o, wb[...], preferred_element_type=F32)
        merged = term if merged is None else merged + term
    y_ref[...] = x + jnp.dot(merged.astype(BF16), wout_ref[...], preferred_element_type=F32)


def _merge(x, odsa, os5, k, v, gmix, wg, wqx, gqx, wb1, wb2, wb3, wout):
    B, S, D = x.shape
    ts = TS_MERGE
    tok = lambda w: pl.BlockSpec((None, ts, w), lambda b, s: (b, s, 0))
    memspec = pl.BlockSpec((None, N_MEM, X_WIDTH), lambda b, s: (b, 0, 0))
    in_specs = [tok(D), tok(DSA_WIDTH),
                tok(S5_WIDTH),
                memspec, memspec] + [_const_spec(a.shape) for a in
                                     (gmix, wg, wqx, gqx, wb1, wb2, wb3, wout)]
    return pl.pallas_call(
        _merge_kernel, grid=(B, S // ts), in_specs=in_specs, out_specs=tok(D),
        out_shape=jax.ShapeDtypeStruct((B, S, D), F32),
        compiler_params=_params(("parallel", "parallel")), name="merge",
    )(x, odsa, os5, k, v, gmix, wg, wqx, gqx, wb1, wb2, wb3, wout)


def _ffn_kernel(x_ref, g_ref, wg_ref, wu_ref, wd_ref, y_ref):
    x = x_ref[...]
    hb = _rms(x, g_ref[...]).astype(BF16)
    a = jnp.dot(hb, wg_ref[...], preferred_element_type=F32)
    b = jnp.dot(hb, wu_ref[...], preferred_element_type=F32)
    act = (jax.nn.silu(a) * b).astype(BF16)
    y_ref[...] = x + jnp.dot(act, wd_ref[...], preferred_element_type=F32)


def _ffn(x2, g, wg, wu, wd):
    n, D = x2.shape
    tok = pl.BlockSpec((TS_FFN, D), lambda t: (t, 0))
    return pl.pallas_call(
        _ffn_kernel, grid=(n // TS_FFN,),
        in_specs=[tok] + [_const_spec(a.shape) for a in (g, wg, wu, wd)],
        out_specs=tok, out_shape=jax.ShapeDtypeStruct((n, D), F32),
        compiler_params=_params(("parallel",)), name="ffn",
    )(x2, g, wg, wu, wd)


def _t5_bucket(n):
    max_exact = REL_BUCKETS // 2
    nf = jnp.maximum(n, 1).astype(F32)
    large = max_exact + (jnp.log(nf / max_exact) / math.log(REL_MAX_DIST / max_exact)
                         * (REL_BUCKETS - max_exact)).astype(jnp.int32)
    large = jnp.minimum(large, REL_BUCKETS - 1)
    return jnp.where(n < max_exact, n, large)


def _toeplitz(w, rows, cols):
    H, L = w.shape
    flat = jnp.tile(w, (1, rows))[:, :rows * (L - 1)]
    return flat.reshape(H, rows, L - 1)[:, :, :cols]


def _near_bias(rel_bias):
    n = jnp.arange(2 * TQ, dtype=jnp.int32)
    f = (rel_bias[_t5_bucket(n)] - rel_bias[REL_BUCKETS - 1][None, :]).T * LOG2E
    w_diag = jnp.concatenate([f[:, :TQ], jnp.broadcast_to(f[:, :1], (f.shape[0], TK - 1))], axis=1)
    w_prev = jnp.concatenate([f[:, TQ:2 * TQ], f[:, 1:TK]], axis=1)
    return jnp.stack([_toeplitz(w_diag, TK, TQ), _toeplitz(w_prev, TK, TQ)], axis=0).astype(F32)


def _s5_mats(a_re, a_im, log_dt, b_re, b_im, c_re, c_im):
    lam = lax.complex(a_re.astype(F32), a_im.astype(F32))
    dt = jnp.exp(log_dt.astype(F32))[:, None]
    lam_bar = jnp.exp(lam * dt)
    b_bar = ((lam_bar - 1.0) / lam)[..., None] * lax.complex(b_re.astype(F32), b_im.astype(F32))
    nstate = S5_GROUPS * S5_STATE
    in_mask = (jnp.arange(S5_WIDTH)[:, None] // S5_GROUP) == (jnp.arange(nstate)[None, :] // S5_STATE)

    def blockdiag_in(w):
        t = jnp.transpose(w, (0, 2, 1)).reshape(S5_WIDTH, S5_STATE)
        return jnp.where(in_mask, jnp.tile(t, (1, S5_GROUPS)), 0.0)

    def blockdiag_out(w):
        t = jnp.transpose(w, (0, 2, 1)).reshape(nstate, S5_GROUP)
        return jnp.where(in_mask.T, jnp.tile(t, (1, S5_GROUPS)), 0.0)

    bm = jnp.concatenate([blockdiag_in(jnp.real(b_bar)), blockdiag_in(jnp.imag(b_bar))], axis=1)
    cm = jnp.concatenate([blockdiag_out(c_re.astype(F32)), blockdiag_out(-c_im.astype(F32))], axis=0)
    return bm.astype(BF16), cm.astype(BF16), jnp.real(lam_bar).reshape(1, -1), jnp.imag(lam_bar).reshape(1, -1)


def kernel(x, mem, rel_bias, w_in, g_mix_norm, g_q_dsa, g_kv_dsa, w_uv_dsa, a_re, a_im, log_dt, b_re, b_im, c_re, c_im, d_skip, w_glu, g_mem_norm, w_mem_kv, g_q_cross, g_k_cross, w_br_dsa, w_br_s5, w_br_cross, w_out, g_ffn_norm, w_ffn_gate, w_ffn_up, w_ffn_down):
    B, S, D = x.shape
    depth = w_in.shape[0]
    offs = [0] + [int(o) for o in np.cumsum(IN_SPLITS)]
    nb = _near_bias(rel_bias)
    row = lambda v: v.reshape(1, -1).astype(F32)
    col = lambda v: v.reshape(-1, 1).astype(F32)
    bf = lambda a: a.astype(BF16)
    for l in range(depth):
        w = bf(w_in[l])
        wq, wc, wqi, wk, ww, wu, wqx, wg = [w[:, offs[k]:offs[k + 1]] for k in range(len(IN_SPLITS))]
        wwT = jnp.pad(ww.T, ((0, BF16_ROWS - IDX_HEADS), (0, 0)))
        gqc = col(g_q_dsa[l]) * (DSA_LATENT ** -0.5 * LOG2E)

        qT, cT, c, qiT, wT, kidx, u = _inproj(
            x, row(g_mix_norm[l]), wq.T, wc, wqi.T, wwT, wk, wu, gqc, row(g_kv_dsa[l]))

        wuvT = bf(jnp.transpose(w_uv_dsa[l], (0, 2, 1)))
        o_dsa = _dsa(qT, qiT, wT, kidx, c, cT, nb, wuvT)

        bm, cm, lre, lim = _s5_mats(a_re[l], a_im[l], log_dt[l], b_re[l], b_im[l], c_re[l], c_im[l])
        lre = jnp.broadcast_to(lre, (B, lre.shape[1]))
        lim = jnp.broadcast_to(lim, (B, lim.shape[1]))
        o_s5 = _s5(u, bm, cm, lre, lim, row(d_skip[l]), bf(w_glu[l]))

        k, v = _memkv(mem, row(g_mem_norm[l]), bf(w_mem_kv[l]), row(g_k_cross[l]))

        x1 = _merge(x, o_dsa, o_s5, k, v, row(g_mix_norm[l]), wg, wqx, row(g_q_cross[l]),
                    bf(w_br_dsa[l]), bf(w_br_s5[l]), bf(w_br_cross[l]), bf(w_out[l]))

        x = _ffn(x1.reshape(B * S, D), row(g_ffn_norm[l]), bf(w_ffn_gate[l]), bf(w_ffn_up[l]),
                 bf(w_ffn_down[l])).reshape(B, S, D)
    return x
```

```python
import math

import jax
import jax.numpy as jnp
import numpy as np
from jax import lax
from jax.experimental import pallas as pl
from jax.experimental.pallas import tpu as pltpu

F32 = jnp.float32
BF16 = jnp.bfloat16

D_MODEL = 1024
N_MEM = 256
EPS = 1e-6
DSA_HEADS = 8
DSA_LATENT = 128
DSA_VDIM = 64
IDX_HEADS = 8
IDX_DIM = 64
TOPK_MAX = 256
REL_BUCKETS = 32
REL_MAX_DIST = 128
S5_WIDTH = 512
S5_GROUP = 16
S5_GROUPS = S5_WIDTH // S5_GROUP
S5_STATE = 64
X_HEADS = 4
X_HEAD_DIM = 128
DSA_WIDTH = DSA_HEADS * DSA_VDIM
X_WIDTH = X_HEADS * X_HEAD_DIM
N_BRANCH = 3
IN_SPLITS = (DSA_HEADS * DSA_LATENT, DSA_LATENT, IDX_HEADS * IDX_DIM, IDX_DIM,
             IDX_HEADS, S5_WIDTH, X_WIDTH, N_BRANCH * D_MODEL)

LANES = 128
SUBLANES = 8
BF16_ROWS = 16
CT_ROWS = DSA_LATENT + BF16_ROWS
VMEM_LIMIT = 56 * 1024 * 1024
LOG2E = math.log2(math.e)

TS_IN = 1024
TQ = 256
TK = 256
N_BISECT = 15
T_S5 = 128
S5_PITCH = T_S5 + SUBLANES
TS_MERGE = 1024
TS_FFN = 512
NEG = -(2.0 ** 100)

NT_DIMS = (((1,), (1,)), ((), ()))


def _rms(x, g):
    ms = jnp.mean(x * x, axis=-1, keepdims=True)
    return x * lax.rsqrt(ms + EPS) * g


def _tree(fn, x):
    while x.shape[0] > 1:
        half = x.shape[0] // 2
        x = fn(x[:half], x[half:])
    return x[0]


def _const_spec(shape):
    nd = len(shape)
    return pl.BlockSpec(shape, lambda *_: (0,) * nd, pipeline_mode=pl.Buffered(1))


def _params(sem):
    return pltpu.CompilerParams(dimension_semantics=sem, vmem_limit_bytes=VMEM_LIMIT)


def _inproj_kernel(x_ref, gmix_ref, wqT_ref, wc_ref, wqiT_ref, wwT_ref, wk_ref, wu_ref,
                   gqc_ref, gkv_ref,
                   qT_ref, cT_ref, c_ref, qiT_ref, wT_ref, kidx_ref, u_ref):
    ts = x_ref.shape[0]
    hb = _rms(x_ref[...], gmix_ref[...]).astype(BF16)

    qT = lax.dot_general(wqT_ref[...], hb, NT_DIMS, preferred_element_type=F32)
    q3 = qT.reshape(DSA_HEADS, DSA_LATENT, ts)
    ms = jnp.mean(q3 * q3, axis=1, keepdims=True)
    qT_ref[...] = (q3 * lax.rsqrt(ms + EPS) * gqc_ref[...][None]).reshape(qT.shape).astype(BF16)

    cn = _rms(jnp.dot(hb, wc_ref[...], preferred_element_type=F32), gkv_ref[...])
    c_ref[...] = cn.astype(BF16)
    cTn = cn.T.astype(BF16)
    for k in range(ts // TK):
        cT_ref[k, :DSA_LATENT, :] = cTn[:, k * TK:(k + 1) * TK]
        cT_ref[k, DSA_LATENT:, :] = jnp.ones((BF16_ROWS, TK), BF16)

    qiT_ref[...] = lax.dot_general(wqiT_ref[...], hb, NT_DIMS, preferred_element_type=F32).astype(BF16)
    wT_ref[...] = lax.dot_general(wwT_ref[...], hb, NT_DIMS, preferred_element_type=F32)
    kidx_ref[...] = jnp.dot(hb, wk_ref[...], preferred_element_type=F32).astype(BF16)

    u_ref[...] = jnp.dot(hb, wu_ref[...], preferred_element_type=F32)


def _inproj(x, gmix, wqT, wc, wqiT, wwT, wk, wu, gqc, gkv):
    B, S, D = x.shape
    ts = TS_IN
    grid = (B, S // ts)
    tok = lambda w: pl.BlockSpec((None, ts, w), lambda b, s: (b, s, 0))
    tokT = lambda r: pl.BlockSpec((None, r, ts), lambda b, s: (b, 0, s))
    hq = DSA_HEADS * DSA_LATENT
    hi = IDX_HEADS * IDX_DIM
    out_shape = (
        jax.ShapeDtypeStruct((B, hq, S), BF16),
        jax.ShapeDtypeStruct((B, S // TK, CT_ROWS, TK), BF16),
        jax.ShapeDtypeStruct((B, S, DSA_LATENT), BF16),
        jax.ShapeDtypeStruct((B, hi, S), BF16),
        jax.ShapeDtypeStruct((B, BF16_ROWS, S), F32),
        jax.ShapeDtypeStruct((B, S, IDX_DIM), BF16),
        jax.ShapeDtypeStruct((B, S, S5_WIDTH), F32),
    )
    out_specs = (tokT(hq),
                 pl.BlockSpec((None, ts // TK, CT_ROWS, TK), lambda b, s: (b, s, 0, 0)),
                 tok(DSA_LATENT), tokT(hi), tokT(BF16_ROWS), tok(IDX_DIM),
                 tok(S5_WIDTH))
    consts = (gmix, wqT, wc, wqiT, wwT, wk, wu, gqc, gkv)
    in_specs = [tok(D)] + [_const_spec(a.shape) for a in consts]
    return pl.pallas_call(
        _inproj_kernel, grid=grid, in_specs=in_specs, out_specs=out_specs, out_shape=out_shape,
        compiler_params=_params(("parallel", "parallel")), name="inproj",
    )(x, *consts)


def _dsa_kernel(qT_ref, qiT_ref, wT_ref, kidx_ref, c_ref, cT_ref, nb_ref, wuvT_ref, o_ref,
                sc_ref, pref_ref, lg_ref, m_ref, acc_ref):
    i = pl.program_id(1)
    nk = sc_ref.shape[0]
    kf = float(TOPK_MAX)
    G = TK // SUBLANES

    def rep(fn, a):
        return jnp.broadcast_to(fn(a, axis=0, keepdims=True), (SUBLANES, TQ))

    def full(v, dt=F32):
        return jnp.full((SUBLANES, TQ), v, dt)

    def key_rows(j):
        return pl.ds(pl.multiple_of(j * TK, TK), TK)

    def score(j):
        ks = kidx_ref[key_rows(j), :]
        acc = None
        for h in range(IDX_HEADS):
            d = jnp.dot(ks, qiT_ref[h * IDX_DIM:(h + 1) * IDX_DIM, :], preferred_element_type=F32)
            t = jnp.maximum(d, 0.0) * wT_ref[h:h + 1, :]
            acc = t if acc is None else acc + t
        return acc

    def score_pair(p, carry):
        sc_ref[2 * p] = score(2 * p)
        sc_ref[2 * p + 1] = score(2 * p + 1)
        return carry

    lax.fori_loop(0, lax.shift_right_logical(i, 1), score_pair, 0)

    @pl.when((i & 1) == 1)
    def _():
        sc_ref[i - 1] = score(i - 1)

    key_t = lax.broadcasted_iota(jnp.int32, (TK, TQ), 0)
    qry_t = lax.broadcasted_iota(jnp.int32, (TK, TQ), 1)
    causal = key_t <= qry_t
    sc_ref[i] = jnp.where(causal, score(i), -jnp.inf)

    @pl.when(i == 0)
    def _():
        sc_ref[0] = jnp.where(causal, 0.0, NEG)

    @pl.when(i > 0)
    def _():
        nt = i + 1

        def tile3(j):
            return sc_ref[j].reshape(G, SUBLANES, TQ)

        def minmax_body(j, carry):
            mn, mx = carry
            s = tile3(j)
            mx = jnp.maximum(mx, _tree(jnp.maximum, s))
            mn = jnp.minimum(mn, _tree(jnp.minimum, jnp.where(s == -jnp.inf, jnp.inf, s)))
            return mn, mx

        mn, mx = lax.fori_loop(0, nt, minmax_body, (full(jnp.inf), full(-jnp.inf)))
        lo, hi = rep(jnp.min, mn), rep(jnp.max, mx)

        def count_ge(thr):
            def body(j, acc):
                return acc + _tree(jnp.add, jnp.where(tile3(j) >= thr[None], 1.0, 0.0))
            return rep(jnp.sum, lax.fori_loop(0, nt, body, full(0.0)))

        def bisect(_, carry):
            lo, hi = carry
            mid = 0.5 * (lo + hi)
            ge = count_ge(mid) >= kf
            return jnp.where(ge, mid, lo), jnp.where(ge, hi, mid)

        lo, hi = lax.fori_loop(0, N_BISECT, bisect, (lo, hi))

        def snap_body(j, am):
            s = tile3(j)
            return jnp.minimum(am, _tree(jnp.minimum, jnp.where(s >= lo[None], s, jnp.inf)))

        cur = rep(jnp.min, lax.fori_loop(0, nt, snap_body, full(jnp.inf)))

        def walk(cur):
            def body(j, carry):
                ac, am = carry
                s = tile3(j)
                g = s > cur[None]
                ac = ac + _tree(jnp.add, jnp.where(g, 1.0, 0.0))
                am = jnp.minimum(am, _tree(jnp.minimum, jnp.where(g, s, jnp.inf)))
                return ac, am
            ac, am = lax.fori_loop(0, nt, body, (full(0.0), full(jnp.inf)))
            return rep(jnp.sum, ac), rep(jnp.min, am)

        def walk_cond(carry):
            _, _, go, it = carry
            return jnp.logical_and(go > 0, it < nk * TK + 2)

        def walk_body(carry):
            cur, _, _, it = carry
            c, nxt = walk(cur)
            move = c >= kf
            go = (jnp.max(jnp.where(move, 1.0, 0.0)) > 0.5).astype(jnp.int32)
            return jnp.where(move, nxt, cur), c, go, it + 1

        kth, cgt, _, _ = lax.while_loop(
            walk_cond, walk_body, (cur, full(0.0), jnp.int32(1), jnp.int32(0)))
        need = kf - cgt

        tril = jnp.where(key_t >= qry_t, 1.0, 0.0).astype(BF16)

        def pref_body(p, carry):
            for j in (2 * p, jnp.minimum(2 * p + 1, nt - 1)):
                e01 = jnp.where(tile3(j) == kth[None], 1.0, 0.0).reshape(TK, TQ).astype(BF16)
                pref_ref[j] = jnp.dot(tril, e01, preferred_element_type=F32)
            return carry

        lax.fori_loop(0, lax.shift_right_logical(nt + 1, 1), pref_body, 0)

        def mask_body(j, offset):
            s = tile3(j)
            pref = pref_ref[j]
            rank = pref.reshape(G, SUBLANES, TQ) + offset[None]
            tie = jnp.where(rank <= need[None], 0.0, NEG)
            mb = jnp.where(s > kth[None], 0.0, jnp.where(s == kth[None], tie, NEG))
            sc_ref[j] = mb.reshape(TK, TQ)
            return offset + jnp.broadcast_to(pref[TK - 1:TK, :], (SUBLANES, TQ))

        lax.fori_loop(0, nt, mask_body, full(0.0))

    m_ref[...] = jnp.full(m_ref.shape, NEG, F32)
    acc_ref[...] = jnp.zeros(acc_ref.shape, F32)
    LG = CT_ROWS // SUBLANES

    RB = TK // BF16_ROWS

    def attend(tiles):
        m_run = [m_ref[h] for h in range(DSA_HEADS)]
        stats = []
        for slot, (j, near) in enumerate(tiles):
            ct = c_ref[key_rows(j), :]
            mbb = sc_ref[j].astype(BF16)
            per_head = []
            for h in range(DSA_HEADS):
                lg = jnp.dot(ct, qT_ref[h * DSA_LATENT:(h + 1) * DSA_LATENT, :],
                             preferred_element_type=F32)
                if near is not None:
                    lg = lg + nb_ref[near, h]
                lgb = lg.astype(BF16) + mbb
                lg_ref[slot, h] = lgb
                tmax = _tree(jnp.maximum, lgb.reshape(RB, BF16_ROWS, TQ)).astype(F32)
                m_new = jnp.maximum(m_run[h], rep(jnp.max, tmax))
                per_head.append((m_run[h], m_new))
                m_run[h] = m_new
            stats.append(per_head)
        for slot, (j, near) in enumerate(tiles):
            ctT = cT_ref[j]
            for h in range(DSA_HEADS):
                m_old, m_new = stats[slot][h]
                alpha = jnp.exp2(m_old - m_new)
                m16 = jnp.concatenate([m_new, m_new], axis=0).astype(BF16)
                x = lg_ref[slot, h].reshape(RB, BF16_ROWS, TQ) - m16[None]
                pv = jnp.dot(ctT, jnp.exp2(x).reshape(TK, TQ), preferred_element_type=F32)
                acc3 = acc_ref[h].reshape(LG, SUBLANES, TQ) * alpha[None]
                acc_ref[h] = acc3.reshape(CT_ROWS, TQ) + pv
        for h in range(DSA_HEADS):
            m_ref[h] = m_run[h]

    nfar = jnp.maximum(i - 1, 0)

    def far_pair(p, carry):
        attend([(2 * p, None), (2 * p + 1, None)])
        return carry

    lax.fori_loop(0, lax.shift_right_logical(nfar, 1), far_pair, 0)

    @pl.when((nfar & 1) == 1)
    def _():
        attend([(nfar - 1, None)])

    @pl.when(i > 0)
    def _():
        attend([(i - 1, 1), (i, 0)])

    @pl.when(i == 0)
    def _():
        attend([(0, 0)])

    outs = []
    for h in range(DSA_HEADS):
        rl = 1.0 / acc_ref[h, DSA_LATENT:DSA_LATENT + SUBLANES, :]
        o3 = acc_ref[h, :DSA_LATENT, :].reshape(DSA_LATENT // SUBLANES, SUBLANES, TQ) * rl[None]
        o = o3.reshape(DSA_LATENT, TQ).astype(BF16)
        outs.append(jnp.dot(wuvT_ref[h], o, preferred_element_type=F32))
    o_ref[...] = jnp.concatenate(outs, axis=0).T.astype(BF16)


def _dsa(qT, qiT, wT, kidx, c, cT, nb, wuvT):
    B, S, _ = c.shape
    nk = S // TK
    grid = (B, S // TQ)
    tileT = lambda r: pl.BlockSpec((None, r, TQ), lambda b, i: (b, 0, i))
    seq = lambda w: pl.BlockSpec((None, S, w), lambda b, i: (b, 0, 0))
    in_specs = [tileT(DSA_HEADS * DSA_LATENT), tileT(IDX_HEADS * IDX_DIM), tileT(BF16_ROWS),
                seq(IDX_DIM), seq(DSA_LATENT),
                pl.BlockSpec((None, nk, CT_ROWS, TK), lambda b, i: (b, 0, 0, 0)),
                _const_spec(nb.shape), _const_spec(wuvT.shape)]
    scratch = [
        pltpu.VMEM((nk, TK, TQ), F32),
        pltpu.VMEM((nk, TK, TQ), F32),
        pltpu.VMEM((2, DSA_HEADS, TK, TQ), BF16),
        pltpu.VMEM((DSA_HEADS, SUBLANES, TQ), F32),
        pltpu.VMEM((DSA_HEADS, CT_ROWS, TQ), F32),
    ]
    return pl.pallas_call(
        _dsa_kernel, grid=grid, in_specs=in_specs,
        out_specs=pl.BlockSpec((None, TQ, DSA_WIDTH), lambda b, i: (b, i, 0)),
        out_shape=jax.ShapeDtypeStruct((B, S, DSA_WIDTH), BF16), scratch_shapes=scratch,
        compiler_params=_params(("parallel", "arbitrary")), name="dsa",
    )(qT, qiT, wT, kidx, c, cT, nb, wuvT)


def _s5_kernel(u_ref, bm_ref, cm_ref, lre_ref, lim_ref, dsk_ref, wglu_ref, o_ref,
               uslab_ref, uil_ref, st_ref, oslab_ref, hre_ref, him_ref):
    nstate = S5_GROUPS * S5_STATE
    nb = hre_ref.shape[0]
    nslab = S5_WIDTH // LANES

    @pl.when(pl.program_id(0) == 0)
    def _():
        hre_ref[...] = jnp.zeros(hre_ref.shape, F32)
        him_ref[...] = jnp.zeros(him_ref.shape, F32)

    for b in range(nb):
        for k in range(nslab):
            uslab_ref[k, b * S5_PITCH:b * S5_PITCH + T_S5, :] = u_ref[b, :, k * LANES:(k + 1) * LANES]

    def gather_step(t, carry):
        for k in range(nslab):
            uil_ref[pl.ds(pl.multiple_of(t * nb, nb), nb), k * LANES:(k + 1) * LANES] = (
                uslab_ref[k, pl.ds(t, nb, stride=S5_PITCH), :])
        return carry

    lax.fori_loop(0, T_S5, gather_step, 0, unroll=4)
    u = uil_ref[...]
    ub = u.astype(BF16)

    kin = 4
    cin = S5_WIDTH // kin
    sin = nstate // kin
    for k in range(kin):
        uk = ub[:, k * cin:(k + 1) * cin]
        for off in (0, nstate):
            cols = slice(off + k * sin, off + (k + 1) * sin)
            st_ref[:, cols] = jnp.dot(uk, bm_ref[k * cin:(k + 1) * cin, cols],
                                      preferred_element_type=F32)

    half = nstate // 2
    for part in range(2):
        re_sl = slice(part * half, (part + 1) * half)
        im_sl = slice(nstate + part * half, nstate + (part + 1) * half)
        lre = lre_ref[:, re_sl]
        lim = lim_ref[:, re_sl]

        def step(t, carry):
            hr, hi = carry
            r = pl.ds(pl.multiple_of(t * nb, nb), nb)
            nr = lre * hr - lim * hi + st_ref[r, re_sl]
            ni = lre * hi + lim * hr + st_ref[r, im_sl]
            st_ref[r, re_sl] = nr
            st_ref[r, im_sl] = ni
            return nr, ni

        hr, hi = lax.fori_loop(0, T_S5, step, (hre_ref[:, re_sl], him_ref[:, re_sl]), unroll=4)
        hre_ref[:, re_sl] = hr
        him_ref[:, re_sl] = hi

    kout = 2
    cout = S5_WIDTH // kout
    sout = nstate // kout
    ys = []
    for k in range(kout):
        acc = None
        for off in (0, nstate):
            rows = slice(off + k * sout, off + (k + 1) * sout)
            part = jnp.dot(st_ref[:, rows].astype(BF16), cm_ref[rows, k * cout:(k + 1) * cout],
                           preferred_element_type=F32)
            acc = part if acc is None else acc + part
        ys.append(acc)
    y = jnp.concatenate(ys, axis=1)
    y = jax.nn.gelu(y + dsk_ref[...] * u)
    z = jnp.dot(y.astype(BF16), wglu_ref[...], preferred_element_type=F32)
    o = y * jax.nn.sigmoid(z)

    for k in range(nslab):
        oslab_ref[k] = o[:, k * LANES:(k + 1) * LANES]
    for b in range(nb):
        for k in range(nslab):
            o_ref[b, :, k * LANES:(k + 1) * LANES] = (
                oslab_ref[k, pl.ds(b, T_S5, stride=nb), :].astype(BF16))


def _s5(u, bm, cm, lre, lim, dsk, wglu):
    B, S, W = u.shape
    tb = T_S5 * B
    nstate = S5_GROUPS * S5_STATE
    tok = pl.BlockSpec((B, T_S5, W), lambda t: (0, t, 0))
    in_specs = [tok, _const_spec(bm.shape), _const_spec(cm.shape), _const_spec(lre.shape),
                _const_spec(lim.shape), _const_spec(dsk.shape), _const_spec(wglu.shape)]
    scratch = [pltpu.VMEM((W // LANES, B * S5_PITCH, LANES), F32),
               pltpu.VMEM((tb, W), F32),
               pltpu.VMEM((tb, 2 * nstate), F32),
               pltpu.VMEM((W // LANES, tb, LANES), F32),
               pltpu.VMEM((B, nstate), F32), pltpu.VMEM((B, nstate), F32)]
    return pl.pallas_call(
        _s5_kernel, grid=(S // T_S5,), in_specs=in_specs, out_specs=tok,
        out_shape=jax.ShapeDtypeStruct((B, S, W), BF16), scratch_shapes=scratch,
        compiler_params=_params(("arbitrary",)), name="s5",
    )(u, bm, cm, lre, lim, dsk, wglu)


def _memkv_kernel(mem_ref, gmem_ref, wkv_ref, gk_ref, k_ref, v_ref):
    mb = _rms(mem_ref[...], gmem_ref[...]).astype(BF16)
    kv = jnp.dot(mb, wkv_ref[...], preferred_element_type=F32)
    for h in range(X_HEADS):
        sl = slice(h * X_HEAD_DIM, (h + 1) * X_HEAD_DIM)
        k_ref[:, sl] = _rms(kv[:, sl], gk_ref[...]).astype(BF16)
    v_ref[...] = kv[:, X_WIDTH:].astype(BF16)


def _memkv(mem, gmem, wkv, gk):
    B, M, D = mem.shape
    blk = lambda w: pl.BlockSpec((None, M, w), lambda b: (b, 0, 0))
    return pl.pallas_call(
        _memkv_kernel, grid=(B,),
        in_specs=[blk(D), _const_spec(gmem.shape), _const_spec(wkv.shape), _const_spec(gk.shape)],
        out_specs=(blk(X_WIDTH), blk(X_WIDTH)),
        out_shape=(jax.ShapeDtypeStruct((B, M, X_WIDTH), BF16),) * 2,
        compiler_params=_params(("parallel",)), name="memkv",
    )(mem, gmem, wkv, gk)


def _merge_kernel(x_ref, odsa_ref, os5_ref, k_ref, v_ref, gmix_ref, wg_ref, wqx_ref, gqx_ref,
                  wb1_ref, wb2_ref, wb3_ref, wout_ref, y_ref):
    x = x_ref[...]
    hb = _rms(x, gmix_ref[...]).astype(BF16)

    qx = jnp.dot(hb, wqx_ref[...], preferred_element_type=F32)
    gqx = gqx_ref[...] * (X_HEAD_DIM ** -0.5)
    ox = []
    for h in range(X_HEADS):
        sl = slice(h * X_HEAD_DIM, (h + 1) * X_HEAD_DIM)
        qh = _rms(qx[:, sl], gqx).astype(BF16)
        lg = lax.dot_general(qh, k_ref[:, sl], NT_DIMS, preferred_element_type=F32)
        p = jnp.exp(lg - jnp.max(lg, axis=-1, keepdims=True))
        pv = jnp.dot(p.astype(BF16), v_ref[:, sl], preferred_element_type=F32)
        ox.append((pv / jnp.sum(p, axis=-1, keepdims=True)).astype(BF16))
    ox = jnp.concatenate(ox, axis=1)

    merged = None
    for br, (o, wb) in enumerate(((odsa_ref[...], wb1_ref), (os5_ref[...], wb2_ref), (ox, wb3_ref))):
        gate = jax.nn.sigmoid(jnp.dot(hb, wg_ref[:, br * D_MODEL:(br + 1) * D_MODEL],
                                      preferred_element_type=F32))
        term = gate * jnp.dot(o, wb[...], preferred_element_type=F32)
        merged = term if merged is None else merged + term
    y_ref[...] = x + jnp.dot(merged.astype(BF16), wout_ref[...], preferred_element_type=F32)


def _merge(x, odsa, os5, k, v, gmix, wg, wqx, gqx, wb1, wb2, wb3, wout):
    B, S, D = x.shape
    ts = TS_MERGE
    tok = lambda w: pl.BlockSpec((None, ts, w), lambda b, s: (b, s, 0))
    memspec = pl.BlockSpec((None, N_MEM, X_WIDTH), lambda b, s: (b, 0, 0))
    in_specs = [tok(D), tok(DSA_WIDTH),
                tok(S5_WIDTH),
                memspec, memspec] + [_const_spec(a.shape) for a in
                                     (gmix, wg, wqx, gqx, wb1, wb2, wb3, wout)]
    return pl.pallas_call(
        _merge_kernel, grid=(B, S // ts), in_specs=in_specs, out_specs=tok(D),
        out_shape=jax.ShapeDtypeStruct((B, S, D), F32),
        compiler_params=_params(("parallel", "parallel")), name="merge",
    )(x, odsa, os5, k, v, gmix, wg, wqx, gqx, wb1, wb2, wb3, wout)


def _ffn_kernel(x_ref, g_ref, wg_ref, wu_ref, wd_ref, y_ref):
    x = x_ref[...]
    hb = _rms(x, g_ref[...]).astype(BF16)
    a = jnp.dot(hb, wg_ref[...], preferred_element_type=F32)
    b = jnp.dot(hb, wu_ref[...], preferred_element_type=F32)
    act = (jax.nn.silu(a) * b).astype(BF16)
    y_ref[...] = x + jnp.dot(act, wd_ref[...], preferred_element_type=F32)


def _ffn(x2, g, wg, wu, wd):
    n, D = x2.shape
    tok = pl.BlockSpec((TS_FFN, D), lambda t: (t, 0))
    return pl.pallas_call(
        _ffn_kernel, grid=(n // TS_FFN,),
        in_specs=[tok] + [_const_spec(a.shape) for a in (g, wg, wu, wd)],
        out_specs=tok, out_shape=jax.ShapeDtypeStruct((n, D), F32),
        compiler_params=_params(("parallel",)), name="ffn",
    )(x2, g, wg, wu, wd)


def _t5_bucket(n):
    max_exact = REL_BUCKETS // 2
    nf = jnp.maximum(n, 1).astype(F32)
    large = max_exact + (jnp.log(nf / max_exact) / math.log(REL_MAX_DIST / max_exact)
                         * (REL_BUCKETS - max_exact)).astype(jnp.int32)
    large = jnp.minimum(large, REL_BUCKETS - 1)
    return jnp.where(n < max_exact, n, large)


def _toeplitz(w, rows, cols):
    H, L = w.shape
    flat = jnp.tile(w, (1, rows))[:, :rows * (L - 1)]
    return flat.reshape(H, rows, L - 1)[:, :, :cols]


def _near_bias(rel_bias):
    n = jnp.arange(2 * TQ, dtype=jnp.int32)
    f = (rel_bias[_t5_bucket(n)] - rel_bias[REL_BUCKETS - 1][None, :]).T * LOG2E
    w_diag = jnp.concatenate([f[:, :TQ], jnp.broadcast_to(f[:, :1], (f.shape[0], TK - 1))], axis=1)
    w_prev = jnp.concatenate([f[:, TQ:2 * TQ], f[:, 1:TK]], axis=1)
    return jnp.stack([_toeplitz(w_diag, TK, TQ), _toeplitz(w_prev, TK, TQ)], axis=0).astype(F32)


def _s5_mats(a_re, a_im, log_dt, b_re, b_im, c_re, c_im):
    lam = lax.complex(a_re.astype(F32), a_im.astype(F32))
    dt = jnp.exp(log_dt.astype(F32))[:, None]
    lam_bar = jnp.exp(lam * dt)
    b_bar = ((lam_bar - 1.0) / lam)[..., None] * lax.complex(b_re.astype(F32), b_im.astype(F32))
    nstate = S5_GROUPS * S5_STATE
    in_mask = (jnp.arange(S5_WIDTH)[:, None] // S5_GROUP) == (jnp.arange(nstate)[None, :] // S5_STATE)

    def blockdiag_in(w):
        t = jnp.transpose(w, (0, 2, 1)).reshape(S5_WIDTH, S5_STATE)
        return jnp.where(in_mask, jnp.tile(t, (1, S5_GROUPS)), 0.0)

    def blockdiag_out(w):
        t = jnp.transpose(w, (0, 2, 1)).reshape(nstate, S5_GROUP)
        return jnp.where(in_mask.T, jnp.tile(t, (1, S5_GROUPS)), 0.0)

    bm = jnp.concatenate([blockdiag_in(jnp.real(b_bar)), blockdiag_in(jnp.imag(b_bar))], axis=1)
    cm = jnp.concatenate([blockdiag_out(c_re.astype(F32)), blockdiag_out(-c_im.astype(F32))], axis=0)
    return bm.astype(BF16), cm.astype(BF16), jnp.real(lam_bar).reshape(1, -1), jnp.imag(lam_bar).reshape(1, -1)


def kernel(x, mem, rel_bias, w_in, g_mix_norm, g_q_dsa, g_kv_dsa, w_uv_dsa, a_re, a_im, log_dt, b_re, b_im, c_re, c_im, d_skip, w_glu, g_mem_norm, w_mem_kv, g_q_cross, g_k_cross, w_br_dsa, w_br_s5, w_br_cross, w_out, g_ffn_norm, w_ffn_gate, w_ffn_up, w_ffn_down):
    B, S, D = x.shape
    depth = w_in.shape[0]
    offs = [0] + [int(o) for o in np.cumsum(IN_SPLITS)]
    nb = _near_bias(rel_bias)
    row = lambda v: v.reshape(1, -1).astype(F32)
    col = lambda v: v.reshape(-1, 1).astype(F32)
    bf = lambda a: a.astype(BF16)
    for l in range(depth):
        w = bf(w_in[l])
        wq, wc, wqi, wk, ww, wu, wqx, wg = [w[:, offs[k]:offs[k + 1]] for k in range(len(IN_SPLITS))]
        wwT = jnp.pad(ww.T, ((0, BF16_ROWS - IDX_HEADS), (0, 0)))
        gqc = col(g_q_dsa[l]) * (DSA_LATENT ** -0.5 * LOG2E)

        qT, cT, c, qiT, wT, kidx, u = _inproj(
            x, row(g_mix_norm[l]), wq.T, wc, wqi.T, wwT, wk, wu, gqc, row(g_kv_dsa[l]))

        wuvT = bf(jnp.transpose(w_uv_dsa[l], (0, 2, 1)))
        o_dsa = _dsa(qT, qiT, wT, kidx, c, cT, nb, wuvT)

        bm, cm, lre, lim = _s5_mats(a_re[l], a_im[l], log_dt[l], b_re[l], b_im[l], c_re[l], c_im[l])
        lre = jnp.broadcast_to(lre, (B, lre.shape[1]))
        lim = jnp.broadcast_to(lim, (B, lim.shape[1]))
        o_s5 = _s5(u, bm, cm, lre, lim, row(d_skip[l]), bf(w_glu[l]))

        k, v = _memkv(mem, row(g_mem_norm[l]), bf(w_mem_kv[l]), row(g_k_cross[l]))

        x1 = _merge(x, o_dsa, o_s5, k, v, row(g_mix_norm[l]), wg, wqx, row(g_q_cross[l]),
                    bf(w_br_dsa[l]), bf(w_br_s5[l]), bf(w_br_cross[l]), bf(w_out[l]))

        x = _ffn(x1.reshape(B * S, D), row(g_ffn_norm[l]), bf(w_ffn_gate[l]), bf(w_ffn_up[l]),
                 bf(w_ffn_down[l])).reshape(B, S, D)
    return x
```

```python
import math

import jax
import jax.numpy as jnp
import numpy as np
from jax import lax
from jax.experimental import pallas as pl
from jax.experimental.pallas import tpu as pltpu

F32 = jnp.float32
BF16 = jnp.bfloat16

D_MODEL = 1024
N_MEM = 256
EPS = 1e-6
DSA_HEADS = 8
DSA_LATENT = 128
DSA_VDIM = 64
IDX_HEADS = 8
IDX_DIM = 64
TOPK_MAX = 256
REL_BUCKETS = 32
REL_MAX_DIST = 128
S5_WIDTH = 512
S5_GROUP = 16
S5_GROUPS = S5_WIDTH // S5_GROUP
S5_STATE = 64
X_HEADS = 4
X_HEAD_DIM = 128
DSA_WIDTH = DSA_HEADS * DSA_VDIM
X_WIDTH = X_HEADS * X_HEAD_DIM
N_BRANCH = 3
IN_SPLITS = (DSA_HEADS * DSA_LATENT, DSA_LATENT, IDX_HEADS * IDX_DIM, IDX_DIM,
             IDX_HEADS, S5_WIDTH, X_WIDTH, N_BRANCH * D_MODEL)

LANES = 128
SUBLANES = 8
BF16_ROWS = 16
CT_ROWS = DSA_LATENT + BF16_ROWS
VMEM_LIMIT = 56 * 1024 * 1024
LOG2E = math.log2(math.e)

TS_IN = 1024
TQ = 256
TK = 256
N_BISECT = 15
T_S5 = 128
S5_PITCH = T_S5 + SUBLANES
TS_MERGE = 1024
TS_FFN = 512
NEG = -(2.0 ** 100)

NT_DIMS = (((1,), (1,)), ((), ()))


def _rms(x, g):
    ms = jnp.mean(x * x, axis=-1, keepdims=True)
    return x * lax.rsqrt(ms + EPS) * g


def _tree(fn, x):
    while x.shape[0] > 1:
        half = x.shape[0] // 2
        x = fn(x[:half], x[half:])
    return x[0]


def _const_spec(shape):
    nd = len(shape)
    return pl.BlockSpec(shape, lambda *_: (0,) * nd, pipeline_mode=pl.Buffered(1))


def _params(sem):
    return pltpu.CompilerParams(dimension_semantics=sem, vmem_limit_bytes=VMEM_LIMIT)


def _inproj_kernel(x_ref, gmix_ref, wqT_ref, wc_ref, wqiT_ref, wwT_ref, wk_ref, wu_ref,
                   gqc_ref, gkv_ref,
                   qT_ref, cT_ref, c_ref, qiT_ref, wT_ref, kidx_ref, u_ref):
    ts = x_ref.shape[0]
    hb = _rms(x_ref[...], gmix_ref[...]).astype(BF16)

    qT = lax.dot_general(wqT_ref[...], hb, NT_DIMS, preferred_element_type=F32)
    q3 = qT.reshape(DSA_HEADS, DSA_LATENT, ts)
    ms = jnp.mean(q3 * q3, axis=1, keepdims=True)
    qT_ref[...] = (q3 * lax.rsqrt(ms + EPS) * gqc_ref[...][None]).reshape(qT.shape).astype(BF16)

    cn = _rms(jnp.dot(hb, wc_ref[...], preferred_element_type=F32), gkv_ref[...])
    c_ref[...] = cn.astype(BF16)
    cTn = cn.T.astype(BF16)
    for k in range(ts // TK):
        cT_ref[k, :DSA_LATENT, :] = cTn[:, k * TK:(k + 1) * TK]
        cT_ref[k, DSA_LATENT:, :] = jnp.ones((BF16_ROWS, TK), BF16)

    qiT_ref[...] = lax.dot_general(wqiT_ref[...], hb, NT_DIMS, preferred_element_type=F32).astype(BF16)
    wT_ref[...] = lax.dot_general(wwT_ref[...], hb, NT_DIMS, preferred_element_type=F32)
    kidx_ref[...] = jnp.dot(hb, wk_ref[...], preferred_element_type=F32).astype(BF16)

    u_ref[...] = jnp.dot(hb, wu_ref[...], preferred_element_type=F32)


def _inproj(x, gmix, wqT, wc, wqiT, wwT, wk, wu, gqc, gkv):
    B, S, D = x.shape
    ts = TS_IN
    grid = (B, S // ts)
    tok = lambda w: pl.BlockSpec((None, ts, w), lambda b, s: (b, s, 0))
    tokT = lambda r: pl.BlockSpec((None, r, ts), lambda b, s: (b, 0, s))
    hq = DSA_HEADS * DSA_LATENT
    hi = IDX_HEADS * IDX_DIM
    out_shape = (
        jax.ShapeDtypeStruct((B, hq, S), BF16),
        jax.ShapeDtypeStruct((B, S // TK, CT_ROWS, TK), BF16),
        jax.ShapeDtypeStruct((B, S, DSA_LATENT), BF16),
        jax.ShapeDtypeStruct((B, hi, S), BF16),
        jax.ShapeDtypeStruct((B, BF16_ROWS, S), F32),
        jax.ShapeDtypeStruct((B, S, IDX_DIM), BF16),
        jax.ShapeDtypeStruct((B, S, S5_WIDTH), F32),
    )
    out_specs = (tokT(hq),
                 pl.BlockSpec((None, ts // TK, CT_ROWS, TK), lambda b, s: (b, s, 0, 0)),
                 tok(DSA_LATENT), tokT(hi), tokT(BF16_ROWS), tok(IDX_DIM),
                 tok(S5_WIDTH))
    consts = (gmix, wqT, wc, wqiT, wwT, wk, wu, gqc, gkv)
    in_specs = [tok(D)] + [_const_spec(a.shape) for a in consts]
    return pl.pallas_call(
        _inproj_kernel, grid=grid, in_specs=in_specs, out_specs=out_specs, out_shape=out_shape,
        compiler_params=_params(("parallel", "parallel")), name="inproj",
    )(x, *consts)


def _dsa_kernel(qT_ref, qiT_ref, wT_ref, kidx_ref, c_ref, cT_ref, nb_ref, wuvT_ref, o_ref,
                sc_ref, pref_ref, lg_ref, m_ref, acc_ref):
    i = pl.program_id(1)
    nk = sc_ref.shape[0]
    kf = float(TOPK_MAX)
    G = TK // SUBLANES

    def rep(fn, a):
        return jnp.broadcast_to(fn(a, axis=0, keepdims=True), (SUBLANES, TQ))

    def full(v, dt=F32):
        return jnp.full((SUBLANES, TQ), v, dt)

    def key_rows(j):
        return pl.ds(pl.multiple_of(j * TK, TK), TK)

    def score(j):
        ks = kidx_ref[key_rows(j), :]
        acc = None
        for h in range(IDX_HEADS):
            d = jnp.dot(ks, qiT_ref[h * IDX_DIM:(h + 1) * IDX_DIM, :], preferred_element_type=F32)
            t = jnp.maximum(d, 0.0) * wT_ref[h:h + 1, :]
            acc = t if acc is None else acc + t
        return acc

    def score_pair(p, carry):
        sc_ref[2 * p] = score(2 * p)
        sc_ref[2 * p + 1] = score(2 * p + 1)
        return carry

    lax.fori_loop(0, lax.shift_right_logical(i, 1), score_pair, 0)

    @pl.when((i & 1) == 1)
    def _():
        sc_ref[i - 1] = score(i - 1)

    key_t = lax.broadcasted_iota(jnp.int32, (TK, TQ), 0)
    qry_t = lax.broadcasted_iota(jnp.int32, (TK, TQ), 1)
    causal = key_t <= qry_t
    sc_ref[i] = jnp.where(causal, score(i), -jnp.inf)

    @pl.when(i == 0)
    def _():
        sc_ref[0] = jnp.where(causal, 0.0, NEG)

    @pl.when(i > 0)
    def _():
        nt = i + 1

        def tile3(j):
            return sc_ref[j].reshape(G, SUBLANES, TQ)

        def minmax_body(j, carry):
            mn, mx = carry
            s = tile3(j)
            mx = jnp.maximum(mx, _tree(jnp.maximum, s))
            mn = jnp.minimum(mn, _tree(jnp.minimum, jnp.where(s == -jnp.inf, jnp.inf, s)))
            return mn, mx

        mn, mx = lax.fori_loop(0, nt, minmax_body, (full(jnp.inf), full(-jnp.inf)))
        lo, hi = rep(jnp.min, mn), rep(jnp.max, mx)

        def count_ge(thr):
            def body(j, acc):
                return acc + _tree(jnp.add, jnp.where(tile3(j) >= thr[None], 1.0, 0.0))
            return rep(jnp.sum, lax.fori_loop(0, nt, body, full(0.0)))

        def bisect(_, carry):
            lo, hi, clo = carry
            mid = 0.5 * (lo + hi)
            cnt = count_ge(mid)
            ge = cnt >= kf
            return jnp.where(ge, mid, lo), jnp.where(ge, hi, mid), jnp.where(ge, cnt, clo)

        nvis = (i * TQ + 1 + lax.broadcasted_iota(jnp.int32, (SUBLANES, TQ), 1)).astype(F32)
        lo, hi, clo = lax.fori_loop(0, N_BISECT, bisect, (lo, hi, nvis))

        def snap_body(j, am):
            s = tile3(j)
            return jnp.minimum(am, _tree(jnp.minimum, jnp.where(s >= lo[None], s, jnp.inf)))

        cur = rep(jnp.min, lax.fori_loop(0, nt, snap_body, full(jnp.inf)))

        def walk(cur):
            def body(j, carry):
                ac, am = carry
                s = tile3(j)
                g = s > cur[None]
                ac = ac + _tree(jnp.add, jnp.where(g, 1.0, 0.0))
                am = jnp.minimum(am, _tree(jnp.minimum, jnp.where(g, s, jnp.inf)))
                return ac, am
            ac, am = lax.fori_loop(0, nt, body, (full(0.0), full(jnp.inf)))
            return rep(jnp.sum, ac), rep(jnp.min, am)

        def walk_cond(carry):
            _, _, _, go, it = carry
            return jnp.logical_and(go > 0, it < nk * TK + 2)

        def walk_body(carry):
            cur, cge, _, _, it = carry
            c, nxt = walk(cur)
            move = c >= kf
            go = (jnp.max(jnp.where(move, 1.0, 0.0)) > 0.5).astype(jnp.int32)
            return jnp.where(move, nxt, cur), jnp.where(move, c, cge), c, go, it + 1

        kth, cge, cgt, _, _ = lax.while_loop(
            walk_cond, walk_body, (cur, clo, full(0.0), jnp.int32(1), jnp.int32(0)))
        need = kf - cgt
        has_excess = jnp.max(jnp.where(cge > kf, 1.0, 0.0)) > 0.5

        @pl.when(jnp.logical_not(has_excess))
        def _():
            def body(j, carry):
                sc_ref[j] = jnp.where(tile3(j) >= kth[None], 0.0, NEG).reshape(TK, TQ)
                return carry
            lax.fori_loop(0, nt, body, 0)

        @pl.when(has_excess)
        def _():
            tril = jnp.where(key_t >= qry_t, 1.0, 0.0).astype(BF16)

            def pref_body(p, carry):
                for j in (2 * p, jnp.minimum(2 * p + 1, nt - 1)):
                    e01 = jnp.where(tile3(j) == kth[None], 1.0, 0.0).reshape(TK, TQ).astype(BF16)
                    pref_ref[j] = jnp.dot(tril, e01, preferred_element_type=F32)
                return carry

            lax.fori_loop(0, lax.shift_right_logical(nt + 1, 1), pref_body, 0)

            def mask_body(j, offset):
                s = tile3(j)
                pref = pref_ref[j]
                rank = pref.reshape(G, SUBLANES, TQ) + offset[None]
                tie = jnp.where(rank <= need[None], 0.0, NEG)
                mb = jnp.where(s > kth[None], 0.0, jnp.where(s == kth[None], tie, NEG))
                sc_ref[j] = mb.reshape(TK, TQ)
                return offset + jnp.broadcast_to(pref[TK - 1:TK, :], (SUBLANES, TQ))

            lax.fori_loop(0, nt, mask_body, full(0.0))

    m_ref[...] = jnp.full(m_ref.shape, NEG, F32)
    acc_ref[...] = jnp.zeros(acc_ref.shape, F32)
    LG = CT_ROWS // SUBLANES

    RB = TK // BF16_ROWS

    def attend(tiles):
        m_run = [m_ref[h] for h in range(DSA_HEADS)]
        stats = []
        for slot, (j, near) in enumerate(tiles):
            ct = c_ref[key_rows(j), :]
            mbb = sc_ref[j].astype(BF16)
            per_head = []
            for h in range(DSA_HEADS):
                lg = jnp.dot(ct, qT_ref[h * DSA_LATENT:(h + 1) * DSA_LATENT, :],
                             preferred_element_type=F32)
                if near is not None:
                    lg = lg + nb_ref[near, h]
                lgb = lg.astype(BF16) + mbb
                lg_ref[slot, h] = lgb
                tmax = _tree(jnp.maximum, lgb.reshape(RB, BF16_ROWS, TQ)).astype(F32)
                m_new = jnp.maximum(m_run[h], rep(jnp.max, tmax))
                per_head.append((m_run[h], m_new))
                m_run[h] = m_new
            stats.append(per_head)
        for slot, (j, near) in enumerate(tiles):
            ctT = cT_ref[j]
            for h in range(DSA_HEADS):
                m_old, m_new = stats[slot][h]
                alpha = jnp.exp2(m_old - m_new)
                m16 = jnp.concatenate([m_new, m_new], axis=0).astype(BF16)
                x = lg_ref[slot, h].reshape(RB, BF16_ROWS, TQ) - m16[None]
                pv = jnp.dot(ctT, jnp.exp2(x).reshape(TK, TQ), preferred_element_type=F32)
                acc3 = acc_ref[h].reshape(LG, SUBLANES, TQ) * alpha[None]
                acc_ref[h] = acc3.reshape(CT_ROWS, TQ) + pv
        for h in range(DSA_HEADS):
            m_ref[h] = m_run[h]

    nfar = jnp.maximum(i - 1, 0)

    def far_pair(p, carry):
        attend([(2 * p, None), (2 * p + 1, None)])
        return carry

    lax.fori_loop(0, lax.shift_right_logical(nfar, 1), far_pair, 0)

    @pl.when((nfar & 1) == 1)
    def _():
        attend([(nfar - 1, None)])

    @pl.when(i > 0)
    def _():
        attend([(i - 1, 1), (i, 0)])

    @pl.when(i == 0)
    def _():
        attend([(0, 0)])

    outs = []
    for h in range(DSA_HEADS):
        rl = 1.0 / acc_ref[h, DSA_LATENT:DSA_LATENT + SUBLANES, :]
        o3 = acc_ref[h, :DSA_LATENT, :].reshape(DSA_LATENT // SUBLANES, SUBLANES, TQ) * rl[None]
        o = o3.reshape(DSA_LATENT, TQ).astype(BF16)
        outs.append(jnp.dot(wuvT_ref[h], o, preferred_element_type=F32))
    o_ref[...] = jnp.concatenate(outs, axis=0).T.astype(BF16)


def _dsa(qT, qiT, wT, kidx, c, cT, nb, wuvT):
    B, S, _ = c.shape
    nk = S // TK
    grid = (B, S // TQ)
    tileT = lambda r: pl.BlockSpec((None, r, TQ), lambda b, i: (b, 0, i))
    seq = lambda w: pl.BlockSpec((None, S, w), lambda b, i: (b, 0, 0))
    in_specs = [tileT(DSA_HEADS * DSA_LATENT), tileT(IDX_HEADS * IDX_DIM), tileT(BF16_ROWS),
                seq(IDX_DIM), seq(DSA_LATENT),
                pl.BlockSpec((None, nk, CT_ROWS, TK), lambda b, i: (b, 0, 0, 0)),
                _const_spec(nb.shape), _const_spec(wuvT.shape)]
    scratch = [
        pltpu.VMEM((nk, TK, TQ), F32),
        pltpu.VMEM((nk, TK, TQ), F32),
        pltpu.VMEM((2, DSA_HEADS, TK, TQ), BF16),
        pltpu.VMEM((DSA_HEADS, SUBLANES, TQ), F32),
        pltpu.VMEM((DSA_HEADS, CT_ROWS, TQ), F32),
    ]
    return pl.pallas_call(
        _dsa_kernel, grid=grid, in_specs=in_specs,
        out_specs=pl.BlockSpec((None, TQ, DSA_WIDTH), lambda b, i: (b, i, 0)),
        out_shape=jax.ShapeDtypeStruct((B, S, DSA_WIDTH), BF16), scratch_shapes=scratch,
        compiler_params=_params(("parallel", "arbitrary")), name="dsa",
    )(qT, qiT, wT, kidx, c, cT, nb, wuvT)


def _s5_kernel(u_ref, bm_ref, cm_ref, lre_ref, lim_ref, dsk_ref, wglu_ref, o_ref,
               uslab_ref, uil_ref, st_ref, oslab_ref, hre_ref, him_ref):
    nstate = S5_GROUPS * S5_STATE
    nb = hre_ref.shape[0]
    nslab = S5_WIDTH // LANES

    @pl.when(pl.program_id(0) == 0)
    def _():
        hre_ref[...] = jnp.zeros(hre_ref.shape, F32)
        him_ref[...] = jnp.zeros(him_ref.shape, F32)

    for b in range(nb):
        for k in range(nslab):
            uslab_ref[k, b * S5_PITCH:b * S5_PITCH + T_S5, :] = u_ref[b, :, k * LANES:(k + 1) * LANES]

    def gather_step(t, carry):
        for k in range(nslab):
            uil_ref[pl.ds(pl.multiple_of(t * nb, nb), nb), k * LANES:(k + 1) * LANES] = (
                uslab_ref[k, pl.ds(t, nb, stride=S5_PITCH), :])
        return carry

    lax.fori_loop(0, T_S5, gather_step, 0, unroll=4)
    u = uil_ref[...]
    ub = u.astype(BF16)

    kin = 4
    cin = S5_WIDTH // kin
    sin = nstate // kin
    for k in range(kin):
        uk = ub[:, k * cin:(k + 1) * cin]
        for off in (0, nstate):
            cols = slice(off + k * sin, off + (k + 1) * sin)
            st_ref[:, cols] = jnp.dot(uk, bm_ref[k * cin:(k + 1) * cin, cols],
                                      preferred_element_type=F32)

    half = nstate // 2
    for part in range(2):
        re_sl = slice(part * half, (part + 1) * half)
        im_sl = slice(nstate + part * half, nstate + (part + 1) * half)
        lre = lre_ref[:, re_sl]
        lim = lim_ref[:, re_sl]

        def step(t, carry):
            hr, hi = carry
            r = pl.ds(pl.multiple_of(t * nb, nb), nb)
            nr = lre * hr - lim * hi + st_ref[r, re_sl]
            ni = lre * hi + lim * hr + st_ref[r, im_sl]
            st_ref[r, re_sl] = nr
            st_ref[r, im_sl] = ni
            return nr, ni

        hr, hi = lax.fori_loop(0, T_S5, step, (hre_ref[:, re_sl], him_ref[:, re_sl]), unroll=4)
        hre_ref[:, re_sl] = hr
        him_ref[:, re_sl] = hi

    kout = 2
    cout = S5_WIDTH // kout
    sout = nstate // kout
    ys = []
    for k in range(kout):
        acc = None
        for off in (0, nstate):
            rows = slice(off + k * sout, off + (k + 1) * sout)
            part = jnp.dot(st_ref[:, rows].astype(BF16), cm_ref[rows, k * cout:(k + 1) * cout],
                           preferred_element_type=F32)
            acc = part if acc is None else acc + part
        ys.append(acc)
    y = jnp.concatenate(ys, axis=1)
    y = jax.nn.gelu(y + dsk_ref[...] * u)
    z = jnp.dot(y.astype(BF16), wglu_ref[...], preferred_element_type=F32)
    o = y * jax.nn.sigmoid(z)

    for k in range(nslab):
        oslab_ref[k] = o[:, k * LANES:(k + 1) * LANES]
    for b in range(nb):
        for k in range(nslab):
            o_ref[b, :, k * LANES:(k + 1) * LANES] = (
                oslab_ref[k, pl.ds(b, T_S5, stride=nb), :].astype(BF16))


def _s5(u, bm, cm, lre, lim, dsk, wglu):
    B, S, W = u.shape
    tb = T_S5 * B
    nstate = S5_GROUPS * S5_STATE
    tok = pl.BlockSpec((B, T_S5, W), lambda t: (0, t, 0))
    in_specs = [tok, _const_spec(bm.shape), _const_spec(cm.shape), _const_spec(lre.shape),
                _const_spec(lim.shape), _const_spec(dsk.shape), _const_spec(wglu.shape)]
    scratch = [pltpu.VMEM((W // LANES, B * S5_PITCH, LANES), F32),
               pltpu.VMEM((tb, W), F32),
               pltpu.VMEM((tb, 2 * nstate), F32),
               pltpu.VMEM((W // LANES, tb, LANES), F32),
               pltpu.VMEM((B, nstate), F32), pltpu.VMEM((B, nstate), F32)]
    return pl.pallas_call(
        _s5_kernel, grid=(S // T_S5,), in_specs=in_specs, out_specs=tok,
        out_shape=jax.ShapeDtypeStruct((B, S, W), BF16), scratch_shapes=scratch,
        compiler_params=_params(("arbitrary",)), name="s5",
    )(u, bm, cm, lre, lim, dsk, wglu)


def _memkv_kernel(mem_ref, gmem_ref, wkv_ref, gk_ref, k_ref, v_ref):
    mb = _rms(mem_ref[...], gmem_ref[...]).astype(BF16)
    kv = jnp.dot(mb, wkv_ref[...], preferred_element_type=F32)
    for h in range(X_HEADS):
        sl = slice(h * X_HEAD_DIM, (h + 1) * X_HEAD_DIM)
        k_ref[:, sl] = _rms(kv[:, sl], gk_ref[...]).astype(BF16)
    v_ref[...] = kv[:, X_WIDTH:].astype(BF16)


def _memkv(mem, gmem, wkv, gk):
    B, M, D = mem.shape
    blk = lambda w: pl.BlockSpec((None, M, w), lambda b: (b, 0, 0))
    return pl.pallas_call(
        _memkv_kernel, grid=(B,),
        in_specs=[blk(D), _const_spec(gmem.shape), _const_spec(wkv.shape), _const_spec(gk.shape)],
        out_specs=(blk(X_WIDTH), blk(X_WIDTH)),
        out_shape=(jax.ShapeDtypeStruct((B, M, X_WIDTH), BF16),) * 2,
        compiler_params=_params(("parallel",)), name="memkv",
    )(mem, gmem, wkv, gk)


def _merge_kernel(x_ref, odsa_ref, os5_ref, k_ref, v_ref, gmix_ref, wg_ref, wqx_ref, gqx_ref,
                  wb1_ref, wb2_ref, wb3_ref, wout_ref, y_ref):
    x = x_ref[...]
    hb = _rms(x, gmix_ref[...]).astype(BF16)

    qx = jnp.dot(hb, wqx_ref[...], preferred_element_type=F32)
    gqx = gqx_ref[...] * (X_HEAD_DIM ** -0.5)
    ox = []
    for h in range(X_HEADS):
        sl = slice(h * X_HEAD_DIM, (h + 1) * X_HEAD_DIM)
        qh = _rms(qx[:, sl], gqx).astype(BF16)
        lg = lax.dot_general(qh, k_ref[:, sl], NT_DIMS, preferred_element_type=F32)
        p = jnp.exp(lg - jnp.max(lg, axis=-1, keepdims=True))
        pv = jnp.dot(p.astype(BF16), v_ref[:, sl], preferred_element_type=F32)
        ox.append((pv / jnp.sum(p, axis=-1, keepdims=True)).astype(BF16))
    ox = jnp.concatenate(ox, axis=1)

    merged = None
    for br, (o, wb) in enumerate(((odsa_ref[...], wb1_ref), (os5_ref[...], wb2_ref), (ox, wb3_ref))):
        gate = jax.nn.sigmoid(jnp.dot(hb, wg_ref[:, br * D_MODEL:(br + 1) * D_MODEL],
                                      preferred_element_type=F32))
        term = gate * jnp.dot(o, wb[...], preferred_element_type=F32)
        merged = term if merged is None else merged + term
    y_ref[...] = x + jnp.dot(merged.astype(BF16), wout_ref[...], preferred_element_type=F32)


def _merge(x, odsa, os5, k, v, gmix, wg, wqx, gqx, wb1, wb2, wb3, wout):
    B, S, D = x.shape
    ts = TS_MERGE
    tok = lambda w: pl.BlockSpec((None, ts, w), lambda b, s: (b, s, 0))
    memspec = pl.BlockSpec((None, N_MEM, X_WIDTH), lambda b, s: (b, 0, 0))
    in_specs = [tok(D), tok(DSA_WIDTH),
                tok(S5_WIDTH),
                memspec, memspec] + [_const_spec(a.shape) for a in
                                     (gmix, wg, wqx, gqx, wb1, wb2, wb3, wout)]
    return pl.pallas_call(
        _merge_kernel, grid=(B, S // ts), in_specs=in_specs, out_specs=tok(D),
        out_shape=jax.ShapeDtypeStruct((B, S, D), F32),
        compiler_params=_params(("parallel", "parallel")), name="merge",
    )(x, odsa, os5, k, v, gmix, wg, wqx, gqx, wb1, wb2, wb3, wout)


def _ffn_kernel(x_ref, g_ref, wg_ref, wu_ref, wd_ref, y_ref):
    x = x_ref[...]
    hb = _rms(x, g_ref[...]).astype(BF16)
    a = jnp.dot(hb, wg_ref[...], preferred_element_type=F32)
    b = jnp.dot(hb, wu_ref[...], preferred_element_type=F32)
    act = (jax.nn.silu(a) * b).astype(BF16)
    y_ref[...] = x + jnp.dot(act, wd_ref[...], preferred_element_type=F32)


def _ffn(x2, g, wg, wu, wd):
    n, D = x2.shape
    tok = pl.BlockSpec((TS_FFN, D), lambda t: (t, 0))
    return pl.pallas_call(
        _ffn_kernel, grid=(n // TS_FFN,),
        in_specs=[tok] + [_const_spec(a.shape) for a in (g, wg, wu, wd)],
        out_specs=tok, out_shape=jax.ShapeDtypeStruct((n, D), F32),
        compiler_params=_params(("parallel",)), name="ffn",
    )(x2, g, wg, wu, wd)


def _t5_bucket(n):
    max_exact = REL_BUCKETS // 2
    nf = jnp.maximum(n, 1).astype(F32)
    large = max_exact + (jnp.log(nf / max_exact) / math.log(REL_MAX_DIST / max_exact)
                         * (REL_BUCKETS - max_exact)).astype(jnp.int32)
    large = jnp.minimum(large, REL_BUCKETS - 1)
    return jnp.where(n < max_exact, n, large)


def _toeplitz(w, rows, cols):
    H, L = w.shape
    flat = jnp.tile(w, (1, rows))[:, :rows * (L - 1)]
    return flat.reshape(H, rows, L - 1)[:, :, :cols]


def _near_bias(rel_bias):
    n = jnp.arange(2 * TQ, dtype=jnp.int32)
    f = (rel_bias[_t5_bucket(n)] - rel_bias[REL_BUCKETS - 1][None, :]).T * LOG2E
    w_diag = jnp.concatenate([f[:, :TQ], jnp.broadcast_to(f[:, :1], (f.shape[0], TK - 1))], axis=1)
    w_prev = jnp.concatenate([f[:, TQ:2 * TQ], f[:, 1:TK]], axis=1)
    return jnp.stack([_toeplitz(w_diag, TK, TQ), _toeplitz(w_prev, TK, TQ)], axis=0).astype(F32)


def _s5_mats(a_re, a_im, log_dt, b_re, b_im, c_re, c_im):
    lam = lax.complex(a_re.astype(F32), a_im.astype(F32))
    dt = jnp.exp(log_dt.astype(F32))[:, None]
    lam_bar = jnp.exp(lam * dt)
    b_bar = ((lam_bar - 1.0) / lam)[..., None] * lax.complex(b_re.astype(F32), b_im.astype(F32))
    nstate = S5_GROUPS * S5_STATE
    in_mask = (jnp.arange(S5_WIDTH)[:, None] // S5_GROUP) == (jnp.arange(nstate)[None, :] // S5_STATE)

    def blockdiag_in(w):
        t = jnp.transpose(w, (0, 2, 1)).reshape(S5_WIDTH, S5_STATE)
        return jnp.where(in_mask, jnp.tile(t, (1, S5_GROUPS)), 0.0)

    def blockdiag_out(w):
        t = jnp.transpose(w, (0, 2, 1)).reshape(nstate, S5_GROUP)
        return jnp.where(in_mask.T, jnp.tile(t, (1, S5_GROUPS)), 0.0)

    bm = jnp.concatenate([blockdiag_in(jnp.real(b_bar)), blockdiag_in(jnp.imag(b_bar))], axis=1)
    cm = jnp.concatenate([blockdiag_out(c_re.astype(F32)), blockdiag_out(-c_im.astype(F32))], axis=0)
    return bm.astype(BF16), cm.astype(BF16), jnp.real(lam_bar).reshape(1, -1), jnp.imag(lam_bar).reshape(1, -1)


def kernel(x, mem, rel_bias, w_in, g_mix_norm, g_q_dsa, g_kv_dsa, w_uv_dsa, a_re, a_im, log_dt, b_re, b_im, c_re, c_im, d_skip, w_glu, g_mem_norm, w_mem_kv, g_q_cross, g_k_cross, w_br_dsa, w_br_s5, w_br_cross, w_out, g_ffn_norm, w_ffn_gate, w_ffn_up, w_ffn_down):
    B, S, D = x.shape
    depth = w_in.shape[0]
    offs = [0] + [int(o) for o in np.cumsum(IN_SPLITS)]
    nb = _near_bias(rel_bias)
    row = lambda v: v.reshape(1, -1).astype(F32)
    col = lambda v: v.reshape(-1, 1).astype(F32)
    bf = lambda a: a.astype(BF16)
    for l in range(depth):
        wq, wc, wqi, wk, ww, wu, wqx, wg = [bf(w_in[l, :, offs[k]:offs[k + 1]])
                                            for k in range(len(IN_SPLITS))]
        wwT = jnp.pad(ww.T, ((0, BF16_ROWS - IDX_HEADS), (0, 0)))
        gqc = col(g_q_dsa[l]) * (DSA_LATENT ** -0.5 * LOG2E)

        qT, cT, c, qiT, wT, kidx, u = _inproj(
            x, row(g_mix_norm[l]), wq.T, wc, wqi.T, wwT, wk, wu, gqc, row(g_kv_dsa[l]))

        wuvT = bf(jnp.transpose(w_uv_dsa[l], (0, 2, 1)))
        o_dsa = _dsa(qT, qiT, wT, kidx, c, cT, nb, wuvT)

        bm, cm, lre, lim = _s5_mats(a_re[l], a_im[l], log_dt[l], b_re[l], b_im[l], c_re[l], c_im[l])
        lre = jnp.broadcast_to(lre, (B, lre.shape[1]))
        lim = jnp.broadcast_to(lim, (B, lim.shape[1]))
        o_s5 = _s5(u, bm, cm, lre, lim, row(d_skip[l]), bf(w_glu[l]))

        k, v = _memkv(mem, row(g_mem_norm[l]), bf(w_mem_kv[l]), row(g_k_cross[l]))

        x1 = _merge(x, o_dsa, o_s5, k, v, row(g_mix_norm[l]), wg, wqx, row(g_q_cross[l]),
                    bf(w_br_dsa[l]), bf(w_br_s5[l]), bf(w_br_cross[l]), bf(w_out[l]))

        x = _ffn(x1.reshape(B * S, D), row(g_ffn_norm[l]), bf(w_ffn_gate[l]), bf(w_ffn_up[l]),
                 bf(w_ffn_down[l])).reshape(B, S, D)
    return x
```

```python
import math

import jax
import jax.numpy as jnp
import numpy as np
from jax import lax
from jax.experimental import pallas as pl
from jax.experimental.pallas import tpu as pltpu

F32 = jnp.float32
BF16 = jnp.bfloat16

D_MODEL = 1024
N_MEM = 256
EPS = 1e-6
DSA_HEADS = 8
DSA_LATENT = 128
DSA_VDIM = 64
IDX_HEADS = 8
IDX_DIM = 64
TOPK_MAX = 256
REL_BUCKETS = 32
REL_MAX_DIST = 128
S5_WIDTH = 512
S5_GROUP = 16
S5_GROUPS = S5_WIDTH // S5_GROUP
S5_STATE = 64
X_HEADS = 4
X_HEAD_DIM = 128
DSA_WIDTH = DSA_HEADS * DSA_VDIM
X_WIDTH = X_HEADS * X_HEAD_DIM
N_BRANCH = 3
IN_SPLITS = (DSA_HEADS * DSA_LATENT, DSA_LATENT, IDX_HEADS * IDX_DIM, IDX_DIM,
             IDX_HEADS, S5_WIDTH, X_WIDTH, N_BRANCH * D_MODEL)

LANES = 128
SUBLANES = 8
BF16_ROWS = 16
CT_ROWS = DSA_LATENT + BF16_ROWS
VMEM_LIMIT = 56 * 1024 * 1024
LOG2E = math.log2(math.e)

TS_IN = 1024
TQ = 256
TK = 256
N_BISECT = 15
T_S5 = 128
S5_PITCH = T_S5 + SUBLANES
TS_MERGE = 1024
TS_FFN = 512
NEG = -(2.0 ** 100)

NT_DIMS = (((1,), (1,)), ((), ()))


def _rms(x, g):
    ms = jnp.mean(x * x, axis=-1, keepdims=True)
    return x * lax.rsqrt(ms + EPS) * g


def _tree(fn, x):
    while x.shape[0] > 1:
        half = x.shape[0] // 2
        x = fn(x[:half], x[half:])
    return x[0]


def _const_spec(shape):
    nd = len(shape)
    return pl.BlockSpec(shape, lambda *_: (0,) * nd, pipeline_mode=pl.Buffered(1))


def _params(sem):
    return pltpu.CompilerParams(dimension_semantics=sem, vmem_limit_bytes=VMEM_LIMIT)


def _inproj_kernel(x_ref, gmix_ref, wqT_ref, wckw_ref, wqiT_ref, wu_ref,
                   gqc_ref, gkv_ref,
                   qT_ref, cT_ref, c_ref, qiT_ref, wT_ref, kidx_ref, u_ref):
    ts = x_ref.shape[0]
    hb = _rms(x_ref[...], gmix_ref[...]).astype(BF16)

    qT = lax.dot_general(wqT_ref[...], hb, NT_DIMS, preferred_element_type=F32)
    q3 = qT.reshape(DSA_HEADS, DSA_LATENT, ts)
    ms = jnp.mean(q3 * q3, axis=1, keepdims=True)
    qT_ref[...] = (q3 * lax.rsqrt(ms + EPS) * gqc_ref[...][None]).reshape(qT.shape).astype(BF16)

    ckw = jnp.dot(hb, wckw_ref[...], preferred_element_type=F32)
    kw = ckw[:, DSA_LATENT:]
    kidx_ref[...] = kw[:, :IDX_DIM].astype(BF16)
    wT_ref[...] = kw.T[IDX_DIM:IDX_DIM + BF16_ROWS, :]

    cn = _rms(ckw[:, :DSA_LATENT], gkv_ref[...])
    c_ref[...] = cn.astype(BF16)
    cTn = cn.T.astype(BF16)
    for k in range(ts // TK):
        cT_ref[k, :DSA_LATENT, :] = cTn[:, k * TK:(k + 1) * TK]
        cT_ref[k, DSA_LATENT:, :] = jnp.ones((BF16_ROWS, TK), BF16)

    qiT_ref[...] = lax.dot_general(wqiT_ref[...], hb, NT_DIMS, preferred_element_type=F32).astype(BF16)

    u_ref[...] = jnp.dot(hb, wu_ref[...], preferred_element_type=F32)


def _inproj(x, gmix, wqT, wckw, wqiT, wu, gqc, gkv):
    B, S, D = x.shape
    ts = TS_IN
    grid = (B, S // ts)
    tok = lambda w: pl.BlockSpec((None, ts, w), lambda b, s: (b, s, 0))
    tokT = lambda r: pl.BlockSpec((None, r, ts), lambda b, s: (b, 0, s))
    hq = DSA_HEADS * DSA_LATENT
    hi = IDX_HEADS * IDX_DIM
    out_shape = (
        jax.ShapeDtypeStruct((B, hq, S), BF16),
        jax.ShapeDtypeStruct((B, S // TK, CT_ROWS, TK), BF16),
        jax.ShapeDtypeStruct((B, S, DSA_LATENT), BF16),
        jax.ShapeDtypeStruct((B, hi, S), BF16),
        jax.ShapeDtypeStruct((B, BF16_ROWS, S), F32),
        jax.ShapeDtypeStruct((B, S, IDX_DIM), BF16),
        jax.ShapeDtypeStruct((B, S, S5_WIDTH), F32),
    )
    out_specs = (tokT(hq),
                 pl.BlockSpec((None, ts // TK, CT_ROWS, TK), lambda b, s: (b, s, 0, 0)),
                 tok(DSA_LATENT), tokT(hi), tokT(BF16_ROWS), tok(IDX_DIM),
                 tok(S5_WIDTH))
    consts = (gmix, wqT, wckw, wqiT, wu, gqc, gkv)
    in_specs = [tok(D)] + [_const_spec(a.shape) for a in consts]
    return pl.pallas_call(
        _inproj_kernel, grid=grid, in_specs=in_specs, out_specs=out_specs, out_shape=out_shape,
        compiler_params=_params(("parallel", "parallel")), name="inproj",
    )(x, *consts)


def _dsa_kernel(qT_ref, qiT_ref, wT_ref, kidx_ref, c_ref, cT_ref, nb_ref, wuvT_ref, o_ref,
                sc_ref, pref_ref, lg_ref, m_ref, acc_ref):
    i = pl.program_id(1)
    nk = sc_ref.shape[0]
    kf = float(TOPK_MAX)
    G = TK // SUBLANES

    def rep(fn, a):
        return jnp.broadcast_to(fn(a, axis=0, keepdims=True), (SUBLANES, TQ))

    def full(v, dt=F32):
        return jnp.full((SUBLANES, TQ), v, dt)

    def key_rows(j):
        return pl.ds(pl.multiple_of(j * TK, TK), TK)

    def score(j):
        ks = kidx_ref[key_rows(j), :]
        acc = None
        for h in range(IDX_HEADS):
            d = jnp.dot(ks, qiT_ref[h * IDX_DIM:(h + 1) * IDX_DIM, :], preferred_element_type=F32)
            t = jnp.maximum(d, 0.0) * wT_ref[h:h + 1, :]
            acc = t if acc is None else acc + t
        return acc

    def score_pair(p, carry):
        sc_ref[2 * p] = score(2 * p)
        sc_ref[2 * p + 1] = score(2 * p + 1)
        return carry

    lax.fori_loop(0, lax.shift_right_logical(i, 1), score_pair, 0)

    @pl.when((i & 1) == 1)
    def _():
        sc_ref[i - 1] = score(i - 1)

    key_t = lax.broadcasted_iota(jnp.int32, (TK, TQ), 0)
    qry_t = lax.broadcasted_iota(jnp.int32, (TK, TQ), 1)
    causal = key_t <= qry_t
    sc_ref[i] = jnp.where(causal, score(i), -jnp.inf)

    @pl.when(i == 0)
    def _():
        sc_ref[0] = jnp.where(causal, 0.0, NEG)

    @pl.when(i > 0)
    def _():
        nt = i + 1

        def tile3(j):
            return sc_ref[j].reshape(G, SUBLANES, TQ)

        def minmax_body(j, carry):
            mn, mx = carry
            s = tile3(j)
            mx = jnp.maximum(mx, _tree(jnp.maximum, s))
            mn = jnp.minimum(mn, _tree(jnp.minimum, jnp.where(s == -jnp.inf, jnp.inf, s)))
            return mn, mx

        mn, mx = lax.fori_loop(0, nt, minmax_body, (full(jnp.inf), full(-jnp.inf)))
        lo, hi = rep(jnp.min, mn), rep(jnp.max, mx)

        def count_ge(thr):
            def body(j, acc):
                return acc + _tree(jnp.add, jnp.where(tile3(j) >= thr[None], 1.0, 0.0))
            return rep(jnp.sum, lax.fori_loop(0, nt, body, full(0.0)))

        def bisect(_, carry):
            lo, hi, clo = carry
            mid = 0.5 * (lo + hi)
            cnt = count_ge(mid)
            ge = cnt >= kf
            return jnp.where(ge, mid, lo), jnp.where(ge, hi, mid), jnp.where(ge, cnt, clo)

        nvis = (i * TQ + 1 + lax.broadcasted_iota(jnp.int32, (SUBLANES, TQ), 1)).astype(F32)
        lo, hi, clo = lax.fori_loop(0, N_BISECT, bisect, (lo, hi, nvis))

        def snap_body(j, am):
            s = tile3(j)
            return jnp.minimum(am, _tree(jnp.minimum, jnp.where(s >= lo[None], s, jnp.inf)))

        cur = rep(jnp.min, lax.fori_loop(0, nt, snap_body, full(jnp.inf)))

        def walk(cur):
            def body(j, carry):
                ac, am = carry
                s = tile3(j)
                g = s > cur[None]
                ac = ac + _tree(jnp.add, jnp.where(g, 1.0, 0.0))
                am = jnp.minimum(am, _tree(jnp.minimum, jnp.where(g, s, jnp.inf)))
                return ac, am
            ac, am = lax.fori_loop(0, nt, body, (full(0.0), full(jnp.inf)))
            return rep(jnp.sum, ac), rep(jnp.min, am)

        def walk_cond(carry):
            _, _, _, go, it = carry
            return jnp.logical_and(go > 0, it < nk * TK + 2)

        def walk_body(carry):
            cur, cge, _, _, it = carry
            c, nxt = walk(cur)
            move = c >= kf
            go = (jnp.max(jnp.where(move, 1.0, 0.0)) > 0.5).astype(jnp.int32)
            return jnp.where(move, nxt, cur), jnp.where(move, c, cge), c, go, it + 1

        kth, cge, cgt, _, _ = lax.while_loop(
            walk_cond, walk_body, (cur, clo, full(0.0), jnp.int32(1), jnp.int32(0)))
        need = kf - cgt
        has_excess = jnp.max(jnp.where(cge > kf, 1.0, 0.0)) > 0.5

        @pl.when(jnp.logical_not(has_excess))
        def _():
            def body(j, carry):
                sc_ref[j] = jnp.where(tile3(j) >= kth[None], 0.0, NEG).reshape(TK, TQ)
                return carry
            lax.fori_loop(0, nt, body, 0)

        @pl.when(has_excess)
        def _():
            tril = jnp.where(key_t >= qry_t, 1.0, 0.0).astype(BF16)

            def pref_body(p, carry):
                for j in (2 * p, jnp.minimum(2 * p + 1, nt - 1)):
                    e01 = jnp.where(tile3(j) == kth[None], 1.0, 0.0).reshape(TK, TQ).astype(BF16)
                    pref_ref[j] = jnp.dot(tril, e01, preferred_element_type=F32)
                return carry

            lax.fori_loop(0, lax.shift_right_logical(nt + 1, 1), pref_body, 0)

            def mask_body(j, offset):
                s = tile3(j)
                pref = pref_ref[j]
                rank = pref.reshape(G, SUBLANES, TQ) + offset[None]
                tie = jnp.where(rank <= need[None], 0.0, NEG)
                mb = jnp.where(s > kth[None], 0.0, jnp.where(s == kth[None], tie, NEG))
                sc_ref[j] = mb.reshape(TK, TQ)
                return offset + jnp.broadcast_to(pref[TK - 1:TK, :], (SUBLANES, TQ))

            lax.fori_loop(0, nt, mask_body, full(0.0))

    m_ref[...] = jnp.full(m_ref.shape, NEG, F32)
    acc_ref[...] = jnp.zeros(acc_ref.shape, F32)
    LG = CT_ROWS // SUBLANES

    RB = TK // BF16_ROWS

    def attend(tiles):
        m_run = [m_ref[h] for h in range(DSA_HEADS)]
        stats = []
        for slot, (j, near) in enumerate(tiles):
            ct = c_ref[key_rows(j), :]
            mbb = sc_ref[j].astype(BF16)
            per_head = []
            for h in range(DSA_HEADS):
                lg = jnp.dot(ct, qT_ref[h * DSA_LATENT:(h + 1) * DSA_LATENT, :],
                             preferred_element_type=F32)
                if near is not None:
                    lg = lg + nb_ref[near, h]
                lgb = lg.astype(BF16) + mbb
                lg_ref[slot, h] = lgb
                tmax = _tree(jnp.maximum, lgb.reshape(RB, BF16_ROWS, TQ)).astype(F32)
                m_new = jnp.maximum(m_run[h], rep(jnp.max, tmax))
                per_head.append((m_run[h], m_new))
                m_run[h] = m_new
            stats.append(per_head)
        for slot, (j, near) in enumerate(tiles):
            ctT = cT_ref[j]
            for h in range(DSA_HEADS):
                m_old, m_new = stats[slot][h]
                alpha = jnp.exp2(m_old - m_new)
                m16 = jnp.concatenate([m_new, m_new], axis=0).astype(BF16)
                x = lg_ref[slot, h].reshape(RB, BF16_ROWS, TQ) - m16[None]
                pv = jnp.dot(ctT, jnp.exp2(x).reshape(TK, TQ), preferred_element_type=F32)
                acc3 = acc_ref[h].reshape(LG, SUBLANES, TQ) * alpha[None]
                acc_ref[h] = acc3.reshape(CT_ROWS, TQ) + pv
        for h in range(DSA_HEADS):
            m_ref[h] = m_run[h]

    nfar = jnp.maximum(i - 1, 0)

    def far_pair(p, carry):
        attend([(2 * p, None), (2 * p + 1, None)])
        return carry

    lax.fori_loop(0, lax.shift_right_logical(nfar, 1), far_pair, 0)

    @pl.when((nfar & 1) == 1)
    def _():
        attend([(nfar - 1, None)])

    @pl.when(i > 0)
    def _():
        attend([(i - 1, 1), (i, 0)])

    @pl.when(i == 0)
    def _():
        attend([(0, 0)])

    outs = []
    for h in range(DSA_HEADS):
        rl = 1.0 / acc_ref[h, DSA_LATENT:DSA_LATENT + SUBLANES, :]
        o3 = acc_ref[h, :DSA_LATENT, :].reshape(DSA_LATENT // SUBLANES, SUBLANES, TQ) * rl[None]
        o = o3.reshape(DSA_LATENT, TQ).astype(BF16)
        outs.append(jnp.dot(wuvT_ref[h], o, preferred_element_type=F32))
    o_ref[...] = jnp.concatenate(outs, axis=0).T.astype(BF16)


def _dsa(qT, qiT, wT, kidx, c, cT, nb, wuvT):
    B, S, _ = c.shape
    nk = S // TK
    grid = (B, S // TQ)
    tileT = lambda r: pl.BlockSpec((None, r, TQ), lambda b, i: (b, 0, i))
    seq = lambda w: pl.BlockSpec((None, S, w), lambda b, i: (b, 0, 0))
    in_specs = [tileT(DSA_HEADS * DSA_LATENT), tileT(IDX_HEADS * IDX_DIM), tileT(BF16_ROWS),
                seq(IDX_DIM), seq(DSA_LATENT),
                pl.BlockSpec((None, nk, CT_ROWS, TK), lambda b, i: (b, 0, 0, 0)),
                _const_spec(nb.shape), _const_spec(wuvT.shape)]
    scratch = [
        pltpu.VMEM((nk, TK, TQ), F32),
        pltpu.VMEM((nk, TK, TQ), F32),
        pltpu.VMEM((2, DSA_HEADS, TK, TQ), BF16),
        pltpu.VMEM((DSA_HEADS, SUBLANES, TQ), F32),
        pltpu.VMEM((DSA_HEADS, CT_ROWS, TQ), F32),
    ]
    return pl.pallas_call(
        _dsa_kernel, grid=grid, in_specs=in_specs,
        out_specs=pl.BlockSpec((None, TQ, DSA_WIDTH), lambda b, i: (b, i, 0)),
        out_shape=jax.ShapeDtypeStruct((B, S, DSA_WIDTH), BF16), scratch_shapes=scratch,
        compiler_params=_params(("parallel", "arbitrary")), name="dsa",
    )(qT, qiT, wT, kidx, c, cT, nb, wuvT)


def _s5_kernel(u_ref, bm_ref, cm_ref, lre_ref, lim_ref, dsk_ref, wglu_ref, o_ref,
               uslab_ref, uil_ref, st_ref, oslab_ref, hre_ref, him_ref):
    nstate = S5_GROUPS * S5_STATE
    nb = hre_ref.shape[0]
    nslab = S5_WIDTH // LANES

    @pl.when(pl.program_id(0) == 0)
    def _():
        hre_ref[...] = jnp.zeros(hre_ref.shape, F32)
        him_ref[...] = jnp.zeros(him_ref.shape, F32)

    for b in range(nb):
        for k in range(nslab):
            uslab_ref[k, b * S5_PITCH:b * S5_PITCH + T_S5, :] = u_ref[b, :, k * LANES:(k + 1) * LANES]

    def gather_step(t, carry):
        for k in range(nslab):
            uil_ref[pl.ds(pl.multiple_of(t * nb, nb), nb), k * LANES:(k + 1) * LANES] = (
                uslab_ref[k, pl.ds(t, nb, stride=S5_PITCH), :])
        return carry

    lax.fori_loop(0, T_S5, gather_step, 0, unroll=4)
    u = uil_ref[...]
    ub = u.astype(BF16)

    kin = 4
    cin = S5_WIDTH // kin
    sin = nstate // kin
    for k in range(kin):
        uk = ub[:, k * cin:(k + 1) * cin]
        for off in (0, nstate):
            cols = slice(off + k * sin, off + (k + 1) * sin)
            st_ref[:, cols] = jnp.dot(uk, bm_ref[k * cin:(k + 1) * cin, cols],
                                      preferred_element_type=F32)

    half = nstate // 2
    for part in range(2):
        re_sl = slice(part * half, (part + 1) * half)
        im_sl = slice(nstate + part * half, nstate + (part + 1) * half)
        lre = lre_ref[:, re_sl]
        lim = lim_ref[:, re_sl]

        def step(t, carry):
            hr, hi = carry
            r = pl.ds(pl.multiple_of(t * nb, nb), nb)
            nr = lre * hr - lim * hi + st_ref[r, re_sl]
            ni = lre * hi + lim * hr + st_ref[r, im_sl]
            st_ref[r, re_sl] = nr
            st_ref[r, im_sl] = ni
            return nr, ni

        hr, hi = lax.fori_loop(0, T_S5, step, (hre_ref[:, re_sl], him_ref[:, re_sl]), unroll=4)
        hre_ref[:, re_sl] = hr
        him_ref[:, re_sl] = hi

    kout = 2
    cout = S5_WIDTH // kout
    sout = nstate // kout
    ys = []
    for k in range(kout):
        acc = None
        for off in (0, nstate):
            rows = slice(off + k * sout, off + (k + 1) * sout)
            part = jnp.dot(st_ref[:, rows].astype(BF16), cm_ref[rows, k * cout:(k + 1) * cout],
                           preferred_element_type=F32)
            acc = part if acc is None else acc + part
        ys.append(acc)
    y = jnp.concatenate(ys, axis=1)
    y = jax.nn.gelu(y + dsk_ref[...] * u)
    z = jnp.dot(y.astype(BF16), wglu_ref[...], preferred_element_type=F32)
    o = y * jax.nn.sigmoid(z)

    for k in range(nslab):
        oslab_ref[k] = o[:, k * LANES:(k + 1) * LANES]
    for b in range(nb):
        for k in range(nslab):
            o_ref[b, :, k * LANES:(k + 1) * LANES] = (
                oslab_ref[k, pl.ds(b, T_S5, stride=nb), :].astype(BF16))


def _s5(u, bm, cm, lre, lim, dsk, wglu):
    B, S, W = u.shape
    tb = T_S5 * B
    nstate = S5_GROUPS * S5_STATE
    tok = pl.BlockSpec((B, T_S5, W), lambda t: (0, t, 0))
    in_specs = [tok, _const_spec(bm.shape), _const_spec(cm.shape), _const_spec(lre.shape),
                _const_spec(lim.shape), _const_spec(dsk.shape), _const_spec(wglu.shape)]
    scratch = [pltpu.VMEM((W // LANES, B * S5_PITCH, LANES), F32),
               pltpu.VMEM((tb, W), F32),
               pltpu.VMEM((tb, 2 * nstate), F32),
               pltpu.VMEM((W // LANES, tb, LANES), F32),
               pltpu.VMEM((B, nstate), F32), pltpu.VMEM((B, nstate), F32)]
    return pl.pallas_call(
        _s5_kernel, grid=(S // T_S5,), in_specs=in_specs, out_specs=tok,
        out_shape=jax.ShapeDtypeStruct((B, S, W), BF16), scratch_shapes=scratch,
        compiler_params=_params(("arbitrary",)), name="s5",
    )(u, bm, cm, lre, lim, dsk, wglu)


def _merge_kernel(x_ref, odsa_ref, os5_ref, mem_ref, gmix_ref, wg_ref, wqx_ref, gqx_ref,
                  wb1_ref, wb2_ref, wb3_ref, wout_ref, gmem_ref, wkv_ref, gk_ref, y_ref,
                  k_ref, v_ref):
    @pl.when(pl.program_id(1) == 0)
    def _():
        mb = _rms(mem_ref[...], gmem_ref[...]).astype(BF16)
        kv = jnp.dot(mb, wkv_ref[...], preferred_element_type=F32)
        for h in range(X_HEADS):
            sl = slice(h * X_HEAD_DIM, (h + 1) * X_HEAD_DIM)
            k_ref[:, sl] = _rms(kv[:, sl], gk_ref[...]).astype(BF16)
        v_ref[...] = kv[:, X_WIDTH:].astype(BF16)

    x = x_ref[...]
    hb = _rms(x, gmix_ref[...]).astype(BF16)

    qx = jnp.dot(hb, wqx_ref[...], preferred_element_type=F32)
    gqx = gqx_ref[...] * (X_HEAD_DIM ** -0.5)
    ox = []
    for h in range(X_HEADS):
        sl = slice(h * X_HEAD_DIM, (h + 1) * X_HEAD_DIM)
        qh = _rms(qx[:, sl], gqx).astype(BF16)
        lg = lax.dot_general(qh, k_ref[:, sl], NT_DIMS, preferred_element_type=F32)
        p = jnp.exp(lg - jnp.max(lg, axis=-1, keepdims=True))
        pv = jnp.dot(p.astype(BF16), v_ref[:, sl], preferred_element_type=F32)
        ox.append((pv / jnp.sum(p, axis=-1, keepdims=True)).astype(BF16))
    ox = jnp.concatenate(ox, axis=1)

    merged = None
    for br, (o, wb) in enumerate(((odsa_ref[...], wb1_ref), (os5_ref[...], wb2_ref), (ox, wb3_ref))):
        gate = jax.nn.sigmoid(jnp.dot(hb, wg_ref[:, br * D_MODEL:(br + 1) * D_MODEL],
                                      preferred_element_type=F32))
        term = gate * jnp.dot(o, wb[...], preferred_element_type=F32)
        merged = term if merged is None else merged + term
    y_ref[...] = x + jnp.dot(merged.astype(BF16), wout_ref[...], preferred_element_type=F32)


def _merge(x, odsa, os5, mem, gmix, wg, wqx, gqx, wb1, wb2, wb3, wout, gmem, wkv, gk):
    B, S, D = x.shape
    ts = TS_MERGE
    tok = lambda w: pl.BlockSpec((None, ts, w), lambda b, s: (b, s, 0))
    consts = (gmix, wg, wqx, gqx, wb1, wb2, wb3, wout, gmem, wkv, gk)
    in_specs = [tok(D), tok(DSA_WIDTH), tok(S5_WIDTH),
                pl.BlockSpec((None, N_MEM, D), lambda b, s: (b, 0, 0))
                ] + [_const_spec(a.shape) for a in consts]
    scratch = [pltpu.VMEM((N_MEM, X_WIDTH), BF16), pltpu.VMEM((N_MEM, X_WIDTH), BF16)]
    return pl.pallas_call(
        _merge_kernel, grid=(B, S // ts), in_specs=in_specs, out_specs=tok(D),
        out_shape=jax.ShapeDtypeStruct((B, S, D), F32), scratch_shapes=scratch,
        compiler_params=_params(("parallel", "arbitrary")), name="merge",
    )(x, odsa, os5, mem, *consts)


def _ffn_kernel(x_ref, g_ref, wg_ref, wu_ref, wd_ref, y_ref):
    x = x_ref[...]
    hb = _rms(x, g_ref[...]).astype(BF16)
    a = jnp.dot(hb, wg_ref[...], preferred_element_type=F32)
    b = jnp.dot(hb, wu_ref[...], preferred_element_type=F32)
    act = (jax.nn.silu(a) * b).astype(BF16)
    y_ref[...] = x + jnp.dot(act, wd_ref[...], preferred_element_type=F32)


def _ffn(x2, g, wg, wu, wd):
    n, D = x2.shape
    tok = pl.BlockSpec((TS_FFN, D), lambda t: (t, 0))
    return pl.pallas_call(
        _ffn_kernel, grid=(n // TS_FFN,),
        in_specs=[tok] + [_const_spec(a.shape) for a in (g, wg, wu, wd)],
        out_specs=tok, out_shape=jax.ShapeDtypeStruct((n, D), F32),
        compiler_params=_params(("parallel",)), name="ffn",
    )(x2, g, wg, wu, wd)


def _t5_bucket(n):
    max_exact = REL_BUCKETS // 2
    nf = jnp.maximum(n, 1).astype(F32)
    large = max_exact + (jnp.log(nf / max_exact) / math.log(REL_MAX_DIST / max_exact)
                         * (REL_BUCKETS - max_exact)).astype(jnp.int32)
    large = jnp.minimum(large, REL_BUCKETS - 1)
    return jnp.where(n < max_exact, n, large)


def _toeplitz(w, rows, cols):
    H, L = w.shape
    flat = jnp.tile(w, (1, rows))[:, :rows * (L - 1)]
    return flat.reshape(H, rows, L - 1)[:, :, :cols]


def _near_bias(rel_bias):
    n = jnp.arange(2 * TQ, dtype=jnp.int32)
    f = (rel_bias[_t5_bucket(n)] - rel_bias[REL_BUCKETS - 1][None, :]).T * LOG2E
    w_diag = jnp.concatenate([f[:, :TQ], jnp.broadcast_to(f[:, :1], (f.shape[0], TK - 1))], axis=1)
    w_prev = jnp.concatenate([f[:, TQ:2 * TQ], f[:, 1:TK]], axis=1)
    return jnp.stack([_toeplitz(w_diag, TK, TQ), _toeplitz(w_prev, TK, TQ)], axis=0).astype(F32)


def _s5_mats(a_re, a_im, log_dt, b_re, b_im, c_re, c_im):
    lam = lax.complex(a_re.astype(F32), a_im.astype(F32))
    dt = jnp.exp(log_dt.astype(F32))[:, None]
    lam_bar = jnp.exp(lam * dt)
    b_bar = ((lam_bar - 1.0) / lam)[..., None] * lax.complex(b_re.astype(F32), b_im.astype(F32))
    nstate = S5_GROUPS * S5_STATE
    in_mask = (jnp.arange(S5_WIDTH)[:, None] // S5_GROUP) == (jnp.arange(nstate)[None, :] // S5_STATE)

    def blockdiag_in(w):
        t = jnp.transpose(w, (0, 2, 1)).reshape(S5_WIDTH, S5_STATE)
        return jnp.where(in_mask, jnp.tile(t, (1, S5_GROUPS)), 0.0)

    def blockdiag_out(w):
        t = jnp.transpose(w, (0, 2, 1)).reshape(nstate, S5_GROUP)
        return jnp.where(in_mask.T, jnp.tile(t, (1, S5_GROUPS)), 0.0)

    bm = jnp.concatenate([blockdiag_in(jnp.real(b_bar)), blockdiag_in(jnp.imag(b_bar))], axis=1)
    cm = jnp.concatenate([blockdiag_out(c_re.astype(F32)), blockdiag_out(-c_im.astype(F32))], axis=0)
    return bm.astype(BF16), cm.astype(BF16), jnp.real(lam_bar).reshape(1, -1), jnp.imag(lam_bar).reshape(1, -1)


def kernel(x, mem, rel_bias, w_in, g_mix_norm, g_q_dsa, g_kv_dsa, w_uv_dsa, a_re, a_im, log_dt, b_re, b_im, c_re, c_im, d_skip, w_glu, g_mem_norm, w_mem_kv, g_q_cross, g_k_cross, w_br_dsa, w_br_s5, w_br_cross, w_out, g_ffn_norm, w_ffn_gate, w_ffn_up, w_ffn_down):
    B, S, D = x.shape
    depth = w_in.shape[0]
    offs = [0] + [int(o) for o in np.cumsum(IN_SPLITS)]
    nb = _near_bias(rel_bias)
    row = lambda v: v.reshape(1, -1).astype(F32)
    col = lambda v: v.reshape(-1, 1).astype(F32)
    bf = lambda a: a.astype(BF16)
    for l in range(depth):
        wq, wc, wqi, wk, ww, wu, wqx, wg = [bf(w_in[l, :, offs[k]:offs[k + 1]])
                                            for k in range(len(IN_SPLITS))]
        wckw = jnp.pad(jnp.concatenate([wc, wk, ww], axis=1),
                       ((0, 0), (0, 2 * LANES - DSA_LATENT - IDX_DIM - IDX_HEADS)))
        gqc = col(g_q_dsa[l]) * (DSA_LATENT ** -0.5 * LOG2E)

        qT, cT, c, qiT, wT, kidx, u = _inproj(
            x, row(g_mix_norm[l]), wq.T, wckw, wqi.T, wu, gqc, row(g_kv_dsa[l]))

        wuvT = bf(jnp.transpose(w_uv_dsa[l], (0, 2, 1)))
        o_dsa = _dsa(qT, qiT, wT, kidx, c, cT, nb, wuvT)

        bm, cm, lre, lim = _s5_mats(a_re[l], a_im[l], log_dt[l], b_re[l], b_im[l], c_re[l], c_im[l])
        lre = jnp.broadcast_to(lre, (B, lre.shape[1]))
        lim = jnp.broadcast_to(lim, (B, lim.shape[1]))
        o_s5 = _s5(u, bm, cm, lre, lim, row(d_skip[l]), bf(w_glu[l]))

        x1 = _merge(x, o_dsa, o_s5, mem, row(g_mix_norm[l]), wg, wqx, row(g_q_cross[l]),
                    bf(w_br_dsa[l]), bf(w_br_s5[l]), bf(w_br_cross[l]), bf(w_out[l]),
                    row(g_mem_norm[l]), bf(w_mem_kv[l]), row(g_k_cross[l]))

        x = _ffn(x1.reshape(B * S, D), row(g_ffn_norm[l]), bf(w_ffn_gate[l]), bf(w_ffn_up[l]),
                 bf(w_ffn_down[l])).reshape(B, S, D)
    return x
```

```python
import math

import jax
import jax.numpy as jnp
import numpy as np
from jax import lax
from jax.experimental import pallas as pl
from jax.experimental.pallas import tpu as pltpu

F32 = jnp.float32
BF16 = jnp.bfloat16

D_MODEL = 1024
N_MEM = 256
EPS = 1e-6
DSA_HEADS = 8
DSA_LATENT = 128
DSA_VDIM = 64
IDX_HEADS = 8
IDX_DIM = 64
TOPK_MAX = 256
REL_BUCKETS = 32
REL_MAX_DIST = 128
S5_WIDTH = 512
S5_GROUP = 16
S5_GROUPS = S5_WIDTH // S5_GROUP
S5_STATE = 64
X_HEADS = 4
X_HEAD_DIM = 128
DSA_WIDTH = DSA_HEADS * DSA_VDIM
X_WIDTH = X_HEADS * X_HEAD_DIM
N_BRANCH = 3
IN_SPLITS = (DSA_HEADS * DSA_LATENT, DSA_LATENT, IDX_HEADS * IDX_DIM, IDX_DIM,
             IDX_HEADS, S5_WIDTH, X_WIDTH, N_BRANCH * D_MODEL)

LANES = 128
SUBLANES = 8
BF16_ROWS = 16
CT_ROWS = DSA_LATENT + BF16_ROWS
VMEM_LIMIT = 56 * 1024 * 1024
LOG2E = math.log2(math.e)

TS_IN = 1024
TQ = 256
TK = 256
N_COARSE = 8
N_BISECT = 8
T_S5 = 128
S5_PITCH = T_S5 + SUBLANES
TS_MERGE = 1024
TS_FFN = 512
NEG = -(2.0 ** 100)

NT_DIMS = (((1,), (1,)), ((), ()))


def _rms(x, g):
    ms = jnp.mean(x * x, axis=-1, keepdims=True)
    return x * lax.rsqrt(ms + EPS) * g


def _tree(fn, x):
    while x.shape[0] > 1:
        half = x.shape[0] // 2
        x = fn(x[:half], x[half:])
    return x[0]


def _const_spec(shape):
    nd = len(shape)
    return pl.BlockSpec(shape, lambda *_: (0,) * nd, pipeline_mode=pl.Buffered(1))


def _params(sem):
    return pltpu.CompilerParams(dimension_semantics=sem, vmem_limit_bytes=VMEM_LIMIT)


def _inproj_kernel(x_ref, gmix_ref, wqT_ref, wckw_ref, wqiT_ref, wu_ref,
                   gqc_ref, gkv_ref,
                   qT_ref, cT_ref, c_ref, qiT_ref, wT_ref, kidx_ref, u_ref):
    ts = x_ref.shape[0]
    hb = _rms(x_ref[...], gmix_ref[...]).astype(BF16)

    qT = lax.dot_general(wqT_ref[...], hb, NT_DIMS, preferred_element_type=F32)
    q3 = qT.reshape(DSA_HEADS, DSA_LATENT, ts)
    ms = jnp.mean(q3 * q3, axis=1, keepdims=True)
    qT_ref[...] = (q3 * lax.rsqrt(ms + EPS) * gqc_ref[...][None]).reshape(qT.shape).astype(BF16)

    ckw = jnp.dot(hb, wckw_ref[...], preferred_element_type=F32)
    kw = ckw[:, DSA_LATENT:]
    kidx_ref[...] = kw[:, :IDX_DIM].astype(BF16)
    wT_ref[...] = kw.T[IDX_DIM:IDX_DIM + BF16_ROWS, :]

    cn = _rms(ckw[:, :DSA_LATENT], gkv_ref[...])
    c_ref[...] = cn.astype(BF16)
    cTn = cn.T.astype(BF16)
    for k in range(ts // TK):
        cT_ref[k, :DSA_LATENT, :] = cTn[:, k * TK:(k + 1) * TK]
        cT_ref[k, DSA_LATENT:, :] = jnp.ones((BF16_ROWS, TK), BF16)

    qiT_ref[...] = lax.dot_general(wqiT_ref[...], hb, NT_DIMS, preferred_element_type=F32).astype(BF16)

    u_ref[...] = jnp.dot(hb, wu_ref[...], preferred_element_type=F32)


def _inproj(x, gmix, wqT, wckw, wqiT, wu, gqc, gkv):
    B, S, D = x.shape
    ts = TS_IN
    grid = (B, S // ts)
    tok = lambda w: pl.BlockSpec((None, ts, w), lambda b, s: (b, s, 0))
    tokT = lambda r: pl.BlockSpec((None, r, ts), lambda b, s: (b, 0, s))
    hq = DSA_HEADS * DSA_LATENT
    hi = IDX_HEADS * IDX_DIM
    out_shape = (
        jax.ShapeDtypeStruct((B, hq, S), BF16),
        jax.ShapeDtypeStruct((B, S // TK, CT_ROWS, TK), BF16),
        jax.ShapeDtypeStruct((B, S, DSA_LATENT), BF16),
        jax.ShapeDtypeStruct((B, hi, S), BF16),
        jax.ShapeDtypeStruct((B, BF16_ROWS, S), F32),
        jax.ShapeDtypeStruct((B, S, IDX_DIM), BF16),
        jax.ShapeDtypeStruct((B, S, S5_WIDTH), F32),
    )
    out_specs = (tokT(hq),
                 pl.BlockSpec((None, ts // TK, CT_ROWS, TK), lambda b, s: (b, s, 0, 0)),
                 tok(DSA_LATENT), tokT(hi), tokT(BF16_ROWS), tok(IDX_DIM),
                 tok(S5_WIDTH))
    consts = (gmix, wqT, wckw, wqiT, wu, gqc, gkv)
    in_specs = [tok(D)] + [_const_spec(a.shape) for a in consts]
    return pl.pallas_call(
        _inproj_kernel, grid=grid, in_specs=in_specs, out_specs=out_specs, out_shape=out_shape,
        compiler_params=_params(("parallel", "parallel")), name="inproj",
    )(x, *consts)


def _dsa_kernel(qT_ref, qiT_ref, wT_ref, kidx_ref, c_ref, cT_ref, nb_ref, wuvT_ref, o_ref,
                sc_ref, scb_ref, mm_ref, pref_ref, lg_ref, m_ref, acc_ref):
    i = pl.program_id(1)
    nk = sc_ref.shape[0]
    kf = float(TOPK_MAX)
    G = TK // SUBLANES
    RB = TK // BF16_ROWS

    def rep(fn, a):
        return jnp.broadcast_to(fn(a, axis=0, keepdims=True), (SUBLANES, TQ))

    def full(v, dt=F32):
        return jnp.full((SUBLANES, TQ), v, dt)

    def key_rows(j):
        return pl.ds(pl.multiple_of(j * TK, TK), TK)

    def score(j):
        ks = kidx_ref[key_rows(j), :]
        acc = None
        for h in range(IDX_HEADS):
            d = jnp.dot(ks, qiT_ref[h * IDX_DIM:(h + 1) * IDX_DIM, :], preferred_element_type=F32)
            t = jnp.maximum(d, 0.0) * wT_ref[h:h + 1, :]
            acc = t if acc is None else acc + t
        return acc

    mm_ref[0] = full(jnp.inf)
    mm_ref[1] = full(-jnp.inf)

    def put_scores(j, s, s_for_min):
        sc_ref[j] = s
        scb_ref[j] = s.astype(BF16)
        mm_ref[0] = jnp.minimum(mm_ref[0], _tree(jnp.minimum, s_for_min.reshape(G, SUBLANES, TQ)))
        mm_ref[1] = jnp.maximum(mm_ref[1], _tree(jnp.maximum, s.reshape(G, SUBLANES, TQ)))

    def score_pair(p, carry):
        for j in (2 * p, 2 * p + 1):
            s = score(j)
            put_scores(j, s, s)
        return carry

    lax.fori_loop(0, lax.shift_right_logical(i, 1), score_pair, 0)

    @pl.when((i & 1) == 1)
    def _():
        s = score(i - 1)
        put_scores(i - 1, s, s)

    key_t = lax.broadcasted_iota(jnp.int32, (TK, TQ), 0)
    qry_t = lax.broadcasted_iota(jnp.int32, (TK, TQ), 1)
    causal = key_t <= qry_t
    s_diag = score(i)
    put_scores(i, jnp.where(causal, s_diag, -jnp.inf), jnp.where(causal, s_diag, jnp.inf))

    @pl.when(i == 0)
    def _():
        sc_ref[0] = jnp.where(causal, 0.0, NEG)

    @pl.when(i > 0)
    def _():
        nt = i + 1

        def tile3(j):
            return sc_ref[j].reshape(G, SUBLANES, TQ)

        lo, hi = rep(jnp.min, mm_ref[0]), rep(jnp.max, mm_ref[1])

        def count_ge(thr):
            def body(j, acc):
                return acc + _tree(jnp.add, jnp.where(tile3(j) >= thr[None], 1.0, 0.0))
            return rep(jnp.sum, lax.fori_loop(0, nt, body, full(0.0)))

        bf_step = 2.0 ** -7
        tiny = 1e-30

        def as_bf16_value(x):
            return x.astype(BF16).astype(F32)

        def count_ge_coarse(thr):
            thr16 = jnp.concatenate([thr, thr], axis=0).astype(BF16)
            one, zero = jnp.ones((), BF16), jnp.zeros((), BF16)

            def body(j, acc):
                t = scb_ref[j].reshape(RB, BF16_ROWS, TQ)
                return acc + _tree(jnp.add, jnp.where(t >= thr16[None], one, zero))
            acc = lax.fori_loop(0, nt, body, jnp.zeros((BF16_ROWS, TQ), BF16))
            return rep(jnp.sum, acc.astype(F32))

        def coarse(_, carry):
            lo_c, hi_c = carry
            mid = as_bf16_value(0.5 * (lo_c + hi_c))
            ge = count_ge_coarse(mid) >= kf
            return jnp.where(ge, mid, lo_c), jnp.where(ge, hi_c, mid)

        lo_c = as_bf16_value(lo - jnp.abs(lo) * bf_step - tiny)
        hi_c = as_bf16_value(hi + jnp.abs(hi) * bf_step + tiny)
        lo_c, hi_c = lax.fori_loop(0, N_COARSE, coarse, (lo_c, hi_c))
        lo = lo_c - jnp.abs(lo_c) * bf_step - tiny
        hi = hi_c

        def bisect(_, carry):
            lo, hi, clo = carry
            mid = 0.5 * (lo + hi)
            cnt = count_ge(mid)
            ge = cnt >= kf
            return jnp.where(ge, mid, lo), jnp.where(ge, hi, mid), jnp.where(ge, cnt, clo)

        lo, hi, clo = lax.fori_loop(0, N_BISECT, bisect, (lo, hi, count_ge(lo)))

        def snap_body(j, am):
            s = tile3(j)
            return jnp.minimum(am, _tree(jnp.minimum, jnp.where(s >= lo[None], s, jnp.inf)))

        cur = rep(jnp.min, lax.fori_loop(0, nt, snap_body, full(jnp.inf)))

        def walk(cur):
            def body(j, carry):
                ac, am = carry
                s = tile3(j)
                g = s > cur[None]
                ac = ac + _tree(jnp.add, jnp.where(g, 1.0, 0.0))
                am = jnp.minimum(am, _tree(jnp.minimum, jnp.where(g, s, jnp.inf)))
                return ac, am
            ac, am = lax.fori_loop(0, nt, body, (full(0.0), full(jnp.inf)))
            return rep(jnp.sum, ac), rep(jnp.min, am)

        def walk_cond(carry):
            _, _, _, go, it = carry
            return jnp.logical_and(go > 0, it < nk * TK + 2)

        def walk_body(carry):
            cur, cge, _, _, it = carry
            c, nxt = walk(cur)
            move = c >= kf
            go = (jnp.max(jnp.where(move, 1.0, 0.0)) > 0.5).astype(jnp.int32)
            return jnp.where(move, nxt, cur), jnp.where(move, c, cge), c, go, it + 1

        kth, cge, cgt, _, _ = lax.while_loop(
            walk_cond, walk_body, (cur, clo, full(0.0), jnp.int32(1), jnp.int32(0)))
        need = kf - cgt
        has_excess = jnp.max(jnp.where(cge > kf, 1.0, 0.0)) > 0.5

        @pl.when(jnp.logical_not(has_excess))
        def _():
            def body(j, carry):
                sc_ref[j] = jnp.where(tile3(j) >= kth[None], 0.0, NEG).reshape(TK, TQ)
                return carry
            lax.fori_loop(0, nt, body, 0)

        @pl.when(has_excess)
        def _():
            tril = jnp.where(key_t >= qry_t, 1.0, 0.0).astype(BF16)

            def pref_body(p, carry):
                for j in (2 * p, jnp.minimum(2 * p + 1, nt - 1)):
                    e01 = jnp.where(tile3(j) == kth[None], 1.0, 0.0).reshape(TK, TQ).astype(BF16)
                    pref_ref[j] = jnp.dot(tril, e01, preferred_element_type=F32)
                return carry

            lax.fori_loop(0, lax.shift_right_logical(nt + 1, 1), pref_body, 0)

            def mask_body(j, offset):
                s = tile3(j)
                pref = pref_ref[j]
                rank = pref.reshape(G, SUBLANES, TQ) + offset[None]
                tie = jnp.where(rank <= need[None], 0.0, NEG)
                mb = jnp.where(s > kth[None], 0.0, jnp.where(s == kth[None], tie, NEG))
                sc_ref[j] = mb.reshape(TK, TQ)
                return offset + jnp.broadcast_to(pref[TK - 1:TK, :], (SUBLANES, TQ))

            lax.fori_loop(0, nt, mask_body, full(0.0))

    m_ref[...] = jnp.full(m_ref.shape, NEG, F32)
    acc_ref[...] = jnp.zeros(acc_ref.shape, F32)
    LG = CT_ROWS // SUBLANES

    def attend(tiles):
        m_run = [m_ref[h] for h in range(DSA_HEADS)]
        stats = []
        for slot, (j, near) in enumerate(tiles):
            ct = c_ref[key_rows(j), :]
            mbb = sc_ref[j].astype(BF16)
            per_head = []
            for h in range(DSA_HEADS):
                lg = jnp.dot(ct, qT_ref[h * DSA_LATENT:(h + 1) * DSA_LATENT, :],
                             preferred_element_type=F32)
                if near is not None:
                    lg = lg + nb_ref[near, h]
                lgb = lg.astype(BF16) + mbb
                lg_ref[slot, h] = lgb
                tmax = _tree(jnp.maximum, lgb.reshape(RB, BF16_ROWS, TQ)).astype(F32)
                m_new = jnp.maximum(m_run[h], rep(jnp.max, tmax))
                per_head.append((m_run[h], m_new))
                m_run[h] = m_new
            stats.append(per_head)
        for slot, (j, near) in enumerate(tiles):
            ctT = cT_ref[j]
            for h in range(DSA_HEADS):
                m_old, m_new = stats[slot][h]
                alpha = jnp.exp2(m_old - m_new)
                m16 = jnp.concatenate([m_new, m_new], axis=0).astype(BF16)
                x = lg_ref[slot, h].reshape(RB, BF16_ROWS, TQ) - m16[None]
                pv = jnp.dot(ctT, jnp.exp2(x).reshape(TK, TQ), preferred_element_type=F32)
                acc3 = acc_ref[h].reshape(LG, SUBLANES, TQ) * alpha[None]
                acc_ref[h] = acc3.reshape(CT_ROWS, TQ) + pv
        for h in range(DSA_HEADS):
            m_ref[h] = m_run[h]

    nfar = jnp.maximum(i - 1, 0)

    def far_pair(p, carry):
        attend([(2 * p, None), (2 * p + 1, None)])
        return carry

    lax.fori_loop(0, lax.shift_right_logical(nfar, 1), far_pair, 0)

    @pl.when((nfar & 1) == 1)
    def _():
        attend([(nfar - 1, None)])

    @pl.when(i > 0)
    def _():
        attend([(i - 1, 1), (i, 0)])

    @pl.when(i == 0)
    def _():
        attend([(0, 0)])

    outs = []
    for h in range(DSA_HEADS):
        rl = 1.0 / acc_ref[h, DSA_LATENT:DSA_LATENT + SUBLANES, :]
        o3 = acc_ref[h, :DSA_LATENT, :].reshape(DSA_LATENT // SUBLANES, SUBLANES, TQ) * rl[None]
        o = o3.reshape(DSA_LATENT, TQ).astype(BF16)
        outs.append(jnp.dot(wuvT_ref[h], o, preferred_element_type=F32))
    o_ref[...] = jnp.concatenate(outs, axis=0).T.astype(BF16)


def _dsa(qT, qiT, wT, kidx, c, cT, nb, wuvT):
    B, S, _ = c.shape
    nk = S // TK
    grid = (B, S // TQ)
    tileT = lambda r: pl.BlockSpec((None, r, TQ), lambda b, i: (b, 0, i))
    seq = lambda w: pl.BlockSpec((None, S, w), lambda b, i: (b, 0, 0))
    in_specs = [tileT(DSA_HEADS * DSA_LATENT), tileT(IDX_HEADS * IDX_DIM), tileT(BF16_ROWS),
                seq(IDX_DIM), seq(DSA_LATENT),
                pl.BlockSpec((None, nk, CT_ROWS, TK), lambda b, i: (b, 0, 0, 0)),
                _const_spec(nb.shape), _const_spec(wuvT.shape)]
    scratch = [
        pltpu.VMEM((nk, TK, TQ), F32),
        pltpu.VMEM((nk, TK, TQ), BF16),
        pltpu.VMEM((2, SUBLANES, TQ), F32),
        pltpu.VMEM((nk, TK, TQ), F32),
        pltpu.VMEM((2, DSA_HEADS, TK, TQ), BF16),
        pltpu.VMEM((DSA_HEADS, SUBLANES, TQ), F32),
        pltpu.VMEM((DSA_HEADS, CT_ROWS, TQ), F32),
    ]
    return pl.pallas_call(
        _dsa_kernel, grid=grid, in_specs=in_specs,
        out_specs=pl.BlockSpec((None, TQ, DSA_WIDTH), lambda b, i: (b, i, 0)),
        out_shape=jax.ShapeDtypeStruct((B, S, DSA_WIDTH), BF16), scratch_shapes=scratch,
        compiler_params=_params(("parallel", "arbitrary")), name="dsa",
    )(qT, qiT, wT, kidx, c, cT, nb, wuvT)


def _s5_kernel(u_ref, bm_ref, cm_ref, lre_ref, lim_ref, dsk_ref, wglu_ref, o_ref,
               uslab_ref, uil_ref, st_ref, oslab_ref, hre_ref, him_ref):
    nstate = S5_GROUPS * S5_STATE
    nb = hre_ref.shape[0]
    nslab = S5_WIDTH // LANES

    @pl.when(pl.program_id(0) == 0)
    def _():
        hre_ref[...] = jnp.zeros(hre_ref.shape, F32)
        him_ref[...] = jnp.zeros(him_ref.shape, F32)

    for b in range(nb):
        for k in range(nslab):
            uslab_ref[k, b * S5_PITCH:b * S5_PITCH + T_S5, :] = u_ref[b, :, k * LANES:(k + 1) * LANES]

    def gather_step(t, carry):
        for k in range(nslab):
            uil_ref[pl.ds(pl.multiple_of(t * nb, nb), nb), k * LANES:(k + 1) * LANES] = (
                uslab_ref[k, pl.ds(t, nb, stride=S5_PITCH), :])
        return carry

    lax.fori_loop(0, T_S5, gather_step, 0, unroll=4)
    u = uil_ref[...]
    ub = u.astype(BF16)

    kin = 4
    cin = S5_WIDTH // kin
    sin = nstate // kin
    for k in range(kin):
        uk = ub[:, k * cin:(k + 1) * cin]
        for off in (0, nstate):
            cols = slice(off + k * sin, off + (k + 1) * sin)
            st_ref[:, cols] = jnp.dot(uk, bm_ref[k * cin:(k + 1) * cin, cols],
                                      preferred_element_type=F32)

    half = nstate // 2
    for part in range(2):
        re_sl = slice(part * half, (part + 1) * half)
        im_sl = slice(nstate + part * half, nstate + (part + 1) * half)
        lre = lre_ref[:, re_sl]
        lim = lim_ref[:, re_sl]

        def step(t, carry):
            hr, hi = carry
            r = pl.ds(pl.multiple_of(t * nb, nb), nb)
            nr = lre * hr - lim * hi + st_ref[r, re_sl]
            ni = lre * hi + lim * hr + st_ref[r, im_sl]
            st_ref[r, re_sl] = nr
            st_ref[r, im_sl] = ni
            return nr, ni

        hr, hi = lax.fori_loop(0, T_S5, step, (hre_ref[:, re_sl], him_ref[:, re_sl]), unroll=4)
        hre_ref[:, re_sl] = hr
        him_ref[:, re_sl] = hi

    kout = 2
    cout = S5_WIDTH // kout
    sout = nstate // kout
    ys = []
    for k in range(kout):
        acc = None
        for off in (0, nstate):
            rows = slice(off + k * sout, off + (k + 1) * sout)
            part = jnp.dot(st_ref[:, rows].astype(BF16), cm_ref[rows, k * cout:(k + 1) * cout],
                           preferred_element_type=F32)
            acc = part if acc is None else acc + part
        ys.append(acc)
    y = jnp.concatenate(ys, axis=1)
    y = jax.nn.gelu(y + dsk_ref[...] * u)
    z = jnp.dot(y.astype(BF16), wglu_ref[...], preferred_element_type=F32)
    o = y * jax.nn.sigmoid(z)

    for k in range(nslab):
        oslab_ref[k] = o[:, k * LANES:(k + 1) * LANES]
    for b in range(nb):
        for k in range(nslab):
            o_ref[b, :, k * LANES:(k + 1) * LANES] = (
                oslab_ref[k, pl.ds(b, T_S5, stride=nb), :].astype(BF16))


def _s5(u, bm, cm, lre, lim, dsk, wglu):
    B, S, W = u.shape
    tb = T_S5 * B
    nstate = S5_GROUPS * S5_STATE
    tok = pl.BlockSpec((B, T_S5, W), lambda t: (0, t, 0))
    in_specs = [tok, _const_spec(bm.shape), _const_spec(cm.shape), _const_spec(lre.shape),
                _const_spec(lim.shape), _const_spec(dsk.shape), _const_spec(wglu.shape)]
    scratch = [pltpu.VMEM((W // LANES, B * S5_PITCH, LANES), F32),
               pltpu.VMEM((tb, W), F32),
               pltpu.VMEM((tb, 2 * nstate), F32),
               pltpu.VMEM((W // LANES, tb, LANES), F32),
               pltpu.VMEM((B, nstate), F32), pltpu.VMEM((B, nstate), F32)]
    return pl.pallas_call(
        _s5_kernel, grid=(S // T_S5,), in_specs=in_specs, out_specs=tok,
        out_shape=jax.ShapeDtypeStruct((B, S, W), BF16), scratch_shapes=scratch,
        compiler_params=_params(("arbitrary",)), name="s5",
    )(u, bm, cm, lre, lim, dsk, wglu)


def _merge_kernel(x_ref, odsa_ref, os5_ref, mem_ref, gmix_ref, wg_ref, wqx_ref, gqx_ref,
                  wb1_ref, wb2_ref, wb3_ref, wout_ref, gmem_ref, wkv_ref, gk_ref, y_ref,
                  k_ref, v_ref):
    @pl.when(pl.program_id(1) == 0)
    def _():
        mb = _rms(mem_ref[...], gmem_ref[...]).astype(BF16)
        kv = jnp.dot(mb, wkv_ref[...], preferred_element_type=F32)
        for h in range(X_HEADS):
            sl = slice(h * X_HEAD_DIM, (h + 1) * X_HEAD_DIM)
            k_ref[:, sl] = _rms(kv[:, sl], gk_ref[...]).astype(BF16)
        v_ref[...] = kv[:, X_WIDTH:].astype(BF16)

    x = x_ref[...]
    hb = _rms(x, gmix_ref[...]).astype(BF16)

    qx = jnp.dot(hb, wqx_ref[...], preferred_element_type=F32)
    gqx = gqx_ref[...] * (X_HEAD_DIM ** -0.5)
    ox = []
    for h in range(X_HEADS):
        sl = slice(h * X_HEAD_DIM, (h + 1) * X_HEAD_DIM)
        qh = _rms(qx[:, sl], gqx).astype(BF16)
        lg = lax.dot_general(qh, k_ref[:, sl], NT_DIMS, preferred_element_type=F32)
        p = jnp.exp(lg - jnp.max(lg, axis=-1, keepdims=True))
        pv = jnp.dot(p.astype(BF16), v_ref[:, sl], preferred_element_type=F32)
        ox.append((pv / jnp.sum(p, axis=-1, keepdims=True)).astype(BF16))
    ox = jnp.concatenate(ox, axis=1)

    merged = None
    for br, (o, wb) in enumerate(((odsa_ref[...], wb1_ref), (os5_ref[...], wb2_ref), (ox, wb3_ref))):
        gate = jax.nn.sigmoid(jnp.dot(hb, wg_ref[:, br * D_MODEL:(br + 1) * D_MODEL],
                                      preferred_element_type=F32))
        term = gate * jnp.dot(o, wb[...], preferred_element_type=F32)
        merged = term if merged is None else merged + term
    y_ref[...] = x + jnp.dot(merged.astype(BF16), wout_ref[...], preferred_element_type=F32)


def _merge(x, odsa, os5, mem, gmix, wg, wqx, gqx, wb1, wb2, wb3, wout, gmem, wkv, gk):
    B, S, D = x.shape
    ts = TS_MERGE
    tok = lambda w: pl.BlockSpec((None, ts, w), lambda b, s: (b, s, 0))
    consts = (gmix, wg, wqx, gqx, wb1, wb2, wb3, wout, gmem, wkv, gk)
    in_specs = [tok(D), tok(DSA_WIDTH), tok(S5_WIDTH),
                pl.BlockSpec((None, N_MEM, D), lambda b, s: (b, 0, 0))
                ] + [_const_spec(a.shape) for a in consts]
    scratch = [pltpu.VMEM((N_MEM, X_WIDTH), BF16), pltpu.VMEM((N_MEM, X_WIDTH), BF16)]
    return pl.pallas_call(
        _merge_kernel, grid=(B, S // ts), in_specs=in_specs, out_specs=tok(D),
        out_shape=jax.ShapeDtypeStruct((B, S, D), F32), scratch_shapes=scratch,
        compiler_params=_params(("parallel", "arbitrary")), name="merge",
    )(x, odsa, os5, mem, *consts)


def _ffn_kernel(x_ref, g_ref, wg_ref, wu_ref, wd_ref, y_ref):
    x = x_ref[...]
    hb = _rms(x, g_ref[...]).astype(BF16)
    a = jnp.dot(hb, wg_ref[...], preferred_element_type=F32)
    b = jnp.dot(hb, wu_ref[...], preferred_element_type=F32)
    act = (jax.nn.silu(a) * b).astype(BF16)
    y_ref[...] = x + jnp.dot(act, wd_ref[...], preferred_element_type=F32)


def _ffn(x2, g, wg, wu, wd):
    n, D = x2.shape
    tok = pl.BlockSpec((TS_FFN, D), lambda t: (t, 0))
    return pl.pallas_call(
        _ffn_kernel, grid=(n // TS_FFN,),
        in_specs=[tok] + [_const_spec(a.shape) for a in (g, wg, wu, wd)],
        out_specs=tok, out_shape=jax.ShapeDtypeStruct((n, D), F32),
        compiler_params=_params(("parallel",)), name="ffn",
    )(x2, g, wg, wu, wd)


def _t5_bucket(n):
    max_exact = REL_BUCKETS // 2
    nf = jnp.maximum(n, 1).astype(F32)
    large = max_exact + (jnp.log(nf / max_exact) / math.log(REL_MAX_DIST / max_exact)
                         * (REL_BUCKETS - max_exact)).astype(jnp.int32)
    large = jnp.minimum(large, REL_BUCKETS - 1)
    return jnp.where(n < max_exact, n, large)


def _toeplitz(w, rows, cols):
    H, L = w.shape
    flat = jnp.tile(w, (1, rows))[:, :rows * (L - 1)]
    return flat.reshape(H, rows, L - 1)[:, :, :cols]


def _near_bias(rel_bias):
    n = jnp.arange(2 * TQ, dtype=jnp.int32)
    f = (rel_bias[_t5_bucket(n)] - rel_bias[REL_BUCKETS - 1][None, :]).T * LOG2E
    w_diag = jnp.concatenate([f[:, :TQ], jnp.broadcast_to(f[:, :1], (f.shape[0], TK - 1))], axis=1)
    w_prev = jnp.concatenate([f[:, TQ:2 * TQ], f[:, 1:TK]], axis=1)
    return jnp.stack([_toeplitz(w_diag, TK, TQ), _toeplitz(w_prev, TK, TQ)], axis=0).astype(F32)


def _s5_mats(a_re, a_im, log_dt, b_re, b_im, c_re, c_im):
    lam = lax.complex(a_re.astype(F32), a_im.astype(F32))
    dt = jnp.exp(log_dt.astype(F32))[:, None]
    lam_bar = jnp.exp(lam * dt)
    b_bar = ((lam_bar - 1.0) / lam)[..., None] * lax.complex(b_re.astype(F32), b_im.astype(F32))
    nstate = S5_GROUPS * S5_STATE
    in_mask = (jnp.arange(S5_WIDTH)[:, None] // S5_GROUP) == (jnp.arange(nstate)[None, :] // S5_STATE)

    def blockdiag_in(w):
        t = jnp.transpose(w, (0, 2, 1)).reshape(S5_WIDTH, S5_STATE)
        return jnp.where(in_mask, jnp.tile(t, (1, S5_GROUPS)), 0.0)

    def blockdiag_out(w):
        t = jnp.transpose(w, (0, 2, 1)).reshape(nstate, S5_GROUP)
        return jnp.where(in_mask.T, jnp.tile(t, (1, S5_GROUPS)), 0.0)

    bm = jnp.concatenate([blockdiag_in(jnp.real(b_bar)), blockdiag_in(jnp.imag(b_bar))], axis=1)
    cm = jnp.concatenate([blockdiag_out(c_re.astype(F32)), blockdiag_out(-c_im.astype(F32))], axis=0)
    return bm.astype(BF16), cm.astype(BF16), jnp.real(lam_bar).reshape(1, -1), jnp.imag(lam_bar).reshape(1, -1)


def kernel(x, mem, rel_bias, w_in, g_mix_norm, g_q_dsa, g_kv_dsa, w_uv_dsa, a_re, a_im, log_dt, b_re, b_im, c_re, c_im, d_skip, w_glu, g_mem_norm, w_mem_kv, g_q_cross, g_k_cross, w_br_dsa, w_br_s5, w_br_cross, w_out, g_ffn_norm, w_ffn_gate, w_ffn_up, w_ffn_down):
    B, S, D = x.shape
    depth = w_in.shape[0]
    offs = [0] + [int(o) for o in np.cumsum(IN_SPLITS)]
    nb = _near_bias(rel_bias)
    row = lambda v: v.reshape(1, -1).astype(F32)
    col = lambda v: v.reshape(-1, 1).astype(F32)
    bf = lambda a: a.astype(BF16)
    for l in range(depth):
        wq, wc, wqi, wk, ww, wu, wqx, wg = [bf(w_in[l, :, offs[k]:offs[k + 1]])
                                            for k in range(len(IN_SPLITS))]
        wckw = jnp.pad(jnp.concatenate([wc, wk, ww], axis=1),
                       ((0, 0), (0, 2 * LANES - DSA_LATENT - IDX_DIM - IDX_HEADS)))
        gqc = col(g_q_dsa[l]) * (DSA_LATENT ** -0.5 * LOG2E)

        qT, cT, c, qiT, wT, kidx, u = _inproj(
            x, row(g_mix_norm[l]), wq.T, wckw, wqi.T, wu, gqc, row(g_kv_dsa[l]))

        wuvT = bf(jnp.transpose(w_uv_dsa[l], (0, 2, 1)))
        o_dsa = _dsa(qT, qiT, wT, kidx, c, cT, nb, wuvT)

        bm, cm, lre, lim = _s5_mats(a_re[l], a_im[l], log_dt[l], b_re[l], b_im[l], c_re[l], c_im[l])
        lre = jnp.broadcast_to(lre, (B, lre.shape[1]))
        lim = jnp.broadcast_to(lim, (B, lim.shape[1]))
        o_s5 = _s5(u, bm, cm, lre, lim, row(d_skip[l]), bf(w_glu[l]))

        x1 = _merge(x, o_dsa, o_s5, mem, row(g_mix_norm[l]), wg, wqx, row(g_q_cross[l]),
                    bf(w_br_dsa[l]), bf(w_br_s5[l]), bf(w_br_cross[l]), bf(w_out[l]),
                    row(g_mem_norm[l]), bf(w_mem_kv[l]), row(g_k_cross[l]))

        x = _ffn(x1.reshape(B * S, D), row(g_ffn_norm[l]), bf(w_ffn_gate[l]), bf(w_ffn_up[l]),
                 bf(w_ffn_down[l])).reshape(B, S, D)
    return x
```

```python
import math

import jax
import jax.numpy as jnp
import numpy as np
from jax import lax
from jax.experimental import pallas as pl
from jax.experimental.pallas import tpu as pltpu

F32 = jnp.float32
BF16 = jnp.bfloat16

D_MODEL = 1024
N_MEM = 256
EPS = 1e-6
DSA_HEADS = 8
DSA_LATENT = 128
DSA_VDIM = 64
IDX_HEADS = 8
IDX_DIM = 64
TOPK_MAX = 256
REL_BUCKETS = 32
REL_MAX_DIST = 128
S5_WIDTH = 512
S5_GROUP = 16
S5_GROUPS = S5_WIDTH // S5_GROUP
S5_STATE = 64
X_HEADS = 4
X_HEAD_DIM = 128
DSA_WIDTH = DSA_HEADS * DSA_VDIM
X_WIDTH = X_HEADS * X_HEAD_DIM
N_BRANCH = 3
IN_SPLITS = (DSA_HEADS * DSA_LATENT, DSA_LATENT, IDX_HEADS * IDX_DIM, IDX_DIM,
             IDX_HEADS, S5_WIDTH, X_WIDTH, N_BRANCH * D_MODEL)

LANES = 128
SUBLANES = 8
BF16_ROWS = 16
CT_ROWS = DSA_LATENT + BF16_ROWS
VMEM_LIMIT = 56 * 1024 * 1024
LOG2E = math.log2(math.e)

TS_IN = 1024
TQ = 256
TK = 256
N_COARSE = 8
N_BISECT = 8
T_S5 = 128
S5_PITCH = T_S5 + SUBLANES
TS_MERGE = 1024
TS_FFN = 512
NEG = -(2.0 ** 100)

NT_DIMS = (((1,), (1,)), ((), ()))


def _rms(x, g):
    ms = jnp.mean(x * x, axis=-1, keepdims=True)
    return x * lax.rsqrt(ms + EPS) * g


def _tree(fn, x):
    while x.shape[0] > 1:
        half = x.shape[0] // 2
        x = fn(x[:half], x[half:])
    return x[0]


def _const_spec(shape):
    nd = len(shape)
    return pl.BlockSpec(shape, lambda *_: (0,) * nd, pipeline_mode=pl.Buffered(1))


def _params(sem):
    return pltpu.CompilerParams(dimension_semantics=sem, vmem_limit_bytes=VMEM_LIMIT)


def _inproj_kernel(x_ref, gmix_ref, wqT_ref, wckw_ref, wqiT_ref, wu_ref,
                   gqc_ref, gkv_ref,
                   qT_ref, cT_ref, c_ref, qiT_ref, wT_ref, kidx_ref, u_ref):
    ts = x_ref.shape[0]
    hb = _rms(x_ref[...], gmix_ref[...]).astype(BF16)

    qT = jnp.dot(hb, wqT_ref[...], preferred_element_type=F32).T
    q3 = qT.reshape(DSA_HEADS, DSA_LATENT, ts)
    ms = jnp.mean(q3 * q3, axis=1, keepdims=True)
    qT_ref[...] = (q3 * lax.rsqrt(ms + EPS) * gqc_ref[...][None]).reshape(qT.shape).astype(BF16)

    ckw = jnp.dot(hb, wckw_ref[...], preferred_element_type=F32)
    kw = ckw[:, DSA_LATENT:]
    kidx_ref[...] = kw[:, :IDX_DIM].astype(BF16)
    wT_ref[...] = kw.T[IDX_DIM:IDX_DIM + BF16_ROWS, :]

    cn = _rms(ckw[:, :DSA_LATENT], gkv_ref[...])
    c_ref[...] = cn.astype(BF16)
    cTn = cn.T.astype(BF16)
    for k in range(ts // TK):
        cT_ref[k, :DSA_LATENT, :] = cTn[:, k * TK:(k + 1) * TK]
        cT_ref[k, DSA_LATENT:, :] = jnp.ones((BF16_ROWS, TK), BF16)

    qiT_ref[...] = jnp.dot(hb, wqiT_ref[...], preferred_element_type=F32).T.astype(BF16)

    u_ref[...] = jnp.dot(hb, wu_ref[...], preferred_element_type=F32)


def _inproj(x, gmix, wqT, wckw, wqiT, wu, gqc, gkv):
    B, S, D = x.shape
    ts = TS_IN
    grid = (B, S // ts)
    tok = lambda w: pl.BlockSpec((None, ts, w), lambda b, s: (b, s, 0))
    tokT = lambda r: pl.BlockSpec((None, r, ts), lambda b, s: (b, 0, s))
    hq = DSA_HEADS * DSA_LATENT
    hi = IDX_HEADS * IDX_DIM
    out_shape = (
        jax.ShapeDtypeStruct((B, hq, S), BF16),
        jax.ShapeDtypeStruct((B, S // TK, CT_ROWS, TK), BF16),
        jax.ShapeDtypeStruct((B, S, DSA_LATENT), BF16),
        jax.ShapeDtypeStruct((B, hi, S), BF16),
        jax.ShapeDtypeStruct((B, BF16_ROWS, S), F32),
        jax.ShapeDtypeStruct((B, S, IDX_DIM), BF16),
        jax.ShapeDtypeStruct((B, S, S5_WIDTH), F32),
    )
    out_specs = (tokT(hq),
                 pl.BlockSpec((None, ts // TK, CT_ROWS, TK), lambda b, s: (b, s, 0, 0)),
                 tok(DSA_LATENT), tokT(hi), tokT(BF16_ROWS), tok(IDX_DIM),
                 tok(S5_WIDTH))
    consts = (gmix, wqT, wckw, wqiT, wu, gqc, gkv)
    in_specs = [tok(D)] + [_const_spec(a.shape) for a in consts]
    return pl.pallas_call(
        _inproj_kernel, grid=grid, in_specs=in_specs, out_specs=out_specs, out_shape=out_shape,
        compiler_params=_params(("parallel", "parallel")), name="inproj",
    )(x, *consts)


def _dsa_kernel(qT_ref, qiT_ref, wT_ref, kidx_ref, c_ref, cT_ref, nb_ref, wuvT_ref, o_ref,
                sc_ref, scb_ref, mm_ref, pref_ref, lg_ref, m_ref, acc_ref):
    i = pl.program_id(1)
    nk = sc_ref.shape[0]
    kf = float(TOPK_MAX)
    G = TK // SUBLANES
    RB = TK // BF16_ROWS

    def rep(fn, a):
        return jnp.broadcast_to(fn(a, axis=0, keepdims=True), (SUBLANES, TQ))

    def full(v, dt=F32):
        return jnp.full((SUBLANES, TQ), v, dt)

    def key_rows(j):
        return pl.ds(pl.multiple_of(j * TK, TK), TK)

    def score(j):
        ks = kidx_ref[key_rows(j), :]
        acc = None
        for h in range(IDX_HEADS):
            d = jnp.dot(ks, qiT_ref[h * IDX_DIM:(h + 1) * IDX_DIM, :], preferred_element_type=F32)
            t = jnp.maximum(d, 0.0) * wT_ref[h:h + 1, :]
            acc = t if acc is None else acc + t
        return acc

    mm_ref[0] = full(jnp.inf)
    mm_ref[1] = full(-jnp.inf)

    def put_scores(j, s, s_for_min):
        sc_ref[j] = s
        scb_ref[j] = s.astype(BF16)
        mm_ref[0] = jnp.minimum(mm_ref[0], _tree(jnp.minimum, s_for_min.reshape(G, SUBLANES, TQ)))
        mm_ref[1] = jnp.maximum(mm_ref[1], _tree(jnp.maximum, s.reshape(G, SUBLANES, TQ)))

    def score_pair(p, carry):
        for j in (2 * p, 2 * p + 1):
            s = score(j)
            put_scores(j, s, s)
        return carry

    lax.fori_loop(0, lax.shift_right_logical(i, 1), score_pair, 0)

    @pl.when((i & 1) == 1)
    def _():
        s = score(i - 1)
        put_scores(i - 1, s, s)

    key_t = lax.broadcasted_iota(jnp.int32, (TK, TQ), 0)
    qry_t = lax.broadcasted_iota(jnp.int32, (TK, TQ), 1)
    causal = key_t <= qry_t
    s_diag = score(i)
    put_scores(i, jnp.where(causal, s_diag, -jnp.inf), jnp.where(causal, s_diag, jnp.inf))

    @pl.when(i == 0)
    def _():
        sc_ref[0] = jnp.where(causal, 0.0, NEG)

    @pl.when(i > 0)
    def _():
        nt = i + 1

        def tile3(j):
            return sc_ref[j].reshape(G, SUBLANES, TQ)

        lo, hi = rep(jnp.min, mm_ref[0]), rep(jnp.max, mm_ref[1])

        def count_ge(thr):
            def body(j, acc):
                return acc + _tree(jnp.add, jnp.where(tile3(j) >= thr[None], 1.0, 0.0))
            return rep(jnp.sum, lax.fori_loop(0, nt, body, full(0.0)))

        bf_step = 2.0 ** -7
        tiny = 1e-30

        def as_bf16_value(x):
            return x.astype(BF16).astype(F32)

        def count_ge_coarse(thr):
            thr16 = jnp.concatenate([thr, thr], axis=0).astype(BF16)
            one, zero = jnp.ones((), BF16), jnp.zeros((), BF16)

            def body(j, acc):
                t = scb_ref[j].reshape(RB, BF16_ROWS, TQ)
                return acc + _tree(jnp.add, jnp.where(t >= thr16[None], one, zero))
            acc = lax.fori_loop(0, nt, body, jnp.zeros((BF16_ROWS, TQ), BF16))
            return rep(jnp.sum, acc.astype(F32))

        def coarse(_, carry):
            lo_c, hi_c = carry
            mid = as_bf16_value(0.5 * (lo_c + hi_c))
            ge = count_ge_coarse(mid) >= kf
            return jnp.where(ge, mid, lo_c), jnp.where(ge, hi_c, mid)

        lo_c = as_bf16_value(lo - jnp.abs(lo) * bf_step - tiny)
        hi_c = as_bf16_value(hi + jnp.abs(hi) * bf_step + tiny)
        lo_c, hi_c = lax.fori_loop(0, N_COARSE, coarse, (lo_c, hi_c))
        lo = lo_c - jnp.abs(lo_c) * bf_step - tiny
        hi = hi_c

        def bisect(_, carry):
            lo, hi, clo = carry
            mid = 0.5 * (lo + hi)
            cnt = count_ge(mid)
            ge = cnt >= kf
            return jnp.where(ge, mid, lo), jnp.where(ge, hi, mid), jnp.where(ge, cnt, clo)

        lo, hi, clo = lax.fori_loop(0, N_BISECT, bisect, (lo, hi, count_ge(lo)))

        def snap_body(j, am):
            s = tile3(j)
            return jnp.minimum(am, _tree(jnp.minimum, jnp.where(s >= lo[None], s, jnp.inf)))

        cur = rep(jnp.min, lax.fori_loop(0, nt, snap_body, full(jnp.inf)))

        def walk(cur):
            def body(j, carry):
                ac, am = carry
                s = tile3(j)
                g = s > cur[None]
                ac = ac + _tree(jnp.add, jnp.where(g, 1.0, 0.0))
                am = jnp.minimum(am, _tree(jnp.minimum, jnp.where(g, s, jnp.inf)))
                return ac, am
            ac, am = lax.fori_loop(0, nt, body, (full(0.0), full(jnp.inf)))
            return rep(jnp.sum, ac), rep(jnp.min, am)

        def walk_cond(carry):
            _, _, _, go, it = carry
            return jnp.logical_and(go > 0, it < nk * TK + 2)

        def walk_body(carry):
            cur, cge, _, _, it = carry
            c, nxt = walk(cur)
            move = c >= kf
            go = (jnp.max(jnp.where(move, 1.0, 0.0)) > 0.5).astype(jnp.int32)
            return jnp.where(move, nxt, cur), jnp.where(move, c, cge), c, go, it + 1

        kth, cge, cgt, _, _ = lax.while_loop(
            walk_cond, walk_body, (cur, clo, full(0.0), jnp.int32(1), jnp.int32(0)))
        need = kf - cgt
        has_excess = jnp.max(jnp.where(cge > kf, 1.0, 0.0)) > 0.5

        @pl.when(jnp.logical_not(has_excess))
        def _():
            def body(j, carry):
                sc_ref[j] = jnp.where(tile3(j) >= kth[None], 0.0, NEG).reshape(TK, TQ)
                return carry
            lax.fori_loop(0, nt, body, 0)

        @pl.when(has_excess)
        def _():
            tril = jnp.where(key_t >= qry_t, 1.0, 0.0).astype(BF16)

            def pref_body(p, carry):
                for j in (2 * p, jnp.minimum(2 * p + 1, nt - 1)):
                    e01 = jnp.where(tile3(j) == kth[None], 1.0, 0.0).reshape(TK, TQ).astype(BF16)
                    pref_ref[j] = jnp.dot(tril, e01, preferred_element_type=F32)
                return carry

            lax.fori_loop(0, lax.shift_right_logical(nt + 1, 1), pref_body, 0)

            def mask_body(j, offset):
                s = tile3(j)
                pref = pref_ref[j]
                rank = pref.reshape(G, SUBLANES, TQ) + offset[None]
                tie = jnp.where(rank <= need[None], 0.0, NEG)
                mb = jnp.where(s > kth[None], 0.0, jnp.where(s == kth[None], tie, NEG))
                sc_ref[j] = mb.reshape(TK, TQ)
                return offset + jnp.broadcast_to(pref[TK - 1:TK, :], (SUBLANES, TQ))

            lax.fori_loop(0, nt, mask_body, full(0.0))

    m_ref[...] = jnp.full(m_ref.shape, NEG, F32)
    acc_ref[...] = jnp.zeros(acc_ref.shape, F32)
    LG = CT_ROWS // SUBLANES

    def attend(tiles):
        m_run = [m_ref[h] for h in range(DSA_HEADS)]
        stats = []
        for slot, (j, near) in enumerate(tiles):
            ct = c_ref[key_rows(j), :]
            mbb = sc_ref[j].astype(BF16)
            per_head = []
            for h in range(DSA_HEADS):
                lg = jnp.dot(ct, qT_ref[h * DSA_LATENT:(h + 1) * DSA_LATENT, :],
                             preferred_element_type=F32)
                if near is not None:
                    lg = lg + nb_ref[near, h]
                lgb = lg.astype(BF16) + mbb
                lg_ref[slot, h] = lgb
                tmax = _tree(jnp.maximum, lgb.reshape(RB, BF16_ROWS, TQ)).astype(F32)
                m_new = jnp.maximum(m_run[h], rep(jnp.max, tmax))
                per_head.append((m_run[h], m_new))
                m_run[h] = m_new
            stats.append(per_head)
        for slot, (j, near) in enumerate(tiles):
            ctT = cT_ref[j]
            for h in range(DSA_HEADS):
                m_old, m_new = stats[slot][h]
                alpha = jnp.exp2(m_old - m_new)
                m16 = jnp.concatenate([m_new, m_new], axis=0).astype(BF16)
                x = lg_ref[slot, h].reshape(RB, BF16_ROWS, TQ) - m16[None]
                pv = jnp.dot(ctT, jnp.exp2(x).reshape(TK, TQ), preferred_element_type=F32)
                acc3 = acc_ref[h].reshape(LG, SUBLANES, TQ) * alpha[None]
                acc_ref[h] = acc3.reshape(CT_ROWS, TQ) + pv
        for h in range(DSA_HEADS):
            m_ref[h] = m_run[h]

    nfar = jnp.maximum(i - 1, 0)

    def far_pair(p, carry):
        attend([(2 * p, None), (2 * p + 1, None)])
        return carry

    lax.fori_loop(0, lax.shift_right_logical(nfar, 1), far_pair, 0)

    @pl.when((nfar & 1) == 1)
    def _():
        attend([(nfar - 1, None)])

    @pl.when(i > 0)
    def _():
        attend([(i - 1, 1), (i, 0)])

    @pl.when(i == 0)
    def _():
        attend([(0, 0)])

    outs = []
    for h in range(DSA_HEADS):
        rl = 1.0 / acc_ref[h, DSA_LATENT:DSA_LATENT + SUBLANES, :]
        o3 = acc_ref[h, :DSA_LATENT, :].reshape(DSA_LATENT // SUBLANES, SUBLANES, TQ) * rl[None]
        o = o3.reshape(DSA_LATENT, TQ).astype(BF16)
        outs.append(jnp.dot(wuvT_ref[h], o, preferred_element_type=F32))
    o_ref[...] = jnp.concatenate(outs, axis=0).T.astype(BF16)


def _dsa(qT, qiT, wT, kidx, c, cT, nb, wuvT):
    B, S, _ = c.shape
    nk = S // TK
    grid = (B, S // TQ)
    tileT = lambda r: pl.BlockSpec((None, r, TQ), lambda b, i: (b, 0, i))
    seq = lambda w: pl.BlockSpec((None, S, w), lambda b, i: (b, 0, 0))
    in_specs = [tileT(DSA_HEADS * DSA_LATENT), tileT(IDX_HEADS * IDX_DIM), tileT(BF16_ROWS),
                seq(IDX_DIM), seq(DSA_LATENT),
                pl.BlockSpec((None, nk, CT_ROWS, TK), lambda b, i: (b, 0, 0, 0)),
                _const_spec(nb.shape), _const_spec(wuvT.shape)]
    scratch = [
        pltpu.VMEM((nk, TK, TQ), F32),
        pltpu.VMEM((nk, TK, TQ), BF16),
        pltpu.VMEM((2, SUBLANES, TQ), F32),
        pltpu.VMEM((nk, TK, TQ), F32),
        pltpu.VMEM((2, DSA_HEADS, TK, TQ), BF16),
        pltpu.VMEM((DSA_HEADS, SUBLANES, TQ), F32),
        pltpu.VMEM((DSA_HEADS, CT_ROWS, TQ), F32),
    ]
    return pl.pallas_call(
        _dsa_kernel, grid=grid, in_specs=in_specs,
        out_specs=pl.BlockSpec((None, TQ, DSA_WIDTH), lambda b, i: (b, i, 0)),
        out_shape=jax.ShapeDtypeStruct((B, S, DSA_WIDTH), BF16), scratch_shapes=scratch,
        compiler_params=_params(("parallel", "arbitrary")), name="dsa",
    )(qT, qiT, wT, kidx, c, cT, nb, wuvT)


def _s5_kernel(u_ref, bm_ref, cm_ref, lre_ref, lim_ref, dsk_ref, wglu_ref, o_ref,
               uslab_ref, uil_ref, st_ref, oslab_ref, hre_ref, him_ref):
    nstate = S5_GROUPS * S5_STATE
    nb = hre_ref.shape[0]
    nslab = S5_WIDTH // LANES

    @pl.when(pl.program_id(0) == 0)
    def _():
        hre_ref[...] = jnp.zeros(hre_ref.shape, F32)
        him_ref[...] = jnp.zeros(him_ref.shape, F32)

    for b in range(nb):
        for k in range(nslab):
            uslab_ref[k, b * S5_PITCH:b * S5_PITCH + T_S5, :] = u_ref[b, :, k * LANES:(k + 1) * LANES]

    def gather_step(t, carry):
        for k in range(nslab):
            uil_ref[pl.ds(pl.multiple_of(t * nb, nb), nb), k * LANES:(k + 1) * LANES] = (
                uslab_ref[k, pl.ds(t, nb, stride=S5_PITCH), :])
        return carry

    lax.fori_loop(0, T_S5, gather_step, 0, unroll=4)
    u = uil_ref[...]
    ub = u.astype(BF16)

    kin = 4
    cin = S5_WIDTH // kin
    sin = nstate // kin
    for k in range(kin):
        uk = ub[:, k * cin:(k + 1) * cin]
        for off in (0, nstate):
            cols = slice(off + k * sin, off + (k + 1) * sin)
            st_ref[:, cols] = jnp.dot(uk, bm_ref[k * cin:(k + 1) * cin, cols],
                                      preferred_element_type=F32)

    half = nstate // 2
    for part in range(2):
        re_sl = slice(part * half, (part + 1) * half)
        im_sl = slice(nstate + part * half, nstate + (part + 1) * half)
        lre = lre_ref[:, re_sl]
        lim = lim_ref[:, re_sl]

        def step(t, carry):
            hr, hi = carry
            r = pl.ds(pl.multiple_of(t * nb, nb), nb)
            nr = lre * hr - lim * hi + st_ref[r, re_sl]
            ni = lre * hi + lim * hr + st_ref[r, im_sl]
            st_ref[r, re_sl] = nr
            st_ref[r, im_sl] = ni
            return nr, ni

        hr, hi = lax.fori_loop(0, T_S5, step, (hre_ref[:, re_sl], him_ref[:, re_sl]), unroll=4)
        hre_ref[:, re_sl] = hr
        him_ref[:, re_sl] = hi

    kout = 2
    cout = S5_WIDTH // kout
    sout = nstate // kout
    ys = []
    for k in range(kout):
        acc = None
        for off in (0, nstate):
            rows = slice(off + k * sout, off + (k + 1) * sout)
            part = jnp.dot(st_ref[:, rows].astype(BF16), cm_ref[rows, k * cout:(k + 1) * cout],
                           preferred_element_type=F32)
            acc = part if acc is None else acc + part
        ys.append(acc)
    y = jnp.concatenate(ys, axis=1)
    y = jax.nn.gelu(y + dsk_ref[...] * u)
    z = jnp.dot(y.astype(BF16), wglu_ref[...], preferred_element_type=F32)
    o = y * jax.nn.sigmoid(z)

    for k in range(nslab):
        oslab_ref[k] = o[:, k * LANES:(k + 1) * LANES]
    for b in range(nb):
        for k in range(nslab):
            o_ref[b, :, k * LANES:(k + 1) * LANES] = (
                oslab_ref[k, pl.ds(b, T_S5, stride=nb), :].astype(BF16))


def _s5(u, bm, cm, lre, lim, dsk, wglu):
    B, S, W = u.shape
    tb = T_S5 * B
    nstate = S5_GROUPS * S5_STATE
    tok = pl.BlockSpec((B, T_S5, W), lambda t: (0, t, 0))
    in_specs = [tok, _const_spec(bm.shape), _const_spec(cm.shape), _const_spec(lre.shape),
                _const_spec(lim.shape), _const_spec(dsk.shape), _const_spec(wglu.shape)]
    scratch = [pltpu.VMEM((W // LANES, B * S5_PITCH, LANES), F32),
               pltpu.VMEM((tb, W), F32),
               pltpu.VMEM((tb, 2 * nstate), F32),
               pltpu.VMEM((W // LANES, tb, LANES), F32),
               pltpu.VMEM((B, nstate), F32), pltpu.VMEM((B, nstate), F32)]
    return pl.pallas_call(
        _s5_kernel, grid=(S // T_S5,), in_specs=in_specs, out_specs=tok,
        out_shape=jax.ShapeDtypeStruct((B, S, W), BF16), scratch_shapes=scratch,
        compiler_params=_params(("arbitrary",)), name="s5",
    )(u, bm, cm, lre, lim, dsk, wglu)


def _merge_kernel(x_ref, odsa_ref, os5_ref, mem_ref, gmix_ref, wg_ref, wqx_ref, gqx_ref,
                  wb1_ref, wb2_ref, wb3_ref, wout_ref, gmem_ref, wkv_ref, gk_ref, y_ref,
                  k_ref, v_ref):
    @pl.when(pl.program_id(1) == 0)
    def _():
        mb = _rms(mem_ref[...], gmem_ref[...]).astype(BF16)
        kv = jnp.dot(mb, wkv_ref[...], preferred_element_type=F32)
        for h in range(X_HEADS):
            sl = slice(h * X_HEAD_DIM, (h + 1) * X_HEAD_DIM)
            k_ref[:, sl] = _rms(kv[:, sl], gk_ref[...]).astype(BF16)
        v_ref[...] = kv[:, X_WIDTH:].astype(BF16)

    x = x_ref[...]
    hb = _rms(x, gmix_ref[...]).astype(BF16)

    qx = jnp.dot(hb, wqx_ref[...], preferred_element_type=F32)
    gqx = gqx_ref[...] * (X_HEAD_DIM ** -0.5)
    ox = []
    for h in range(X_HEADS):
        sl = slice(h * X_HEAD_DIM, (h + 1) * X_HEAD_DIM)
        qh = _rms(qx[:, sl], gqx).astype(BF16)
        lg = lax.dot_general(qh, k_ref[:, sl], NT_DIMS, preferred_element_type=F32)
        p = jnp.exp(lg - jnp.max(lg, axis=-1, keepdims=True))
        pv = jnp.dot(p.astype(BF16), v_ref[:, sl], preferred_element_type=F32)
        ox.append((pv / jnp.sum(p, axis=-1, keepdims=True)).astype(BF16))
    ox = jnp.concatenate(ox, axis=1)

    merged = None
    for br, (o, wb) in enumerate(((odsa_ref[...], wb1_ref), (os5_ref[...], wb2_ref), (ox, wb3_ref))):
        gate = jax.nn.sigmoid(jnp.dot(hb, wg_ref[:, br * D_MODEL:(br + 1) * D_MODEL],
                                      preferred_element_type=F32))
        term = gate * jnp.dot(o, wb[...], preferred_element_type=F32)
        merged = term if merged is None else merged + term
    y_ref[...] = x + jnp.dot(merged.astype(BF16), wout_ref[...], preferred_element_type=F32)


def _merge(x, odsa, os5, mem, gmix, wg, wqx, gqx, wb1, wb2, wb3, wout, gmem, wkv, gk):
    B, S, D = x.shape
    ts = TS_MERGE
    tok = lambda w: pl.BlockSpec((None, ts, w), lambda b, s: (b, s, 0))
    consts = (gmix, wg, wqx, gqx, wb1, wb2, wb3, wout, gmem, wkv, gk)
    in_specs = [tok(D), tok(DSA_WIDTH), tok(S5_WIDTH),
                pl.BlockSpec((None, N_MEM, D), lambda b, s: (b, 0, 0))
                ] + [_const_spec(a.shape) for a in consts]
    scratch = [pltpu.VMEM((N_MEM, X_WIDTH), BF16), pltpu.VMEM((N_MEM, X_WIDTH), BF16)]
    return pl.pallas_call(
        _merge_kernel, grid=(B, S // ts), in_specs=in_specs, out_specs=tok(D),
        out_shape=jax.ShapeDtypeStruct((B, S, D), F32), scratch_shapes=scratch,
        compiler_params=_params(("parallel", "arbitrary")), name="merge",
    )(x, odsa, os5, mem, *consts)


def _ffn_kernel(x_ref, g_ref, wg_ref, wu_ref, wd_ref, y_ref):
    x = x_ref[...]
    hb = _rms(x, g_ref[...]).astype(BF16)
    a = jnp.dot(hb, wg_ref[...], preferred_element_type=F32)
    b = jnp.dot(hb, wu_ref[...], preferred_element_type=F32)
    act = (jax.nn.silu(a) * b).astype(BF16)
    y_ref[...] = x + jnp.dot(act, wd_ref[...], preferred_element_type=F32)


def _ffn(x2, g, wg, wu, wd):
    n, D = x2.shape
    tok = pl.BlockSpec((TS_FFN, D), lambda t: (t, 0))
    return pl.pallas_call(
        _ffn_kernel, grid=(n // TS_FFN,),
        in_specs=[tok] + [_const_spec(a.shape) for a in (g, wg, wu, wd)],
        out_specs=tok, out_shape=jax.ShapeDtypeStruct((n, D), F32),
        compiler_params=_params(("parallel",)), name="ffn",
    )(x2, g, wg, wu, wd)


def _t5_bucket(n):
    max_exact = REL_BUCKETS // 2
    nf = jnp.maximum(n, 1).astype(F32)
    large = max_exact + (jnp.log(nf / max_exact) / math.log(REL_MAX_DIST / max_exact)
                         * (REL_BUCKETS - max_exact)).astype(jnp.int32)
    large = jnp.minimum(large, REL_BUCKETS - 1)
    return jnp.where(n < max_exact, n, large)


def _toeplitz(w, rows, cols):
    H, L = w.shape
    flat = jnp.tile(w, (1, rows))[:, :rows * (L - 1)]
    return flat.reshape(H, rows, L - 1)[:, :, :cols]


def _near_bias(rel_bias):
    n = jnp.arange(2 * TQ, dtype=jnp.int32)
    f = (rel_bias[_t5_bucket(n)] - rel_bias[REL_BUCKETS - 1][None, :]).T * LOG2E
    w_diag = jnp.concatenate([f[:, :TQ], jnp.broadcast_to(f[:, :1], (f.shape[0], TK - 1))], axis=1)
    w_prev = jnp.concatenate([f[:, TQ:2 * TQ], f[:, 1:TK]], axis=1)
    return jnp.stack([_toeplitz(w_diag, TK, TQ), _toeplitz(w_prev, TK, TQ)], axis=0).astype(F32)


def _s5_mats(a_re, a_im, log_dt, b_re, b_im, c_re, c_im):
    lam = lax.complex(a_re.astype(F32), a_im.astype(F32))
    dt = jnp.exp(log_dt.astype(F32))[:, None]
    lam_bar = jnp.exp(lam * dt)
    b_bar = ((lam_bar - 1.0) / lam)[..., None] * lax.complex(b_re.astype(F32), b_im.astype(F32))
    nstate = S5_GROUPS * S5_STATE
    in_mask = (jnp.arange(S5_WIDTH)[:, None] // S5_GROUP) == (jnp.arange(nstate)[None, :] // S5_STATE)

    def blockdiag_in(w):
        t = jnp.transpose(w, (0, 2, 1)).reshape(S5_WIDTH, S5_STATE)
        return jnp.where(in_mask, jnp.tile(t, (1, S5_GROUPS)), 0.0)

    def blockdiag_out(w):
        t = jnp.transpose(w, (0, 2, 1)).reshape(nstate, S5_GROUP)
        return jnp.where(in_mask.T, jnp.tile(t, (1, S5_GROUPS)), 0.0)

    bm = jnp.concatenate([blockdiag_in(jnp.real(b_bar)), blockdiag_in(jnp.imag(b_bar))], axis=1)
    cm = jnp.concatenate([blockdiag_out(c_re.astype(F32)), blockdiag_out(-c_im.astype(F32))], axis=0)
    return bm.astype(BF16), cm.astype(BF16), jnp.real(lam_bar).reshape(1, -1), jnp.imag(lam_bar).reshape(1, -1)


def kernel(x, mem, rel_bias, w_in, g_mix_norm, g_q_dsa, g_kv_dsa, w_uv_dsa, a_re, a_im, log_dt, b_re, b_im, c_re, c_im, d_skip, w_glu, g_mem_norm, w_mem_kv, g_q_cross, g_k_cross, w_br_dsa, w_br_s5, w_br_cross, w_out, g_ffn_norm, w_ffn_gate, w_ffn_up, w_ffn_down):
    B, S, D = x.shape
    depth = w_in.shape[0]
    offs = [0] + [int(o) for o in np.cumsum(IN_SPLITS)]
    nb = _near_bias(rel_bias)
    row = lambda v: v.reshape(1, -1).astype(F32)
    col = lambda v: v.reshape(-1, 1).astype(F32)
    bf = lambda a: a.astype(BF16)
    for l in range(depth):
        wq, wc, wqi, wk, ww, wu, wqx, wg = [bf(w_in[l, :, offs[k]:offs[k + 1]])
                                            for k in range(len(IN_SPLITS))]
        wckw = jnp.pad(jnp.concatenate([wc, wk, ww], axis=1),
                       ((0, 0), (0, 2 * LANES - DSA_LATENT - IDX_DIM - IDX_HEADS)))
        gqc = col(g_q_dsa[l]) * (DSA_LATENT ** -0.5 * LOG2E)

        qT, cT, c, qiT, wT, kidx, u = _inproj(
            x, row(g_mix_norm[l]), wq, wckw, wqi, wu, gqc, row(g_kv_dsa[l]))

        wuvT = bf(jnp.transpose(w_uv_dsa[l], (0, 2, 1)))
        o_dsa = _dsa(qT, qiT, wT, kidx, c, cT, nb, wuvT)

        bm, cm, lre, lim = _s5_mats(a_re[l], a_im[l], log_dt[l], b_re[l], b_im[l], c_re[l], c_im[l])
        lre = jnp.broadcast_to(lre, (B, lre.shape[1]))
        lim = jnp.broadcast_to(lim, (B, lim.shape[1]))
        o_s5 = _s5(u, bm, cm, lre, lim, row(d_skip[l]), bf(w_glu[l]))

        x1 = _merge(x, o_dsa, o_s5, mem, row(g_mix_norm[l]), wg, wqx, row(g_q_cross[l]),
                    bf(w_br_dsa[l]), bf(w_br_s5[l]), bf(w_br_cross[l]), bf(w_out[l]),
                    row(g_mem_norm[l]), bf(w_mem_kv[l]), row(g_k_cross[l]))

        x = _ffn(x1.reshape(B * S, D), row(g_ffn_norm[l]), bf(w_ffn_gate[l]), bf(w_ffn_up[l]),
                 bf(w_ffn_down[l])).reshape(B, S, D)
    return x
```

```python
import math

import jax
import jax.numpy as jnp
import numpy as np
from jax import lax
from jax.experimental import pallas as pl
from jax.experimental.pallas import tpu as pltpu

F32 = jnp.float32
BF16 = jnp.bfloat16

D_MODEL = 1024
N_MEM = 256
EPS = 1e-6
DSA_HEADS = 8
DSA_LATENT = 128
DSA_VDIM = 64
IDX_HEADS = 8
IDX_DIM = 64
TOPK_MAX = 256
REL_BUCKETS = 32
REL_MAX_DIST = 128
S5_WIDTH = 512
S5_GROUP = 16
S5_GROUPS = S5_WIDTH // S5_GROUP
S5_STATE = 64
X_HEADS = 4
X_HEAD_DIM = 128
DSA_WIDTH = DSA_HEADS * DSA_VDIM
X_WIDTH = X_HEADS * X_HEAD_DIM
N_BRANCH = 3
IN_SPLITS = (DSA_HEADS * DSA_LATENT, DSA_LATENT, IDX_HEADS * IDX_DIM, IDX_DIM,
             IDX_HEADS, S5_WIDTH, X_WIDTH, N_BRANCH * D_MODEL)

LANES = 128
SUBLANES = 8
BF16_ROWS = 16
CT_ROWS = DSA_LATENT + BF16_ROWS
VMEM_LIMIT = 56 * 1024 * 1024
LOG2E = math.log2(math.e)

TS_IN = 1024
TQ = 256
TK = 256
NB = REL_MAX_DIST
N_COARSE = 8
N_BISECT = 8
T_S5 = 128
S5_PITCH = T_S5 + SUBLANES
TS_MERGE = 1024
TS_FFN = 512
NEG = -(2.0 ** 100)

NT_DIMS = (((1,), (1,)), ((), ()))


def _rms(x, g):
    ms = jnp.mean(x * x, axis=-1, keepdims=True)
    return x * lax.rsqrt(ms + EPS) * g


def _tree(fn, x):
    while x.shape[0] > 1:
        half = x.shape[0] // 2
        x = fn(x[:half], x[half:])
    return x[0]


def _const_spec(shape):
    nd = len(shape)
    return pl.BlockSpec(shape, lambda *_: (0,) * nd, pipeline_mode=pl.Buffered(1))


def _params(sem):
    return pltpu.CompilerParams(dimension_semantics=sem, vmem_limit_bytes=VMEM_LIMIT)


def _inproj_kernel(x_ref, gmix_ref, wqT_ref, wckw_ref, wqiT_ref, wu_ref,
                   gqc_ref, gkv_ref,
                   qT_ref, cT_ref, c_ref, qiT_ref, wT_ref, kidx_ref, u_ref):
    ts = x_ref.shape[0]
    hb = _rms(x_ref[...], gmix_ref[...]).astype(BF16)

    qT = jnp.dot(hb, wqT_ref[...], preferred_element_type=F32).T
    q3 = qT.reshape(DSA_HEADS, DSA_LATENT, ts)
    ms = jnp.mean(q3 * q3, axis=1, keepdims=True)
    qT_ref[...] = (q3 * lax.rsqrt(ms + EPS) * gqc_ref[...][None]).reshape(qT.shape).astype(BF16)

    ckw = jnp.dot(hb, wckw_ref[...], preferred_element_type=F32)
    kw = ckw[:, DSA_LATENT:]
    kidx_ref[...] = kw[:, :IDX_DIM].astype(BF16)
    wT_ref[...] = kw.T[IDX_DIM:IDX_DIM + BF16_ROWS, :]

    cn = _rms(ckw[:, :DSA_LATENT], gkv_ref[...])
    c_ref[...] = cn.astype(BF16)
    cTn = cn.T.astype(BF16)
    for k in range(ts // TK):
        cT_ref[k, :DSA_LATENT, :] = cTn[:, k * TK:(k + 1) * TK]
        cT_ref[k, DSA_LATENT:, :] = jnp.ones((BF16_ROWS, TK), BF16)

    qiT_ref[...] = jnp.dot(hb, wqiT_ref[...], preferred_element_type=F32).T.astype(BF16)

    u_ref[...] = jnp.dot(hb, wu_ref[...], preferred_element_type=F32)


def _inproj(x, gmix, wqT, wckw, wqiT, wu, gqc, gkv):
    B, S, D = x.shape
    ts = TS_IN
    grid = (B, S // ts)
    tok = lambda w: pl.BlockSpec((None, ts, w), lambda b, s: (b, s, 0))
    tokT = lambda r: pl.BlockSpec((None, r, ts), lambda b, s: (b, 0, s))
    hq = DSA_HEADS * DSA_LATENT
    hi = IDX_HEADS * IDX_DIM
    out_shape = (
        jax.ShapeDtypeStruct((B, hq, S), BF16),
        jax.ShapeDtypeStruct((B, S // TK, CT_ROWS, TK), BF16),
        jax.ShapeDtypeStruct((B, S, DSA_LATENT), BF16),
        jax.ShapeDtypeStruct((B, hi, S), BF16),
        jax.ShapeDtypeStruct((B, BF16_ROWS, S), F32),
        jax.ShapeDtypeStruct((B, S, IDX_DIM), BF16),
        jax.ShapeDtypeStruct((B, S, S5_WIDTH), F32),
    )
    out_specs = (tokT(hq),
                 pl.BlockSpec((None, ts // TK, CT_ROWS, TK), lambda b, s: (b, s, 0, 0)),
                 tok(DSA_LATENT), tokT(hi), tokT(BF16_ROWS), tok(IDX_DIM),
                 tok(S5_WIDTH))
    consts = (gmix, wqT, wckw, wqiT, wu, gqc, gkv)
    in_specs = [tok(D)] + [_const_spec(a.shape) for a in consts]
    return pl.pallas_call(
        _inproj_kernel, grid=grid, in_specs=in_specs, out_specs=out_specs, out_shape=out_shape,
        compiler_params=_params(("parallel", "parallel")), name="inproj",
    )(x, *consts)


def _dsa_kernel(qT_ref, qiT_ref, wT_ref, kidx_ref, c_ref, cT_ref, nb_ref, wuvT_ref, o_ref,
                sc_ref, scb_ref, mm_ref, pref_ref, lg_ref, m_ref, acc_ref):
    i = pl.program_id(1)
    nk = sc_ref.shape[0]
    kf = float(TOPK_MAX)
    G = TK // SUBLANES
    RB = TK // BF16_ROWS

    def rep(fn, a):
        return jnp.broadcast_to(fn(a, axis=0, keepdims=True), (SUBLANES, TQ))

    def full(v, dt=F32):
        return jnp.full((SUBLANES, TQ), v, dt)

    def key_rows(j):
        return pl.ds(pl.multiple_of(j * TK, TK), TK)

    def score(j):
        ks = kidx_ref[key_rows(j), :]
        acc = None
        for h in range(IDX_HEADS):
            d = jnp.dot(ks, qiT_ref[h * IDX_DIM:(h + 1) * IDX_DIM, :], preferred_element_type=F32)
            t = jnp.maximum(d, 0.0) * wT_ref[h:h + 1, :]
            acc = t if acc is None else acc + t
        return acc

    mm_ref[0] = full(jnp.inf)
    mm_ref[1] = full(-jnp.inf)

    def put_scores(j, s, s_for_min):
        sc_ref[j] = s
        scb_ref[j] = s.astype(BF16)
        mm_ref[0] = jnp.minimum(mm_ref[0], _tree(jnp.minimum, s_for_min.reshape(G, SUBLANES, TQ)))
        mm_ref[1] = jnp.maximum(mm_ref[1], _tree(jnp.maximum, s.reshape(G, SUBLANES, TQ)))

    def score_pair(p, carry):
        for j in (2 * p, 2 * p + 1):
            s = score(j)
            put_scores(j, s, s)
        return carry

    lax.fori_loop(0, lax.shift_right_logical(i, 1), score_pair, 0)

    @pl.when((i & 1) == 1)
    def _():
        s = score(i - 1)
        put_scores(i - 1, s, s)

    key_t = lax.broadcasted_iota(jnp.int32, (TK, TQ), 0)
    qry_t = lax.broadcasted_iota(jnp.int32, (TK, TQ), 1)
    causal = key_t <= qry_t
    s_diag = score(i)
    put_scores(i, jnp.where(causal, s_diag, -jnp.inf), jnp.where(causal, s_diag, jnp.inf))

    @pl.when(i == 0)
    def _():
        sc_ref[0] = jnp.where(causal, 0.0, NEG)

    @pl.when(i > 0)
    def _():
        nt = i + 1

        def tile3(j):
            return sc_ref[j].reshape(G, SUBLANES, TQ)

        lo, hi = rep(jnp.min, mm_ref[0]), rep(jnp.max, mm_ref[1])

        def count_ge(thr):
            def body(j, acc):
                return acc + _tree(jnp.add, jnp.where(tile3(j) >= thr[None], 1.0, 0.0))
            return rep(jnp.sum, lax.fori_loop(0, nt, body, full(0.0)))

        bf_step = 2.0 ** -7
        tiny = 1e-30

        def as_bf16_value(x):
            return x.astype(BF16).astype(F32)

        def count_ge_coarse(thr):
            thr16 = jnp.concatenate([thr, thr], axis=0).astype(BF16)
            one, zero = jnp.ones((), BF16), jnp.zeros((), BF16)

            def body(j, acc):
                t = scb_ref[j].reshape(RB, BF16_ROWS, TQ)
                return acc + _tree(jnp.add, jnp.where(t >= thr16[None], one, zero))
            acc = lax.fori_loop(0, nt, body, jnp.zeros((BF16_ROWS, TQ), BF16))
            return rep(jnp.sum, acc.astype(F32))

        def coarse(_, carry):
            lo_c, hi_c = carry
            mid = as_bf16_value(0.5 * (lo_c + hi_c))
            ge = count_ge_coarse(mid) >= kf
            return jnp.where(ge, mid, lo_c), jnp.where(ge, hi_c, mid)

        lo_c = as_bf16_value(lo - jnp.abs(lo) * bf_step - tiny)
        hi_c = as_bf16_value(hi + jnp.abs(hi) * bf_step + tiny)
        lo_c, hi_c = lax.fori_loop(0, N_COARSE, coarse, (lo_c, hi_c))
        lo = lo_c - jnp.abs(lo_c) * bf_step - tiny
        hi = hi_c

        def bisect(_, carry):
            lo, hi, clo = carry
            mid = 0.5 * (lo + hi)
            cnt = count_ge(mid)
            ge = cnt >= kf
            return jnp.where(ge, mid, lo), jnp.where(ge, hi, mid), jnp.where(ge, cnt, clo)

        lo, hi, clo = lax.fori_loop(0, N_BISECT, bisect, (lo, hi, count_ge(lo)))

        def snap_body(j, am):
            s = tile3(j)
            return jnp.minimum(am, _tree(jnp.minimum, jnp.where(s >= lo[None], s, jnp.inf)))

        cur = rep(jnp.min, lax.fori_loop(0, nt, snap_body, full(jnp.inf)))

        def walk(cur):
            def body(j, carry):
                ac, am = carry
                s = tile3(j)
                g = s > cur[None]
                ac = ac + _tree(jnp.add, jnp.where(g, 1.0, 0.0))
                am = jnp.minimum(am, _tree(jnp.minimum, jnp.where(g, s, jnp.inf)))
                return ac, am
            ac, am = lax.fori_loop(0, nt, body, (full(0.0), full(jnp.inf)))
            return rep(jnp.sum, ac), rep(jnp.min, am)

        def walk_cond(carry):
            _, _, _, go, it = carry
            return jnp.logical_and(go > 0, it < nk * TK + 2)

        def walk_body(carry):
            cur, cge, _, _, it = carry
            c, nxt = walk(cur)
            move = c >= kf
            go = (jnp.max(jnp.where(move, 1.0, 0.0)) > 0.5).astype(jnp.int32)
            return jnp.where(move, nxt, cur), jnp.where(move, c, cge), c, go, it + 1

        kth, cge, cgt, _, _ = lax.while_loop(
            walk_cond, walk_body, (cur, clo, full(0.0), jnp.int32(1), jnp.int32(0)))
        need = kf - cgt
        has_excess = jnp.max(jnp.where(cge > kf, 1.0, 0.0)) > 0.5

        @pl.when(jnp.logical_not(has_excess))
        def _():
            def body(j, carry):
                sc_ref[j] = jnp.where(tile3(j) >= kth[None], 0.0, NEG).reshape(TK, TQ)
                return carry
            lax.fori_loop(0, nt, body, 0)

        @pl.when(has_excess)
        def _():
            tril = jnp.where(key_t >= qry_t, 1.0, 0.0).astype(BF16)

            def pref_body(p, carry):
                for j in (2 * p, jnp.minimum(2 * p + 1, nt - 1)):
                    e01 = jnp.where(tile3(j) == kth[None], 1.0, 0.0).reshape(TK, TQ).astype(BF16)
                    pref_ref[j] = jnp.dot(tril, e01, preferred_element_type=F32)
                return carry

            lax.fori_loop(0, lax.shift_right_logical(nt + 1, 1), pref_body, 0)

            def mask_body(j, offset):
                s = tile3(j)
                pref = pref_ref[j]
                rank = pref.reshape(G, SUBLANES, TQ) + offset[None]
                tie = jnp.where(rank <= need[None], 0.0, NEG)
                mb = jnp.where(s > kth[None], 0.0, jnp.where(s == kth[None], tie, NEG))
                sc_ref[j] = mb.reshape(TK, TQ)
                return offset + jnp.broadcast_to(pref[TK - 1:TK, :], (SUBLANES, TQ))

            lax.fori_loop(0, nt, mask_body, full(0.0))

    m_ref[...] = jnp.full(m_ref.shape, NEG, F32)
    acc_ref[...] = jnp.zeros(acc_ref.shape, F32)
    LG = CT_ROWS // SUBLANES

    def near_bias(kind, h):
        z = jnp.zeros((NB, NB), F32)
        b0, b1 = nb_ref[0, h], nb_ref[1, h]
        blocks = [[b0, b1], [z, b0]] if kind == 0 else [[z, z], [b1, z]]
        return jnp.concatenate([jnp.concatenate(r, axis=1) for r in blocks], axis=0)

    def attend(tiles):
        m_run = [m_ref[h] for h in range(DSA_HEADS)]
        stats = []
        for slot, (j, near) in enumerate(tiles):
            ct = c_ref[key_rows(j), :]
            mbb = sc_ref[j].astype(BF16)
            per_head = []
            for h in range(DSA_HEADS):
                lg = jnp.dot(ct, qT_ref[h * DSA_LATENT:(h + 1) * DSA_LATENT, :],
                             preferred_element_type=F32)
                if near is not None:
                    lg = lg + near_bias(near, h)
                lgb = lg.astype(BF16) + mbb
                lg_ref[slot, h] = lgb
                tmax = _tree(jnp.maximum, lgb.reshape(RB, BF16_ROWS, TQ)).astype(F32)
                m_new = jnp.maximum(m_run[h], rep(jnp.max, tmax))
                per_head.append((m_run[h], m_new))
                m_run[h] = m_new
            stats.append(per_head)
        for slot, (j, near) in enumerate(tiles):
            ctT = cT_ref[j]
            for h in range(DSA_HEADS):
                m_old, m_new = stats[slot][h]
                alpha = jnp.exp2(m_old - m_new)
                m16 = jnp.concatenate([m_new, m_new], axis=0).astype(BF16)
                x = lg_ref[slot, h].reshape(RB, BF16_ROWS, TQ) - m16[None]
                pv = jnp.dot(ctT, jnp.exp2(x).reshape(TK, TQ), preferred_element_type=F32)
                acc3 = acc_ref[h].reshape(LG, SUBLANES, TQ) * alpha[None]
                acc_ref[h] = acc3.reshape(CT_ROWS, TQ) + pv
        for h in range(DSA_HEADS):
            m_ref[h] = m_run[h]

    nfar = jnp.maximum(i - 1, 0)

    def far_pair(p, carry):
        attend([(2 * p, None), (2 * p + 1, None)])
        return carry

    lax.fori_loop(0, lax.shift_right_logical(nfar, 1), far_pair, 0)

    @pl.when((nfar & 1) == 1)
    def _():
        attend([(nfar - 1, None)])

    @pl.when(i > 0)
    def _():
        attend([(i - 1, 1), (i, 0)])

    @pl.when(i == 0)
    def _():
        attend([(0, 0)])

    outs = []
    for h in range(DSA_HEADS):
        rl = 1.0 / acc_ref[h, DSA_LATENT:DSA_LATENT + SUBLANES, :]
        o3 = acc_ref[h, :DSA_LATENT, :].reshape(DSA_LATENT // SUBLANES, SUBLANES, TQ) * rl[None]
        o = o3.reshape(DSA_LATENT, TQ).astype(BF16)
        outs.append(jnp.dot(wuvT_ref[h], o, preferred_element_type=F32))
    o_ref[...] = jnp.concatenate(outs, axis=0).T.astype(BF16)


def _dsa(qT, qiT, wT, kidx, c, cT, nb, wuvT):
    B, S, _ = c.shape
    nk = S // TK
    grid = (B, S // TQ)
    tileT = lambda r: pl.BlockSpec((None, r, TQ), lambda b, i: (b, 0, i))
    seq = lambda w: pl.BlockSpec((None, S, w), lambda b, i: (b, 0, 0))
    in_specs = [tileT(DSA_HEADS * DSA_LATENT), tileT(IDX_HEADS * IDX_DIM), tileT(BF16_ROWS),
                seq(IDX_DIM), seq(DSA_LATENT),
                pl.BlockSpec((None, nk, CT_ROWS, TK), lambda b, i: (b, 0, 0, 0)),
                _const_spec(nb.shape), _const_spec(wuvT.shape)]
    scratch = [
        pltpu.VMEM((nk, TK, TQ), F32),
        pltpu.VMEM((nk, TK, TQ), BF16),
        pltpu.VMEM((2, SUBLANES, TQ), F32),
        pltpu.VMEM((nk, TK, TQ), F32),
        pltpu.VMEM((2, DSA_HEADS, TK, TQ), BF16),
        pltpu.VMEM((DSA_HEADS, SUBLANES, TQ), F32),
        pltpu.VMEM((DSA_HEADS, CT_ROWS, TQ), F32),
    ]
    return pl.pallas_call(
        _dsa_kernel, grid=grid, in_specs=in_specs,
        out_specs=pl.BlockSpec((None, TQ, DSA_WIDTH), lambda b, i: (b, i, 0)),
        out_shape=jax.ShapeDtypeStruct((B, S, DSA_WIDTH), BF16), scratch_shapes=scratch,
        compiler_params=_params(("parallel", "arbitrary")), name="dsa",
    )(qT, qiT, wT, kidx, c, cT, nb, wuvT)


def _s5_kernel(u_ref, bm_ref, cm_ref, lre_ref, lim_ref, dsk_ref, wglu_ref, o_ref,
               uslab_ref, uil_ref, st_ref, oslab_ref, hre_ref, him_ref):
    nstate = S5_GROUPS * S5_STATE
    nb = hre_ref.shape[0]
    nslab = S5_WIDTH // LANES

    @pl.when(pl.program_id(0) == 0)
    def _():
        hre_ref[...] = jnp.zeros(hre_ref.shape, F32)
        him_ref[...] = jnp.zeros(him_ref.shape, F32)

    for b in range(nb):
        for k in range(nslab):
            uslab_ref[k, b * S5_PITCH:b * S5_PITCH + T_S5, :] = u_ref[b, :, k * LANES:(k + 1) * LANES]

    def gather_step(t, carry):
        for k in range(nslab):
            uil_ref[pl.ds(pl.multiple_of(t * nb, nb), nb), k * LANES:(k + 1) * LANES] = (
                uslab_ref[k, pl.ds(t, nb, stride=S5_PITCH), :])
        return carry

    lax.fori_loop(0, T_S5, gather_step, 0, unroll=4)
    u = uil_ref[...]
    ub = u.astype(BF16)

    kin = 4
    cin = S5_WIDTH // kin
    sin = nstate // kin
    for k in range(kin):
        uk = ub[:, k * cin:(k + 1) * cin]
        for off in (0, nstate):
            cols = slice(off + k * sin, off + (k + 1) * sin)
            st_ref[:, cols] = jnp.dot(uk, bm_ref[k * cin:(k + 1) * cin, cols],
                                      preferred_element_type=F32)

    half = nstate // 2
    for part in range(2):
        re_sl = slice(part * half, (part + 1) * half)
        im_sl = slice(nstate + part * half, nstate + (part + 1) * half)
        lre = lre_ref[:, re_sl]
        lim = lim_ref[:, re_sl]

        def step(t, carry):
            hr, hi = carry
            r = pl.ds(pl.multiple_of(t * nb, nb), nb)
            nr = lre * hr - lim * hi + st_ref[r, re_sl]
            ni = lre * hi + lim * hr + st_ref[r, im_sl]
            st_ref[r, re_sl] = nr
            st_ref[r, im_sl] = ni
            return nr, ni

        hr, hi = lax.fori_loop(0, T_S5, step, (hre_ref[:, re_sl], him_ref[:, re_sl]), unroll=4)
        hre_ref[:, re_sl] = hr
        him_ref[:, re_sl] = hi

    kout = 2
    cout = S5_WIDTH // kout
    sout = nstate // kout
    ys = []
    for k in range(kout):
        acc = None
        for off in (0, nstate):
            rows = slice(off + k * sout, off + (k + 1) * sout)
            part = jnp.dot(st_ref[:, rows].astype(BF16), cm_ref[rows, k * cout:(k + 1) * cout],
                           preferred_element_type=F32)
            acc = part if acc is None else acc + part
        ys.append(acc)
    y = jnp.concatenate(ys, axis=1)
    y = jax.nn.gelu(y + dsk_ref[...] * u)
    z = jnp.dot(y.astype(BF16), wglu_ref[...], preferred_element_type=F32)
    o = y * jax.nn.sigmoid(z)

    for k in range(nslab):
        oslab_ref[k] = o[:, k * LANES:(k + 1) * LANES]
    for b in range(nb):
        for k in range(nslab):
            o_ref[b, :, k * LANES:(k + 1) * LANES] = (
                oslab_ref[k, pl.ds(b, T_S5, stride=nb), :].astype(BF16))


def _s5(u, bm, cm, lre, lim, dsk, wglu):
    B, S, W = u.shape
    tb = T_S5 * B
    nstate = S5_GROUPS * S5_STATE
    tok = pl.BlockSpec((B, T_S5, W), lambda t: (0, t, 0))
    in_specs = [tok, _const_spec(bm.shape), _const_spec(cm.shape), _const_spec(lre.shape),
                _const_spec(lim.shape), _const_spec(dsk.shape), _const_spec(wglu.shape)]
    scratch = [pltpu.VMEM((W // LANES, B * S5_PITCH, LANES), F32),
               pltpu.VMEM((tb, W), F32),
               pltpu.VMEM((tb, 2 * nstate), F32),
               pltpu.VMEM((W // LANES, tb, LANES), F32),
               pltpu.VMEM((B, nstate), F32), pltpu.VMEM((B, nstate), F32)]
    return pl.pallas_call(
        _s5_kernel, grid=(S // T_S5,), in_specs=in_specs, out_specs=tok,
        out_shape=jax.ShapeDtypeStruct((B, S, W), BF16), scratch_shapes=scratch,
        compiler_params=_params(("arbitrary",)), name="s5",
    )(u, bm, cm, lre, lim, dsk, wglu)


def _merge_kernel(x_ref, odsa_ref, os5_ref, mem_ref, gmix_ref, wg_ref, wqx_ref, gqx_ref,
                  wb1_ref, wb2_ref, wb3_ref, wout_ref, gmem_ref, wkv_ref, gk_ref, y_ref,
                  k_ref, v_ref):
    @pl.when(pl.program_id(1) == 0)
    def _():
        mb = _rms(mem_ref[...], gmem_ref[...]).astype(BF16)
        kv = jnp.dot(mb, wkv_ref[...], preferred_element_type=F32)
        for h in range(X_HEADS):
            sl = slice(h * X_HEAD_DIM, (h + 1) * X_HEAD_DIM)
            k_ref[:, sl] = _rms(kv[:, sl], gk_ref[...]).astype(BF16)
        v_ref[...] = kv[:, X_WIDTH:].astype(BF16)

    x = x_ref[...]
    hb = _rms(x, gmix_ref[...]).astype(BF16)

    qx = jnp.dot(hb, wqx_ref[...], preferred_element_type=F32)
    gqx = gqx_ref[...] * (X_HEAD_DIM ** -0.5)
    ox = []
    for h in range(X_HEADS):
        sl = slice(h * X_HEAD_DIM, (h + 1) * X_HEAD_DIM)
        qh = _rms(qx[:, sl], gqx).astype(BF16)
        lg = lax.dot_general(qh, k_ref[:, sl], NT_DIMS, preferred_element_type=F32)
        p = jnp.exp(lg - jnp.max(lg, axis=-1, keepdims=True))
        pv = jnp.dot(p.astype(BF16), v_ref[:, sl], preferred_element_type=F32)
        ox.append((pv / jnp.sum(p, axis=-1, keepdims=True)).astype(BF16))
    ox = jnp.concatenate(ox, axis=1)

    merged = None
    for br, (o, wb) in enumerate(((odsa_ref[...], wb1_ref), (os5_ref[...], wb2_ref), (ox, wb3_ref))):
        gate = jax.nn.sigmoid(jnp.dot(hb, wg_ref[:, br * D_MODEL:(br + 1) * D_MODEL],
                                      preferred_element_type=F32))
        term = gate * jnp.dot(o, wb[...], preferred_element_type=F32)
        merged = term if merged is None else merged + term
    y_ref[...] = x + jnp.dot(merged.astype(BF16), wout_ref[...], preferred_element_type=F32)


def _merge(x, odsa, os5, mem, gmix, wg, wqx, gqx, wb1, wb2, wb3, wout, gmem, wkv, gk):
    B, S, D = x.shape
    ts = TS_MERGE
    tok = lambda w: pl.BlockSpec((None, ts, w), lambda b, s: (b, s, 0))
    consts = (gmix, wg, wqx, gqx, wb1, wb2, wb3, wout, gmem, wkv, gk)
    in_specs = [tok(D), tok(DSA_WIDTH), tok(S5_WIDTH),
                pl.BlockSpec((None, N_MEM, D), lambda b, s: (b, 0, 0))
                ] + [_const_spec(a.shape) for a in consts]
    scratch = [pltpu.VMEM((N_MEM, X_WIDTH), BF16), pltpu.VMEM((N_MEM, X_WIDTH), BF16)]
    return pl.pallas_call(
        _merge_kernel, grid=(B, S // ts), in_specs=in_specs, out_specs=tok(D),
        out_shape=jax.ShapeDtypeStruct((B, S, D), F32), scratch_shapes=scratch,
        compiler_params=_params(("parallel", "arbitrary")), name="merge",
    )(x, odsa, os5, mem, *consts)


def _ffn_kernel(x_ref, g_ref, wg_ref, wu_ref, wd_ref, y_ref):
    x = x_ref[...]
    hb = _rms(x, g_ref[...]).astype(BF16)
    a = jnp.dot(hb, wg_ref[...], preferred_element_type=F32)
    b = jnp.dot(hb, wu_ref[...], preferred_element_type=F32)
    act = (jax.nn.silu(a) * b).astype(BF16)
    y_ref[...] = x + jnp.dot(act, wd_ref[...], preferred_element_type=F32)


def _ffn(x2, g, wg, wu, wd):
    n, D = x2.shape
    tok = pl.BlockSpec((TS_FFN, D), lambda t: (t, 0))
    return pl.pallas_call(
        _ffn_kernel, grid=(n // TS_FFN,),
        in_specs=[tok] + [_const_spec(a.shape) for a in (g, wg, wu, wd)],
        out_specs=tok, out_shape=jax.ShapeDtypeStruct((n, D), F32),
        compiler_params=_params(("parallel",)), name="ffn",
    )(x2, g, wg, wu, wd)


def _t5_bucket(n):
    max_exact = REL_BUCKETS // 2
    nf = jnp.maximum(n, 1).astype(F32)
    large = max_exact + (jnp.log(nf / max_exact) / math.log(REL_MAX_DIST / max_exact)
                         * (REL_BUCKETS - max_exact)).astype(jnp.int32)
    large = jnp.minimum(large, REL_BUCKETS - 1)
    return jnp.where(n < max_exact, n, large)


def _toeplitz(w, rows, cols):
    H, L = w.shape
    flat = jnp.tile(w, (1, rows))[:, :rows * (L - 1)]
    return flat.reshape(H, rows, L - 1)[:, :, :cols]


def _near_bias(rel_bias):
    n = jnp.arange(2 * NB, dtype=jnp.int32)
    f = (rel_bias[_t5_bucket(n)] - rel_bias[REL_BUCKETS - 1][None, :]).T * LOG2E
    w0 = jnp.concatenate([f[:, :NB], jnp.broadcast_to(f[:, :1], (f.shape[0], NB - 1))], axis=1)
    w1 = jnp.concatenate([f[:, NB:2 * NB], f[:, 1:NB]], axis=1)
    return jnp.stack([_toeplitz(w0, NB, NB), _toeplitz(w1, NB, NB)], axis=0).astype(F32)


def _s5_mats(a_re, a_im, log_dt, b_re, b_im, c_re, c_im):
    lam = lax.complex(a_re.astype(F32), a_im.astype(F32))
    dt = jnp.exp(log_dt.astype(F32))[:, None]
    lam_bar = jnp.exp(lam * dt)
    b_bar = ((lam_bar - 1.0) / lam)[..., None] * lax.complex(b_re.astype(F32), b_im.astype(F32))
    nstate = S5_GROUPS * S5_STATE
    in_mask = (jnp.arange(S5_WIDTH)[:, None] // S5_GROUP) == (jnp.arange(nstate)[None, :] // S5_STATE)

    def blockdiag_in(w):
        t = jnp.transpose(w, (0, 2, 1)).reshape(S5_WIDTH, S5_STATE)
        return jnp.where(in_mask, jnp.tile(t, (1, S5_GROUPS)), 0.0)

    def blockdiag_out(w):
        t = jnp.transpose(w, (0, 2, 1)).reshape(nstate, S5_GROUP)
        return jnp.where(in_mask.T, jnp.tile(t, (1, S5_GROUPS)), 0.0)

    bm = jnp.concatenate([blockdiag_in(jnp.real(b_bar)), blockdiag_in(jnp.imag(b_bar))], axis=1)
    cm = jnp.concatenate([blockdiag_out(c_re.astype(F32)), blockdiag_out(-c_im.astype(F32))], axis=0)
    return bm.astype(BF16), cm.astype(BF16), jnp.real(lam_bar).reshape(1, -1), jnp.imag(lam_bar).reshape(1, -1)


def kernel(x, mem, rel_bias, w_in, g_mix_norm, g_q_dsa, g_kv_dsa, w_uv_dsa, a_re, a_im, log_dt, b_re, b_im, c_re, c_im, d_skip, w_glu, g_mem_norm, w_mem_kv, g_q_cross, g_k_cross, w_br_dsa, w_br_s5, w_br_cross, w_out, g_ffn_norm, w_ffn_gate, w_ffn_up, w_ffn_down):
    B, S, D = x.shape
    depth = w_in.shape[0]
    offs = [0] + [int(o) for o in np.cumsum(IN_SPLITS)]
    nb = _near_bias(rel_bias)
    row = lambda v: v.reshape(1, -1).astype(F32)
    col = lambda v: v.reshape(-1, 1).astype(F32)
    bf = lambda a: a.astype(BF16)
    for l in range(depth):
        wq, wc, wqi, wk, ww, wu, wqx, wg = [bf(w_in[l, :, offs[k]:offs[k + 1]])
                                            for k in range(len(IN_SPLITS))]
        wckw = jnp.pad(jnp.concatenate([wc, wk, ww], axis=1),
                       ((0, 0), (0, 2 * LANES - DSA_LATENT - IDX_DIM - IDX_HEADS)))
        gqc = col(g_q_dsa[l]) * (DSA_LATENT ** -0.5 * LOG2E)

        qT, cT, c, qiT, wT, kidx, u = _inproj(
            x, row(g_mix_norm[l]), wq, wckw, wqi, wu, gqc, row(g_kv_dsa[l]))

        wuvT = bf(jnp.transpose(w_uv_dsa[l], (0, 2, 1)))
        o_dsa = _dsa(qT, qiT, wT, kidx, c, cT, nb, wuvT)

        bm, cm, lre, lim = _s5_mats(a_re[l], a_im[l], log_dt[l], b_re[l], b_im[l], c_re[l], c_im[l])
        lre = jnp.broadcast_to(lre, (B, lre.shape[1]))
        lim = jnp.broadcast_to(lim, (B, lim.shape[1]))
        o_s5 = _s5(u, bm, cm, lre, lim, row(d_skip[l]), bf(w_glu[l]))

        x1 = _merge(x, o_dsa, o_s5, mem, row(g_mix_norm[l]), wg, wqx, row(g_q_cross[l]),
                    bf(w_br_dsa[l]), bf(w_br_s5[l]), bf(w_br_cross[l]), bf(w_out[l]),
                    row(g_mem_norm[l]), bf(w_mem_kv[l]), row(g_k_cross[l]))

        x = _ffn(x1.reshape(B * S, D), row(g_ffn_norm[l]), bf(w_ffn_gate[l]), bf(w_ffn_up[l]),
                 bf(w_ffn_down[l])).reshape(B, S, D)
    return x
```

```python
import math

import jax
import jax.numpy as jnp
import numpy as np
from jax import lax
from jax.experimental import pallas as pl
from jax.experimental.pallas import tpu as pltpu

F32 = jnp.float32
BF16 = jnp.bfloat16

D_MODEL = 1024
N_MEM = 256
EPS = 1e-6
DSA_HEADS = 8
DSA_LATENT = 128
DSA_VDIM = 64
IDX_HEADS = 8
IDX_DIM = 64
TOPK_MAX = 256
REL_BUCKETS = 32
REL_MAX_DIST = 128
S5_WIDTH = 512
S5_GROUP = 16
S5_GROUPS = S5_WIDTH // S5_GROUP
S5_STATE = 64
X_HEADS = 4
X_HEAD_DIM = 128
DSA_WIDTH = DSA_HEADS * DSA_VDIM
X_WIDTH = X_HEADS * X_HEAD_DIM
N_BRANCH = 3
IN_SPLITS = (DSA_HEADS * DSA_LATENT, DSA_LATENT, IDX_HEADS * IDX_DIM, IDX_DIM,
             IDX_HEADS, S5_WIDTH, X_WIDTH, N_BRANCH * D_MODEL)

LANES = 128
SUBLANES = 8
BF16_ROWS = 16
CT_ROWS = DSA_LATENT + BF16_ROWS
VMEM_LIMIT = 56 * 1024 * 1024
LOG2E = math.log2(math.e)

TS_IN = 1024
TQ = 256
TK = 256
NB = REL_MAX_DIST
N_COARSE = 8
N_BISECT = 8
T_S5 = 128
S5_IN_BLOCKS = 4
S5_OUT_BLOCKS = 2
S5_PITCH = T_S5 + SUBLANES
TS_MERGE = 1024
TS_FFN = 512
NEG = -(2.0 ** 100)

NT_DIMS = (((1,), (1,)), ((), ()))


def _rms(x, g):
    ms = jnp.mean(x * x, axis=-1, keepdims=True)
    return x * lax.rsqrt(ms + EPS) * g


def _tree(fn, x):
    while x.shape[0] > 1:
        half = x.shape[0] // 2
        x = fn(x[:half], x[half:])
    return x[0]


def _const_spec(shape):
    nd = len(shape)
    return pl.BlockSpec(shape, lambda *_: (0,) * nd, pipeline_mode=pl.Buffered(1))


def _params(sem):
    return pltpu.CompilerParams(dimension_semantics=sem, vmem_limit_bytes=VMEM_LIMIT)


def _inproj_kernel(x_ref, gmix_ref, wqT_ref, wckw_ref, wqiT_ref, wu_ref,
                   gqc_ref, gkv_ref,
                   qT_ref, cT_ref, c_ref, qiT_ref, wT_ref, kidx_ref, u_ref):
    ts = x_ref.shape[0]
    hb = _rms(x_ref[...], gmix_ref[...]).astype(BF16)

    qT = jnp.dot(hb, wqT_ref[...], preferred_element_type=F32).T
    q3 = qT.reshape(DSA_HEADS, DSA_LATENT, ts)
    ms = jnp.mean(q3 * q3, axis=1, keepdims=True)
    qT_ref[...] = (q3 * lax.rsqrt(ms + EPS) * gqc_ref[...][None]).reshape(qT.shape).astype(BF16)

    ckw = jnp.dot(hb, wckw_ref[...], preferred_element_type=F32)
    kw = ckw[:, DSA_LATENT:]
    kidx_ref[...] = kw[:, :IDX_DIM].astype(BF16)
    wT_ref[...] = kw.T[IDX_DIM:IDX_DIM + BF16_ROWS, :]

    cn = _rms(ckw[:, :DSA_LATENT], gkv_ref[...])
    c_ref[...] = cn.astype(BF16)
    cTn = cn.T.astype(BF16)
    for k in range(ts // TK):
        cT_ref[k, :DSA_LATENT, :] = cTn[:, k * TK:(k + 1) * TK]
        cT_ref[k, DSA_LATENT:, :] = jnp.ones((BF16_ROWS, TK), BF16)

    qiT_ref[...] = jnp.dot(hb, wqiT_ref[...], preferred_element_type=F32).T.astype(BF16)

    u_ref[...] = jnp.dot(hb, wu_ref[...], preferred_element_type=F32)


def _inproj(x, gmix, wqT, wckw, wqiT, wu, gqc, gkv):
    B, S, D = x.shape
    ts = TS_IN
    grid = (B, S // ts)
    tok = lambda w: pl.BlockSpec((None, ts, w), lambda b, s: (b, s, 0))
    tokT = lambda r: pl.BlockSpec((None, r, ts), lambda b, s: (b, 0, s))
    hq = DSA_HEADS * DSA_LATENT
    hi = IDX_HEADS * IDX_DIM
    out_shape = (
        jax.ShapeDtypeStruct((B, hq, S), BF16),
        jax.ShapeDtypeStruct((B, S // TK, CT_ROWS, TK), BF16),
        jax.ShapeDtypeStruct((B, S, DSA_LATENT), BF16),
        jax.ShapeDtypeStruct((B, hi, S), BF16),
        jax.ShapeDtypeStruct((B, BF16_ROWS, S), F32),
        jax.ShapeDtypeStruct((B, S, IDX_DIM), BF16),
        jax.ShapeDtypeStruct((B, S, S5_WIDTH), F32),
    )
    out_specs = (tokT(hq),
                 pl.BlockSpec((None, ts // TK, CT_ROWS, TK), lambda b, s: (b, s, 0, 0)),
                 tok(DSA_LATENT), tokT(hi), tokT(BF16_ROWS), tok(IDX_DIM),
                 tok(S5_WIDTH))
    consts = (gmix, wqT, wckw, wqiT, wu, gqc, gkv)
    in_specs = [tok(D)] + [_const_spec(a.shape) for a in consts]
    return pl.pallas_call(
        _inproj_kernel, grid=grid, in_specs=in_specs, out_specs=out_specs, out_shape=out_shape,
        compiler_params=_params(("parallel", "parallel")), name="inproj",
    )(x, *consts)


def _dsa_kernel(qT_ref, qiT_ref, wT_ref, kidx_ref, c_ref, cT_ref, nb_ref, wuvT_ref, o_ref,
                sc_ref, scb_ref, mm_ref, pref_ref, lg_ref, m_ref, acc_ref):
    i = pl.program_id(1)
    nk = sc_ref.shape[0]
    kf = float(TOPK_MAX)
    G = TK // SUBLANES
    RB = TK // BF16_ROWS

    def rep(fn, a):
        return jnp.broadcast_to(fn(a, axis=0, keepdims=True), (SUBLANES, TQ))

    def full(v, dt=F32):
        return jnp.full((SUBLANES, TQ), v, dt)

    def key_rows(j):
        return pl.ds(pl.multiple_of(j * TK, TK), TK)

    def score(j):
        ks = kidx_ref[key_rows(j), :]
        acc = None
        for h in range(IDX_HEADS):
            d = jnp.dot(ks, qiT_ref[h * IDX_DIM:(h + 1) * IDX_DIM, :], preferred_element_type=F32)
            t = jnp.maximum(d, 0.0) * wT_ref[h:h + 1, :]
            acc = t if acc is None else acc + t
        return acc

    mm_ref[0] = full(jnp.inf)
    mm_ref[1] = full(-jnp.inf)

    def put_scores(j, s, s_for_min):
        sc_ref[j] = s
        scb_ref[j] = s.astype(BF16)
        mm_ref[0] = jnp.minimum(mm_ref[0], _tree(jnp.minimum, s_for_min.reshape(G, SUBLANES, TQ)))
        mm_ref[1] = jnp.maximum(mm_ref[1], _tree(jnp.maximum, s.reshape(G, SUBLANES, TQ)))

    def score_pair(p, carry):
        for j in (2 * p, 2 * p + 1):
            s = score(j)
            put_scores(j, s, s)
        return carry

    lax.fori_loop(0, lax.shift_right_logical(i, 1), score_pair, 0)

    @pl.when((i & 1) == 1)
    def _():
        s = score(i - 1)
        put_scores(i - 1, s, s)

    key_t = lax.broadcasted_iota(jnp.int32, (TK, TQ), 0)
    qry_t = lax.broadcasted_iota(jnp.int32, (TK, TQ), 1)
    causal = key_t <= qry_t
    s_diag = score(i)
    put_scores(i, jnp.where(causal, s_diag, -jnp.inf), jnp.where(causal, s_diag, jnp.inf))

    @pl.when(i == 0)
    def _():
        sc_ref[0] = jnp.where(causal, 0.0, NEG)

    @pl.when(i > 0)
    def _():
        nt = i + 1

        def tile3(j):
            return sc_ref[j].reshape(G, SUBLANES, TQ)

        lo, hi = rep(jnp.min, mm_ref[0]), rep(jnp.max, mm_ref[1])

        def count_ge(thr):
            def body(j, acc):
                return acc + _tree(jnp.add, jnp.where(tile3(j) >= thr[None], 1.0, 0.0))
            return rep(jnp.sum, lax.fori_loop(0, nt, body, full(0.0)))

        bf_step = 2.0 ** -7
        tiny = 1e-30

        def as_bf16_value(x):
            return x.astype(BF16).astype(F32)

        def count_ge_coarse(thr):
            thr16 = jnp.concatenate([thr, thr], axis=0).astype(BF16)
            one, zero = jnp.ones((), BF16), jnp.zeros((), BF16)

            def body(j, acc):
                t = scb_ref[j].reshape(RB, BF16_ROWS, TQ)
                return acc + _tree(jnp.add, jnp.where(t >= thr16[None], one, zero))
            acc = lax.fori_loop(0, nt, body, jnp.zeros((BF16_ROWS, TQ), BF16))
            return rep(jnp.sum, acc.astype(F32))

        def coarse(_, carry):
            lo_c, hi_c = carry
            mid = as_bf16_value(0.5 * (lo_c + hi_c))
            ge = count_ge_coarse(mid) >= kf
            return jnp.where(ge, mid, lo_c), jnp.where(ge, hi_c, mid)

        lo_c = as_bf16_value(lo - jnp.abs(lo) * bf_step - tiny)
        hi_c = as_bf16_value(hi + jnp.abs(hi) * bf_step + tiny)
        lo_c, hi_c = lax.fori_loop(0, N_COARSE, coarse, (lo_c, hi_c))
        lo = lo_c - jnp.abs(lo_c) * bf_step - tiny
        hi = hi_c

        def bisect(_, carry):
            lo, hi, clo = carry
            mid = 0.5 * (lo + hi)
            cnt = count_ge(mid)
            ge = cnt >= kf
            return jnp.where(ge, mid, lo), jnp.where(ge, hi, mid), jnp.where(ge, cnt, clo)

        lo, hi, clo = lax.fori_loop(0, N_BISECT, bisect, (lo, hi, count_ge(lo)))

        def snap_body(j, am):
            s = tile3(j)
            return jnp.minimum(am, _tree(jnp.minimum, jnp.where(s >= lo[None], s, jnp.inf)))

        cur = rep(jnp.min, lax.fori_loop(0, nt, snap_body, full(jnp.inf)))

        def walk(cur):
            def body(j, carry):
                ac, am = carry
                s = tile3(j)
                g = s > cur[None]
                ac = ac + _tree(jnp.add, jnp.where(g, 1.0, 0.0))
                am = jnp.minimum(am, _tree(jnp.minimum, jnp.where(g, s, jnp.inf)))
                return ac, am
            ac, am = lax.fori_loop(0, nt, body, (full(0.0), full(jnp.inf)))
            return rep(jnp.sum, ac), rep(jnp.min, am)

        def walk_cond(carry):
            _, _, _, go, it = carry
            return jnp.logical_and(go > 0, it < nk * TK + 2)

        def walk_body(carry):
            cur, cge, _, _, it = carry
            c, nxt = walk(cur)
            move = c >= kf
            go = (jnp.max(jnp.where(move, 1.0, 0.0)) > 0.5).astype(jnp.int32)
            return jnp.where(move, nxt, cur), jnp.where(move, c, cge), c, go, it + 1

        kth, cge, cgt, _, _ = lax.while_loop(
            walk_cond, walk_body, (cur, clo, full(0.0), jnp.int32(1), jnp.int32(0)))
        need = kf - cgt
        has_excess = jnp.max(jnp.where(cge > kf, 1.0, 0.0)) > 0.5

        @pl.when(jnp.logical_not(has_excess))
        def _():
            def body(j, carry):
                sc_ref[j] = jnp.where(tile3(j) >= kth[None], 0.0, NEG).reshape(TK, TQ)
                return carry
            lax.fori_loop(0, nt, body, 0)

        @pl.when(has_excess)
        def _():
            tril = jnp.where(key_t >= qry_t, 1.0, 0.0).astype(BF16)

            def pref_body(p, carry):
                for j in (2 * p, jnp.minimum(2 * p + 1, nt - 1)):
                    e01 = jnp.where(tile3(j) == kth[None], 1.0, 0.0).reshape(TK, TQ).astype(BF16)
                    pref_ref[j] = jnp.dot(tril, e01, preferred_element_type=F32)
                return carry

            lax.fori_loop(0, lax.shift_right_logical(nt + 1, 1), pref_body, 0)

            def mask_body(j, offset):
                s = tile3(j)
                pref = pref_ref[j]
                rank = pref.reshape(G, SUBLANES, TQ) + offset[None]
                tie = jnp.where(rank <= need[None], 0.0, NEG)
                mb = jnp.where(s > kth[None], 0.0, jnp.where(s == kth[None], tie, NEG))
                sc_ref[j] = mb.reshape(TK, TQ)
                return offset + jnp.broadcast_to(pref[TK - 1:TK, :], (SUBLANES, TQ))

            lax.fori_loop(0, nt, mask_body, full(0.0))

    m_ref[...] = jnp.full(m_ref.shape, NEG, F32)
    acc_ref[...] = jnp.zeros(acc_ref.shape, F32)
    LG = CT_ROWS // SUBLANES

    def near_bias(kind, h):
        z = jnp.zeros((NB, NB), F32)
        b0, b1 = nb_ref[0, h], nb_ref[1, h]
        blocks = [[b0, b1], [z, b0]] if kind == 0 else [[z, z], [b1, z]]
        return jnp.concatenate([jnp.concatenate(r, axis=1) for r in blocks], axis=0)

    def attend(tiles):
        m_run = [m_ref[h] for h in range(DSA_HEADS)]
        stats = []
        for slot, (j, near) in enumerate(tiles):
            ct = c_ref[key_rows(j), :]
            mbb = sc_ref[j].astype(BF16)
            per_head = []
            for h in range(DSA_HEADS):
                lg = jnp.dot(ct, qT_ref[h * DSA_LATENT:(h + 1) * DSA_LATENT, :],
                             preferred_element_type=F32)
                if near is not None:
                    lg = lg + near_bias(near, h)
                lgb = lg.astype(BF16) + mbb
                lg_ref[slot, h] = lgb
                tmax = _tree(jnp.maximum, lgb.reshape(RB, BF16_ROWS, TQ)).astype(F32)
                m_new = jnp.maximum(m_run[h], rep(jnp.max, tmax))
                per_head.append((m_run[h], m_new))
                m_run[h] = m_new
            stats.append(per_head)
        for slot, (j, near) in enumerate(tiles):
            ctT = cT_ref[j]
            for h in range(DSA_HEADS):
                m_old, m_new = stats[slot][h]
                alpha = jnp.exp2(m_old - m_new)
                m16 = jnp.concatenate([m_new, m_new], axis=0).astype(BF16)
                x = lg_ref[slot, h].reshape(RB, BF16_ROWS, TQ) - m16[None]
                pv = jnp.dot(ctT, jnp.exp2(x).reshape(TK, TQ), preferred_element_type=F32)
                acc3 = acc_ref[h].reshape(LG, SUBLANES, TQ) * alpha[None]
                acc_ref[h] = acc3.reshape(CT_ROWS, TQ) + pv
        for h in range(DSA_HEADS):
            m_ref[h] = m_run[h]

    nfar = jnp.maximum(i - 1, 0)

    def far_pair(p, carry):
        attend([(2 * p, None), (2 * p + 1, None)])
        return carry

    lax.fori_loop(0, lax.shift_right_logical(nfar, 1), far_pair, 0)

    @pl.when((nfar & 1) == 1)
    def _():
        attend([(nfar - 1, None)])

    @pl.when(i > 0)
    def _():
        attend([(i - 1, 1), (i, 0)])

    @pl.when(i == 0)
    def _():
        attend([(0, 0)])

    outs = []
    for h in range(DSA_HEADS):
        rl = 1.0 / acc_ref[h, DSA_LATENT:DSA_LATENT + SUBLANES, :]
        o3 = acc_ref[h, :DSA_LATENT, :].reshape(DSA_LATENT // SUBLANES, SUBLANES, TQ) * rl[None]
        o = o3.reshape(DSA_LATENT, TQ).astype(BF16)
        outs.append(jnp.dot(wuvT_ref[h], o, preferred_element_type=F32))
    o_ref[...] = jnp.concatenate(outs, axis=0).T.astype(BF16)


def _dsa(qT, qiT, wT, kidx, c, cT, nb, wuvT):
    B, S, _ = c.shape
    nk = S // TK
    grid = (B, S // TQ)
    tileT = lambda r: pl.BlockSpec((None, r, TQ), lambda b, i: (b, 0, i))
    seq = lambda w: pl.BlockSpec((None, S, w), lambda b, i: (b, 0, 0))
    in_specs = [tileT(DSA_HEADS * DSA_LATENT), tileT(IDX_HEADS * IDX_DIM), tileT(BF16_ROWS),
                seq(IDX_DIM), seq(DSA_LATENT),
                pl.BlockSpec((None, nk, CT_ROWS, TK), lambda b, i: (b, 0, 0, 0)),
                _const_spec(nb.shape), _const_spec(wuvT.shape)]
    scratch = [
        pltpu.VMEM((nk, TK, TQ), F32),
        pltpu.VMEM((nk, TK, TQ), BF16),
        pltpu.VMEM((2, SUBLANES, TQ), F32),
        pltpu.VMEM((nk, TK, TQ), F32),
        pltpu.VMEM((2, DSA_HEADS, TK, TQ), BF16),
        pltpu.VMEM((DSA_HEADS, SUBLANES, TQ), F32),
        pltpu.VMEM((DSA_HEADS, CT_ROWS, TQ), F32),
    ]
    return pl.pallas_call(
        _dsa_kernel, grid=grid, in_specs=in_specs,
        out_specs=pl.BlockSpec((None, TQ, DSA_WIDTH), lambda b, i: (b, i, 0)),
        out_shape=jax.ShapeDtypeStruct((B, S, DSA_WIDTH), BF16), scratch_shapes=scratch,
        compiler_params=_params(("parallel", "arbitrary")), name="dsa",
    )(qT, qiT, wT, kidx, c, cT, nb, wuvT)


def _s5_kernel(u_ref, bm_ref, cm_ref, lre_ref, lim_ref, dsk_ref, wglu_ref, o_ref,
               uslab_ref, uil_ref, st0_ref, st1_ref, oslab_ref, hre_ref, him_ref):
    st_refs = (st0_ref, st1_ref)
    nstate = S5_GROUPS * S5_STATE
    nb = hre_ref.shape[0]
    nslab = S5_WIDTH // LANES

    @pl.when(pl.program_id(0) == 0)
    def _():
        hre_ref[...] = jnp.zeros(hre_ref.shape, F32)
        him_ref[...] = jnp.zeros(him_ref.shape, F32)

    for b in range(nb):
        for k in range(nslab):
            uslab_ref[k, b * S5_PITCH:b * S5_PITCH + T_S5, :] = u_ref[b, :, k * LANES:(k + 1) * LANES]

    def gather_step(t, carry):
        for k in range(nslab):
            uil_ref[pl.ds(pl.multiple_of(t * nb, nb), nb), k * LANES:(k + 1) * LANES] = (
                uslab_ref[k, pl.ds(t, nb, stride=S5_PITCH), :])
        return carry

    lax.fori_loop(0, T_S5, gather_step, 0, unroll=4)

    tsub = T_S5 // len(st_refs)
    rsub = tsub * nb

    def in_matmul(c):
        ub = uil_ref[c * rsub:(c + 1) * rsub, :].astype(BF16)
        cin = S5_WIDTH // S5_IN_BLOCKS
        sin = nstate // S5_IN_BLOCKS
        for k in range(S5_IN_BLOCKS):
            uk = ub[:, k * cin:(k + 1) * cin]
            for off in (0, nstate):
                cols = slice(off + k * sin, off + (k + 1) * sin)
                st_refs[c][:, cols] = jnp.dot(uk, bm_ref[k * cin:(k + 1) * cin, cols],
                                              preferred_element_type=F32)

    def scan(c):
        st_ref = st_refs[c]
        half = nstate // 2
        for part in range(2):
            re_sl = slice(part * half, (part + 1) * half)
            im_sl = slice(nstate + part * half, nstate + (part + 1) * half)
            lre = lre_ref[:, re_sl]
            lim = lim_ref[:, re_sl]
            hr, hi = hre_ref[:, re_sl], him_ref[:, re_sl]
            for t in range(tsub):
                r = slice(t * nb, (t + 1) * nb)
                hr, hi = (lre * hr - lim * hi + st_ref[r, re_sl],
                          lre * hi + lim * hr + st_ref[r, im_sl])
                st_ref[r, re_sl] = hr
                st_ref[r, im_sl] = hi
            hre_ref[:, re_sl] = hr
            him_ref[:, re_sl] = hi

    def out_matmul(c):
        st_ref = st_refs[c]
        cout = S5_WIDTH // S5_OUT_BLOCKS
        sout = nstate // S5_OUT_BLOCKS
        ys = []
        for k in range(S5_OUT_BLOCKS):
            acc = None
            for off in (0, nstate):
                rows = slice(off + k * sout, off + (k + 1) * sout)
                part = jnp.dot(st_ref[:, rows].astype(BF16), cm_ref[rows, k * cout:(k + 1) * cout],
                               preferred_element_type=F32)
                acc = part if acc is None else acc + part
            ys.append(acc)
        y = jnp.concatenate(ys, axis=1)
        y = jax.nn.gelu(y + dsk_ref[...] * uil_ref[c * rsub:(c + 1) * rsub, :])
        z = jnp.dot(y.astype(BF16), wglu_ref[...], preferred_element_type=F32)
        o = y * jax.nn.sigmoid(z)
        for k in range(nslab):
            oslab_ref[k, c * rsub:(c + 1) * rsub, :] = o[:, k * LANES:(k + 1) * LANES]

    for c in range(len(st_refs)):
        in_matmul(c)
    for c in range(len(st_refs)):
        scan(c)
    for c in range(len(st_refs)):
        out_matmul(c)

    for b in range(nb):
        for k in range(nslab):
            o_ref[b, :, k * LANES:(k + 1) * LANES] = (
                oslab_ref[k, pl.ds(b, T_S5, stride=nb), :].astype(BF16))


def _s5(u, bm, cm, lre, lim, dsk, wglu):
    B, S, W = u.shape
    tb = T_S5 * B
    nstate = S5_GROUPS * S5_STATE
    tok = pl.BlockSpec((B, T_S5, W), lambda t: (0, t, 0))
    in_specs = [tok, _const_spec(bm.shape), _const_spec(cm.shape), _const_spec(lre.shape),
                _const_spec(lim.shape), _const_spec(dsk.shape), _const_spec(wglu.shape)]
    scratch = [pltpu.VMEM((W // LANES, B * S5_PITCH, LANES), F32),
               pltpu.VMEM((tb, W), F32),
               pltpu.VMEM((tb // 2, 2 * nstate), F32),
               pltpu.VMEM((tb // 2, 2 * nstate), F32),
               pltpu.VMEM((W // LANES, tb, LANES), F32),
               pltpu.VMEM((B, nstate), F32), pltpu.VMEM((B, nstate), F32)]
    return pl.pallas_call(
        _s5_kernel, grid=(S // T_S5,), in_specs=in_specs, out_specs=tok,
        out_shape=jax.ShapeDtypeStruct((B, S, W), BF16), scratch_shapes=scratch,
        compiler_params=_params(("arbitrary",)), name="s5",
    )(u, bm, cm, lre, lim, dsk, wglu)


def _merge_kernel(x_ref, odsa_ref, os5_ref, mem_ref, gmix_ref, wg_ref, wqx_ref, gqx_ref,
                  wb1_ref, wb2_ref, wb3_ref, wout_ref, gmem_ref, wkv_ref, gk_ref, y_ref,
                  k_ref, v_ref):
    @pl.when(pl.program_id(1) == 0)
    def _():
        mb = _rms(mem_ref[...], gmem_ref[...]).astype(BF16)
        kv = jnp.dot(mb, wkv_ref[...], preferred_element_type=F32)
        for h in range(X_HEADS):
            sl = slice(h * X_HEAD_DIM, (h + 1) * X_HEAD_DIM)
            k_ref[:, sl] = _rms(kv[:, sl], gk_ref[...]).astype(BF16)
        v_ref[...] = kv[:, X_WIDTH:].astype(BF16)

    x = x_ref[...]
    hb = _rms(x, gmix_ref[...]).astype(BF16)

    qx = jnp.dot(hb, wqx_ref[...], preferred_element_type=F32)
    gqx = gqx_ref[...] * (X_HEAD_DIM ** -0.5)
    ox = []
    for h in range(X_HEADS):
        sl = slice(h * X_HEAD_DIM, (h + 1) * X_HEAD_DIM)
        qh = _rms(qx[:, sl], gqx).astype(BF16)
        lg = lax.dot_general(qh, k_ref[:, sl], NT_DIMS, preferred_element_type=F32)
        p = jnp.exp(lg - jnp.max(lg, axis=-1, keepdims=True))
        pv = jnp.dot(p.astype(BF16), v_ref[:, sl], preferred_element_type=F32)
        ox.append((pv / jnp.sum(p, axis=-1, keepdims=True)).astype(BF16))
    ox = jnp.concatenate(ox, axis=1)

    merged = None
    for br, (o, wb) in enumerate(((odsa_ref[...], wb1_ref), (os5_ref[...], wb2_ref), (ox, wb3_ref))):
        gate = jax.nn.sigmoid(jnp.dot(hb, wg_ref[:, br * D_MODEL:(br + 1) * D_MODEL],
                                      preferred_element_type=F32))
        term = gate * jnp.dot(o, wb[...], preferred_element_type=F32)
        merged = term if merged is None else merged + term
    y_ref[...] = x + jnp.dot(merged.astype(BF16), wout_ref[...], preferred_element_type=F32)


def _merge(x, odsa, os5, mem, gmix, wg, wqx, gqx, wb1, wb2, wb3, wout, gmem, wkv, gk):
    B, S, D = x.shape
    ts = TS_MERGE
    tok = lambda w: pl.BlockSpec((None, ts, w), lambda b, s: (b, s, 0))
    consts = (gmix, wg, wqx, gqx, wb1, wb2, wb3, wout, gmem, wkv, gk)
    in_specs = [tok(D), tok(DSA_WIDTH), tok(S5_WIDTH),
                pl.BlockSpec((None, N_MEM, D), lambda b, s: (b, 0, 0))
                ] + [_const_spec(a.shape) for a in consts]
    scratch = [pltpu.VMEM((N_MEM, X_WIDTH), BF16), pltpu.VMEM((N_MEM, X_WIDTH), BF16)]
    return pl.pallas_call(
        _merge_kernel, grid=(B, S // ts), in_specs=in_specs, out_specs=tok(D),
        out_shape=jax.ShapeDtypeStruct((B, S, D), F32), scratch_shapes=scratch,
        compiler_params=_params(("parallel", "arbitrary")), name="merge",
    )(x, odsa, os5, mem, *consts)


def _ffn_kernel(x_ref, g_ref, wg_ref, wu_ref, wd_ref, y_ref):
    x = x_ref[...]
    hb = _rms(x, g_ref[...]).astype(BF16)
    a = jnp.dot(hb, wg_ref[...], preferred_element_type=F32)
    b = jnp.dot(hb, wu_ref[...], preferred_element_type=F32)
    act = (jax.nn.silu(a) * b).astype(BF16)
    y_ref[...] = x + jnp.dot(act, wd_ref[...], preferred_element_type=F32)


def _ffn(x2, g, wg, wu, wd):
    n, D = x2.shape
    tok = pl.BlockSpec((TS_FFN, D), lambda t: (t, 0))
    return pl.pallas_call(
        _ffn_kernel, grid=(n // TS_FFN,),
        in_specs=[tok] + [_const_spec(a.shape) for a in (g, wg, wu, wd)],
        out_specs=tok, out_shape=jax.ShapeDtypeStruct((n, D), F32),
        compiler_params=_params(("parallel",)), name="ffn",
    )(x2, g, wg, wu, wd)


def _t5_bucket(n):
    max_exact = REL_BUCKETS // 2
    nf = jnp.maximum(n, 1).astype(F32)
    large = max_exact + (jnp.log(nf / max_exact) / math.log(REL_MAX_DIST / max_exact)
                         * (REL_BUCKETS - max_exact)).astype(jnp.int32)
    large = jnp.minimum(large, REL_BUCKETS - 1)
    return jnp.where(n < max_exact, n, large)


def _toeplitz(w, rows, cols):
    H, L = w.shape
    flat = jnp.tile(w, (1, rows))[:, :rows * (L - 1)]
    return flat.reshape(H, rows, L - 1)[:, :, :cols]


def _near_bias(rel_bias):
    n = jnp.arange(2 * NB, dtype=jnp.int32)
    f = (rel_bias[_t5_bucket(n)] - rel_bias[REL_BUCKETS - 1][None, :]).T * LOG2E
    w0 = jnp.concatenate([f[:, :NB], jnp.broadcast_to(f[:, :1], (f.shape[0], NB - 1))], axis=1)
    w1 = jnp.concatenate([f[:, NB:2 * NB], f[:, 1:NB]], axis=1)
    return jnp.stack([_toeplitz(w0, NB, NB), _toeplitz(w1, NB, NB)], axis=0).astype(F32)


def _s5_mats(a_re, a_im, log_dt, b_re, b_im, c_re, c_im):
    lam = lax.complex(a_re.astype(F32), a_im.astype(F32))
    dt = jnp.exp(log_dt.astype(F32))[:, None]
    lam_bar = jnp.exp(lam * dt)
    b_bar = ((lam_bar - 1.0) / lam)[..., None] * lax.complex(b_re.astype(F32), b_im.astype(F32))
    nstate = S5_GROUPS * S5_STATE
    in_mask = (jnp.arange(S5_WIDTH)[:, None] // S5_GROUP) == (jnp.arange(nstate)[None, :] // S5_STATE)

    def blockdiag_in(w):
        t = jnp.transpose(w, (0, 2, 1)).reshape(S5_WIDTH, S5_STATE)
        return jnp.where(in_mask, jnp.tile(t, (1, S5_GROUPS)), 0.0)

    def blockdiag_out(w):
        t = jnp.transpose(w, (0, 2, 1)).reshape(nstate, S5_GROUP)
        return jnp.where(in_mask.T, jnp.tile(t, (1, S5_GROUPS)), 0.0)

    bm = jnp.concatenate([blockdiag_in(jnp.real(b_bar)), blockdiag_in(jnp.imag(b_bar))], axis=1)
    cm = jnp.concatenate([blockdiag_out(c_re.astype(F32)), blockdiag_out(-c_im.astype(F32))], axis=0)
    return bm.astype(BF16), cm.astype(BF16), jnp.real(lam_bar).reshape(1, -1), jnp.imag(lam_bar).reshape(1, -1)


def kernel(x, mem, rel_bias, w_in, g_mix_norm, g_q_dsa, g_kv_dsa, w_uv_dsa, a_re, a_im, log_dt, b_re, b_im, c_re, c_im, d_skip, w_glu, g_mem_norm, w_mem_kv, g_q_cross, g_k_cross, w_br_dsa, w_br_s5, w_br_cross, w_out, g_ffn_norm, w_ffn_gate, w_ffn_up, w_ffn_down):
    B, S, D = x.shape
    depth = w_in.shape[0]
    offs = [0] + [int(o) for o in np.cumsum(IN_SPLITS)]
    nb = _near_bias(rel_bias)
    row = lambda v: v.reshape(1, -1).astype(F32)
    col = lambda v: v.reshape(-1, 1).astype(F32)
    bf = lambda a: a.astype(BF16)
    for l in range(depth):
        wq, wc, wqi, wk, ww, wu, wqx, wg = [bf(w_in[l, :, offs[k]:offs[k + 1]])
                                            for k in range(len(IN_SPLITS))]
        wckw = jnp.pad(jnp.concatenate([wc, wk, ww], axis=1),
                       ((0, 0), (0, 2 * LANES - DSA_LATENT - IDX_DIM - IDX_HEADS)))
        gqc = col(g_q_dsa[l]) * (DSA_LATENT ** -0.5 * LOG2E)

        qT, cT, c, qiT, wT, kidx, u = _inproj(
            x, row(g_mix_norm[l]), wq, wckw, wqi, wu, gqc, row(g_kv_dsa[l]))

        wuvT = bf(jnp.transpose(w_uv_dsa[l], (0, 2, 1)))
        o_dsa = _dsa(qT, qiT, wT, kidx, c, cT, nb, wuvT)

        bm, cm, lre, lim = _s5_mats(a_re[l], a_im[l], log_dt[l], b_re[l], b_im[l], c_re[l], c_im[l])
        lre = jnp.broadcast_to(lre, (B, lre.shape[1]))
        lim = jnp.broadcast_to(lim, (B, lim.shape[1]))
        o_s5 = _s5(u, bm, cm, lre, lim, row(d_skip[l]), bf(w_glu[l]))

        x1 = _merge(x, o_dsa, o_s5, mem, row(g_mix_norm[l]), wg, wqx, row(g_q_cross[l]),
                    bf(w_br_dsa[l]), bf(w_br_s5[l]), bf(w_br_cross[l]), bf(w_out[l]),
                    row(g_mem_norm[l]), bf(w_mem_kv[l]), row(g_k_cross[l]))

        x = _ffn(x1.reshape(B * S, D), row(g_ffn_norm[l]), bf(w_ffn_gate[l]), bf(w_ffn_up[l]),
                 bf(w_ffn_down[l])).reshape(B, S, D)
    return x
```

```python
import math

import jax
import jax.numpy as jnp
import numpy as np
from jax import lax
from jax.experimental import pallas as pl
from jax.experimental.pallas import tpu as pltpu

F32 = jnp.float32
BF16 = jnp.bfloat16

D_MODEL = 1024
N_MEM = 256
EPS = 1e-6
DSA_HEADS = 8
DSA_LATENT = 128
DSA_VDIM = 64
IDX_HEADS = 8
IDX_DIM = 64
TOPK_MAX = 256
REL_BUCKETS = 32
REL_MAX_DIST = 128
S5_WIDTH = 512
S5_GROUP = 16
S5_GROUPS = S5_WIDTH // S5_GROUP
S5_STATE = 64
X_HEADS = 4
X_HEAD_DIM = 128
DSA_WIDTH = DSA_HEADS * DSA_VDIM
X_WIDTH = X_HEADS * X_HEAD_DIM
N_BRANCH = 3
IN_SPLITS = (DSA_HEADS * DSA_LATENT, DSA_LATENT, IDX_HEADS * IDX_DIM, IDX_DIM,
             IDX_HEADS, S5_WIDTH, X_WIDTH, N_BRANCH * D_MODEL)

LANES = 128
SUBLANES = 8
BF16_ROWS = 16
CT_ROWS = DSA_LATENT + BF16_ROWS
VMEM_LIMIT = 56 * 1024 * 1024
LOG2E = math.log2(math.e)

TS_IN = 1024
TQ = 256
TK = 256
FAR_GROUP = 2
NB = REL_MAX_DIST
N_COARSE = 8
N_BISECT = 8
T_S5 = 128
S5_SUB = 4
S5_IN_BLOCKS = 4
S5_OUT_BLOCKS = 2
S5_PITCH = T_S5 + SUBLANES
TS_MERGE = 1024
TS_FFN = 512
NEG = -(2.0 ** 100)

NT_DIMS = (((1,), (1,)), ((), ()))


def _rms(x, g):
    ms = jnp.mean(x * x, axis=-1, keepdims=True)
    return x * lax.rsqrt(ms + EPS) * g


def _tree(fn, x):
    while x.shape[0] > 1:
        half = x.shape[0] // 2
        x = fn(x[:half], x[half:])
    return x[0]


def _const_spec(shape):
    nd = len(shape)
    return pl.BlockSpec(shape, lambda *_: (0,) * nd, pipeline_mode=pl.Buffered(1))


def _params(sem):
    return pltpu.CompilerParams(dimension_semantics=sem, vmem_limit_bytes=VMEM_LIMIT)


def _inproj_kernel(x_ref, gmix_ref, wqT_ref, wckw_ref, wqiT_ref, wu_ref,
                   gqc_ref, gkv_ref,
                   qT_ref, cT_ref, c_ref, qiT_ref, wT_ref, kidx_ref, u_ref):
    ts = x_ref.shape[0]
    hb = _rms(x_ref[...], gmix_ref[...]).astype(BF16)

    qT = jnp.dot(hb, wqT_ref[...], preferred_element_type=F32).T
    q3 = qT.reshape(DSA_HEADS, DSA_LATENT, ts)
    ms = jnp.mean(q3 * q3, axis=1, keepdims=True)
    qT_ref[...] = (q3 * lax.rsqrt(ms + EPS) * gqc_ref[...][None]).reshape(qT.shape).astype(BF16)

    ckw = jnp.dot(hb, wckw_ref[...], preferred_element_type=F32)
    kw = ckw[:, DSA_LATENT:]
    kidx_ref[...] = kw[:, :IDX_DIM].astype(BF16)
    wT_ref[...] = kw.T[IDX_DIM:IDX_DIM + BF16_ROWS, :]

    cn = _rms(ckw[:, :DSA_LATENT], gkv_ref[...])
    c_ref[...] = cn.astype(BF16)
    cTn = cn.T.astype(BF16)
    for k in range(ts // TK):
        cT_ref[k, :DSA_LATENT, :] = cTn[:, k * TK:(k + 1) * TK]
        cT_ref[k, DSA_LATENT:, :] = jnp.ones((BF16_ROWS, TK), BF16)

    qiT_ref[...] = jnp.dot(hb, wqiT_ref[...], preferred_element_type=F32).T.astype(BF16)

    u_ref[...] = jnp.dot(hb, wu_ref[...], preferred_element_type=F32)


def _inproj(x, gmix, wqT, wckw, wqiT, wu, gqc, gkv):
    B, S, D = x.shape
    ts = TS_IN
    grid = (B, S // ts)
    tok = lambda w: pl.BlockSpec((None, ts, w), lambda b, s: (b, s, 0))
    tokT = lambda r: pl.BlockSpec((None, r, ts), lambda b, s: (b, 0, s))
    hq = DSA_HEADS * DSA_LATENT
    hi = IDX_HEADS * IDX_DIM
    out_shape = (
        jax.ShapeDtypeStruct((B, hq, S), BF16),
        jax.ShapeDtypeStruct((B, S // TK, CT_ROWS, TK), BF16),
        jax.ShapeDtypeStruct((B, S, DSA_LATENT), BF16),
        jax.ShapeDtypeStruct((B, hi, S), BF16),
        jax.ShapeDtypeStruct((B, BF16_ROWS, S), F32),
        jax.ShapeDtypeStruct((B, S, IDX_DIM), BF16),
        jax.ShapeDtypeStruct((B, S, S5_WIDTH), F32),
    )
    out_specs = (tokT(hq),
                 pl.BlockSpec((None, ts // TK, CT_ROWS, TK), lambda b, s: (b, s, 0, 0)),
                 tok(DSA_LATENT), tokT(hi), tokT(BF16_ROWS), tok(IDX_DIM),
                 tok(S5_WIDTH))
    consts = (gmix, wqT, wckw, wqiT, wu, gqc, gkv)
    in_specs = [tok(D)] + [_const_spec(a.shape) for a in consts]
    return pl.pallas_call(
        _inproj_kernel, grid=grid, in_specs=in_specs, out_specs=out_specs, out_shape=out_shape,
        compiler_params=_params(("parallel", "parallel")), name="inproj",
    )(x, *consts)


def _dsa_kernel(qT_ref, qiT_ref, wT_ref, kidx_ref, c_ref, cT_ref, nb_ref, wuvT_ref, o_ref,
                sc_ref, scb_ref, mm_ref, pref_ref, lg_ref, m_ref, acc_ref):
    i = pl.program_id(1)
    nk = sc_ref.shape[0]
    kf = float(TOPK_MAX)
    G = TK // SUBLANES
    RB = TK // BF16_ROWS

    def rep(fn, a):
        return jnp.broadcast_to(fn(a, axis=0, keepdims=True), (SUBLANES, TQ))

    def full(v, dt=F32):
        return jnp.full((SUBLANES, TQ), v, dt)

    def key_rows(j):
        return pl.ds(pl.multiple_of(j * TK, TK), TK)

    def score(j):
        ks = kidx_ref[key_rows(j), :]
        acc = None
        for h in range(IDX_HEADS):
            d = jnp.dot(ks, qiT_ref[h * IDX_DIM:(h + 1) * IDX_DIM, :], preferred_element_type=F32)
            t = jnp.maximum(d, 0.0) * wT_ref[h:h + 1, :]
            acc = t if acc is None else acc + t
        return acc

    mm_ref[0] = full(jnp.inf)
    mm_ref[1] = full(-jnp.inf)

    def put_scores(j, s, s_for_min):
        sc_ref[j] = s
        scb_ref[j] = s.astype(BF16)
        mm_ref[0] = jnp.minimum(mm_ref[0], _tree(jnp.minimum, s_for_min.reshape(G, SUBLANES, TQ)))
        mm_ref[1] = jnp.maximum(mm_ref[1], _tree(jnp.maximum, s.reshape(G, SUBLANES, TQ)))

    def score_pair(p, carry):
        for j in (2 * p, 2 * p + 1):
            s = score(j)
            put_scores(j, s, s)
        return carry

    lax.fori_loop(0, lax.shift_right_logical(i, 1), score_pair, 0)

    @pl.when((i & 1) == 1)
    def _():
        s = score(i - 1)
        put_scores(i - 1, s, s)

    key_t = lax.broadcasted_iota(jnp.int32, (TK, TQ), 0)
    qry_t = lax.broadcasted_iota(jnp.int32, (TK, TQ), 1)
    causal = key_t <= qry_t
    s_diag = score(i)
    put_scores(i, jnp.where(causal, s_diag, -jnp.inf), jnp.where(causal, s_diag, jnp.inf))

    @pl.when(i == 0)
    def _():
        sc_ref[0] = jnp.where(causal, 0.0, NEG)

    @pl.when(i > 0)
    def _():
        nt = i + 1

        def tile3(j):
            return sc_ref[j].reshape(G, SUBLANES, TQ)

        lo, hi = rep(jnp.min, mm_ref[0]), rep(jnp.max, mm_ref[1])

        def count_ge(thr):
            def body(j, acc):
                return acc + _tree(jnp.add, jnp.where(tile3(j) >= thr[None], 1.0, 0.0))
            return rep(jnp.sum, lax.fori_loop(0, nt, body, full(0.0)))

        bf_step = 2.0 ** -7
        tiny = 1e-30

        def as_bf16_value(x):
            return x.astype(BF16).astype(F32)

        def count_ge_coarse(thr):
            thr16 = jnp.concatenate([thr, thr], axis=0).astype(BF16)
            one, zero = jnp.ones((), BF16), jnp.zeros((), BF16)

            def body(j, acc):
                t = scb_ref[j].reshape(RB, BF16_ROWS, TQ)
                return acc + _tree(jnp.add, jnp.where(t >= thr16[None], one, zero))
            acc = lax.fori_loop(0, nt, body, jnp.zeros((BF16_ROWS, TQ), BF16))
            return rep(jnp.sum, acc.astype(F32))

        def coarse(_, carry):
            lo_c, hi_c = carry
            mid = as_bf16_value(0.5 * (lo_c + hi_c))
            ge = count_ge_coarse(mid) >= kf
            return jnp.where(ge, mid, lo_c), jnp.where(ge, hi_c, mid)

        lo_c = as_bf16_value(lo - jnp.abs(lo) * bf_step - tiny)
        hi_c = as_bf16_value(hi + jnp.abs(hi) * bf_step + tiny)
        lo_c, hi_c = lax.fori_loop(0, N_COARSE, coarse, (lo_c, hi_c))
        lo = lo_c - jnp.abs(lo_c) * bf_step - tiny
        hi = hi_c

        def bisect(_, carry):
            lo, hi, clo = carry
            mid = 0.5 * (lo + hi)
            cnt = count_ge(mid)
            ge = cnt >= kf
            return jnp.where(ge, mid, lo), jnp.where(ge, hi, mid), jnp.where(ge, cnt, clo)

        lo, hi, clo = lax.fori_loop(0, N_BISECT, bisect, (lo, hi, count_ge(lo)))

        def snap_body(j, am):
            s = tile3(j)
            return jnp.minimum(am, _tree(jnp.minimum, jnp.where(s >= lo[None], s, jnp.inf)))

        cur = rep(jnp.min, lax.fori_loop(0, nt, snap_body, full(jnp.inf)))

        def walk(cur):
            def body(j, carry):
                ac, am = carry
                s = tile3(j)
                g = s > cur[None]
                ac = ac + _tree(jnp.add, jnp.where(g, 1.0, 0.0))
                am = jnp.minimum(am, _tree(jnp.minimum, jnp.where(g, s, jnp.inf)))
                return ac, am
            ac, am = lax.fori_loop(0, nt, body, (full(0.0), full(jnp.inf)))
            return rep(jnp.sum, ac), rep(jnp.min, am)

        def walk_cond(carry):
            _, _, _, go, it = carry
            return jnp.logical_and(go > 0, it < nk * TK + 2)

        def walk_body(carry):
            cur, cge, _, _, it = carry
            c, nxt = walk(cur)
            move = c >= kf
            go = (jnp.max(jnp.where(move, 1.0, 0.0)) > 0.5).astype(jnp.int32)
            return jnp.where(move, nxt, cur), jnp.where(move, c, cge), c, go, it + 1

        kth, cge, cgt, _, _ = lax.while_loop(
            walk_cond, walk_body, (cur, clo, full(0.0), jnp.int32(1), jnp.int32(0)))
        need = kf - cgt
        has_excess = jnp.max(jnp.where(cge > kf, 1.0, 0.0)) > 0.5

        @pl.when(jnp.logical_not(has_excess))
        def _():
            def body(j, carry):
                sc_ref[j] = jnp.where(tile3(j) >= kth[None], 0.0, NEG).reshape(TK, TQ)
                return carry
            lax.fori_loop(0, nt, body, 0)

        @pl.when(has_excess)
        def _():
            tril = jnp.where(key_t >= qry_t, 1.0, 0.0).astype(BF16)

            def pref_body(p, carry):
                for j in (2 * p, jnp.minimum(2 * p + 1, nt - 1)):
                    e01 = jnp.where(tile3(j) == kth[None], 1.0, 0.0).reshape(TK, TQ).astype(BF16)
                    pref_ref[j] = jnp.dot(tril, e01, preferred_element_type=F32)
                return carry

            lax.fori_loop(0, lax.shift_right_logical(nt + 1, 1), pref_body, 0)

            def mask_body(j, offset):
                s = tile3(j)
                pref = pref_ref[j]
                rank = pref.reshape(G, SUBLANES, TQ) + offset[None]
                tie = jnp.where(rank <= need[None], 0.0, NEG)
                mb = jnp.where(s > kth[None], 0.0, jnp.where(s == kth[None], tie, NEG))
                sc_ref[j] = mb.reshape(TK, TQ)
                return offset + jnp.broadcast_to(pref[TK - 1:TK, :], (SUBLANES, TQ))

            lax.fori_loop(0, nt, mask_body, full(0.0))

    m_ref[...] = jnp.full(m_ref.shape, NEG, F32)
    acc_ref[...] = jnp.zeros(acc_ref.shape, F32)
    LG = CT_ROWS // SUBLANES

    def near_bias(kind, h):
        z = jnp.zeros((NB, NB), F32)
        b0, b1 = nb_ref[0, h], nb_ref[1, h]
        blocks = [[b0, b1], [z, b0]] if kind == 0 else [[z, z], [b1, z]]
        return jnp.concatenate([jnp.concatenate(r, axis=1) for r in blocks], axis=0)

    def attend(tiles):
        m_run = [m_ref[h] for h in range(DSA_HEADS)]
        stats = {}

        def phase_a(slot, h, ct, mbb, near):
            lg = jnp.dot(ct, qT_ref[h * DSA_LATENT:(h + 1) * DSA_LATENT, :],
                         preferred_element_type=F32)
            if near is not None:
                lg = lg + near_bias(near, h)
            lgb = lg.astype(BF16) + mbb
            lg_ref[slot, h] = lgb
            tmax = _tree(jnp.maximum, lgb.reshape(RB, BF16_ROWS, TQ)).astype(F32)
            m_new = jnp.maximum(m_run[h], rep(jnp.max, tmax))
            stats[slot, h] = (m_run[h], m_new)
            m_run[h] = m_new

        def phase_b(slot, h, ctT):
            m_old, m_new = stats[slot, h]
            alpha = jnp.exp2(m_old - m_new)
            m16 = jnp.concatenate([m_new, m_new], axis=0).astype(BF16)
            x = lg_ref[slot, h].reshape(RB, BF16_ROWS, TQ) - m16[None]
            pv = jnp.dot(ctT, jnp.exp2(x).reshape(TK, TQ), preferred_element_type=F32)
            acc3 = acc_ref[h].reshape(LG, SUBLANES, TQ) * alpha[None]
            acc_ref[h] = acc3.reshape(CT_ROWS, TQ) + pv

        ops = {}
        for slot in range(len(tiles) + 1):
            if slot < len(tiles):
                j, near = tiles[slot]
                ops[slot] = (c_ref[key_rows(j), :], sc_ref[j].astype(BF16), near, cT_ref[j])
            for h in range(DSA_HEADS):
                if slot < len(tiles):
                    phase_a(slot, h, ops[slot][0], ops[slot][1], ops[slot][2])
                if slot >= 1:
                    phase_b(slot - 1, h, ops[slot - 1][3])
        for h in range(DSA_HEADS):
            m_ref[h] = m_run[h]

    nfar = jnp.maximum(i - 1, 0)

    def far_group(p, carry):
        attend([(FAR_GROUP * p + t, None) for t in range(FAR_GROUP)])
        return carry

    nfull = lax.shift_right_logical(nfar, FAR_GROUP.bit_length() - 1)
    lax.fori_loop(0, nfull, far_group, 0)
    done = nfull * FAR_GROUP
    size = FAR_GROUP // 2
    while size >= 1:
        @pl.when((nfar & size) != 0)
        def _(done=done, size=size):
            attend([(done + t, None) for t in range(size)])
        done = done + (nfar & size)
        size //= 2

    @pl.when(i > 0)
    def _():
        attend([(i - 1, 1), (i, 0)])

    @pl.when(i == 0)
    def _():
        attend([(0, 0)])

    outs = []
    for h in range(DSA_HEADS):
        rl = 1.0 / acc_ref[h, DSA_LATENT:DSA_LATENT + SUBLANES, :]
        o3 = acc_ref[h, :DSA_LATENT, :].reshape(DSA_LATENT // SUBLANES, SUBLANES, TQ) * rl[None]
        o = o3.reshape(DSA_LATENT, TQ).astype(BF16)
        outs.append(jnp.dot(wuvT_ref[h], o, preferred_element_type=F32))
    o_ref[...] = jnp.concatenate(outs, axis=0).T.astype(BF16)


def _dsa(qT, qiT, wT, kidx, c, cT, nb, wuvT):
    B, S, _ = c.shape
    nk = S // TK
    grid = (B, S // TQ)
    tileT = lambda r: pl.BlockSpec((None, r, TQ), lambda b, i: (b, 0, i))
    seq = lambda w: pl.BlockSpec((None, S, w), lambda b, i: (b, 0, 0))
    in_specs = [tileT(DSA_HEADS * DSA_LATENT), tileT(IDX_HEADS * IDX_DIM), tileT(BF16_ROWS),
                seq(IDX_DIM), seq(DSA_LATENT),
                pl.BlockSpec((None, nk, CT_ROWS, TK), lambda b, i: (b, 0, 0, 0)),
                _const_spec(nb.shape), _const_spec(wuvT.shape)]
    scratch = [
        pltpu.VMEM((nk, TK, TQ), F32),
        pltpu.VMEM((nk, TK, TQ), BF16),
        pltpu.VMEM((2, SUBLANES, TQ), F32),
        pltpu.VMEM((nk, TK, TQ), F32),
        pltpu.VMEM((FAR_GROUP, DSA_HEADS, TK, TQ), BF16),
        pltpu.VMEM((DSA_HEADS, SUBLANES, TQ), F32),
        pltpu.VMEM((DSA_HEADS, CT_ROWS, TQ), F32),
    ]
    return pl.pallas_call(
        _dsa_kernel, grid=grid, in_specs=in_specs,
        out_specs=pl.BlockSpec((None, TQ, DSA_WIDTH), lambda b, i: (b, i, 0)),
        out_shape=jax.ShapeDtypeStruct((B, S, DSA_WIDTH), BF16), scratch_shapes=scratch,
        compiler_params=_params(("parallel", "arbitrary")), name="dsa",
    )(qT, qiT, wT, kidx, c, cT, nb, wuvT)


def _s5_kernel(u_ref, bm_ref, cm_ref, lre_ref, lim_ref, dsk_ref, wglu_ref, o_ref,
               uslab_ref, uil_ref, oslab_ref, hre_ref, him_ref, *st_refs):
    nstate = S5_GROUPS * S5_STATE
    nb = hre_ref.shape[0]
    nslab = S5_WIDTH // LANES

    @pl.when(pl.program_id(0) == 0)
    def _():
        hre_ref[...] = jnp.zeros(hre_ref.shape, F32)
        him_ref[...] = jnp.zeros(him_ref.shape, F32)

    for b in range(nb):
        for k in range(nslab):
            uslab_ref[k, b * S5_PITCH:b * S5_PITCH + T_S5, :] = u_ref[b, :, k * LANES:(k + 1) * LANES]

    def gather_step(t, carry):
        for k in range(nslab):
            uil_ref[pl.ds(pl.multiple_of(t * nb, nb), nb), k * LANES:(k + 1) * LANES] = (
                uslab_ref[k, pl.ds(t, nb, stride=S5_PITCH), :])
        return carry

    lax.fori_loop(0, T_S5, gather_step, 0, unroll=4)

    tsub = T_S5 // len(st_refs)
    rsub = tsub * nb

    def in_matmul(c):
        ub = uil_ref[c * rsub:(c + 1) * rsub, :].astype(BF16)
        cin = S5_WIDTH // S5_IN_BLOCKS
        sin = nstate // S5_IN_BLOCKS
        for k in range(S5_IN_BLOCKS):
            uk = ub[:, k * cin:(k + 1) * cin]
            for off in (0, nstate):
                cols = slice(off + k * sin, off + (k + 1) * sin)
                st_refs[c][:, cols] = jnp.dot(uk, bm_ref[k * cin:(k + 1) * cin, cols],
                                              preferred_element_type=F32)

    def scan(c):
        st_ref = st_refs[c]
        half = nstate // 2
        for part in range(2):
            re_sl = slice(part * half, (part + 1) * half)
            im_sl = slice(nstate + part * half, nstate + (part + 1) * half)
            lre = lre_ref[:, re_sl]
            lim = lim_ref[:, re_sl]
            hr, hi = hre_ref[:, re_sl], him_ref[:, re_sl]
            for t in range(tsub):
                r = slice(t * nb, (t + 1) * nb)
                hr, hi = (lre * hr - lim * hi + st_ref[r, re_sl],
                          lre * hi + lim * hr + st_ref[r, im_sl])
                st_ref[r, re_sl] = hr
                st_ref[r, im_sl] = hi
            hre_ref[:, re_sl] = hr
            him_ref[:, re_sl] = hi

    def out_matmul(c):
        st_ref = st_refs[c]
        cout = S5_WIDTH // S5_OUT_BLOCKS
        sout = nstate // S5_OUT_BLOCKS
        ys = []
        for k in range(S5_OUT_BLOCKS):
            acc = None
            for off in (0, nstate):
                rows = slice(off + k * sout, off + (k + 1) * sout)
                part = jnp.dot(st_ref[:, rows].astype(BF16), cm_ref[rows, k * cout:(k + 1) * cout],
                               preferred_element_type=F32)
                acc = part if acc is None else acc + part
            ys.append(acc)
        y = jnp.concatenate(ys, axis=1)
        y = jax.nn.gelu(y + dsk_ref[...] * uil_ref[c * rsub:(c + 1) * rsub, :])
        z = jnp.dot(y.astype(BF16), wglu_ref[...], preferred_element_type=F32)
        o = y * jax.nn.sigmoid(z)
        for k in range(nslab):
            oslab_ref[k, c * rsub:(c + 1) * rsub, :] = o[:, k * LANES:(k + 1) * LANES]

    nsub = len(st_refs)
    for c in range(nsub + 2):
        if c < nsub:
            in_matmul(c)
        if 1 <= c <= nsub:
            scan(c - 1)
        if c >= 2:
            out_matmul(c - 2)

    for b in range(nb):
        for k in range(nslab):
            o_ref[b, :, k * LANES:(k + 1) * LANES] = (
                oslab_ref[k, pl.ds(b, T_S5, stride=nb), :].astype(BF16))


def _s5(u, bm, cm, lre, lim, dsk, wglu):
    B, S, W = u.shape
    tb = T_S5 * B
    nstate = S5_GROUPS * S5_STATE
    tok = pl.BlockSpec((B, T_S5, W), lambda t: (0, t, 0))
    in_specs = [tok, _const_spec(bm.shape), _const_spec(cm.shape), _const_spec(lre.shape),
                _const_spec(lim.shape), _const_spec(dsk.shape), _const_spec(wglu.shape)]
    scratch = [pltpu.VMEM((W // LANES, B * S5_PITCH, LANES), F32),
               pltpu.VMEM((tb, W), F32),
               pltpu.VMEM((W // LANES, tb, LANES), F32),
               pltpu.VMEM((B, nstate), F32), pltpu.VMEM((B, nstate), F32)]
    scratch += [pltpu.VMEM((tb // S5_SUB, 2 * nstate), F32) for _ in range(S5_SUB)]
    return pl.pallas_call(
        _s5_kernel, grid=(S // T_S5,), in_specs=in_specs, out_specs=tok,
        out_shape=jax.ShapeDtypeStruct((B, S, W), BF16), scratch_shapes=scratch,
        compiler_params=_params(("arbitrary",)), name="s5",
    )(u, bm, cm, lre, lim, dsk, wglu)


def _merge_kernel(x_ref, odsa_ref, os5_ref, mem_ref, gmix_ref, wg_ref, wqx_ref, gqx_ref,
                  wb1_ref, wb2_ref, wb3_ref, wout_ref, gmem_ref, wkv_ref, gk_ref, y_ref,
                  k_ref, v_ref):
    @pl.when(pl.program_id(1) == 0)
    def _():
        mb = _rms(mem_ref[...], gmem_ref[...]).astype(BF16)
        kv = jnp.dot(mb, wkv_ref[...], preferred_element_type=F32)
        for h in range(X_HEADS):
            sl = slice(h * X_HEAD_DIM, (h + 1) * X_HEAD_DIM)
            k_ref[:, sl] = _rms(kv[:, sl], gk_ref[...]).astype(BF16)
        v_ref[...] = kv[:, X_WIDTH:].astype(BF16)

    x = x_ref[...]
    hb = _rms(x, gmix_ref[...]).astype(BF16)

    qx = jnp.dot(hb, wqx_ref[...], preferred_element_type=F32)
    gqx = gqx_ref[...] * (X_HEAD_DIM ** -0.5)
    ox = []
    for h in range(X_HEADS):
        sl = slice(h * X_HEAD_DIM, (h + 1) * X_HEAD_DIM)
        qh = _rms(qx[:, sl], gqx).astype(BF16)
        lg = lax.dot_general(qh, k_ref[:, sl], NT_DIMS, preferred_element_type=F32)
        p = jnp.exp(lg - jnp.max(lg, axis=-1, keepdims=True))
        pv = jnp.dot(p.astype(BF16), v_ref[:, sl], preferred_element_type=F32)
        ox.append((pv / jnp.sum(p, axis=-1, keepdims=True)).astype(BF16))
    ox = jnp.concatenate(ox, axis=1)

    merged = None
    for br, (o, wb) in enumerate(((odsa_ref[...], wb1_ref), (os5_ref[...], wb2_ref), (ox, wb3_ref))):
        gate = jax.nn.sigmoid(jnp.dot(hb, wg_ref[:, br * D_MODEL:(br + 1) * D_MODEL],
                                      preferred_element_type=F32))
        term = gate * jnp.dot(o, wb[...], preferred_element_type=F32)
        merged = term if merged is None else merged + term
    y_ref[...] = x + jnp.dot(merged.astype(BF16), wout_ref[...], preferred_element_type=F32)


def _merge(x, odsa, os5, mem, gmix, wg, wqx, gqx, wb1, wb2, wb3, wout, gmem, wkv, gk):
    B, S, D = x.shape
    ts = TS_MERGE
    tok = lambda w: pl.BlockSpec((None, ts, w), lambda b, s: (b, s, 0))
    consts = (gmix, wg, wqx, gqx, wb1, wb2, wb3, wout, gmem, wkv, gk)
    in_specs = [tok(D), tok(DSA_WIDTH), tok(S5_WIDTH),
                pl.BlockSpec((None, N_MEM, D), lambda b, s: (b, 0, 0))
                ] + [_const_spec(a.shape) for a in consts]
    scratch = [pltpu.VMEM((N_MEM, X_WIDTH), BF16), pltpu.VMEM((N_MEM, X_WIDTH), BF16)]
    return pl.pallas_call(
        _merge_kernel, grid=(B, S // ts), in_specs=in_specs, out_specs=tok(D),
        out_shape=jax.ShapeDtypeStruct((B, S, D), F32), scratch_shapes=scratch,
        compiler_params=_params(("parallel", "arbitrary")), name="merge",
    )(x, odsa, os5, mem, *consts)


def _ffn_kernel(x_ref, g_ref, wg_ref, wu_ref, wd_ref, y_ref):
    x = x_ref[...]
    hb = _rms(x, g_ref[...]).astype(BF16)
    a = jnp.dot(hb, wg_ref[...], preferred_element_type=F32)
    b = jnp.dot(hb, wu_ref[...], preferred_element_type=F32)
    act = (jax.nn.silu(a) * b).astype(BF16)
    y_ref[...] = x + jnp.dot(act, wd_ref[...], preferred_element_type=F32)


def _ffn(x2, g, wg, wu, wd):
    n, D = x2.shape
    tok = pl.BlockSpec((TS_FFN, D), lambda t: (t, 0))
    return pl.pallas_call(
        _ffn_kernel, grid=(n // TS_FFN,),
        in_specs=[tok] + [_const_spec(a.shape) for a in (g, wg, wu, wd)],
        out_specs=tok, out_shape=jax.ShapeDtypeStruct((n, D), F32),
        compiler_params=_params(("parallel",)), name="ffn",
    )(x2, g, wg, wu, wd)


def _t5_bucket(n):
    max_exact = REL_BUCKETS // 2
    nf = jnp.maximum(n, 1).astype(F32)
    large = max_exact + (jnp.log(nf / max_exact) / math.log(REL_MAX_DIST / max_exact)
                         * (REL_BUCKETS - max_exact)).astype(jnp.int32)
    large = jnp.minimum(large, REL_BUCKETS - 1)
    return jnp.where(n < max_exact, n, large)


def _toeplitz(w, rows, cols):
    H, L = w.shape
    flat = jnp.tile(w, (1, rows))[:, :rows * (L - 1)]
    return flat.reshape(H, rows, L - 1)[:, :, :cols]


def _near_bias(rel_bias):
    n = jnp.arange(2 * NB, dtype=jnp.int32)
    f = (rel_bias[_t5_bucket(n)] - rel_bias[REL_BUCKETS - 1][None, :]).T * LOG2E
    w0 = jnp.concatenate([f[:, :NB], jnp.broadcast_to(f[:, :1], (f.shape[0], NB - 1))], axis=1)
    w1 = jnp.concatenate([f[:, NB:2 * NB], f[:, 1:NB]], axis=1)
    return jnp.stack([_toeplitz(w0, NB, NB), _toeplitz(w1, NB, NB)], axis=0).astype(F32)


def _s5_mats(a_re, a_im, log_dt, b_re, b_im, c_re, c_im):
    lam = lax.complex(a_re.astype(F32), a_im.astype(F32))
    dt = jnp.exp(log_dt.astype(F32))[:, None]
    lam_bar = jnp.exp(lam * dt)
    b_bar = ((lam_bar - 1.0) / lam)[..., None] * lax.complex(b_re.astype(F32), b_im.astype(F32))
    nstate = S5_GROUPS * S5_STATE
    in_mask = (jnp.arange(S5_WIDTH)[:, None] // S5_GROUP) == (jnp.arange(nstate)[None, :] // S5_STATE)

    def blockdiag_in(w):
        t = jnp.transpose(w, (0, 2, 1)).reshape(S5_WIDTH, S5_STATE)
        return jnp.where(in_mask, jnp.tile(t, (1, S5_GROUPS)), 0.0)

    def blockdiag_out(w):
        t = jnp.transpose(w, (0, 2, 1)).reshape(nstate, S5_GROUP)
        return jnp.where(in_mask.T, jnp.tile(t, (1, S5_GROUPS)), 0.0)

    bm = jnp.concatenate([blockdiag_in(jnp.real(b_bar)), blockdiag_in(jnp.imag(b_bar))], axis=1)
    cm = jnp.concatenate([blockdiag_out(c_re.astype(F32)), blockdiag_out(-c_im.astype(F32))], axis=0)
    return bm.astype(BF16), cm.astype(BF16), jnp.real(lam_bar).reshape(1, -1), jnp.imag(lam_bar).reshape(1, -1)


def kernel(x, mem, rel_bias, w_in, g_mix_norm, g_q_dsa, g_kv_dsa, w_uv_dsa, a_re, a_im, log_dt, b_re, b_im, c_re, c_im, d_skip, w_glu, g_mem_norm, w_mem_kv, g_q_cross, g_k_cross, w_br_dsa, w_br_s5, w_br_cross, w_out, g_ffn_norm, w_ffn_gate, w_ffn_up, w_ffn_down):
    B, S, D = x.shape
    depth = w_in.shape[0]
    offs = [0] + [int(o) for o in np.cumsum(IN_SPLITS)]
    nb = _near_bias(rel_bias)
    row = lambda v: v.reshape(1, -1).astype(F32)
    col = lambda v: v.reshape(-1, 1).astype(F32)
    bf = lambda a: a.astype(BF16)
    for l in range(depth):
        wq, wc, wqi, wk, ww, wu, wqx, wg = [bf(w_in[l, :, offs[k]:offs[k + 1]])
                                            for k in range(len(IN_SPLITS))]
        wckw = jnp.pad(jnp.concatenate([wc, wk, ww], axis=1),
                       ((0, 0), (0, 2 * LANES - DSA_LATENT - IDX_DIM - IDX_HEADS)))
        gqc = col(g_q_dsa[l]) * (DSA_LATENT ** -0.5 * LOG2E)

        qT, cT, c, qiT, wT, kidx, u = _inproj(
            x, row(g_mix_norm[l]), wq, wckw, wqi, wu, gqc, row(g_kv_dsa[l]))

        wuvT = bf(jnp.transpose(w_uv_dsa[l], (0, 2, 1)))
        o_dsa = _dsa(qT, qiT, wT, kidx, c, cT, nb, wuvT)

        bm, cm, lre, lim = _s5_mats(a_re[l], a_im[l], log_dt[l], b_re[l], b_im[l], c_re[l], c_im[l])
        lre = jnp.broadcast_to(lre, (B, lre.shape[1]))
        lim = jnp.broadcast_to(lim, (B, lim.shape[1]))
        o_s5 = _s5(u, bm, cm, lre, lim, row(d_skip[l]), bf(w_glu[l]))

        x1 = _merge(x, o_dsa, o_s5, mem, row(g_mix_norm[l]), wg, wqx, row(g_q_cross[l]),
                    bf(w_br_dsa[l]), bf(w_br_s5[l]), bf(w_br_cross[l]), bf(w_out[l]),
                    row(g_mem_norm[l]), bf(w_mem_kv[l]), row(g_k_cross[l]))

        x = _ffn(x1.reshape(B * S, D), row(g_ffn_norm[l]), bf(w_ffn_gate[l]), bf(w_ffn_up[l]),
                 bf(w_ffn_down[l])).reshape(B, S, D)
    return x
```

```python
import math

import jax
import jax.numpy as jnp
import numpy as np
from jax import lax
from jax.experimental import pallas as pl
from jax.experimental.pallas import tpu as pltpu

F32 = jnp.float32
BF16 = jnp.bfloat16

D_MODEL = 1024
N_MEM = 256
EPS = 1e-6
DSA_HEADS = 8
DSA_LATENT = 128
DSA_VDIM = 64
IDX_HEADS = 8
IDX_DIM = 64
TOPK_MAX = 256
REL_BUCKETS = 32
REL_MAX_DIST = 128
S5_WIDTH = 512
S5_GROUP = 16
S5_GROUPS = S5_WIDTH // S5_GROUP
S5_STATE = 64
X_HEADS = 4
X_HEAD_DIM = 128
DSA_WIDTH = DSA_HEADS * DSA_VDIM
X_WIDTH = X_HEADS * X_HEAD_DIM
N_BRANCH = 3
IN_SPLITS = (DSA_HEADS * DSA_LATENT, DSA_LATENT, IDX_HEADS * IDX_DIM, IDX_DIM,
             IDX_HEADS, S5_WIDTH, X_WIDTH, N_BRANCH * D_MODEL)

LANES = 128
SUBLANES = 8
BF16_ROWS = 16
CT_ROWS = DSA_LATENT + BF16_ROWS
VMEM_LIMIT = 56 * 1024 * 1024
LOG2E = math.log2(math.e)

TS_IN = 1024
TQ = 256
TK = 256
FAR_GROUP = 2
ATT_LAG = 8
NB = REL_MAX_DIST
N_COARSE = 8
N_BISECT = 8
T_S5 = 128
S5_SUB = 4
S5_IN_BLOCKS = 4
S5_OUT_BLOCKS = 2
S5_PITCH = T_S5 + SUBLANES
TS_MERGE = 1024
TS_FFN = 512
NEG = -(2.0 ** 100)

NT_DIMS = (((1,), (1,)), ((), ()))


def _rms(x, g):
    ms = jnp.mean(x * x, axis=-1, keepdims=True)
    return x * lax.rsqrt(ms + EPS) * g


def _tree(fn, x):
    while x.shape[0] > 1:
        half = x.shape[0] // 2
        x = fn(x[:half], x[half:])
    return x[0]


def _const_spec(shape):
    nd = len(shape)
    return pl.BlockSpec(shape, lambda *_: (0,) * nd, pipeline_mode=pl.Buffered(1))


def _params(sem):
    return pltpu.CompilerParams(dimension_semantics=sem, vmem_limit_bytes=VMEM_LIMIT)


def _inproj_kernel(x_ref, gmix_ref, wqT_ref, wckw_ref, wqiT_ref, wu_ref,
                   gqc_ref, gkv_ref,
                   qT_ref, cT_ref, c_ref, qiT_ref, wT_ref, kidx_ref, u_ref):
    ts = x_ref.shape[0]
    hb = _rms(x_ref[...], gmix_ref[...]).astype(BF16)

    qT = jnp.dot(hb, wqT_ref[...], preferred_element_type=F32).T
    q3 = qT.reshape(DSA_HEADS, DSA_LATENT, ts)
    ms = jnp.mean(q3 * q3, axis=1, keepdims=True)
    qT_ref[...] = (q3 * lax.rsqrt(ms + EPS) * gqc_ref[...][None]).reshape(qT.shape).astype(BF16)

    ckw = jnp.dot(hb, wckw_ref[...], preferred_element_type=F32)
    kw = ckw[:, DSA_LATENT:]
    kidx_ref[...] = kw[:, :IDX_DIM].astype(BF16)
    wT_ref[...] = kw.T[IDX_DIM:IDX_DIM + BF16_ROWS, :]

    cn = _rms(ckw[:, :DSA_LATENT], gkv_ref[...])
    c_ref[...] = cn.astype(BF16)
    cTn = cn.T.astype(BF16)
    for k in range(ts // TK):
        cT_ref[k, :DSA_LATENT, :] = cTn[:, k * TK:(k + 1) * TK]
        cT_ref[k, DSA_LATENT:, :] = jnp.ones((BF16_ROWS, TK), BF16)

    qiT_ref[...] = jnp.dot(hb, wqiT_ref[...], preferred_element_type=F32).T.astype(BF16)

    u_ref[...] = jnp.dot(hb, wu_ref[...], preferred_element_type=F32)


def _inproj(x, gmix, wqT, wckw, wqiT, wu, gqc, gkv):
    B, S, D = x.shape
    ts = TS_IN
    grid = (B, S // ts)
    tok = lambda w: pl.BlockSpec((None, ts, w), lambda b, s: (b, s, 0))
    tokT = lambda r: pl.BlockSpec((None, r, ts), lambda b, s: (b, 0, s))
    hq = DSA_HEADS * DSA_LATENT
    hi = IDX_HEADS * IDX_DIM
    out_shape = (
        jax.ShapeDtypeStruct((B, hq, S), BF16),
        jax.ShapeDtypeStruct((B, S // TK, CT_ROWS, TK), BF16),
        jax.ShapeDtypeStruct((B, S, DSA_LATENT), BF16),
        jax.ShapeDtypeStruct((B, hi, S), BF16),
        jax.ShapeDtypeStruct((B, BF16_ROWS, S), F32),
        jax.ShapeDtypeStruct((B, S, IDX_DIM), BF16),
        jax.ShapeDtypeStruct((B, S, S5_WIDTH), F32),
    )
    out_specs = (tokT(hq),
                 pl.BlockSpec((None, ts // TK, CT_ROWS, TK), lambda b, s: (b, s, 0, 0)),
                 tok(DSA_LATENT), tokT(hi), tokT(BF16_ROWS), tok(IDX_DIM),
                 tok(S5_WIDTH))
    consts = (gmix, wqT, wckw, wqiT, wu, gqc, gkv)
    in_specs = [tok(D)] + [_const_spec(a.shape) for a in consts]
    return pl.pallas_call(
        _inproj_kernel, grid=grid, in_specs=in_specs, out_specs=out_specs, out_shape=out_shape,
        compiler_params=_params(("parallel", "parallel")), name="inproj",
    )(x, *consts)


def _dsa_kernel(qT_ref, qiT_ref, wT_ref, kidx_ref, c_ref, cT_ref, nb_ref, wuvT_ref, o_ref,
                sc_ref, scb_ref, mm_ref, pref_ref, lg_ref, m_ref, acc_ref):
    i = pl.program_id(1)
    nk = sc_ref.shape[0]
    kf = float(TOPK_MAX)
    G = TK // SUBLANES
    RB = TK // BF16_ROWS

    def rep(fn, a):
        return jnp.broadcast_to(fn(a, axis=0, keepdims=True), (SUBLANES, TQ))

    def full(v, dt=F32):
        return jnp.full((SUBLANES, TQ), v, dt)

    def key_rows(j):
        return pl.ds(pl.multiple_of(j * TK, TK), TK)

    def score(j):
        ks = kidx_ref[key_rows(j), :]
        acc = None
        for h in range(IDX_HEADS):
            d = jnp.dot(ks, qiT_ref[h * IDX_DIM:(h + 1) * IDX_DIM, :], preferred_element_type=F32)
            t = jnp.maximum(d, 0.0) * wT_ref[h:h + 1, :]
            acc = t if acc is None else acc + t
        return acc

    mm_ref[0] = full(jnp.inf)
    mm_ref[1] = full(-jnp.inf)

    def put_scores(j, s, s_for_min):
        sc_ref[j] = s
        scb_ref[j] = s.astype(BF16)
        mm_ref[0] = jnp.minimum(mm_ref[0], _tree(jnp.minimum, s_for_min.reshape(G, SUBLANES, TQ)))
        mm_ref[1] = jnp.maximum(mm_ref[1], _tree(jnp.maximum, s.reshape(G, SUBLANES, TQ)))

    def score_pair(p, carry):
        for j in (2 * p, 2 * p + 1):
            s = score(j)
            put_scores(j, s, s)
        return carry

    lax.fori_loop(0, lax.shift_right_logical(i, 1), score_pair, 0)

    @pl.when((i & 1) == 1)
    def _():
        s = score(i - 1)
        put_scores(i - 1, s, s)

    key_t = lax.broadcasted_iota(jnp.int32, (TK, TQ), 0)
    qry_t = lax.broadcasted_iota(jnp.int32, (TK, TQ), 1)
    causal = key_t <= qry_t
    s_diag = score(i)
    put_scores(i, jnp.where(causal, s_diag, -jnp.inf), jnp.where(causal, s_diag, jnp.inf))

    @pl.when(i == 0)
    def _():
        sc_ref[0] = jnp.where(causal, 0.0, NEG)

    @pl.when(i > 0)
    def _():
        nt = i + 1

        def tile3(j):
            return sc_ref[j].reshape(G, SUBLANES, TQ)

        lo, hi = rep(jnp.min, mm_ref[0]), rep(jnp.max, mm_ref[1])

        def count_ge(thr):
            def body(j, acc):
                return acc + _tree(jnp.add, jnp.where(tile3(j) >= thr[None], 1.0, 0.0))
            return rep(jnp.sum, lax.fori_loop(0, nt, body, full(0.0)))

        bf_step = 2.0 ** -7
        tiny = 1e-30

        def as_bf16_value(x):
            return x.astype(BF16).astype(F32)

        def count_ge_coarse(thr):
            thr16 = jnp.concatenate([thr, thr], axis=0).astype(BF16)
            one, zero = jnp.ones((), BF16), jnp.zeros((), BF16)

            def body(j, acc):
                t = scb_ref[j].reshape(RB, BF16_ROWS, TQ)
                return acc + _tree(jnp.add, jnp.where(t >= thr16[None], one, zero))
            acc = lax.fori_loop(0, nt, body, jnp.zeros((BF16_ROWS, TQ), BF16))
            return rep(jnp.sum, acc.astype(F32))

        def coarse(_, carry):
            lo_c, hi_c = carry
            mid = as_bf16_value(0.5 * (lo_c + hi_c))
            ge = count_ge_coarse(mid) >= kf
            return jnp.where(ge, mid, lo_c), jnp.where(ge, hi_c, mid)

        lo_c = as_bf16_value(lo - jnp.abs(lo) * bf_step - tiny)
        hi_c = as_bf16_value(hi + jnp.abs(hi) * bf_step + tiny)
        lo_c, hi_c = lax.fori_loop(0, N_COARSE, coarse, (lo_c, hi_c))
        lo = lo_c - jnp.abs(lo_c) * bf_step - tiny
        hi = hi_c

        def bisect(_, carry):
            lo, hi, clo = carry
            mid = 0.5 * (lo + hi)
            cnt = count_ge(mid)
            ge = cnt >= kf
            return jnp.where(ge, mid, lo), jnp.where(ge, hi, mid), jnp.where(ge, cnt, clo)

        lo, hi, clo = lax.fori_loop(0, N_BISECT, bisect, (lo, hi, count_ge(lo)))

        def snap_body(j, am):
            s = tile3(j)
            return jnp.minimum(am, _tree(jnp.minimum, jnp.where(s >= lo[None], s, jnp.inf)))

        cur = rep(jnp.min, lax.fori_loop(0, nt, snap_body, full(jnp.inf)))

        def walk(cur):
            def body(j, carry):
                ac, am = carry
                s = tile3(j)
                g = s > cur[None]
                ac = ac + _tree(jnp.add, jnp.where(g, 1.0, 0.0))
                am = jnp.minimum(am, _tree(jnp.minimum, jnp.where(g, s, jnp.inf)))
                return ac, am
            ac, am = lax.fori_loop(0, nt, body, (full(0.0), full(jnp.inf)))
            return rep(jnp.sum, ac), rep(jnp.min, am)

        def walk_cond(carry):
            _, _, _, go, it = carry
            return jnp.logical_and(go > 0, it < nk * TK + 2)

        def walk_body(carry):
            cur, cge, _, _, it = carry
            c, nxt = walk(cur)
            move = c >= kf
            go = (jnp.max(jnp.where(move, 1.0, 0.0)) > 0.5).astype(jnp.int32)
            return jnp.where(move, nxt, cur), jnp.where(move, c, cge), c, go, it + 1

        kth, cge, cgt, _, _ = lax.while_loop(
            walk_cond, walk_body, (cur, clo, full(0.0), jnp.int32(1), jnp.int32(0)))
        need = kf - cgt
        has_excess = jnp.max(jnp.where(cge > kf, 1.0, 0.0)) > 0.5

        @pl.when(jnp.logical_not(has_excess))
        def _():
            def body(j, carry):
                sc_ref[j] = jnp.where(tile3(j) >= kth[None], 0.0, NEG).reshape(TK, TQ)
                return carry
            lax.fori_loop(0, nt, body, 0)

        @pl.when(has_excess)
        def _():
            tril = jnp.where(key_t >= qry_t, 1.0, 0.0).astype(BF16)

            def pref_body(p, carry):
                for j in (2 * p, jnp.minimum(2 * p + 1, nt - 1)):
                    e01 = jnp.where(tile3(j) == kth[None], 1.0, 0.0).reshape(TK, TQ).astype(BF16)
                    pref_ref[j] = jnp.dot(tril, e01, preferred_element_type=F32)
                return carry

            lax.fori_loop(0, lax.shift_right_logical(nt + 1, 1), pref_body, 0)

            def mask_body(j, offset):
                s = tile3(j)
                pref = pref_ref[j]
                rank = pref.reshape(G, SUBLANES, TQ) + offset[None]
                tie = jnp.where(rank <= need[None], 0.0, NEG)
                mb = jnp.where(s > kth[None], 0.0, jnp.where(s == kth[None], tie, NEG))
                sc_ref[j] = mb.reshape(TK, TQ)
                return offset + jnp.broadcast_to(pref[TK - 1:TK, :], (SUBLANES, TQ))

            lax.fori_loop(0, nt, mask_body, full(0.0))

    m_ref[...] = jnp.full(m_ref.shape, NEG, F32)
    acc_ref[...] = jnp.zeros(acc_ref.shape, F32)
    LG = CT_ROWS // SUBLANES

    def near_bias(kind, h):
        z = jnp.zeros((NB, NB), F32)
        b0, b1 = nb_ref[0, h], nb_ref[1, h]
        blocks = [[b0, b1], [z, b0]] if kind == 0 else [[z, z], [b1, z]]
        return jnp.concatenate([jnp.concatenate(r, axis=1) for r in blocks], axis=0)

    def attend(tiles):
        m_run = [m_ref[h] for h in range(DSA_HEADS)]
        stats = {}

        def phase_a(slot, h, ct, mbb, near):
            lg = jnp.dot(ct, qT_ref[h * DSA_LATENT:(h + 1) * DSA_LATENT, :],
                         preferred_element_type=F32)
            if near is not None:
                lg = lg + near_bias(near, h)
            lgb = lg.astype(BF16) + mbb
            lg_ref[slot, h] = lgb
            tmax = _tree(jnp.maximum, lgb.reshape(RB, BF16_ROWS, TQ)).astype(F32)
            m_new = jnp.maximum(m_run[h], rep(jnp.max, tmax))
            stats[slot, h] = (m_run[h], m_new)
            m_run[h] = m_new

        def phase_b(slot, h, ctT):
            m_old, m_new = stats[slot, h]
            alpha = jnp.exp2(m_old - m_new)
            m16 = jnp.concatenate([m_new, m_new], axis=0).astype(BF16)
            x = lg_ref[slot, h].reshape(RB, BF16_ROWS, TQ) - m16[None]
            pv = jnp.dot(ctT, jnp.exp2(x).reshape(TK, TQ), preferred_element_type=F32)
            acc3 = acc_ref[h].reshape(LG, SUBLANES, TQ) * alpha[None]
            acc_ref[h] = acc3.reshape(CT_ROWS, TQ) + pv

        ops = [(c_ref[key_rows(j), :], sc_ref[j].astype(BF16), near, cT_ref[j]) for j, near in tiles]
        units = [(slot, h) for slot in range(len(tiles)) for h in range(DSA_HEADS)]
        for k in range(len(units) + ATT_LAG):
            if k < len(units):
                slot, h = units[k]
                phase_a(slot, h, ops[slot][0], ops[slot][1], ops[slot][2])
            if k >= ATT_LAG:
                slot, h = units[k - ATT_LAG]
                phase_b(slot, h, ops[slot][3])
        for h in range(DSA_HEADS):
            m_ref[h] = m_run[h]

    nfar = jnp.maximum(i - 1, 0)

    def far_group(p, carry):
        attend([(FAR_GROUP * p + t, None) for t in range(FAR_GROUP)])
        return carry

    nfull = lax.shift_right_logical(nfar, FAR_GROUP.bit_length() - 1)
    lax.fori_loop(0, nfull, far_group, 0)
    done = nfull * FAR_GROUP
    size = FAR_GROUP // 2
    while size >= 1:
        @pl.when((nfar & size) != 0)
        def _(done=done, size=size):
            attend([(done + t, None) for t in range(size)])
        done = done + (nfar & size)
        size //= 2

    @pl.when(i > 0)
    def _():
        attend([(i - 1, 1), (i, 0)])

    @pl.when(i == 0)
    def _():
        attend([(0, 0)])

    outs = []
    for h in range(DSA_HEADS):
        rl = 1.0 / acc_ref[h, DSA_LATENT:DSA_LATENT + SUBLANES, :]
        o3 = acc_ref[h, :DSA_LATENT, :].reshape(DSA_LATENT // SUBLANES, SUBLANES, TQ) * rl[None]
        o = o3.reshape(DSA_LATENT, TQ).astype(BF16)
        outs.append(jnp.dot(wuvT_ref[h], o, preferred_element_type=F32))
    o_ref[...] = jnp.concatenate(outs, axis=0).T.astype(BF16)


def _dsa(qT, qiT, wT, kidx, c, cT, nb, wuvT):
    B, S, _ = c.shape
    nk = S // TK
    grid = (B, S // TQ)
    tileT = lambda r: pl.BlockSpec((None, r, TQ), lambda b, i: (b, 0, i))
    seq = lambda w: pl.BlockSpec((None, S, w), lambda b, i: (b, 0, 0))
    in_specs = [tileT(DSA_HEADS * DSA_LATENT), tileT(IDX_HEADS * IDX_DIM), tileT(BF16_ROWS),
                seq(IDX_DIM), seq(DSA_LATENT),
                pl.BlockSpec((None, nk, CT_ROWS, TK), lambda b, i: (b, 0, 0, 0)),
                _const_spec(nb.shape), _const_spec(wuvT.shape)]
    scratch = [
        pltpu.VMEM((nk, TK, TQ), F32),
        pltpu.VMEM((nk, TK, TQ), BF16),
        pltpu.VMEM((2, SUBLANES, TQ), F32),
        pltpu.VMEM((nk, TK, TQ), F32),
        pltpu.VMEM((FAR_GROUP, DSA_HEADS, TK, TQ), BF16),
        pltpu.VMEM((DSA_HEADS, SUBLANES, TQ), F32),
        pltpu.VMEM((DSA_HEADS, CT_ROWS, TQ), F32),
    ]
    return pl.pallas_call(
        _dsa_kernel, grid=grid, in_specs=in_specs,
        out_specs=pl.BlockSpec((None, TQ, DSA_WIDTH), lambda b, i: (b, i, 0)),
        out_shape=jax.ShapeDtypeStruct((B, S, DSA_WIDTH), BF16), scratch_shapes=scratch,
        compiler_params=_params(("parallel", "arbitrary")), name="dsa",
    )(qT, qiT, wT, kidx, c, cT, nb, wuvT)


def _s5_kernel(u_ref, bm_ref, cm_ref, lre_ref, lim_ref, dsk_ref, wglu_ref, o_ref,
               uslab_ref, uil_ref, oslab_ref, hre_ref, him_ref, *st_refs):
    nstate = S5_GROUPS * S5_STATE
    nb = hre_ref.shape[0]
    nslab = S5_WIDTH // LANES

    @pl.when(pl.program_id(0) == 0)
    def _():
        hre_ref[...] = jnp.zeros(hre_ref.shape, F32)
        him_ref[...] = jnp.zeros(him_ref.shape, F32)

    for b in range(nb):
        for k in range(nslab):
            uslab_ref[k, b * S5_PITCH:b * S5_PITCH + T_S5, :] = u_ref[b, :, k * LANES:(k + 1) * LANES]

    tsub = T_S5 // len(st_refs)
    rsub = tsub * nb

    def gather(c):
        for t in range(c * tsub, (c + 1) * tsub):
            for k in range(nslab):
                uil_ref[t * nb:(t + 1) * nb, k * LANES:(k + 1) * LANES] = (
                    uslab_ref[k, pl.ds(t, nb, stride=S5_PITCH), :])

    def in_matmul(c):
        gather(c)
        ub = uil_ref[c * rsub:(c + 1) * rsub, :].astype(BF16)
        cin = S5_WIDTH // S5_IN_BLOCKS
        sin = nstate // S5_IN_BLOCKS
        for k in range(S5_IN_BLOCKS):
            uk = ub[:, k * cin:(k + 1) * cin]
            for off in (0, nstate):
                cols = slice(off + k * sin, off + (k + 1) * sin)
                st_refs[c][:, cols] = jnp.dot(uk, bm_ref[k * cin:(k + 1) * cin, cols],
                                              preferred_element_type=F32)

    def scan(c):
        st_ref = st_refs[c]
        half = nstate // 2
        for part in range(2):
            re_sl = slice(part * half, (part + 1) * half)
            im_sl = slice(nstate + part * half, nstate + (part + 1) * half)
            lre = lre_ref[:, re_sl]
            lim = lim_ref[:, re_sl]
            hr, hi = hre_ref[:, re_sl], him_ref[:, re_sl]
            for t in range(tsub):
                r = slice(t * nb, (t + 1) * nb)
                hr, hi = (lre * hr - lim * hi + st_ref[r, re_sl],
                          lre * hi + lim * hr + st_ref[r, im_sl])
                st_ref[r, re_sl] = hr
                st_ref[r, im_sl] = hi
            hre_ref[:, re_sl] = hr
            him_ref[:, re_sl] = hi

    def out_matmul(c):
        st_ref = st_refs[c]
        cout = S5_WIDTH // S5_OUT_BLOCKS
        sout = nstate // S5_OUT_BLOCKS
        ys = []
        for k in range(S5_OUT_BLOCKS):
            acc = None
            for off in (0, nstate):
                rows = slice(off + k * sout, off + (k + 1) * sout)
                part = jnp.dot(st_ref[:, rows].astype(BF16), cm_ref[rows, k * cout:(k + 1) * cout],
                               preferred_element_type=F32)
                acc = part if acc is None else acc + part
            ys.append(acc)
        y = jnp.concatenate(ys, axis=1)
        y = jax.nn.gelu(y + dsk_ref[...] * uil_ref[c * rsub:(c + 1) * rsub, :])
        z = jnp.dot(y.astype(BF16), wglu_ref[...], preferred_element_type=F32)
        o = y * jax.nn.sigmoid(z)
        for k in range(nslab):
            oslab_ref[k, c * rsub:(c + 1) * rsub, :] = o[:, k * LANES:(k + 1) * LANES]
        for b in range(nb):
            for k in range(nslab):
                o_ref[b, c * tsub:(c + 1) * tsub, k * LANES:(k + 1) * LANES] = (
                    oslab_ref[k, pl.ds(c * rsub + b, tsub, stride=nb), :].astype(BF16))

    nsub = len(st_refs)
    for c in range(nsub + 2):
        if c < nsub:
            in_matmul(c)
        if 1 <= c <= nsub:
            scan(c - 1)
        if c >= 2:
            out_matmul(c - 2)


def _s5(u, bm, cm, lre, lim, dsk, wglu):
    B, S, W = u.shape
    tb = T_S5 * B
    nstate = S5_GROUPS * S5_STATE
    tok = pl.BlockSpec((B, T_S5, W), lambda t: (0, t, 0))
    in_specs = [tok, _const_spec(bm.shape), _const_spec(cm.shape), _const_spec(lre.shape),
                _const_spec(lim.shape), _const_spec(dsk.shape), _const_spec(wglu.shape)]
    scratch = [pltpu.VMEM((W // LANES, B * S5_PITCH, LANES), F32),
               pltpu.VMEM((tb, W), F32),
               pltpu.VMEM((W // LANES, tb, LANES), F32),
               pltpu.VMEM((B, nstate), F32), pltpu.VMEM((B, nstate), F32)]
    scratch += [pltpu.VMEM((tb // S5_SUB, 2 * nstate), F32) for _ in range(S5_SUB)]
    return pl.pallas_call(
        _s5_kernel, grid=(S // T_S5,), in_specs=in_specs, out_specs=tok,
        out_shape=jax.ShapeDtypeStruct((B, S, W), BF16), scratch_shapes=scratch,
        compiler_params=_params(("arbitrary",)), name="s5",
    )(u, bm, cm, lre, lim, dsk, wglu)


def _merge_kernel(x_ref, odsa_ref, os5_ref, mem_ref, gmix_ref, wg_ref, wqx_ref, gqx_ref,
                  wb1_ref, wb2_ref, wb3_ref, wout_ref, gmem_ref, wkv_ref, gk_ref, y_ref,
                  k_ref, v_ref):
    @pl.when(pl.program_id(1) == 0)
    def _():
        mb = _rms(mem_ref[...], gmem_ref[...]).astype(BF16)
        kv = jnp.dot(mb, wkv_ref[...], preferred_element_type=F32)
        for h in range(X_HEADS):
            sl = slice(h * X_HEAD_DIM, (h + 1) * X_HEAD_DIM)
            k_ref[:, sl] = _rms(kv[:, sl], gk_ref[...]).astype(BF16)
        v_ref[...] = kv[:, X_WIDTH:].astype(BF16)

    x = x_ref[...]
    hb = _rms(x, gmix_ref[...]).astype(BF16)

    qx = jnp.dot(hb, wqx_ref[...], preferred_element_type=F32)
    gqx = gqx_ref[...] * (X_HEAD_DIM ** -0.5)
    ox = []
    for h in range(X_HEADS):
        sl = slice(h * X_HEAD_DIM, (h + 1) * X_HEAD_DIM)
        qh = _rms(qx[:, sl], gqx).astype(BF16)
        lg = lax.dot_general(qh, k_ref[:, sl], NT_DIMS, preferred_element_type=F32)
        p = jnp.exp(lg - jnp.max(lg, axis=-1, keepdims=True))
        pv = jnp.dot(p.astype(BF16), v_ref[:, sl], preferred_element_type=F32)
        ox.append((pv / jnp.sum(p, axis=-1, keepdims=True)).astype(BF16))
    ox = jnp.concatenate(ox, axis=1)

    merged = None
    for br, (o, wb) in enumerate(((odsa_ref[...], wb1_ref), (os5_ref[...], wb2_ref), (ox, wb3_ref))):
        gate = jax.nn.sigmoid(jnp.dot(hb, wg_ref[:, br * D_MODEL:(br + 1) * D_MODEL],
                                      preferred_element_type=F32))
        term = gate * jnp.dot(o, wb[...], preferred_element_type=F32)
        merged = term if merged is None else merged + term
    y_ref[...] = x + jnp.dot(merged.astype(BF16), wout_ref[...], preferred_element_type=F32)


def _merge(x, odsa, os5, mem, gmix, wg, wqx, gqx, wb1, wb2, wb3, wout, gmem, wkv, gk):
    B, S, D = x.shape
    ts = TS_MERGE
    tok = lambda w: pl.BlockSpec((None, ts, w), lambda b, s: (b, s, 0))
    consts = (gmix, wg, wqx, gqx, wb1, wb2, wb3, wout, gmem, wkv, gk)
    in_specs = [tok(D), tok(DSA_WIDTH), tok(S5_WIDTH),
                pl.BlockSpec((None, N_MEM, D), lambda b, s: (b, 0, 0))
                ] + [_const_spec(a.shape) for a in consts]
    scratch = [pltpu.VMEM((N_MEM, X_WIDTH), BF16), pltpu.VMEM((N_MEM, X_WIDTH), BF16)]
    return pl.pallas_call(
        _merge_kernel, grid=(B, S // ts), in_specs=in_specs, out_specs=tok(D),
        out_shape=jax.ShapeDtypeStruct((B, S, D), F32), scratch_shapes=scratch,
        compiler_params=_params(("parallel", "arbitrary")), name="merge",
    )(x, odsa, os5, mem, *consts)


def _ffn_kernel(x_ref, g_ref, wg_ref, wu_ref, wd_ref, y_ref):
    x = x_ref[...]
    hb = _rms(x, g_ref[...]).astype(BF16)
    a = jnp.dot(hb, wg_ref[...], preferred_element_type=F32)
    b = jnp.dot(hb, wu_ref[...], preferred_element_type=F32)
    act = (jax.nn.silu(a) * b).astype(BF16)
    y_ref[...] = x + jnp.dot(act, wd_ref[...], preferred_element_type=F32)


def _ffn(x2, g, wg, wu, wd):
    n, D = x2.shape
    tok = pl.BlockSpec((TS_FFN, D), lambda t: (t, 0))
    return pl.pallas_call(
        _ffn_kernel, grid=(n // TS_FFN,),
        in_specs=[tok] + [_const_spec(a.shape) for a in (g, wg, wu, wd)],
        out_specs=tok, out_shape=jax.ShapeDtypeStruct((n, D), F32),
        compiler_params=_params(("parallel",)), name="ffn",
    )(x2, g, wg, wu, wd)


def _t5_bucket(n):
    max_exact = REL_BUCKETS // 2
    nf = jnp.maximum(n, 1).astype(F32)
    large = max_exact + (jnp.log(nf / max_exact) / math.log(REL_MAX_DIST / max_exact)
                         * (REL_BUCKETS - max_exact)).astype(jnp.int32)
    large = jnp.minimum(large, REL_BUCKETS - 1)
    return jnp.where(n < max_exact, n, large)


def _toeplitz(w, rows, cols):
    H, L = w.shape
    flat = jnp.tile(w, (1, rows))[:, :rows * (L - 1)]
    return flat.reshape(H, rows, L - 1)[:, :, :cols]


def _near_bias(rel_bias):
    n = jnp.arange(2 * NB, dtype=jnp.int32)
    f = (rel_bias[_t5_bucket(n)] - rel_bias[REL_BUCKETS - 1][None, :]).T * LOG2E
    w0 = jnp.concatenate([f[:, :NB], jnp.broadcast_to(f[:, :1], (f.shape[0], NB - 1))], axis=1)
    w1 = jnp.concatenate([f[:, NB:2 * NB], f[:, 1:NB]], axis=1)
    return jnp.stack([_toeplitz(w0, NB, NB), _toeplitz(w1, NB, NB)], axis=0).astype(F32)


def _s5_mats(a_re, a_im, log_dt, b_re, b_im, c_re, c_im):
    lam = lax.complex(a_re.astype(F32), a_im.astype(F32))
    dt = jnp.exp(log_dt.astype(F32))[:, None]
    lam_bar = jnp.exp(lam * dt)
    b_bar = ((lam_bar - 1.0) / lam)[..., None] * lax.complex(b_re.astype(F32), b_im.astype(F32))
    nstate = S5_GROUPS * S5_STATE
    in_mask = (jnp.arange(S5_WIDTH)[:, None] // S5_GROUP) == (jnp.arange(nstate)[None, :] // S5_STATE)

    def blockdiag_in(w):
        t = jnp.transpose(w, (0, 2, 1)).reshape(S5_WIDTH, S5_STATE)
        return jnp.where(in_mask, jnp.tile(t, (1, S5_GROUPS)), 0.0)

    def blockdiag_out(w):
        t = jnp.transpose(w, (0, 2, 1)).reshape(nstate, S5_GROUP)
        return jnp.where(in_mask.T, jnp.tile(t, (1, S5_GROUPS)), 0.0)

    bm = jnp.concatenate([blockdiag_in(jnp.real(b_bar)), blockdiag_in(jnp.imag(b_bar))], axis=1)
    cm = jnp.concatenate([blockdiag_out(c_re.astype(F32)), blockdiag_out(-c_im.astype(F32))], axis=0)
    return bm.astype(BF16), cm.astype(BF16), jnp.real(lam_bar).reshape(1, -1), jnp.imag(lam_bar).reshape(1, -1)


def kernel(x, mem, rel_bias, w_in, g_mix_norm, g_q_dsa, g_kv_dsa, w_uv_dsa, a_re, a_im, log_dt, b_re, b_im, c_re, c_im, d_skip, w_glu, g_mem_norm, w_mem_kv, g_q_cross, g_k_cross, w_br_dsa, w_br_s5, w_br_cross, w_out, g_ffn_norm, w_ffn_gate, w_ffn_up, w_ffn_down):
    B, S, D = x.shape
    depth = w_in.shape[0]
    assert D == D_MODEL and w_in.shape[2] == sum(IN_SPLITS) and mem.shape[1] == N_MEM
    assert TQ == TK == 2 * NB and min(TOPK_MAX, S // 4) == TOPK_MAX and TQ <= TOPK_MAX
    assert S % TS_IN == 0 and S % T_S5 == 0 and S % TS_MERGE == 0 and (B * S) % TS_FFN == 0
    assert T_S5 % S5_SUB == 0 and S5_PITCH % (2 * SUBLANES) == SUBLANES
    offs = [0] + [int(o) for o in np.cumsum(IN_SPLITS)]
    nb = _near_bias(rel_bias)
    row = lambda v: v.reshape(1, -1).astype(F32)
    col = lambda v: v.reshape(-1, 1).astype(F32)
    bf = lambda a: a.astype(BF16)
    for l in range(depth):
        wq, wc, wqi, wk, ww, wu, wqx, wg = [bf(w_in[l, :, offs[k]:offs[k + 1]])
                                            for k in range(len(IN_SPLITS))]
        wckw = jnp.pad(jnp.concatenate([wc, wk, ww], axis=1),
                       ((0, 0), (0, 2 * LANES - DSA_LATENT - IDX_DIM - IDX_HEADS)))
        gqc = col(g_q_dsa[l]) * (DSA_LATENT ** -0.5 * LOG2E)

        qT, cT, c, qiT, wT, kidx, u = _inproj(
            x, row(g_mix_norm[l]), wq, wckw, wqi, wu, gqc, row(g_kv_dsa[l]))

        wuvT = bf(jnp.transpose(w_uv_dsa[l], (0, 2, 1)))
        o_dsa = _dsa(qT, qiT, wT, kidx, c, cT, nb, wuvT)

        bm, cm, lre, lim = _s5_mats(a_re[l], a_im[l], log_dt[l], b_re[l], b_im[l], c_re[l], c_im[l])
        lre = jnp.broadcast_to(lre, (B, lre.shape[1]))
        lim = jnp.broadcast_to(lim, (B, lim.shape[1]))
        o_s5 = _s5(u, bm, cm, lre, lim, row(d_skip[l]), bf(w_glu[l]))

        x1 = _merge(x, o_dsa, o_s5, mem, row(g_mix_norm[l]), wg, wqx, row(g_q_cross[l]),
                    bf(w_br_dsa[l]), bf(w_br_s5[l]), bf(w_br_cross[l]), bf(w_out[l]),
                    row(g_mem_norm[l]), bf(w_mem_kv[l]), row(g_k_cross[l]))

        x = _ffn(x1.reshape(B * S, D), row(g_ffn_norm[l]), bf(w_ffn_gate[l]), bf(w_ffn_up[l]),
                 bf(w_ffn_down[l])).reshape(B, S, D)
    return x
```

```python
import math

import jax
import jax.numpy as jnp
import numpy as np
from jax import lax
from jax.experimental import pallas as pl
from jax.experimental.pallas import tpu as pltpu

F32 = jnp.float32
BF16 = jnp.bfloat16

D_MODEL = 1024
N_MEM = 256
EPS = 1e-6
DSA_HEADS = 8
DSA_LATENT = 128
DSA_VDIM = 64
IDX_HEADS = 8
IDX_DIM = 64
TOPK_MAX = 256
REL_BUCKETS = 32
REL_MAX_DIST = 128
S5_WIDTH = 512
S5_GROUP = 16
S5_GROUPS = S5_WIDTH // S5_GROUP
S5_STATE = 64
X_HEADS = 4
X_HEAD_DIM = 128
DSA_WIDTH = DSA_HEADS * DSA_VDIM
X_WIDTH = X_HEADS * X_HEAD_DIM
N_BRANCH = 3
IN_SPLITS = (DSA_HEADS * DSA_LATENT, DSA_LATENT, IDX_HEADS * IDX_DIM, IDX_DIM,
             IDX_HEADS, S5_WIDTH, X_WIDTH, N_BRANCH * D_MODEL)

LANES = 128
SUBLANES = 8
BF16_ROWS = 16
CT_ROWS = DSA_LATENT + BF16_ROWS
VMEM_LIMIT = 56 * 1024 * 1024
LOG2E = math.log2(math.e)

TS_IN = 1024
TQ = 256
TK = 256
FAR_GROUP = 2
ATT_LAG = 8
NB = REL_MAX_DIST
N_COARSE = 8
N_BISECT = 8
T_S5 = 128
S5_SUB = 4
S5_IN_BLOCKS = 4
S5_OUT_BLOCKS = 2
S5_PITCH = T_S5 + SUBLANES
TS_MERGE = 1024
TS_FFN = 512
NEG = -(2.0 ** 100)

NT_DIMS = (((1,), (1,)), ((), ()))


def _rms(x, g):
    ms = jnp.mean(x * x, axis=-1, keepdims=True)
    return x * lax.rsqrt(ms + EPS) * g


def _tree(fn, x):
    while x.shape[0] > 1:
        half = x.shape[0] // 2
        x = fn(x[:half], x[half:])
    return x[0]


def _const_spec(shape):
    nd = len(shape)
    return pl.BlockSpec(shape, lambda *_: (0,) * nd, pipeline_mode=pl.Buffered(1))


def _params(sem):
    return pltpu.CompilerParams(dimension_semantics=sem, vmem_limit_bytes=VMEM_LIMIT)


IN_OFFS = tuple(int(o) for o in np.cumsum((0,) + IN_SPLITS))
WSPLIT_ROWS = 128


def _wsplit_kernel(w_ref, wq_ref, wckw_ref, wqi_ref, wu_ref, wqx_ref, wg_ref):
    piece = lambda k: w_ref[:, IN_OFFS[k]:IN_OFFS[k + 1]].astype(BF16)
    wq_ref[...] = piece(0)
    wqi_ref[...] = piece(2)
    wu_ref[...] = piece(5)
    wqx_ref[...] = piece(6)
    wg_ref[...] = piece(7)
    wckw_ref[:, :DSA_LATENT] = piece(1)
    kw = w_ref[:, IN_OFFS[3]:IN_OFFS[3] + LANES]
    lane = lax.broadcasted_iota(jnp.int32, kw.shape, 1)
    wckw_ref[:, DSA_LATENT:] = jnp.where(lane < IDX_DIM + IDX_HEADS, kw, 0.0).astype(BF16)


def _wsplit(w_in, l):
    _, D, W = w_in.shape
    rows = lambda w: pl.BlockSpec((WSPLIT_ROWS, w), lambda r: (r, 0))
    widths = (IN_SPLITS[0], 2 * LANES, IN_SPLITS[2], IN_SPLITS[5], IN_SPLITS[6], IN_SPLITS[7])
    return pl.pallas_call(
        _wsplit_kernel, grid=(D // WSPLIT_ROWS,),
        in_specs=[pl.BlockSpec((None, WSPLIT_ROWS, W), lambda r: (l, r, 0))],
        out_specs=[rows(w) for w in widths],
        out_shape=[jax.ShapeDtypeStruct((D, w), BF16) for w in widths],
        compiler_params=_params(("parallel",)), name="wsplit",
    )(w_in)


def _inproj_kernel(x_ref, gmix_ref, wqT_ref, wckw_ref, wqiT_ref, wu_ref,
                   gqc_ref, gkv_ref,
                   qT_ref, cT_ref, c_ref, qiT_ref, wT_ref, kidx_ref, u_ref):
    ts = x_ref.shape[0]
    hb = _rms(x_ref[...], gmix_ref[...]).astype(BF16)

    qT = jnp.dot(hb, wqT_ref[...], preferred_element_type=F32).T
    q3 = qT.reshape(DSA_HEADS, DSA_LATENT, ts)
    ms = jnp.mean(q3 * q3, axis=1, keepdims=True)
    qT_ref[...] = (q3 * lax.rsqrt(ms + EPS) * gqc_ref[...][None]).reshape(qT.shape).astype(BF16)

    ckw = jnp.dot(hb, wckw_ref[...], preferred_element_type=F32)
    kw = ckw[:, DSA_LATENT:]
    kidx_ref[...] = kw[:, :IDX_DIM].astype(BF16)
    wT_ref[...] = kw.T[IDX_DIM:IDX_DIM + BF16_ROWS, :]

    cn = _rms(ckw[:, :DSA_LATENT], gkv_ref[...])
    c_ref[...] = cn.astype(BF16)
    cTn = cn.T.astype(BF16)
    for k in range(ts // TK):
        cT_ref[k, :DSA_LATENT, :] = cTn[:, k * TK:(k + 1) * TK]
        cT_ref[k, DSA_LATENT:, :] = jnp.ones((BF16_ROWS, TK), BF16)

    qiT_ref[...] = jnp.dot(hb, wqiT_ref[...], preferred_element_type=F32).T.astype(BF16)

    u_ref[...] = jnp.dot(hb, wu_ref[...], preferred_element_type=F32)


def _inproj(x, gmix, wqT, wckw, wqiT, wu, gqc, gkv):
    B, S, D = x.shape
    ts = TS_IN
    grid = (B, S // ts)
    tok = lambda w: pl.BlockSpec((None, ts, w), lambda b, s: (b, s, 0))
    tokT = lambda r: pl.BlockSpec((None, r, ts), lambda b, s: (b, 0, s))
    hq = DSA_HEADS * DSA_LATENT
    hi = IDX_HEADS * IDX_DIM
    out_shape = (
        jax.ShapeDtypeStruct((B, hq, S), BF16),
        jax.ShapeDtypeStruct((B, S // TK, CT_ROWS, TK), BF16),
        jax.ShapeDtypeStruct((B, S, DSA_LATENT), BF16),
        jax.ShapeDtypeStruct((B, hi, S), BF16),
        jax.ShapeDtypeStruct((B, BF16_ROWS, S), F32),
        jax.ShapeDtypeStruct((B, S, IDX_DIM), BF16),
        jax.ShapeDtypeStruct((B, S, S5_WIDTH), F32),
    )
    out_specs = (tokT(hq),
                 pl.BlockSpec((None, ts // TK, CT_ROWS, TK), lambda b, s: (b, s, 0, 0)),
                 tok(DSA_LATENT), tokT(hi), tokT(BF16_ROWS), tok(IDX_DIM),
                 tok(S5_WIDTH))
    consts = (gmix, wqT, wckw, wqiT, wu, gqc, gkv)
    in_specs = [tok(D)] + [_const_spec(a.shape) for a in consts]
    return pl.pallas_call(
        _inproj_kernel, grid=grid, in_specs=in_specs, out_specs=out_specs, out_shape=out_shape,
        compiler_params=_params(("parallel", "parallel")), name="inproj",
    )(x, *consts)


def _dsa_kernel(qT_ref, qiT_ref, wT_ref, kidx_ref, c_ref, cT_ref, nb_ref, wuvT_ref, o_ref,
                sc_ref, scb_ref, mm_ref, pref_ref, lg_ref, m_ref, acc_ref):
    i = pl.program_id(1)
    nk = sc_ref.shape[0]
    kf = float(TOPK_MAX)
    G = TK // SUBLANES
    RB = TK // BF16_ROWS

    def rep(fn, a):
        return jnp.broadcast_to(fn(a, axis=0, keepdims=True), (SUBLANES, TQ))

    def full(v, dt=F32):
        return jnp.full((SUBLANES, TQ), v, dt)

    def key_rows(j):
        return pl.ds(pl.multiple_of(j * TK, TK), TK)

    def score(j):
        ks = kidx_ref[key_rows(j), :]
        acc = None
        for h in range(IDX_HEADS):
            d = jnp.dot(ks, qiT_ref[h * IDX_DIM:(h + 1) * IDX_DIM, :], preferred_element_type=F32)
            t = jnp.maximum(d, 0.0) * wT_ref[h:h + 1, :]
            acc = t if acc is None else acc + t
        return acc

    mm_ref[0] = full(jnp.inf)
    mm_ref[1] = full(-jnp.inf)

    def put_scores(j, s, s_for_min):
        sc_ref[j] = s
        scb_ref[j] = s.astype(BF16)
        mm_ref[0] = jnp.minimum(mm_ref[0], _tree(jnp.minimum, s_for_min.reshape(G, SUBLANES, TQ)))
        mm_ref[1] = jnp.maximum(mm_ref[1], _tree(jnp.maximum, s.reshape(G, SUBLANES, TQ)))

    def score_pair(p, carry):
        for j in (2 * p, 2 * p + 1):
            s = score(j)
            put_scores(j, s, s)
        return carry

    lax.fori_loop(0, lax.shift_right_logical(i, 1), score_pair, 0)

    @pl.when((i & 1) == 1)
    def _():
        s = score(i - 1)
        put_scores(i - 1, s, s)

    key_t = lax.broadcasted_iota(jnp.int32, (TK, TQ), 0)
    qry_t = lax.broadcasted_iota(jnp.int32, (TK, TQ), 1)
    causal = key_t <= qry_t
    s_diag = score(i)
    put_scores(i, jnp.where(causal, s_diag, -jnp.inf), jnp.where(causal, s_diag, jnp.inf))

    @pl.when(i == 0)
    def _():
        sc_ref[0] = jnp.where(causal, 0.0, NEG)

    @pl.when(i > 0)
    def _():
        nt = i + 1

        def tile3(j):
            return sc_ref[j].reshape(G, SUBLANES, TQ)

        lo, hi = rep(jnp.min, mm_ref[0]), rep(jnp.max, mm_ref[1])

        def count_ge(thr):
            def body(j, acc):
                return acc + _tree(jnp.add, jnp.where(tile3(j) >= thr[None], 1.0, 0.0))
            return rep(jnp.sum, lax.fori_loop(0, nt, body, full(0.0)))

        bf_step = 2.0 ** -7
        tiny = 1e-30

        def as_bf16_value(x):
            return x.astype(BF16).astype(F32)

        def count_ge_coarse(thr):
            thr16 = jnp.concatenate([thr, thr], axis=0).astype(BF16)
            one, zero = jnp.ones((), BF16), jnp.zeros((), BF16)

            def body(j, acc):
                t = scb_ref[j].reshape(RB, BF16_ROWS, TQ)
                return acc + _tree(jnp.add, jnp.where(t >= thr16[None], one, zero))
            acc = lax.fori_loop(0, nt, body, jnp.zeros((BF16_ROWS, TQ), BF16))
            return rep(jnp.sum, acc.astype(F32))

        def coarse(_, carry):
            lo_c, hi_c = carry
            mid = as_bf16_value(0.5 * (lo_c + hi_c))
            ge = count_ge_coarse(mid) >= kf
            return jnp.where(ge, mid, lo_c), jnp.where(ge, hi_c, mid)

        lo_c = as_bf16_value(lo - jnp.abs(lo) * bf_step - tiny)
        hi_c = as_bf16_value(hi + jnp.abs(hi) * bf_step + tiny)
        lo_c, hi_c = lax.fori_loop(0, N_COARSE, coarse, (lo_c, hi_c))
        lo = lo_c - jnp.abs(lo_c) * bf_step - tiny
        hi = hi_c

        def bisect(_, carry):
            lo, hi, clo = carry
            mid = 0.5 * (lo + hi)
            cnt = count_ge(mid)
            ge = cnt >= kf
            return jnp.where(ge, mid, lo), jnp.where(ge, hi, mid), jnp.where(ge, cnt, clo)

        lo, hi, clo = lax.fori_loop(0, N_BISECT, bisect, (lo, hi, count_ge(lo)))

        def snap_body(j, am):
            s = tile3(j)
            return jnp.minimum(am, _tree(jnp.minimum, jnp.where(s >= lo[None], s, jnp.inf)))

        cur = rep(jnp.min, lax.fori_loop(0, nt, snap_body, full(jnp.inf)))

        def walk(cur):
            def body(j, carry):
                ac, am = carry
                s = tile3(j)
                g = s > cur[None]
                ac = ac + _tree(jnp.add, jnp.where(g, 1.0, 0.0))
                am = jnp.minimum(am, _tree(jnp.minimum, jnp.where(g, s, jnp.inf)))
                return ac, am
            ac, am = lax.fori_loop(0, nt, body, (full(0.0), full(jnp.inf)))
            return rep(jnp.sum, ac), rep(jnp.min, am)

        def walk_cond(carry):
            _, _, _, go, it = carry
            return jnp.logical_and(go > 0, it < nk * TK + 2)

        def walk_body(carry):
            cur, cge, _, _, it = carry
            c, nxt = walk(cur)
            move = c >= kf
            go = (jnp.max(jnp.where(move, 1.0, 0.0)) > 0.5).astype(jnp.int32)
            return jnp.where(move, nxt, cur), jnp.where(move, c, cge), c, go, it + 1

        kth, cge, cgt, _, _ = lax.while_loop(
            walk_cond, walk_body, (cur, clo, full(0.0), jnp.int32(1), jnp.int32(0)))
        need = kf - cgt
        has_excess = jnp.max(jnp.where(cge > kf, 1.0, 0.0)) > 0.5

        @pl.when(jnp.logical_not(has_excess))
        def _():
            def body(j, carry):
                sc_ref[j] = jnp.where(tile3(j) >= kth[None], 0.0, NEG).reshape(TK, TQ)
                return carry
            lax.fori_loop(0, nt, body, 0)

        @pl.when(has_excess)
        def _():
            tril = jnp.where(key_t >= qry_t, 1.0, 0.0).astype(BF16)

            def pref_body(p, carry):
                for j in (2 * p, jnp.minimum(2 * p + 1, nt - 1)):
                    e01 = jnp.where(tile3(j) == kth[None], 1.0, 0.0).reshape(TK, TQ).astype(BF16)
                    pref_ref[j] = jnp.dot(tril, e01, preferred_element_type=F32)
                return carry

            lax.fori_loop(0, lax.shift_right_logical(nt + 1, 1), pref_body, 0)

            def mask_body(j, offset):
                s = tile3(j)
                pref = pref_ref[j]
                rank = pref.reshape(G, SUBLANES, TQ) + offset[None]
                tie = jnp.where(rank <= need[None], 0.0, NEG)
                mb = jnp.where(s > kth[None], 0.0, jnp.where(s == kth[None], tie, NEG))
                sc_ref[j] = mb.reshape(TK, TQ)
                return offset + jnp.broadcast_to(pref[TK - 1:TK, :], (SUBLANES, TQ))

            lax.fori_loop(0, nt, mask_body, full(0.0))

    m_ref[...] = jnp.full(m_ref.shape, NEG, F32)
    acc_ref[...] = jnp.zeros(acc_ref.shape, F32)
    LG = CT_ROWS // SUBLANES

    def near_bias(kind, h):
        z = jnp.zeros((NB, NB), F32)
        b0, b1 = nb_ref[0, h], nb_ref[1, h]
        blocks = [[b0, b1], [z, b0]] if kind == 0 else [[z, z], [b1, z]]
        return jnp.concatenate([jnp.concatenate(r, axis=1) for r in blocks], axis=0)

    def attend(tiles):
        m_run = [m_ref[h] for h in range(DSA_HEADS)]
        stats = {}

        def phase_a(slot, h, ct, mbb, near):
            lg = jnp.dot(ct, qT_ref[h * DSA_LATENT:(h + 1) * DSA_LATENT, :],
                         preferred_element_type=F32)
            if near is not None:
                lg = lg + near_bias(near, h)
            lgb = lg.astype(BF16) + mbb
            lg_ref[slot, h] = lgb
            tmax = _tree(jnp.maximum, lgb.reshape(RB, BF16_ROWS, TQ)).astype(F32)
            m_new = jnp.maximum(m_run[h], rep(jnp.max, tmax))
            stats[slot, h] = (m_run[h], m_new)
            m_run[h] = m_new

        def phase_b(slot, h, ctT):
            m_old, m_new = stats[slot, h]
            alpha = jnp.exp2(m_old - m_new)
            m16 = jnp.concatenate([m_new, m_new], axis=0).astype(BF16)
            x = lg_ref[slot, h].reshape(RB, BF16_ROWS, TQ) - m16[None]
            pv = jnp.dot(ctT, jnp.exp2(x).reshape(TK, TQ), preferred_element_type=F32)
            acc3 = acc_ref[h].reshape(LG, SUBLANES, TQ) * alpha[None]
            acc_ref[h] = acc3.reshape(CT_ROWS, TQ) + pv

        ops = [(c_ref[key_rows(j), :], sc_ref[j].astype(BF16), near, cT_ref[j]) for j, near in tiles]
        units = [(slot, h) for slot in range(len(tiles)) for h in range(DSA_HEADS)]
        for k in range(len(units) + ATT_LAG):
            if k < len(units):
                slot, h = units[k]
                phase_a(slot, h, ops[slot][0], ops[slot][1], ops[slot][2])
            if k >= ATT_LAG:
                slot, h = units[k - ATT_LAG]
                phase_b(slot, h, ops[slot][3])
        for h in range(DSA_HEADS):
            m_ref[h] = m_run[h]

    nfar = jnp.maximum(i - 1, 0)

    def far_group(p, carry):
        attend([(FAR_GROUP * p + t, None) for t in range(FAR_GROUP)])
        return carry

    nfull = lax.shift_right_logical(nfar, FAR_GROUP.bit_length() - 1)
    lax.fori_loop(0, nfull, far_group, 0)
    done = nfull * FAR_GROUP
    size = FAR_GROUP // 2
    while size >= 1:
        @pl.when((nfar & size) != 0)
        def _(done=done, size=size):
            attend([(done + t, None) for t in range(size)])
        done = done + (nfar & size)
        size //= 2

    @pl.when(i > 0)
    def _():
        attend([(i - 1, 1), (i, 0)])

    @pl.when(i == 0)
    def _():
        attend([(0, 0)])

    outs = []
    for h in range(DSA_HEADS):
        rl = 1.0 / acc_ref[h, DSA_LATENT:DSA_LATENT + SUBLANES, :]
        o3 = acc_ref[h, :DSA_LATENT, :].reshape(DSA_LATENT // SUBLANES, SUBLANES, TQ) * rl[None]
        o = o3.reshape(DSA_LATENT, TQ).astype(BF16)
        outs.append(jnp.dot(wuvT_ref[h], o, preferred_element_type=F32))
    o_ref[...] = jnp.concatenate(outs, axis=0).T.astype(BF16)


def _dsa(qT, qiT, wT, kidx, c, cT, nb, wuvT):
    B, S, _ = c.shape
    nk = S // TK
    grid = (B, S // TQ)
    tileT = lambda r: pl.BlockSpec((None, r, TQ), lambda b, i: (b, 0, i))
    seq = lambda w: pl.BlockSpec((None, S, w), lambda b, i: (b, 0, 0))
    in_specs = [tileT(DSA_HEADS * DSA_LATENT), tileT(IDX_HEADS * IDX_DIM), tileT(BF16_ROWS),
                seq(IDX_DIM), seq(DSA_LATENT),
                pl.BlockSpec((None, nk, CT_ROWS, TK), lambda b, i: (b, 0, 0, 0)),
                _const_spec(nb.shape), _const_spec(wuvT.shape)]
    scratch = [
        pltpu.VMEM((nk, TK, TQ), F32),
        pltpu.VMEM((nk, TK, TQ), BF16),
        pltpu.VMEM((2, SUBLANES, TQ), F32),
        pltpu.VMEM((nk, TK, TQ), F32),
        pltpu.VMEM((FAR_GROUP, DSA_HEADS, TK, TQ), BF16),
        pltpu.VMEM((DSA_HEADS, SUBLANES, TQ), F32),
        pltpu.VMEM((DSA_HEADS, CT_ROWS, TQ), F32),
    ]
    return pl.pallas_call(
        _dsa_kernel, grid=grid, in_specs=in_specs,
        out_specs=pl.BlockSpec((None, TQ, DSA_WIDTH), lambda b, i: (b, i, 0)),
        out_shape=jax.ShapeDtypeStruct((B, S, DSA_WIDTH), BF16), scratch_shapes=scratch,
        compiler_params=_params(("parallel", "arbitrary")), name="dsa",
    )(qT, qiT, wT, kidx, c, cT, nb, wuvT)


def _s5_kernel(u_ref, bm_ref, cm_ref, lre_ref, lim_ref, dsk_ref, wglu_ref, o_ref,
               uslab_ref, uil_ref, oslab_ref, hre_ref, him_ref, *st_refs):
    nstate = S5_GROUPS * S5_STATE
    nb = hre_ref.shape[0]
    nslab = S5_WIDTH // LANES

    @pl.when(pl.program_id(0) == 0)
    def _():
        hre_ref[...] = jnp.zeros(hre_ref.shape, F32)
        him_ref[...] = jnp.zeros(him_ref.shape, F32)

    for b in range(nb):
        for k in range(nslab):
            uslab_ref[k, b * S5_PITCH:b * S5_PITCH + T_S5, :] = u_ref[b, :, k * LANES:(k + 1) * LANES]

    tsub = T_S5 // len(st_refs)
    rsub = tsub * nb

    def gather(c):
        for t in range(c * tsub, (c + 1) * tsub):
            for k in range(nslab):
                uil_ref[t * nb:(t + 1) * nb, k * LANES:(k + 1) * LANES] = (
                    uslab_ref[k, pl.ds(t, nb, stride=S5_PITCH), :])

    def in_matmul(c):
        gather(c)
        ub = uil_ref[c * rsub:(c + 1) * rsub, :].astype(BF16)
        cin = S5_WIDTH // S5_IN_BLOCKS
        sin = nstate // S5_IN_BLOCKS
        for k in range(S5_IN_BLOCKS):
            uk = ub[:, k * cin:(k + 1) * cin]
            for off in (0, nstate):
                cols = slice(off + k * sin, off + (k + 1) * sin)
                st_refs[c][:, cols] = jnp.dot(uk, bm_ref[k * cin:(k + 1) * cin, cols],
                                              preferred_element_type=F32)

    def scan(c):
        st_ref = st_refs[c]
        half = nstate // 2
        for part in range(2):
            re_sl = slice(part * half, (part + 1) * half)
            im_sl = slice(nstate + part * half, nstate + (part + 1) * half)
            lre = lre_ref[:, re_sl]
            lim = lim_ref[:, re_sl]
            hr, hi = hre_ref[:, re_sl], him_ref[:, re_sl]
            for t in range(tsub):
                r = slice(t * nb, (t + 1) * nb)
                hr, hi = (lre * hr - lim * hi + st_ref[r, re_sl],
                          lre * hi + lim * hr + st_ref[r, im_sl])
                st_ref[r, re_sl] = hr
                st_ref[r, im_sl] = hi
            hre_ref[:, re_sl] = hr
            him_ref[:, re_sl] = hi

    def out_matmul(c):
        st_ref = st_refs[c]
        cout = S5_WIDTH // S5_OUT_BLOCKS
        sout = nstate // S5_OUT_BLOCKS
        ys = []
        for k in range(S5_OUT_BLOCKS):
            acc = None
            for off in (0, nstate):
                rows = slice(off + k * sout, off + (k + 1) * sout)
                part = jnp.dot(st_ref[:, rows].astype(BF16), cm_ref[rows, k * cout:(k + 1) * cout],
                               preferred_element_type=F32)
                acc = part if acc is None else acc + part
            ys.append(acc)
        y = jnp.concatenate(ys, axis=1)
        y = jax.nn.gelu(y + dsk_ref[...] * uil_ref[c * rsub:(c + 1) * rsub, :])
        z = jnp.dot(y.astype(BF16), wglu_ref[...], preferred_element_type=F32)
        o = y * jax.nn.sigmoid(z)
        for k in range(nslab):
            oslab_ref[k, c * rsub:(c + 1) * rsub, :] = o[:, k * LANES:(k + 1) * LANES]
        for b in range(nb):
            for k in range(nslab):
                o_ref[b, c * tsub:(c + 1) * tsub, k * LANES:(k + 1) * LANES] = (
                    oslab_ref[k, pl.ds(c * rsub + b, tsub, stride=nb), :].astype(BF16))

    nsub = len(st_refs)
    for c in range(nsub + 2):
        if c < nsub:
            in_matmul(c)
        if 1 <= c <= nsub:
            scan(c - 1)
        if c >= 2:
            out_matmul(c - 2)


def _s5(u, bm, cm, lre, lim, dsk, wglu):
    B, S, W = u.shape
    tb = T_S5 * B
    nstate = S5_GROUPS * S5_STATE
    tok = pl.BlockSpec((B, T_S5, W), lambda t: (0, t, 0))
    in_specs = [tok, _const_spec(bm.shape), _const_spec(cm.shape), _const_spec(lre.shape),
                _const_spec(lim.shape), _const_spec(dsk.shape), _const_spec(wglu.shape)]
    scratch = [pltpu.VMEM((W // LANES, B * S5_PITCH, LANES), F32),
               pltpu.VMEM((tb, W), F32),
               pltpu.VMEM((W // LANES, tb, LANES), F32),
               pltpu.VMEM((B, nstate), F32), pltpu.VMEM((B, nstate), F32)]
    scratch += [pltpu.VMEM((tb // S5_SUB, 2 * nstate), F32) for _ in range(S5_SUB)]
    return pl.pallas_call(
        _s5_kernel, grid=(S // T_S5,), in_specs=in_specs, out_specs=tok,
        out_shape=jax.ShapeDtypeStruct((B, S, W), BF16), scratch_shapes=scratch,
        compiler_params=_params(("arbitrary",)), name="s5",
    )(u, bm, cm, lre, lim, dsk, wglu)


def _merge_kernel(x_ref, odsa_ref, os5_ref, mem_ref, gmix_ref, wg_ref, wqx_ref, gqx_ref,
                  wb1_ref, wb2_ref, wb3_ref, wout_ref, gmem_ref, wkv_ref, gk_ref, y_ref,
                  k_ref, v_ref):
    @pl.when(pl.program_id(1) == 0)
    def _():
        mb = _rms(mem_ref[...], gmem_ref[...]).astype(BF16)
        kv = jnp.dot(mb, wkv_ref[...], preferred_element_type=F32)
        for h in range(X_HEADS):
            sl = slice(h * X_HEAD_DIM, (h + 1) * X_HEAD_DIM)
            k_ref[:, sl] = _rms(kv[:, sl], gk_ref[...]).astype(BF16)
        v_ref[...] = kv[:, X_WIDTH:].astype(BF16)

    x = x_ref[...]
    hb = _rms(x, gmix_ref[...]).astype(BF16)

    qx = jnp.dot(hb, wqx_ref[...], preferred_element_type=F32)
    gqx = gqx_ref[...] * (X_HEAD_DIM ** -0.5)
    ox = []
    for h in range(X_HEADS):
        sl = slice(h * X_HEAD_DIM, (h + 1) * X_HEAD_DIM)
        qh = _rms(qx[:, sl], gqx).astype(BF16)
        lg = lax.dot_general(qh, k_ref[:, sl], NT_DIMS, preferred_element_type=F32)
        p = jnp.exp(lg - jnp.max(lg, axis=-1, keepdims=True))
        pv = jnp.dot(p.astype(BF16), v_ref[:, sl], preferred_element_type=F32)
        ox.append((pv / jnp.sum(p, axis=-1, keepdims=True)).astype(BF16))
    ox = jnp.concatenate(ox, axis=1)

    merged = None
    for br, (o, wb) in enumerate(((odsa_ref[...], wb1_ref), (os5_ref[...], wb2_ref), (ox, wb3_ref))):
        gate = jax.nn.sigmoid(jnp.dot(hb, wg_ref[:, br * D_MODEL:(br + 1) * D_MODEL],
                                      preferred_element_type=F32))
        term = gate * jnp.dot(o, wb[...], preferred_element_type=F32)
        merged = term if merged is None else merged + term
    y_ref[...] = x + jnp.dot(merged.astype(BF16), wout_ref[...], preferred_element_type=F32)


def _merge(x, odsa, os5, mem, gmix, wg, wqx, gqx, wb1, wb2, wb3, wout, gmem, wkv, gk):
    B, S, D = x.shape
    ts = TS_MERGE
    tok = lambda w: pl.BlockSpec((None, ts, w), lambda b, s: (b, s, 0))
    consts = (gmix, wg, wqx, gqx, wb1, wb2, wb3, wout, gmem, wkv, gk)
    in_specs = [tok(D), tok(DSA_WIDTH), tok(S5_WIDTH),
                pl.BlockSpec((None, N_MEM, D), lambda b, s: (b, 0, 0))
                ] + [_const_spec(a.shape) for a in consts]
    scratch = [pltpu.VMEM((N_MEM, X_WIDTH), BF16), pltpu.VMEM((N_MEM, X_WIDTH), BF16)]
    return pl.pallas_call(
        _merge_kernel, grid=(B, S // ts), in_specs=in_specs, out_specs=tok(D),
        out_shape=jax.ShapeDtypeStruct((B, S, D), F32), scratch_shapes=scratch,
        compiler_params=_params(("parallel", "arbitrary")), name="merge",
    )(x, odsa, os5, mem, *consts)


def _ffn_kernel(x_ref, g_ref, wg_ref, wu_ref, wd_ref, y_ref):
    x = x_ref[...]
    hb = _rms(x, g_ref[...]).astype(BF16)
    a = jnp.dot(hb, wg_ref[...], preferred_element_type=F32)
    b = jnp.dot(hb, wu_ref[...], preferred_element_type=F32)
    act = (jax.nn.silu(a) * b).astype(BF16)
    y_ref[...] = x + jnp.dot(act, wd_ref[...], preferred_element_type=F32)


def _ffn(x2, g, wg, wu, wd):
    n, D = x2.shape
    tok = pl.BlockSpec((TS_FFN, D), lambda t: (t, 0))
    return pl.pallas_call(
        _ffn_kernel, grid=(n // TS_FFN,),
        in_specs=[tok] + [_const_spec(a.shape) for a in (g, wg, wu, wd)],
        out_specs=tok, out_shape=jax.ShapeDtypeStruct((n, D), F32),
        compiler_params=_params(("parallel",)), name="ffn",
    )(x2, g, wg, wu, wd)


def _t5_bucket(n):
    max_exact = REL_BUCKETS // 2
    nf = jnp.maximum(n, 1).astype(F32)
    large = max_exact + (jnp.log(nf / max_exact) / math.log(REL_MAX_DIST / max_exact)
                         * (REL_BUCKETS - max_exact)).astype(jnp.int32)
    large = jnp.minimum(large, REL_BUCKETS - 1)
    return jnp.where(n < max_exact, n, large)


def _toeplitz(w, rows, cols):
    H, L = w.shape
    flat = jnp.tile(w, (1, rows))[:, :rows * (L - 1)]
    return flat.reshape(H, rows, L - 1)[:, :, :cols]


def _near_bias(rel_bias):
    n = jnp.arange(2 * NB, dtype=jnp.int32)
    f = (rel_bias[_t5_bucket(n)] - rel_bias[REL_BUCKETS - 1][None, :]).T * LOG2E
    w0 = jnp.concatenate([f[:, :NB], jnp.broadcast_to(f[:, :1], (f.shape[0], NB - 1))], axis=1)
    w1 = jnp.concatenate([f[:, NB:2 * NB], f[:, 1:NB]], axis=1)
    return jnp.stack([_toeplitz(w0, NB, NB), _toeplitz(w1, NB, NB)], axis=0).astype(F32)


def _s5_mats(a_re, a_im, log_dt, b_re, b_im, c_re, c_im):
    lam = lax.complex(a_re.astype(F32), a_im.astype(F32))
    dt = jnp.exp(log_dt.astype(F32))[:, None]
    lam_bar = jnp.exp(lam * dt)
    b_bar = ((lam_bar - 1.0) / lam)[..., None] * lax.complex(b_re.astype(F32), b_im.astype(F32))
    nstate = S5_GROUPS * S5_STATE
    in_mask = (jnp.arange(S5_WIDTH)[:, None] // S5_GROUP) == (jnp.arange(nstate)[None, :] // S5_STATE)

    def blockdiag_in(w):
        t = jnp.transpose(w, (0, 2, 1)).reshape(S5_WIDTH, S5_STATE)
        return jnp.where(in_mask, jnp.tile(t, (1, S5_GROUPS)), 0.0)

    def blockdiag_out(w):
        t = jnp.transpose(w, (0, 2, 1)).reshape(nstate, S5_GROUP)
        return jnp.where(in_mask.T, jnp.tile(t, (1, S5_GROUPS)), 0.0)

    bm = jnp.concatenate([blockdiag_in(jnp.real(b_bar)), blockdiag_in(jnp.imag(b_bar))], axis=1)
    cm = jnp.concatenate([blockdiag_out(c_re.astype(F32)), blockdiag_out(-c_im.astype(F32))], axis=0)
    return bm.astype(BF16), cm.astype(BF16), jnp.real(lam_bar).reshape(1, -1), jnp.imag(lam_bar).reshape(1, -1)


def kernel(x, mem, rel_bias, w_in, g_mix_norm, g_q_dsa, g_kv_dsa, w_uv_dsa, a_re, a_im, log_dt, b_re, b_im, c_re, c_im, d_skip, w_glu, g_mem_norm, w_mem_kv, g_q_cross, g_k_cross, w_br_dsa, w_br_s5, w_br_cross, w_out, g_ffn_norm, w_ffn_gate, w_ffn_up, w_ffn_down):
    B, S, D = x.shape
    depth = w_in.shape[0]
    assert D == D_MODEL and w_in.shape[2] == sum(IN_SPLITS) and mem.shape[1] == N_MEM
    assert TQ == TK == 2 * NB and min(TOPK_MAX, S // 4) == TOPK_MAX and TQ <= TOPK_MAX
    assert S % TS_IN == 0 and S % T_S5 == 0 and S % TS_MERGE == 0 and (B * S) % TS_FFN == 0
    assert T_S5 % S5_SUB == 0 and S5_PITCH % (2 * SUBLANES) == SUBLANES
    assert D % WSPLIT_ROWS == 0 and IN_OFFS[1] % LANES == 0 and IN_OFFS[3] % LANES == 0
    nb = _near_bias(rel_bias)
    row = lambda v: v.reshape(1, -1).astype(F32)
    col = lambda v: v.reshape(-1, 1).astype(F32)
    bf = lambda a: a.astype(BF16)
    for l in range(depth):
        wq, wckw, wqi, wu, wqx, wg = _wsplit(w_in, l)
        gqc = col(g_q_dsa[l]) * (DSA_LATENT ** -0.5 * LOG2E)

        qT, cT, c, qiT, wT, kidx, u = _inproj(
            x, row(g_mix_norm[l]), wq, wckw, wqi, wu, gqc, row(g_kv_dsa[l]))

        wuvT = bf(jnp.transpose(w_uv_dsa[l], (0, 2, 1)))
        o_dsa = _dsa(qT, qiT, wT, kidx, c, cT, nb, wuvT)

        bm, cm, lre, lim = _s5_mats(a_re[l], a_im[l], log_dt[l], b_re[l], b_im[l], c_re[l], c_im[l])
        lre = jnp.broadcast_to(lre, (B, lre.shape[1]))
        lim = jnp.broadcast_to(lim, (B, lim.shape[1]))
        o_s5 = _s5(u, bm, cm, lre, lim, row(d_skip[l]), bf(w_glu[l]))

        x1 = _merge(x, o_dsa, o_s5, mem, row(g_mix_norm[l]), wg, wqx, row(g_q_cross[l]),
                    bf(w_br_dsa[l]), bf(w_br_s5[l]), bf(w_br_cross[l]), bf(w_out[l]),
                    row(g_mem_norm[l]), bf(w_mem_kv[l]), row(g_k_cross[l]))

        x = _ffn(x1.reshape(B * S, D), row(g_ffn_norm[l]), bf(w_ffn_gate[l]), bf(w_ffn_up[l]),
                 bf(w_ffn_down[l])).reshape(B, S, D)
    return x
```

```python
import math

import jax
import jax.numpy as jnp
import numpy as np
from jax import lax
from jax.experimental import pallas as pl
from jax.experimental.pallas import tpu as pltpu

F32 = jnp.float32
BF16 = jnp.bfloat16

D_MODEL = 1024
N_MEM = 256
EPS = 1e-6
DSA_HEADS = 8
DSA_LATENT = 128
DSA_VDIM = 64
IDX_HEADS = 8
IDX_DIM = 64
TOPK_MAX = 256
REL_BUCKETS = 32
REL_MAX_DIST = 128
S5_WIDTH = 512
S5_GROUP = 16
S5_GROUPS = S5_WIDTH // S5_GROUP
S5_STATE = 64
X_HEADS = 4
X_HEAD_DIM = 128
DSA_WIDTH = DSA_HEADS * DSA_VDIM
X_WIDTH = X_HEADS * X_HEAD_DIM
N_BRANCH = 3
IN_SPLITS = (DSA_HEADS * DSA_LATENT, DSA_LATENT, IDX_HEADS * IDX_DIM, IDX_DIM,
             IDX_HEADS, S5_WIDTH, X_WIDTH, N_BRANCH * D_MODEL)

LANES = 128
SUBLANES = 8
BF16_ROWS = 16
CT_ROWS = DSA_LATENT + BF16_ROWS
VMEM_LIMIT = 56 * 1024 * 1024
LOG2E = math.log2(math.e)

TS_IN = 1024
TQ = 256
TK = 256
FAR_GROUP = 2
ATT_LAG = 8
NB = REL_MAX_DIST
N_COARSE = 8
N_BISECT = 8
T_S5 = 128
S5_SUB = 4
S5_IN_BLOCKS = 4
S5_OUT_BLOCKS = 2
S5_PITCH = T_S5 + SUBLANES
TS_MERGE = 1024
TS_FFN = 512
NEG = -(2.0 ** 100)

NT_DIMS = (((1,), (1,)), ((), ()))


def _rms(x, g):
    ms = jnp.mean(x * x, axis=-1, keepdims=True)
    return x * lax.rsqrt(ms + EPS) * g


def _tree(fn, x):
    while x.shape[0] > 1:
        half = x.shape[0] // 2
        x = fn(x[:half], x[half:])
    return x[0]


def _const_spec(shape):
    nd = len(shape)
    return pl.BlockSpec(shape, lambda *_: (0,) * nd, pipeline_mode=pl.Buffered(1))


def _params(sem):
    return pltpu.CompilerParams(dimension_semantics=sem, vmem_limit_bytes=VMEM_LIMIT)


def _inproj_kernel(x_ref, gmix_ref, wqT_ref, wckwT_ref, wqiT_ref, wuT_ref,
                   gqc_ref, gkv_ref,
                   qT_ref, cT_ref, c_ref, qiT_ref, wT_ref, kidx_ref, u_ref):
    ts = x_ref.shape[0]
    hb = _rms(x_ref[...], gmix_ref[...]).astype(BF16)
    proj = lambda wT_ref: lax.dot_general(hb, wT_ref[...], NT_DIMS, preferred_element_type=F32)

    qT = proj(wqT_ref).T
    q3 = qT.reshape(DSA_HEADS, DSA_LATENT, ts)
    ms = jnp.mean(q3 * q3, axis=1, keepdims=True)
    qT_ref[...] = (q3 * lax.rsqrt(ms + EPS) * gqc_ref[...][None]).reshape(qT.shape).astype(BF16)

    ckw = proj(wckwT_ref)
    kw = ckw[:, DSA_LATENT:]
    kidx_ref[...] = kw[:, :IDX_DIM].astype(BF16)
    wT_ref[...] = kw.T[IDX_DIM:IDX_DIM + BF16_ROWS, :]

    cn = _rms(ckw[:, :DSA_LATENT], gkv_ref[...])
    c_ref[...] = cn.astype(BF16)
    cTn = cn.T.astype(BF16)
    for k in range(ts // TK):
        cT_ref[k, :DSA_LATENT, :] = cTn[:, k * TK:(k + 1) * TK]
        cT_ref[k, DSA_LATENT:, :] = jnp.ones((BF16_ROWS, TK), BF16)

    qiT_ref[...] = proj(wqiT_ref).T.astype(BF16)

    u_ref[...] = proj(wuT_ref)


def _inproj(x, gmix, wqT, wckwT, wqiT, wuT, gqc, gkv):
    B, S, D = x.shape
    ts = TS_IN
    grid = (B, S // ts)
    tok = lambda w: pl.BlockSpec((None, ts, w), lambda b, s: (b, s, 0))
    tokT = lambda r: pl.BlockSpec((None, r, ts), lambda b, s: (b, 0, s))
    hq = DSA_HEADS * DSA_LATENT
    hi = IDX_HEADS * IDX_DIM
    out_shape = (
        jax.ShapeDtypeStruct((B, hq, S), BF16),
        jax.ShapeDtypeStruct((B, S // TK, CT_ROWS, TK), BF16),
        jax.ShapeDtypeStruct((B, S, DSA_LATENT), BF16),
        jax.ShapeDtypeStruct((B, hi, S), BF16),
        jax.ShapeDtypeStruct((B, BF16_ROWS, S), F32),
        jax.ShapeDtypeStruct((B, S, IDX_DIM), BF16),
        jax.ShapeDtypeStruct((B, S, S5_WIDTH), F32),
    )
    out_specs = (tokT(hq),
                 pl.BlockSpec((None, ts // TK, CT_ROWS, TK), lambda b, s: (b, s, 0, 0)),
                 tok(DSA_LATENT), tokT(hi), tokT(BF16_ROWS), tok(IDX_DIM),
                 tok(S5_WIDTH))
    consts = (gmix, wqT, wckwT, wqiT, wuT, gqc, gkv)
    in_specs = [tok(D)] + [_const_spec(a.shape) for a in consts]
    return pl.pallas_call(
        _inproj_kernel, grid=grid, in_specs=in_specs, out_specs=out_specs, out_shape=out_shape,
        compiler_params=_params(("parallel", "parallel")), name="inproj",
    )(x, *consts)


def _dsa_kernel(qT_ref, qiT_ref, wT_ref, kidx_ref, c_ref, cT_ref, nb_ref, wuvT_ref, o_ref,
                sc_ref, scb_ref, mm_ref, pref_ref, lg_ref, m_ref, acc_ref):
    i = pl.program_id(1)
    nk = sc_ref.shape[0]
    kf = float(TOPK_MAX)
    G = TK // SUBLANES
    RB = TK // BF16_ROWS

    def rep(fn, a):
        return jnp.broadcast_to(fn(a, axis=0, keepdims=True), (SUBLANES, TQ))

    def full(v, dt=F32):
        return jnp.full((SUBLANES, TQ), v, dt)

    def key_rows(j):
        return pl.ds(pl.multiple_of(j * TK, TK), TK)

    def score(j):
        ks = kidx_ref[key_rows(j), :]
        acc = None
        for h in range(IDX_HEADS):
            d = jnp.dot(ks, qiT_ref[h * IDX_DIM:(h + 1) * IDX_DIM, :], preferred_element_type=F32)
            t = jnp.maximum(d, 0.0) * wT_ref[h:h + 1, :]
            acc = t if acc is None else acc + t
        return acc

    mm_ref[0] = full(jnp.inf)
    mm_ref[1] = full(-jnp.inf)

    def put_scores(j, s, s_for_min):
        sc_ref[j] = s
        scb_ref[j] = s.astype(BF16)
        mm_ref[0] = jnp.minimum(mm_ref[0], _tree(jnp.minimum, s_for_min.reshape(G, SUBLANES, TQ)))
        mm_ref[1] = jnp.maximum(mm_ref[1], _tree(jnp.maximum, s.reshape(G, SUBLANES, TQ)))

    def score_pair(p, carry):
        for j in (2 * p, 2 * p + 1):
            s = score(j)
            put_scores(j, s, s)
        return carry

    lax.fori_loop(0, lax.shift_right_logical(i, 1), score_pair, 0)

    @pl.when((i & 1) == 1)
    def _():
        s = score(i - 1)
        put_scores(i - 1, s, s)

    key_t = lax.broadcasted_iota(jnp.int32, (TK, TQ), 0)
    qry_t = lax.broadcasted_iota(jnp.int32, (TK, TQ), 1)
    causal = key_t <= qry_t
    s_diag = score(i)
    put_scores(i, jnp.where(causal, s_diag, -jnp.inf), jnp.where(causal, s_diag, jnp.inf))

    @pl.when(i == 0)
    def _():
        sc_ref[0] = jnp.where(causal, 0.0, NEG)

    @pl.when(i > 0)
    def _():
        nt = i + 1

        def tile3(j):
            return sc_ref[j].reshape(G, SUBLANES, TQ)

        lo, hi = rep(jnp.min, mm_ref[0]), rep(jnp.max, mm_ref[1])

        def count_ge(thr):
            def body(j, acc):
                return acc + _tree(jnp.add, jnp.where(tile3(j) >= thr[None], 1.0, 0.0))
            return rep(jnp.sum, lax.fori_loop(0, nt, body, full(0.0)))

        bf_step = 2.0 ** -7
        tiny = 1e-30

        def as_bf16_value(x):
            return x.astype(BF16).astype(F32)

        def count_ge_coarse(thr):
            thr16 = jnp.concatenate([thr, thr], axis=0).astype(BF16)
            one, zero = jnp.ones((), BF16), jnp.zeros((), BF16)

            def body(j, acc):
                t = scb_ref[j].reshape(RB, BF16_ROWS, TQ)
                return acc + _tree(jnp.add, jnp.where(t >= thr16[None], one, zero))
            acc = lax.fori_loop(0, nt, body, jnp.zeros((BF16_ROWS, TQ), BF16))
            return rep(jnp.sum, acc.astype(F32))

        def coarse(_, carry):
            lo_c, hi_c = carry
            mid = as_bf16_value(0.5 * (lo_c + hi_c))
            ge = count_ge_coarse(mid) >= kf
            return jnp.where(ge, mid, lo_c), jnp.where(ge, hi_c, mid)

        lo_c = as_bf16_value(lo - jnp.abs(lo) * bf_step - tiny)
        hi_c = as_bf16_value(hi + jnp.abs(hi) * bf_step + tiny)
        lo_c, hi_c = lax.fori_loop(0, N_COARSE, coarse, (lo_c, hi_c))
        lo = lo_c - jnp.abs(lo_c) * bf_step - tiny
        hi = hi_c

        def bisect(_, carry):
            lo, hi, clo = carry
            mid = 0.5 * (lo + hi)
            cnt = count_ge(mid)
            ge = cnt >= kf
            return jnp.where(ge, mid, lo), jnp.where(ge, hi, mid), jnp.where(ge, cnt, clo)

        lo, hi, clo = lax.fori_loop(0, N_BISECT, bisect, (lo, hi, count_ge(lo)))

        def snap_body(j, am):
            s = tile3(j)
            return jnp.minimum(am, _tree(jnp.minimum, jnp.where(s >= lo[None], s, jnp.inf)))

        cur = rep(jnp.min, lax.fori_loop(0, nt, snap_body, full(jnp.inf)))

        def walk(cur):
            def body(j, carry):
                ac, am = carry
                s = tile3(j)
                g = s > cur[None]
                ac = ac + _tree(jnp.add, jnp.where(g, 1.0, 0.0))
                am = jnp.minimum(am, _tree(jnp.minimum, jnp.where(g, s, jnp.inf)))
                return ac, am
            ac, am = lax.fori_loop(0, nt, body, (full(0.0), full(jnp.inf)))
            return rep(jnp.sum, ac), rep(jnp.min, am)

        def walk_cond(carry):
            _, _, _, go, it = carry
            return jnp.logical_and(go > 0, it < nk * TK + 2)

        def walk_body(carry):
            cur, cge, _, _, it = carry
            c, nxt = walk(cur)
            move = c >= kf
            go = (jnp.max(jnp.where(move, 1.0, 0.0)) > 0.5).astype(jnp.int32)
            return jnp.where(move, nxt, cur), jnp.where(move, c, cge), c, go, it + 1

        kth, cge, cgt, _, _ = lax.while_loop(
            walk_cond, walk_body, (cur, clo, full(0.0), jnp.int32(1), jnp.int32(0)))
        need = kf - cgt
        has_excess = jnp.max(jnp.where(cge > kf, 1.0, 0.0)) > 0.5

        @pl.when(jnp.logical_not(has_excess))
        def _():
            def body(j, carry):
                sc_ref[j] = jnp.where(tile3(j) >= kth[None], 0.0, NEG).reshape(TK, TQ)
                return carry
            lax.fori_loop(0, nt, body, 0)

        @pl.when(has_excess)
        def _():
            tril = jnp.where(key_t >= qry_t, 1.0, 0.0).astype(BF16)

            def pref_body(p, carry):
                for j in (2 * p, jnp.minimum(2 * p + 1, nt - 1)):
                    e01 = jnp.where(tile3(j) == kth[None], 1.0, 0.0).reshape(TK, TQ).astype(BF16)
                    pref_ref[j] = jnp.dot(tril, e01, preferred_element_type=F32)
                return carry

            lax.fori_loop(0, lax.shift_right_logical(nt + 1, 1), pref_body, 0)

            def mask_body(j, offset):
                s = tile3(j)
                pref = pref_ref[j]
                rank = pref.reshape(G, SUBLANES, TQ) + offset[None]
                tie = jnp.where(rank <= need[None], 0.0, NEG)
                mb = jnp.where(s > kth[None], 0.0, jnp.where(s == kth[None], tie, NEG))
                sc_ref[j] = mb.reshape(TK, TQ)
                return offset + jnp.broadcast_to(pref[TK - 1:TK, :], (SUBLANES, TQ))

            lax.fori_loop(0, nt, mask_body, full(0.0))

    m_ref[...] = jnp.full(m_ref.shape, NEG, F32)
    acc_ref[...] = jnp.zeros(acc_ref.shape, F32)
    LG = CT_ROWS // SUBLANES

    def near_bias(kind, h):
        z = jnp.zeros((NB, NB), F32)
        b0, b1 = nb_ref[0, h], nb_ref[1, h]
        blocks = [[b0, b1], [z, b0]] if kind == 0 else [[z, z], [b1, z]]
        return jnp.concatenate([jnp.concatenate(r, axis=1) for r in blocks], axis=0)

    def attend(tiles):
        m_run = [m_ref[h] for h in range(DSA_HEADS)]
        stats = {}

        def phase_a(slot, h, ct, mbb, near):
            lg = jnp.dot(ct, qT_ref[h * DSA_LATENT:(h + 1) * DSA_LATENT, :],
                         preferred_element_type=F32)
            if near is not None:
                lg = lg + near_bias(near, h)
            lgb = lg.astype(BF16) + mbb
            lg_ref[slot, h] = lgb
            tmax = _tree(jnp.maximum, lgb.reshape(RB, BF16_ROWS, TQ)).astype(F32)
            m_new = jnp.maximum(m_run[h], rep(jnp.max, tmax))
            stats[slot, h] = (m_run[h], m_new)
            m_run[h] = m_new

        def phase_b(slot, h, ctT):
            m_old, m_new = stats[slot, h]
            alpha = jnp.exp2(m_old - m_new)
            m16 = jnp.concatenate([m_new, m_new], axis=0).astype(BF16)
            x = lg_ref[slot, h].reshape(RB, BF16_ROWS, TQ) - m16[None]
            pv = jnp.dot(ctT, jnp.exp2(x).reshape(TK, TQ), preferred_element_type=F32)
            acc3 = acc_ref[h].reshape(LG, SUBLANES, TQ) * alpha[None]
            acc_ref[h] = acc3.reshape(CT_ROWS, TQ) + pv

        ops = [(c_ref[key_rows(j), :], sc_ref[j].astype(BF16), near, cT_ref[j]) for j, near in tiles]
        units = [(slot, h) for slot in range(len(tiles)) for h in range(DSA_HEADS)]
        for k in range(len(units) + ATT_LAG):
            if k < len(units):
                slot, h = units[k]
                phase_a(slot, h, ops[slot][0], ops[slot][1], ops[slot][2])
            if k >= ATT_LAG:
                slot, h = units[k - ATT_LAG]
                phase_b(slot, h, ops[slot][3])
        for h in range(DSA_HEADS):
            m_ref[h] = m_run[h]

    nfar = jnp.maximum(i - 1, 0)

    def far_group(p, carry):
        attend([(FAR_GROUP * p + t, None) for t in range(FAR_GROUP)])
        return carry

    nfull = lax.shift_right_logical(nfar, FAR_GROUP.bit_length() - 1)
    lax.fori_loop(0, nfull, far_group, 0)
    done = nfull * FAR_GROUP
    size = FAR_GROUP // 2
    while size >= 1:
        @pl.when((nfar & size) != 0)
        def _(done=done, size=size):
            attend([(done + t, None) for t in range(size)])
        done = done + (nfar & size)
        size //= 2

    @pl.when(i > 0)
    def _():
        attend([(i - 1, 1), (i, 0)])

    @pl.when(i == 0)
    def _():
        attend([(0, 0)])

    outs = []
    for h in range(DSA_HEADS):
        rl = 1.0 / acc_ref[h, DSA_LATENT:DSA_LATENT + SUBLANES, :]
        o3 = acc_ref[h, :DSA_LATENT, :].reshape(DSA_LATENT // SUBLANES, SUBLANES, TQ) * rl[None]
        o = o3.reshape(DSA_LATENT, TQ).astype(BF16)
        outs.append(jnp.dot(wuvT_ref[h], o, preferred_element_type=F32))
    o_ref[...] = jnp.concatenate(outs, axis=0).T.astype(BF16)


def _dsa(qT, qiT, wT, kidx, c, cT, nb, wuvT):
    B, S, _ = c.shape
    nk = S // TK
    grid = (B, S // TQ)
    tileT = lambda r: pl.BlockSpec((None, r, TQ), lambda b, i: (b, 0, i))
    seq = lambda w: pl.BlockSpec((None, S, w), lambda b, i: (b, 0, 0))
    in_specs = [tileT(DSA_HEADS * DSA_LATENT), tileT(IDX_HEADS * IDX_DIM), tileT(BF16_ROWS),
                seq(IDX_DIM), seq(DSA_LATENT),
                pl.BlockSpec((None, nk, CT_ROWS, TK), lambda b, i: (b, 0, 0, 0)),
                _const_spec(nb.shape), _const_spec(wuvT.shape)]
    scratch = [
        pltpu.VMEM((nk, TK, TQ), F32),
        pltpu.VMEM((nk, TK, TQ), BF16),
        pltpu.VMEM((2, SUBLANES, TQ), F32),
        pltpu.VMEM((nk, TK, TQ), F32),
        pltpu.VMEM((FAR_GROUP, DSA_HEADS, TK, TQ), BF16),
        pltpu.VMEM((DSA_HEADS, SUBLANES, TQ), F32),
        pltpu.VMEM((DSA_HEADS, CT_ROWS, TQ), F32),
    ]
    return pl.pallas_call(
        _dsa_kernel, grid=grid, in_specs=in_specs,
        out_specs=pl.BlockSpec((None, TQ, DSA_WIDTH), lambda b, i: (b, i, 0)),
        out_shape=jax.ShapeDtypeStruct((B, S, DSA_WIDTH), BF16), scratch_shapes=scratch,
        compiler_params=_params(("parallel", "arbitrary")), name="dsa",
    )(qT, qiT, wT, kidx, c, cT, nb, wuvT)


def _s5_kernel(u_ref, bm_ref, cm_ref, lre_ref, lim_ref, dsk_ref, wglu_ref, o_ref,
               uslab_ref, uil_ref, oslab_ref, hre_ref, him_ref, *st_refs):
    nstate = S5_GROUPS * S5_STATE
    nb = hre_ref.shape[0]
    nslab = S5_WIDTH // LANES

    @pl.when(pl.program_id(0) == 0)
    def _():
        hre_ref[...] = jnp.zeros(hre_ref.shape, F32)
        him_ref[...] = jnp.zeros(him_ref.shape, F32)

    for b in range(nb):
        for k in range(nslab):
            uslab_ref[k, b * S5_PITCH:b * S5_PITCH + T_S5, :] = u_ref[b, :, k * LANES:(k + 1) * LANES]

    tsub = T_S5 // len(st_refs)
    rsub = tsub * nb

    def gather(c):
        for t in range(c * tsub, (c + 1) * tsub):
            for k in range(nslab):
                uil_ref[t * nb:(t + 1) * nb, k * LANES:(k + 1) * LANES] = (
                    uslab_ref[k, pl.ds(t, nb, stride=S5_PITCH), :])

    def in_matmul(c):
        gather(c)
        ub = uil_ref[c * rsub:(c + 1) * rsub, :].astype(BF16)
        cin = S5_WIDTH // S5_IN_BLOCKS
        sin = nstate // S5_IN_BLOCKS
        for k in range(S5_IN_BLOCKS):
            uk = ub[:, k * cin:(k + 1) * cin]
            for off in (0, nstate):
                cols = slice(off + k * sin, off + (k + 1) * sin)
                st_refs[c][:, cols] = jnp.dot(uk, bm_ref[k * cin:(k + 1) * cin, cols],
                                              preferred_element_type=F32)

    def scan(c):
        st_ref = st_refs[c]
        half = nstate // 2
        for part in range(2):
            re_sl = slice(part * half, (part + 1) * half)
            im_sl = slice(nstate + part * half, nstate + (part + 1) * half)
            lre = lre_ref[:, re_sl]
            lim = lim_ref[:, re_sl]
            hr, hi = hre_ref[:, re_sl], him_ref[:, re_sl]
            for t in range(tsub):
                r = slice(t * nb, (t + 1) * nb)
                hr, hi = (lre * hr - lim * hi + st_ref[r, re_sl],
                          lre * hi + lim * hr + st_ref[r, im_sl])
                st_ref[r, re_sl] = hr
                st_ref[r, im_sl] = hi
            hre_ref[:, re_sl] = hr
            him_ref[:, re_sl] = hi

    def out_matmul(c):
        st_ref = st_refs[c]
        cout = S5_WIDTH // S5_OUT_BLOCKS
        sout = nstate // S5_OUT_BLOCKS
        ys = []
        for k in range(S5_OUT_BLOCKS):
            acc = None
            for off in (0, nstate):
                rows = slice(off + k * sout, off + (k + 1) * sout)
                part = jnp.dot(st_ref[:, rows].astype(BF16), cm_ref[rows, k * cout:(k + 1) * cout],
                               preferred_element_type=F32)
                acc = part if acc is None else acc + part
            ys.append(acc)
        y = jnp.concatenate(ys, axis=1)
        y = jax.nn.gelu(y + dsk_ref[...] * uil_ref[c * rsub:(c + 1) * rsub, :])
        z = jnp.dot(y.astype(BF16), wglu_ref[...], preferred_element_type=F32)
        o = y * jax.nn.sigmoid(z)
        for k in range(nslab):
            oslab_ref[k, c * rsub:(c + 1) * rsub, :] = o[:, k * LANES:(k + 1) * LANES]
        for b in range(nb):
            for k in range(nslab):
                o_ref[b, c * tsub:(c + 1) * tsub, k * LANES:(k + 1) * LANES] = (
                    oslab_ref[k, pl.ds(c * rsub + b, tsub, stride=nb), :].astype(BF16))

    nsub = len(st_refs)
    for c in range(nsub + 2):
        if c < nsub:
            in_matmul(c)
        if 1 <= c <= nsub:
            scan(c - 1)
        if c >= 2:
            out_matmul(c - 2)


def _s5(u, bm, cm, lre, lim, dsk, wglu):
    B, S, W = u.shape
    tb = T_S5 * B
    nstate = S5_GROUPS * S5_STATE
    tok = pl.BlockSpec((B, T_S5, W), lambda t: (0, t, 0))
    in_specs = [tok, _const_spec(bm.shape), _const_spec(cm.shape), _const_spec(lre.shape),
                _const_spec(lim.shape), _const_spec(dsk.shape), _const_spec(wglu.shape)]
    scratch = [pltpu.VMEM((W // LANES, B * S5_PITCH, LANES), F32),
               pltpu.VMEM((tb, W), F32),
               pltpu.VMEM((W // LANES, tb, LANES), F32),
               pltpu.VMEM((B, nstate), F32), pltpu.VMEM((B, nstate), F32)]
    scratch += [pltpu.VMEM((tb // S5_SUB, 2 * nstate), F32) for _ in range(S5_SUB)]
    return pl.pallas_call(
        _s5_kernel, grid=(S // T_S5,), in_specs=in_specs, out_specs=tok,
        out_shape=jax.ShapeDtypeStruct((B, S, W), BF16), scratch_shapes=scratch,
        compiler_params=_params(("arbitrary",)), name="s5",
    )(u, bm, cm, lre, lim, dsk, wglu)


def _merge_kernel(x_ref, odsa_ref, os5_ref, mem_ref, gmix_ref, wgT_ref, wqxT_ref, gqx_ref,
                  wb1_ref, wb2_ref, wb3_ref, wout_ref, gmem_ref, wkv_ref, gk_ref, y_ref,
                  k_ref, v_ref):
    @pl.when(pl.program_id(1) == 0)
    def _():
        mb = _rms(mem_ref[...], gmem_ref[...]).astype(BF16)
        kv = jnp.dot(mb, wkv_ref[...], preferred_element_type=F32)
        for h in range(X_HEADS):
            sl = slice(h * X_HEAD_DIM, (h + 1) * X_HEAD_DIM)
            k_ref[:, sl] = _rms(kv[:, sl], gk_ref[...]).astype(BF16)
        v_ref[...] = kv[:, X_WIDTH:].astype(BF16)

    x = x_ref[...]
    hb = _rms(x, gmix_ref[...]).astype(BF16)

    qx = lax.dot_general(hb, wqxT_ref[...], NT_DIMS, preferred_element_type=F32)
    gqx = gqx_ref[...] * (X_HEAD_DIM ** -0.5)
    ox = []
    for h in range(X_HEADS):
        sl = slice(h * X_HEAD_DIM, (h + 1) * X_HEAD_DIM)
        qh = _rms(qx[:, sl], gqx).astype(BF16)
        lg = lax.dot_general(qh, k_ref[:, sl], NT_DIMS, preferred_element_type=F32)
        p = jnp.exp(lg - jnp.max(lg, axis=-1, keepdims=True))
        pv = jnp.dot(p.astype(BF16), v_ref[:, sl], preferred_element_type=F32)
        ox.append((pv / jnp.sum(p, axis=-1, keepdims=True)).astype(BF16))
    ox = jnp.concatenate(ox, axis=1)

    merged = None
    for br, (o, wb) in enumerate(((odsa_ref[...], wb1_ref), (os5_ref[...], wb2_ref), (ox, wb3_ref))):
        gate = jax.nn.sigmoid(lax.dot_general(hb, wgT_ref[br * D_MODEL:(br + 1) * D_MODEL, :],
                                              NT_DIMS, preferred_element_type=F32))
        term = gate * jnp.dot(o, wb[...], preferred_element_type=F32)
        merged = term if merged is None else merged + term
    y_ref[...] = x + jnp.dot(merged.astype(BF16), wout_ref[...], preferred_element_type=F32)


def _merge(x, odsa, os5, mem, gmix, wgT, wqxT, gqx, wb1, wb2, wb3, wout, gmem, wkv, gk):
    B, S, D = x.shape
    ts = TS_MERGE
    tok = lambda w: pl.BlockSpec((None, ts, w), lambda b, s: (b, s, 0))
    consts = (gmix, wgT, wqxT, gqx, wb1, wb2, wb3, wout, gmem, wkv, gk)
    in_specs = [tok(D), tok(DSA_WIDTH), tok(S5_WIDTH),
                pl.BlockSpec((None, N_MEM, D), lambda b, s: (b, 0, 0))
                ] + [_const_spec(a.shape) for a in consts]
    scratch = [pltpu.VMEM((N_MEM, X_WIDTH), BF16), pltpu.VMEM((N_MEM, X_WIDTH), BF16)]
    return pl.pallas_call(
        _merge_kernel, grid=(B, S // ts), in_specs=in_specs, out_specs=tok(D),
        out_shape=jax.ShapeDtypeStruct((B, S, D), F32), scratch_shapes=scratch,
        compiler_params=_params(("parallel", "arbitrary")), name="merge",
    )(x, odsa, os5, mem, *consts)


def _ffn_kernel(x_ref, g_ref, wg_ref, wu_ref, wd_ref, y_ref):
    x = x_ref[...]
    hb = _rms(x, g_ref[...]).astype(BF16)
    a = jnp.dot(hb, wg_ref[...], preferred_element_type=F32)
    b = jnp.dot(hb, wu_ref[...], preferred_element_type=F32)
    act = (jax.nn.silu(a) * b).astype(BF16)
    y_ref[...] = x + jnp.dot(act, wd_ref[...], preferred_element_type=F32)


def _ffn(x2, g, wg, wu, wd):
    n, D = x2.shape
    tok = pl.BlockSpec((TS_FFN, D), lambda t: (t, 0))
    return pl.pallas_call(
        _ffn_kernel, grid=(n // TS_FFN,),
        in_specs=[tok] + [_const_spec(a.shape) for a in (g, wg, wu, wd)],
        out_specs=tok, out_shape=jax.ShapeDtypeStruct((n, D), F32),
        compiler_params=_params(("parallel",)), name="ffn",
    )(x2, g, wg, wu, wd)


def _t5_bucket(n):
    max_exact = REL_BUCKETS // 2
    nf = jnp.maximum(n, 1).astype(F32)
    large = max_exact + (jnp.log(nf / max_exact) / math.log(REL_MAX_DIST / max_exact)
                         * (REL_BUCKETS - max_exact)).astype(jnp.int32)
    large = jnp.minimum(large, REL_BUCKETS - 1)
    return jnp.where(n < max_exact, n, large)


def _toeplitz(w, rows, cols):
    H, L = w.shape
    flat = jnp.tile(w, (1, rows))[:, :rows * (L - 1)]
    return flat.reshape(H, rows, L - 1)[:, :, :cols]


def _near_bias(rel_bias):
    n = jnp.arange(2 * NB, dtype=jnp.int32)
    f = (rel_bias[_t5_bucket(n)] - rel_bias[REL_BUCKETS - 1][None, :]).T * LOG2E
    w0 = jnp.concatenate([f[:, :NB], jnp.broadcast_to(f[:, :1], (f.shape[0], NB - 1))], axis=1)
    w1 = jnp.concatenate([f[:, NB:2 * NB], f[:, 1:NB]], axis=1)
    return jnp.stack([_toeplitz(w0, NB, NB), _toeplitz(w1, NB, NB)], axis=0).astype(F32)


def _s5_mats(a_re, a_im, log_dt, b_re, b_im, c_re, c_im):
    lam = lax.complex(a_re.astype(F32), a_im.astype(F32))
    dt = jnp.exp(log_dt.astype(F32))[:, None]
    lam_bar = jnp.exp(lam * dt)
    b_bar = ((lam_bar - 1.0) / lam)[..., None] * lax.complex(b_re.astype(F32), b_im.astype(F32))
    nstate = S5_GROUPS * S5_STATE
    in_mask = (jnp.arange(S5_WIDTH)[:, None] // S5_GROUP) == (jnp.arange(nstate)[None, :] // S5_STATE)

    def blockdiag_in(w):
        t = jnp.transpose(w, (0, 2, 1)).reshape(S5_WIDTH, S5_STATE)
        return jnp.where(in_mask, jnp.tile(t, (1, S5_GROUPS)), 0.0)

    def blockdiag_out(w):
        t = jnp.transpose(w, (0, 2, 1)).reshape(nstate, S5_GROUP)
        return jnp.where(in_mask.T, jnp.tile(t, (1, S5_GROUPS)), 0.0)

    bm = jnp.concatenate([blockdiag_in(jnp.real(b_bar)), blockdiag_in(jnp.imag(b_bar))], axis=1)
    cm = jnp.concatenate([blockdiag_out(c_re.astype(F32)), blockdiag_out(-c_im.astype(F32))], axis=0)
    return bm.astype(BF16), cm.astype(BF16), jnp.real(lam_bar).reshape(1, -1), jnp.imag(lam_bar).reshape(1, -1)


def kernel(x, mem, rel_bias, w_in, g_mix_norm, g_q_dsa, g_kv_dsa, w_uv_dsa, a_re, a_im, log_dt, b_re, b_im, c_re, c_im, d_skip, w_glu, g_mem_norm, w_mem_kv, g_q_cross, g_k_cross, w_br_dsa, w_br_s5, w_br_cross, w_out, g_ffn_norm, w_ffn_gate, w_ffn_up, w_ffn_down):
    B, S, D = x.shape
    depth = w_in.shape[0]
    assert D == D_MODEL and w_in.shape[2] == sum(IN_SPLITS) and mem.shape[1] == N_MEM
    assert TQ == TK == 2 * NB and min(TOPK_MAX, S // 4) == TOPK_MAX and TQ <= TOPK_MAX
    assert S % TS_IN == 0 and S % T_S5 == 0 and S % TS_MERGE == 0 and (B * S) % TS_FFN == 0
    assert T_S5 % S5_SUB == 0 and S5_PITCH % (2 * SUBLANES) == SUBLANES
    offs = [0] + [int(o) for o in np.cumsum(IN_SPLITS)]
    nb = _near_bias(rel_bias)
    row = lambda v: v.reshape(1, -1).astype(F32)
    col = lambda v: v.reshape(-1, 1).astype(F32)
    bf = lambda a: a.astype(BF16)
    for l in range(depth):
        w_inT = jnp.transpose(w_in[l])
        wqT, wcT, wqiT, wkT, wwT, wuT, wqxT, wgT = [bf(w_inT[offs[k]:offs[k + 1]])
                                                    for k in range(len(IN_SPLITS))]
        wckwT = jnp.pad(jnp.concatenate([wcT, wkT, wwT], axis=0),
                        ((0, 2 * LANES - DSA_LATENT - IDX_DIM - IDX_HEADS), (0, 0)))
        gqc = col(g_q_dsa[l]) * (DSA_LATENT ** -0.5 * LOG2E)

        qT, cT, c, qiT, wT, kidx, u = _inproj(
            x, row(g_mix_norm[l]), wqT, wckwT, wqiT, wuT, gqc, row(g_kv_dsa[l]))

        wuvT = bf(jnp.transpose(w_uv_dsa[l], (0, 2, 1)))
        o_dsa = _dsa(qT, qiT, wT, kidx, c, cT, nb, wuvT)

        bm, cm, lre, lim = _s5_mats(a_re[l], a_im[l], log_dt[l], b_re[l], b_im[l], c_re[l], c_im[l])
        lre = jnp.broadcast_to(lre, (B, lre.shape[1]))
        lim = jnp.broadcast_to(lim, (B, lim.shape[1]))
        o_s5 = _s5(u, bm, cm, lre, lim, row(d_skip[l]), bf(w_glu[l]))

        x1 = _merge(x, o_dsa, o_s5, mem, row(g_mix_norm[l]), wgT, wqxT, row(g_q_cross[l]),
                    bf(w_br_dsa[l]), bf(w_br_s5[l]), bf(w_br_cross[l]), bf(w_out[l]),
                    row(g_mem_norm[l]), bf(w_mem_kv[l]), row(g_k_cross[l]))

        x = _ffn(x1.reshape(B * S, D), row(g_ffn_norm[l]), bf(w_ffn_gate[l]), bf(w_ffn_up[l]),
                 bf(w_ffn_down[l])).reshape(B, S, D)
    return x
```

```python
import math

import jax
import jax.numpy as jnp
import numpy as np
from jax import lax
from jax.experimental import pallas as pl
from jax.experimental.pallas import tpu as pltpu

F32 = jnp.float32
BF16 = jnp.bfloat16

D_MODEL = 1024
N_MEM = 256
EPS = 1e-6
DSA_HEADS = 8
DSA_LATENT = 128
DSA_VDIM = 64
IDX_HEADS = 8
IDX_DIM = 64
TOPK_MAX = 256
REL_BUCKETS = 32
REL_MAX_DIST = 128
S5_WIDTH = 512
S5_GROUP = 16
S5_GROUPS = S5_WIDTH // S5_GROUP
S5_STATE = 64
X_HEADS = 4
X_HEAD_DIM = 128
DSA_WIDTH = DSA_HEADS * DSA_VDIM
X_WIDTH = X_HEADS * X_HEAD_DIM
N_BRANCH = 3
IN_SPLITS = (DSA_HEADS * DSA_LATENT, DSA_LATENT, IDX_HEADS * IDX_DIM, IDX_DIM,
             IDX_HEADS, S5_WIDTH, X_WIDTH, N_BRANCH * D_MODEL)

LANES = 128
SUBLANES = 8
BF16_ROWS = 16
CT_ROWS = DSA_LATENT + BF16_ROWS
VMEM_LIMIT = 56 * 1024 * 1024
LOG2E = math.log2(math.e)

TS_IN = 1024
TQ = 256
TK = 256
FAR_GROUP = 2
ATT_LAG = 8
NB = REL_MAX_DIST
N_COARSE = 8
N_BISECT = 8
T_S5 = 128
S5_SUB = 4
S5_IN_BLOCKS = 4
S5_OUT_BLOCKS = 2
S5_PITCH = T_S5 + SUBLANES
TS_MERGE = 1024
TS_FFN = 512
NEG = -(2.0 ** 100)

NT_DIMS = (((1,), (1,)), ((), ()))


def _rms(x, g):
    ms = jnp.mean(x * x, axis=-1, keepdims=True)
    return x * lax.rsqrt(ms + EPS) * g


def _tree(fn, x):
    while x.shape[0] > 1:
        half = x.shape[0] // 2
        x = fn(x[:half], x[half:])
    return x[0]


def _const_spec(shape):
    nd = len(shape)
    return pl.BlockSpec(shape, lambda *_: (0,) * nd, pipeline_mode=pl.Buffered(1))


def _params(sem):
    return pltpu.CompilerParams(dimension_semantics=sem, vmem_limit_bytes=VMEM_LIMIT)


IN_OFFS = tuple(int(o) for o in np.cumsum((0,) + IN_SPLITS))
W_CHUNK = 512


def _stream_rows(w_hbm, buf_ref, sem, jobs):
    def copy(i):
        row0, n, _ = jobs[i]
        return pltpu.make_async_copy(w_hbm.at[pl.ds(row0, n), :],
                                     buf_ref.at[i % 2, pl.ds(0, n), :], sem.at[i % 2])

    copy(0).start()
    for i, (_, n, consume) in enumerate(jobs):
        if i + 1 < len(jobs):
            copy(i + 1).start()
        copy(i).wait()
        consume(buf_ref[i % 2, :n, :])


def _cast_into(dst_ref, row0, n):
    def consume(rows):
        dst_ref[row0:row0 + n, :] = rows.astype(BF16)
    return consume


def _inproj_kernel(x_ref, gmix_ref, w_hbm, gqc_ref, gkv_ref,
                   qT_ref, cT_ref, c_ref, qiT_ref, wT_ref, kidx_ref, u_ref,
                   wqT_ref, wckwT_ref, wqiT_ref, wuT_ref, wbuf_ref, wsem):
    @pl.when((pl.program_id(0) == 0) & (pl.program_id(1) == 0))
    def _():
        def keys_and_head_weights(rows):
            r = lax.broadcasted_iota(jnp.int32, rows.shape, 0)
            wckwT_ref[DSA_LATENT:, :] = jnp.where(r < IDX_DIM + IDX_HEADS, rows, 0.0).astype(BF16)

        jobs = [(IN_OFFS[0] + r, W_CHUNK, _cast_into(wqT_ref, r, W_CHUNK))
                for r in range(0, IN_SPLITS[0], W_CHUNK)]
        jobs += [(IN_OFFS[1], DSA_LATENT, _cast_into(wckwT_ref, 0, DSA_LATENT)),
                 (IN_OFFS[3], LANES, keys_and_head_weights),
                 (IN_OFFS[2], IN_SPLITS[2], _cast_into(wqiT_ref, 0, IN_SPLITS[2])),
                 (IN_OFFS[5], IN_SPLITS[5], _cast_into(wuT_ref, 0, IN_SPLITS[5]))]
        _stream_rows(w_hbm, wbuf_ref, wsem, jobs)

    ts = x_ref.shape[0]
    hb = _rms(x_ref[...], gmix_ref[...]).astype(BF16)
    proj = lambda wT_ref: lax.dot_general(hb, wT_ref[...], NT_DIMS, preferred_element_type=F32)

    qT = proj(wqT_ref).T
    q3 = qT.reshape(DSA_HEADS, DSA_LATENT, ts)
    ms = jnp.mean(q3 * q3, axis=1, keepdims=True)
    qT_ref[...] = (q3 * lax.rsqrt(ms + EPS) * gqc_ref[...][None]).reshape(qT.shape).astype(BF16)

    ckw = proj(wckwT_ref)
    kw = ckw[:, DSA_LATENT:]
    kidx_ref[...] = kw[:, :IDX_DIM].astype(BF16)
    wT_ref[...] = kw.T[IDX_DIM:IDX_DIM + BF16_ROWS, :]

    cn = _rms(ckw[:, :DSA_LATENT], gkv_ref[...])
    c_ref[...] = cn.astype(BF16)
    cTn = cn.T.astype(BF16)
    for k in range(ts // TK):
        cT_ref[k, :DSA_LATENT, :] = cTn[:, k * TK:(k + 1) * TK]
        cT_ref[k, DSA_LATENT:, :] = jnp.ones((BF16_ROWS, TK), BF16)

    qiT_ref[...] = proj(wqiT_ref).T.astype(BF16)

    u_ref[...] = proj(wuT_ref)


def _inproj(x, gmix, w_inT, gqc, gkv):
    B, S, D = x.shape
    assert all(o % SUBLANES == 0 for o in IN_OFFS) and IN_SPLITS[0] % W_CHUNK == 0
    assert max(IN_SPLITS[2], IN_SPLITS[5], IN_SPLITS[6], LANES) <= W_CHUNK
    ts = TS_IN
    grid = (B, S // ts)
    tok = lambda w: pl.BlockSpec((None, ts, w), lambda b, s: (b, s, 0))
    tokT = lambda r: pl.BlockSpec((None, r, ts), lambda b, s: (b, 0, s))
    hq = DSA_HEADS * DSA_LATENT
    hi = IDX_HEADS * IDX_DIM
    out_shape = (
        jax.ShapeDtypeStruct((B, hq, S), BF16),
        jax.ShapeDtypeStruct((B, S // TK, CT_ROWS, TK), BF16),
        jax.ShapeDtypeStruct((B, S, DSA_LATENT), BF16),
        jax.ShapeDtypeStruct((B, hi, S), BF16),
        jax.ShapeDtypeStruct((B, BF16_ROWS, S), F32),
        jax.ShapeDtypeStruct((B, S, IDX_DIM), BF16),
        jax.ShapeDtypeStruct((B, S, S5_WIDTH), F32),
    )
    out_specs = (tokT(hq),
                 pl.BlockSpec((None, ts // TK, CT_ROWS, TK), lambda b, s: (b, s, 0, 0)),
                 tok(DSA_LATENT), tokT(hi), tokT(BF16_ROWS), tok(IDX_DIM),
                 tok(S5_WIDTH))
    in_specs = [tok(D), _const_spec(gmix.shape), pl.BlockSpec(memory_space=pl.ANY),
                _const_spec(gqc.shape), _const_spec(gkv.shape)]
    scratch = [pltpu.VMEM((n, D), BF16) for n in (hq, 2 * LANES, hi, S5_WIDTH)]
    scratch += [pltpu.VMEM((2, W_CHUNK, D), F32), pltpu.SemaphoreType.DMA((2,))]
    return pl.pallas_call(
        _inproj_kernel, grid=grid, in_specs=in_specs, out_specs=out_specs, out_shape=out_shape,
        scratch_shapes=scratch,
        compiler_params=_params(("arbitrary", "arbitrary")), name="inproj",
    )(x, gmix, w_inT, gqc, gkv)


def _dsa_kernel(qT_ref, qiT_ref, wT_ref, kidx_ref, c_ref, cT_ref, nb_ref, wuvT_ref, o_ref,
                sc_ref, scb_ref, mm_ref, pref_ref, lg_ref, m_ref, acc_ref):
    i = pl.program_id(1)
    nk = sc_ref.shape[0]
    kf = float(TOPK_MAX)
    G = TK // SUBLANES
    RB = TK // BF16_ROWS

    def rep(fn, a):
        return jnp.broadcast_to(fn(a, axis=0, keepdims=True), (SUBLANES, TQ))

    def full(v, dt=F32):
        return jnp.full((SUBLANES, TQ), v, dt)

    def key_rows(j):
        return pl.ds(pl.multiple_of(j * TK, TK), TK)

    def score(j):
        ks = kidx_ref[key_rows(j), :]
        acc = None
        for h in range(IDX_HEADS):
            d = jnp.dot(ks, qiT_ref[h * IDX_DIM:(h + 1) * IDX_DIM, :], preferred_element_type=F32)
            t = jnp.maximum(d, 0.0) * wT_ref[h:h + 1, :]
            acc = t if acc is None else acc + t
        return acc

    mm_ref[0] = full(jnp.inf)
    mm_ref[1] = full(-jnp.inf)

    def put_scores(j, s, s_for_min):
        sc_ref[j] = s
        scb_ref[j] = s.astype(BF16)
        mm_ref[0] = jnp.minimum(mm_ref[0], _tree(jnp.minimum, s_for_min.reshape(G, SUBLANES, TQ)))
        mm_ref[1] = jnp.maximum(mm_ref[1], _tree(jnp.maximum, s.reshape(G, SUBLANES, TQ)))

    def score_pair(p, carry):
        for j in (2 * p, 2 * p + 1):
            s = score(j)
            put_scores(j, s, s)
        return carry

    lax.fori_loop(0, lax.shift_right_logical(i, 1), score_pair, 0)

    @pl.when((i & 1) == 1)
    def _():
        s = score(i - 1)
        put_scores(i - 1, s, s)

    key_t = lax.broadcasted_iota(jnp.int32, (TK, TQ), 0)
    qry_t = lax.broadcasted_iota(jnp.int32, (TK, TQ), 1)
    causal = key_t <= qry_t
    s_diag = score(i)
    put_scores(i, jnp.where(causal, s_diag, -jnp.inf), jnp.where(causal, s_diag, jnp.inf))

    @pl.when(i == 0)
    def _():
        sc_ref[0] = jnp.where(causal, 0.0, NEG)

    @pl.when(i > 0)
    def _():
        nt = i + 1

        def tile3(j):
            return sc_ref[j].reshape(G, SUBLANES, TQ)

        lo, hi = rep(jnp.min, mm_ref[0]), rep(jnp.max, mm_ref[1])

        def count_ge(thr):
            def body(j, acc):
                return acc + _tree(jnp.add, jnp.where(tile3(j) >= thr[None], 1.0, 0.0))
            return rep(jnp.sum, lax.fori_loop(0, nt, body, full(0.0)))

        bf_step = 2.0 ** -7
        tiny = 1e-30

        def as_bf16_value(x):
            return x.astype(BF16).astype(F32)

        def count_ge_coarse(thr):
            thr16 = jnp.concatenate([thr, thr], axis=0).astype(BF16)
            one, zero = jnp.ones((), BF16), jnp.zeros((), BF16)

            def body(j, acc):
                t = scb_ref[j].reshape(RB, BF16_ROWS, TQ)
                return acc + _tree(jnp.add, jnp.where(t >= thr16[None], one, zero))
            acc = lax.fori_loop(0, nt, body, jnp.zeros((BF16_ROWS, TQ), BF16))
            return rep(jnp.sum, acc.astype(F32))

        def coarse(_, carry):
            lo_c, hi_c = carry
            mid = as_bf16_value(0.5 * (lo_c + hi_c))
            ge = count_ge_coarse(mid) >= kf
            return jnp.where(ge, mid, lo_c), jnp.where(ge, hi_c, mid)

        lo_c = as_bf16_value(lo - jnp.abs(lo) * bf_step - tiny)
        hi_c = as_bf16_value(hi + jnp.abs(hi) * bf_step + tiny)
        lo_c, hi_c = lax.fori_loop(0, N_COARSE, coarse, (lo_c, hi_c))
        lo = lo_c - jnp.abs(lo_c) * bf_step - tiny
        hi = hi_c

        def bisect(_, carry):
            lo, hi, clo = carry
            mid = 0.5 * (lo + hi)
            cnt = count_ge(mid)
            ge = cnt >= kf
            return jnp.where(ge, mid, lo), jnp.where(ge, hi, mid), jnp.where(ge, cnt, clo)

        lo, hi, clo = lax.fori_loop(0, N_BISECT, bisect, (lo, hi, count_ge(lo)))

        def snap_body(j, am):
            s = tile3(j)
            return jnp.minimum(am, _tree(jnp.minimum, jnp.where(s >= lo[None], s, jnp.inf)))

        cur = rep(jnp.min, lax.fori_loop(0, nt, snap_body, full(jnp.inf)))

        def walk(cur):
            def body(j, carry):
                ac, am = carry
                s = tile3(j)
                g = s > cur[None]
                ac = ac + _tree(jnp.add, jnp.where(g, 1.0, 0.0))
                am = jnp.minimum(am, _tree(jnp.minimum, jnp.where(g, s, jnp.inf)))
                return ac, am
            ac, am = lax.fori_loop(0, nt, body, (full(0.0), full(jnp.inf)))
            return rep(jnp.sum, ac), rep(jnp.min, am)

        def walk_cond(carry):
            _, _, _, go, it = carry
            return jnp.logical_and(go > 0, it < nk * TK + 2)

        def walk_body(carry):
            cur, cge, _, _, it = carry
            c, nxt = walk(cur)
            move = c >= kf
            go = (jnp.max(jnp.where(move, 1.0, 0.0)) > 0.5).astype(jnp.int32)
            return jnp.where(move, nxt, cur), jnp.where(move, c, cge), c, go, it + 1

        kth, cge, cgt, _, _ = lax.while_loop(
            walk_cond, walk_body, (cur, clo, full(0.0), jnp.int32(1), jnp.int32(0)))
        need = kf - cgt
        has_excess = jnp.max(jnp.where(cge > kf, 1.0, 0.0)) > 0.5

        @pl.when(jnp.logical_not(has_excess))
        def _():
            def body(j, carry):
                sc_ref[j] = jnp.where(tile3(j) >= kth[None], 0.0, NEG).reshape(TK, TQ)
                return carry
            lax.fori_loop(0, nt, body, 0)

        @pl.when(has_excess)
        def _():
            tril = jnp.where(key_t >= qry_t, 1.0, 0.0).astype(BF16)

            def pref_body(p, carry):
                for j in (2 * p, jnp.minimum(2 * p + 1, nt - 1)):
                    e01 = jnp.where(tile3(j) == kth[None], 1.0, 0.0).reshape(TK, TQ).astype(BF16)
                    pref_ref[j] = jnp.dot(tril, e01, preferred_element_type=F32)
                return carry

            lax.fori_loop(0, lax.shift_right_logical(nt + 1, 1), pref_body, 0)

            def mask_body(j, offset):
                s = tile3(j)
                pref = pref_ref[j]
                rank = pref.reshape(G, SUBLANES, TQ) + offset[None]
                tie = jnp.where(rank <= need[None], 0.0, NEG)
                mb = jnp.where(s > kth[None], 0.0, jnp.where(s == kth[None], tie, NEG))
                sc_ref[j] = mb.reshape(TK, TQ)
                return offset + jnp.broadcast_to(pref[TK - 1:TK, :], (SUBLANES, TQ))

            lax.fori_loop(0, nt, mask_body, full(0.0))

    m_ref[...] = jnp.full(m_ref.shape, NEG, F32)
    acc_ref[...] = jnp.zeros(acc_ref.shape, F32)
    LG = CT_ROWS // SUBLANES

    def near_bias(kind, h):
        z = jnp.zeros((NB, NB), F32)
        b0, b1 = nb_ref[0, h], nb_ref[1, h]
        blocks = [[b0, b1], [z, b0]] if kind == 0 else [[z, z], [b1, z]]
        return jnp.concatenate([jnp.concatenate(r, axis=1) for r in blocks], axis=0)

    def attend(tiles):
        m_run = [m_ref[h] for h in range(DSA_HEADS)]
        stats = {}

        def phase_a(slot, h, ct, mbb, near):
            lg = jnp.dot(ct, qT_ref[h * DSA_LATENT:(h + 1) * DSA_LATENT, :],
                         preferred_element_type=F32)
            if near is not None:
                lg = lg + near_bias(near, h)
            lgb = lg.astype(BF16) + mbb
            lg_ref[slot, h] = lgb
            tmax = _tree(jnp.maximum, lgb.reshape(RB, BF16_ROWS, TQ)).astype(F32)
            m_new = jnp.maximum(m_run[h], rep(jnp.max, tmax))
            stats[slot, h] = (m_run[h], m_new)
            m_run[h] = m_new

        def phase_b(slot, h, ctT):
            m_old, m_new = stats[slot, h]
            alpha = jnp.exp2(m_old - m_new)
            m16 = jnp.concatenate([m_new, m_new], axis=0).astype(BF16)
            x = lg_ref[slot, h].reshape(RB, BF16_ROWS, TQ) - m16[None]
            pv = jnp.dot(ctT, jnp.exp2(x).reshape(TK, TQ), preferred_element_type=F32)
            acc3 = acc_ref[h].reshape(LG, SUBLANES, TQ) * alpha[None]
            acc_ref[h] = acc3.reshape(CT_ROWS, TQ) + pv

        ops = [(c_ref[key_rows(j), :], sc_ref[j].astype(BF16), near, cT_ref[j]) for j, near in tiles]
        units = [(slot, h) for slot in range(len(tiles)) for h in range(DSA_HEADS)]
        for k in range(len(units) + ATT_LAG):
            if k < len(units):
                slot, h = units[k]
                phase_a(slot, h, ops[slot][0], ops[slot][1], ops[slot][2])
            if k >= ATT_LAG:
                slot, h = units[k - ATT_LAG]
                phase_b(slot, h, ops[slot][3])
        for h in range(DSA_HEADS):
            m_ref[h] = m_run[h]

    nfar = jnp.maximum(i - 1, 0)

    def far_group(p, carry):
        attend([(FAR_GROUP * p + t, None) for t in range(FAR_GROUP)])
        return carry

    nfull = lax.shift_right_logical(nfar, FAR_GROUP.bit_length() - 1)
    lax.fori_loop(0, nfull, far_group, 0)
    done = nfull * FAR_GROUP
    size = FAR_GROUP // 2
    while size >= 1:
        @pl.when((nfar & size) != 0)
        def _(done=done, size=size):
            attend([(done + t, None) for t in range(size)])
        done = done + (nfar & size)
        size //= 2

    @pl.when(i > 0)
    def _():
        attend([(i - 1, 1), (i, 0)])

    @pl.when(i == 0)
    def _():
        attend([(0, 0)])

    outs = []
    for h in range(DSA_HEADS):
        rl = 1.0 / acc_ref[h, DSA_LATENT:DSA_LATENT + SUBLANES, :]
        o3 = acc_ref[h, :DSA_LATENT, :].reshape(DSA_LATENT // SUBLANES, SUBLANES, TQ) * rl[None]
        o = o3.reshape(DSA_LATENT, TQ).astype(BF16)
        outs.append(jnp.dot(wuvT_ref[h], o, preferred_element_type=F32))
    o_ref[...] = jnp.concatenate(outs, axis=0).T.astype(BF16)


def _dsa(qT, qiT, wT, kidx, c, cT, nb, wuvT):
    B, S, _ = c.shape
    nk = S // TK
    grid = (B, S // TQ)
    tileT = lambda r: pl.BlockSpec((None, r, TQ), lambda b, i: (b, 0, i))
    seq = lambda w: pl.BlockSpec((None, S, w), lambda b, i: (b, 0, 0))
    in_specs = [tileT(DSA_HEADS * DSA_LATENT), tileT(IDX_HEADS * IDX_DIM), tileT(BF16_ROWS),
                seq(IDX_DIM), seq(DSA_LATENT),
                pl.BlockSpec((None, nk, CT_ROWS, TK), lambda b, i: (b, 0, 0, 0)),
                _const_spec(nb.shape), _const_spec(wuvT.shape)]
    scratch = [
        pltpu.VMEM((nk, TK, TQ), F32),
        pltpu.VMEM((nk, TK, TQ), BF16),
        pltpu.VMEM((2, SUBLANES, TQ), F32),
        pltpu.VMEM((nk, TK, TQ), F32),
        pltpu.VMEM((FAR_GROUP, DSA_HEADS, TK, TQ), BF16),
        pltpu.VMEM((DSA_HEADS, SUBLANES, TQ), F32),
        pltpu.VMEM((DSA_HEADS, CT_ROWS, TQ), F32),
    ]
    return pl.pallas_call(
        _dsa_kernel, grid=grid, in_specs=in_specs,
        out_specs=pl.BlockSpec((None, TQ, DSA_WIDTH), lambda b, i: (b, i, 0)),
        out_shape=jax.ShapeDtypeStruct((B, S, DSA_WIDTH), BF16), scratch_shapes=scratch,
        compiler_params=_params(("parallel", "arbitrary")), name="dsa",
    )(qT, qiT, wT, kidx, c, cT, nb, wuvT)


def _s5_kernel(u_ref, bm_ref, cm_ref, lre_ref, lim_ref, dsk_ref, wglu_ref, o_ref,
               uslab_ref, uil_ref, oslab_ref, hre_ref, him_ref, *st_refs):
    nstate = S5_GROUPS * S5_STATE
    nb = hre_ref.shape[0]
    nslab = S5_WIDTH // LANES

    @pl.when(pl.program_id(0) == 0)
    def _():
        hre_ref[...] = jnp.zeros(hre_ref.shape, F32)
        him_ref[...] = jnp.zeros(him_ref.shape, F32)

    for b in range(nb):
        for k in range(nslab):
            uslab_ref[k, b * S5_PITCH:b * S5_PITCH + T_S5, :] = u_ref[b, :, k * LANES:(k + 1) * LANES]

    tsub = T_S5 // len(st_refs)
    rsub = tsub * nb

    def gather(c):
        for t in range(c * tsub, (c + 1) * tsub):
            for k in range(nslab):
                uil_ref[t * nb:(t + 1) * nb, k * LANES:(k + 1) * LANES] = (
                    uslab_ref[k, pl.ds(t, nb, stride=S5_PITCH), :])

    def in_matmul(c):
        gather(c)
        ub = uil_ref[c * rsub:(c + 1) * rsub, :].astype(BF16)
        cin = S5_WIDTH // S5_IN_BLOCKS
        sin = nstate // S5_IN_BLOCKS
        for k in range(S5_IN_BLOCKS):
            uk = ub[:, k * cin:(k + 1) * cin]
            for part, off in enumerate((0, nstate)):
                cols = slice(off + k * sin, off + (k + 1) * sin)
                bk = bm_ref[k * cin:(k + 1) * cin, part * sin:(part + 1) * sin]
                st_refs[c][:, cols] = jnp.dot(uk, bk, preferred_element_type=F32)

    def scan(c):
        st_ref = st_refs[c]
        half = nstate // 2
        for part in range(2):
            re_sl = slice(part * half, (part + 1) * half)
            im_sl = slice(nstate + part * half, nstate + (part + 1) * half)
            lre = lre_ref[:, re_sl]
            lim = lim_ref[:, re_sl]
            hr, hi = hre_ref[:, re_sl], him_ref[:, re_sl]
            for t in range(tsub):
                r = slice(t * nb, (t + 1) * nb)
                hr, hi = (lre * hr - lim * hi + st_ref[r, re_sl],
                          lre * hi + lim * hr + st_ref[r, im_sl])
                st_ref[r, re_sl] = hr
                st_ref[r, im_sl] = hi
            hre_ref[:, re_sl] = hr
            him_ref[:, re_sl] = hi

    def out_matmul(c):
        st_ref = st_refs[c]
        sout = nstate // S5_OUT_BLOCKS
        ys = []
        for k in range(S5_OUT_BLOCKS):
            acc = None
            for off in (0, nstate):
                rows = slice(off + k * sout, off + (k + 1) * sout)
                part = jnp.dot(st_ref[:, rows].astype(BF16), cm_ref[rows, :],
                               preferred_element_type=F32)
                acc = part if acc is None else acc + part
            ys.append(acc)
        y = jnp.concatenate(ys, axis=1)
        y = jax.nn.gelu(y + dsk_ref[...] * uil_ref[c * rsub:(c + 1) * rsub, :])
        z = jnp.dot(y.astype(BF16), wglu_ref[...], preferred_element_type=F32)
        o = y * jax.nn.sigmoid(z)
        for k in range(nslab):
            oslab_ref[k, c * rsub:(c + 1) * rsub, :] = o[:, k * LANES:(k + 1) * LANES]
        for b in range(nb):
            for k in range(nslab):
                o_ref[b, c * tsub:(c + 1) * tsub, k * LANES:(k + 1) * LANES] = (
                    oslab_ref[k, pl.ds(c * rsub + b, tsub, stride=nb), :].astype(BF16))

    nsub = len(st_refs)
    for c in range(nsub + 2):
        if c < nsub:
            in_matmul(c)
        if 1 <= c <= nsub:
            scan(c - 1)
        if c >= 2:
            out_matmul(c - 2)


def _s5(u, bm, cm, lre, lim, dsk, wglu):
    B, S, W = u.shape
    tb = T_S5 * B
    nstate = S5_GROUPS * S5_STATE
    tok = pl.BlockSpec((B, T_S5, W), lambda t: (0, t, 0))
    in_specs = [tok, _const_spec(bm.shape), _const_spec(cm.shape), _const_spec(lre.shape),
                _const_spec(lim.shape), _const_spec(dsk.shape), _const_spec(wglu.shape)]
    scratch = [pltpu.VMEM((W // LANES, B * S5_PITCH, LANES), F32),
               pltpu.VMEM((tb, W), F32),
               pltpu.VMEM((W // LANES, tb, LANES), F32),
               pltpu.VMEM((B, nstate), F32), pltpu.VMEM((B, nstate), F32)]
    scratch += [pltpu.VMEM((tb // S5_SUB, 2 * nstate), F32) for _ in range(S5_SUB)]
    return pl.pallas_call(
        _s5_kernel, grid=(S // T_S5,), in_specs=in_specs, out_specs=tok,
        out_shape=jax.ShapeDtypeStruct((B, S, W), BF16), scratch_shapes=scratch,
        compiler_params=_params(("arbitrary",)), name="s5",
    )(u, bm, cm, lre, lim, dsk, wglu)


def _merge_kernel(x_ref, odsa_ref, os5_ref, mem_ref, gmix_ref, w_hbm, gqx_ref,
                  wb1_ref, wb2_ref, wb3_ref, wout_ref, gmem_ref, wkv_ref, gk_ref, y_ref,
                  k_ref, v_ref, wqxT_ref, wgT_ref, wbuf_ref, wsem):
    @pl.when((pl.program_id(0) == 0) & (pl.program_id(1) == 0))
    def _():
        jobs = [(IN_OFFS[6], X_WIDTH, _cast_into(wqxT_ref, 0, X_WIDTH))]
        jobs += [(IN_OFFS[7] + r, W_CHUNK, _cast_into(wgT_ref, r, W_CHUNK))
                 for r in range(0, IN_SPLITS[7], W_CHUNK)]
        _stream_rows(w_hbm, wbuf_ref, wsem, jobs)

    @pl.when(pl.program_id(1) == 0)
    def _():
        mb = _rms(mem_ref[...], gmem_ref[...]).astype(BF16)
        kv = jnp.dot(mb, wkv_ref[...], preferred_element_type=F32)
        for h in range(X_HEADS):
            sl = slice(h * X_HEAD_DIM, (h + 1) * X_HEAD_DIM)
            k_ref[:, sl] = _rms(kv[:, sl], gk_ref[...]).astype(BF16)
        v_ref[...] = kv[:, X_WIDTH:].astype(BF16)

    x = x_ref[...]
    hb = _rms(x, gmix_ref[...]).astype(BF16)

    qx = lax.dot_general(hb, wqxT_ref[...], NT_DIMS, preferred_element_type=F32)
    gqx = gqx_ref[...] * (X_HEAD_DIM ** -0.5)
    ox = []
    for h in range(X_HEADS):
        sl = slice(h * X_HEAD_DIM, (h + 1) * X_HEAD_DIM)
        qh = _rms(qx[:, sl], gqx).astype(BF16)
        lg = lax.dot_general(qh, k_ref[:, sl], NT_DIMS, preferred_element_type=F32)
        p = jnp.exp(lg - jnp.max(lg, axis=-1, keepdims=True))
        pv = jnp.dot(p.astype(BF16), v_ref[:, sl], preferred_element_type=F32)
        ox.append((pv / jnp.sum(p, axis=-1, keepdims=True)).astype(BF16))
    ox = jnp.concatenate(ox, axis=1)

    merged = None
    for br, (o, wb) in enumerate(((odsa_ref[...], wb1_ref), (os5_ref[...], wb2_ref), (ox, wb3_ref))):
        gate = jax.nn.sigmoid(lax.dot_general(hb, wgT_ref[br * D_MODEL:(br + 1) * D_MODEL, :],
                                              NT_DIMS, preferred_element_type=F32))
        term = gate * jnp.dot(o, wb[...], preferred_element_type=F32)
        merged = term if merged is None else merged + term
    y_ref[...] = x + jnp.dot(merged.astype(BF16), wout_ref[...], preferred_element_type=F32)


def _merge(x, odsa, os5, mem, gmix, w_inT, gqx, wb1, wb2, wb3, wout, gmem, wkv, gk):
    B, S, D = x.shape
    ts = TS_MERGE
    tok = lambda w: pl.BlockSpec((None, ts, w), lambda b, s: (b, s, 0))
    consts = (gqx, wb1, wb2, wb3, wout, gmem, wkv, gk)
    assert IN_SPLITS[7] % W_CHUNK == 0
    in_specs = [tok(D), tok(DSA_WIDTH), tok(S5_WIDTH),
                pl.BlockSpec((None, N_MEM, D), lambda b, s: (b, 0, 0)),
                _const_spec(gmix.shape), pl.BlockSpec(memory_space=pl.ANY)
                ] + [_const_spec(a.shape) for a in consts]
    scratch = [pltpu.VMEM((N_MEM, X_WIDTH), BF16), pltpu.VMEM((N_MEM, X_WIDTH), BF16),
               pltpu.VMEM((X_WIDTH, D), BF16), pltpu.VMEM((N_BRANCH * D, D), BF16),
               pltpu.VMEM((2, W_CHUNK, D), F32), pltpu.SemaphoreType.DMA((2,))]
    return pl.pallas_call(
        _merge_kernel, grid=(B, S // ts), in_specs=in_specs, out_specs=tok(D),
        out_shape=jax.ShapeDtypeStruct((B, S, D), F32), scratch_shapes=scratch,
        compiler_params=_params(("arbitrary", "arbitrary")), name="merge",
    )(x, odsa, os5, mem, gmix, w_inT, *consts)


def _ffn_kernel(x_ref, g_ref, wg_ref, wu_ref, wd_ref, y_ref):
    x = x_ref[...]
    hb = _rms(x, g_ref[...]).astype(BF16)
    a = jnp.dot(hb, wg_ref[...], preferred_element_type=F32)
    b = jnp.dot(hb, wu_ref[...], preferred_element_type=F32)
    act = (jax.nn.silu(a) * b).astype(BF16)
    y_ref[...] = x + jnp.dot(act, wd_ref[...], preferred_element_type=F32)


def _ffn(x2, g, wg, wu, wd):
    n, D = x2.shape
    tok = pl.BlockSpec((TS_FFN, D), lambda t: (t, 0))
    return pl.pallas_call(
        _ffn_kernel, grid=(n // TS_FFN,),
        in_specs=[tok] + [_const_spec(a.shape) for a in (g, wg, wu, wd)],
        out_specs=tok, out_shape=jax.ShapeDtypeStruct((n, D), F32),
        compiler_params=_params(("parallel",)), name="ffn",
    )(x2, g, wg, wu, wd)


def _t5_bucket(n):
    max_exact = REL_BUCKETS // 2
    nf = jnp.maximum(n, 1).astype(F32)
    large = max_exact + (jnp.log(nf / max_exact) / math.log(REL_MAX_DIST / max_exact)
                         * (REL_BUCKETS - max_exact)).astype(jnp.int32)
    large = jnp.minimum(large, REL_BUCKETS - 1)
    return jnp.where(n < max_exact, n, large)


def _toeplitz(w, rows, cols):
    H, L = w.shape
    flat = jnp.tile(w, (1, rows))[:, :rows * (L - 1)]
    return flat.reshape(H, rows, L - 1)[:, :, :cols]


def _near_bias(rel_bias):
    n = jnp.arange(2 * NB, dtype=jnp.int32)
    f = (rel_bias[_t5_bucket(n)] - rel_bias[REL_BUCKETS - 1][None, :]).T * LOG2E
    w0 = jnp.concatenate([f[:, :NB], jnp.broadcast_to(f[:, :1], (f.shape[0], NB - 1))], axis=1)
    w1 = jnp.concatenate([f[:, NB:2 * NB], f[:, 1:NB]], axis=1)
    return jnp.stack([_toeplitz(w0, NB, NB), _toeplitz(w1, NB, NB)], axis=0).astype(F32)


def _s5_mats(a_re, a_im, log_dt, b_re, b_im, c_re, c_im):
    lam = lax.complex(a_re.astype(F32), a_im.astype(F32))
    dt = jnp.exp(log_dt.astype(F32))[:, None]
    lam_bar = jnp.exp(lam * dt)
    b_bar = ((lam_bar - 1.0) / lam)[..., None] * lax.complex(b_re.astype(F32), b_im.astype(F32))
    nstate = S5_GROUPS * S5_STATE
    gin, gout = S5_GROUPS // S5_IN_BLOCKS, S5_GROUPS // S5_OUT_BLOCKS

    def blockdiag_in(w):
        t = jnp.transpose(w, (0, 2, 1)).reshape(S5_WIDTH, S5_STATE)
        mask = ((jnp.arange(S5_WIDTH)[:, None] // S5_GROUP) % gin
                == jnp.arange(gin * S5_STATE)[None, :] // S5_STATE)
        return jnp.where(mask, jnp.tile(t, (1, gin)), 0.0)

    def blockdiag_out(w):
        t = jnp.transpose(w, (0, 2, 1)).reshape(nstate, S5_GROUP)
        mask = ((jnp.arange(nstate)[:, None] // S5_STATE) % gout
                == jnp.arange(gout * S5_GROUP)[None, :] // S5_GROUP)
        return jnp.where(mask, jnp.tile(t, (1, gout)), 0.0)

    bm = jnp.concatenate([blockdiag_in(jnp.real(b_bar)), blockdiag_in(jnp.imag(b_bar))], axis=1)
    cm = jnp.concatenate([blockdiag_out(c_re.astype(F32)), blockdiag_out(-c_im.astype(F32))], axis=0)
    return bm.astype(BF16), cm.astype(BF16), jnp.real(lam_bar).reshape(1, -1), jnp.imag(lam_bar).reshape(1, -1)


def kernel(x, mem, rel_bias, w_in, g_mix_norm, g_q_dsa, g_kv_dsa, w_uv_dsa, a_re, a_im, log_dt, b_re, b_im, c_re, c_im, d_skip, w_glu, g_mem_norm, w_mem_kv, g_q_cross, g_k_cross, w_br_dsa, w_br_s5, w_br_cross, w_out, g_ffn_norm, w_ffn_gate, w_ffn_up, w_ffn_down):
    B, S, D = x.shape
    depth = w_in.shape[0]
    assert D == D_MODEL and w_in.shape[2] == sum(IN_SPLITS) and mem.shape[1] == N_MEM
    assert TQ == TK == 2 * NB and min(TOPK_MAX, S // 4) == TOPK_MAX and TQ <= TOPK_MAX
    assert S % TS_IN == 0 and S % T_S5 == 0 and S % TS_MERGE == 0 and (B * S) % TS_FFN == 0
    assert T_S5 % S5_SUB == 0 and S5_PITCH % (2 * SUBLANES) == SUBLANES
    nb = _near_bias(rel_bias)
    row = lambda v: v.reshape(1, -1).astype(F32)
    col = lambda v: v.reshape(-1, 1).astype(F32)
    bf = lambda a: a.astype(BF16)
    for l in range(depth):
        w_inT = jnp.transpose(w_in[l]).astype(F32)
        gqc = col(g_q_dsa[l]) * (DSA_LATENT ** -0.5 * LOG2E)

        qT, cT, c, qiT, wT, kidx, u = _inproj(x, row(g_mix_norm[l]), w_inT, gqc, row(g_kv_dsa[l]))

        wuvT = bf(jnp.transpose(w_uv_dsa[l], (0, 2, 1)))
        o_dsa = _dsa(qT, qiT, wT, kidx, c, cT, nb, wuvT)

        bm, cm, lre, lim = _s5_mats(a_re[l], a_im[l], log_dt[l], b_re[l], b_im[l], c_re[l], c_im[l])
        lre = jnp.broadcast_to(lre, (B, lre.shape[1]))
        lim = jnp.broadcast_to(lim, (B, lim.shape[1]))
        o_s5 = _s5(u, bm, cm, lre, lim, row(d_skip[l]), bf(w_glu[l]))

        x1 = _merge(x, o_dsa, o_s5, mem, row(g_mix_norm[l]), w_inT, row(g_q_cross[l]),
                    bf(w_br_dsa[l]), bf(w_br_s5[l]), bf(w_br_cross[l]), bf(w_out[l]),
                    row(g_mem_norm[l]), bf(w_mem_kv[l]), row(g_k_cross[l]))

        x = _ffn(x1.reshape(B * S, D), row(g_ffn_norm[l]), bf(w_ffn_gate[l]), bf(w_ffn_up[l]),
                 bf(w_ffn_down[l])).reshape(B, S, D)
    return x
```

```python
import math

import jax
import jax.numpy as jnp
import numpy as np
from jax import lax
from jax.experimental import pallas as pl
from jax.experimental.pallas import tpu as pltpu

F32 = jnp.float32
BF16 = jnp.bfloat16

D_MODEL = 1024
N_MEM = 256
EPS = 1e-6
DSA_HEADS = 8
DSA_LATENT = 128
DSA_VDIM = 64
IDX_HEADS = 8
IDX_DIM = 64
TOPK_MAX = 256
REL_BUCKETS = 32
REL_MAX_DIST = 128
S5_WIDTH = 512
S5_GROUP = 16
S5_GROUPS = S5_WIDTH // S5_GROUP
S5_STATE = 64
X_HEADS = 4
X_HEAD_DIM = 128
DSA_WIDTH = DSA_HEADS * DSA_VDIM
X_WIDTH = X_HEADS * X_HEAD_DIM
N_BRANCH = 3
IN_SPLITS = (DSA_HEADS * DSA_LATENT, DSA_LATENT, IDX_HEADS * IDX_DIM, IDX_DIM,
             IDX_HEADS, S5_WIDTH, X_WIDTH, N_BRANCH * D_MODEL)

LANES = 128
SUBLANES = 8
BF16_ROWS = 16
CT_ROWS = DSA_LATENT + BF16_ROWS
VMEM_LIMIT = 56 * 1024 * 1024
LOG2E = math.log2(math.e)

TS_IN = 1024
TQ = 256
TK = 256
FAR_GROUP = 2
ATT_LAG = 8
NB = REL_MAX_DIST
N_COARSE = 8
N_BISECT = 8
T_S5 = 256
S5_SUB = 8
S5_BUFS = 4
S5_IN_BLOCKS = 4
S5_OUT_BLOCKS = 2
S5_PITCH = T_S5 + SUBLANES
TS_MERGE = 1024
TS_FFN = 512
NEG = -(2.0 ** 100)

NT_DIMS = (((1,), (1,)), ((), ()))


def _rms(x, g):
    ms = jnp.mean(x * x, axis=-1, keepdims=True)
    return x * lax.rsqrt(ms + EPS) * g


def _tree(fn, x):
    while x.shape[0] > 1:
        half = x.shape[0] // 2
        x = fn(x[:half], x[half:])
    return x[0]


def _const_spec(shape):
    nd = len(shape)
    return pl.BlockSpec(shape, lambda *_: (0,) * nd, pipeline_mode=pl.Buffered(1))


def _params(sem):
    return pltpu.CompilerParams(dimension_semantics=sem, vmem_limit_bytes=VMEM_LIMIT)


def _inproj_kernel(x_ref, gmix_ref, wqT_ref, wckwT_ref, wqiT_ref, wuT_ref,
                   gqc_ref, gkv_ref,
                   qT_ref, cT_ref, c_ref, qiT_ref, wT_ref, kidx_ref, u_ref):
    ts = x_ref.shape[0]
    hb = _rms(x_ref[...], gmix_ref[...]).astype(BF16)
    proj = lambda wT_ref: lax.dot_general(hb, wT_ref[...], NT_DIMS, preferred_element_type=F32)

    qT = proj(wqT_ref).T
    q3 = qT.reshape(DSA_HEADS, DSA_LATENT, ts)
    ms = jnp.mean(q3 * q3, axis=1, keepdims=True)
    qT_ref[...] = (q3 * lax.rsqrt(ms + EPS) * gqc_ref[...][None]).reshape(qT.shape).astype(BF16)

    ckw = proj(wckwT_ref)
    kw = ckw[:, DSA_LATENT:]
    kidx_ref[...] = kw[:, :IDX_DIM].astype(BF16)
    wT_ref[...] = kw.T[IDX_DIM:IDX_DIM + BF16_ROWS, :]

    cn = _rms(ckw[:, :DSA_LATENT], gkv_ref[...])
    c_ref[...] = cn.astype(BF16)
    cTn = cn.T.astype(BF16)
    for k in range(ts // TK):
        cT_ref[k, :DSA_LATENT, :] = cTn[:, k * TK:(k + 1) * TK]
        cT_ref[k, DSA_LATENT:, :] = jnp.ones((BF16_ROWS, TK), BF16)

    qiT_ref[...] = proj(wqiT_ref).T.astype(BF16)

    u_ref[...] = proj(wuT_ref)


def _inproj(x, gmix, wqT, wckwT, wqiT, wuT, gqc, gkv):
    B, S, D = x.shape
    ts = TS_IN
    grid = (B, S // ts)
    tok = lambda w: pl.BlockSpec((None, ts, w), lambda b, s: (b, s, 0))
    tokT = lambda r: pl.BlockSpec((None, r, ts), lambda b, s: (b, 0, s))
    hq = DSA_HEADS * DSA_LATENT
    hi = IDX_HEADS * IDX_DIM
    out_shape = (
        jax.ShapeDtypeStruct((B, hq, S), BF16),
        jax.ShapeDtypeStruct((B, S // TK, CT_ROWS, TK), BF16),
        jax.ShapeDtypeStruct((B, S, DSA_LATENT), BF16),
        jax.ShapeDtypeStruct((B, hi, S), BF16),
        jax.ShapeDtypeStruct((B, BF16_ROWS, S), F32),
        jax.ShapeDtypeStruct((B, S, IDX_DIM), BF16),
        jax.ShapeDtypeStruct((B, S, S5_WIDTH), F32),
    )
    out_specs = (tokT(hq),
                 pl.BlockSpec((None, ts // TK, CT_ROWS, TK), lambda b, s: (b, s, 0, 0)),
                 tok(DSA_LATENT), tokT(hi), tokT(BF16_ROWS), tok(IDX_DIM),
                 tok(S5_WIDTH))
    consts = (gmix, wqT, wckwT, wqiT, wuT, gqc, gkv)
    in_specs = [tok(D)] + [_const_spec(a.shape) for a in consts]
    return pl.pallas_call(
        _inproj_kernel, grid=grid, in_specs=in_specs, out_specs=out_specs, out_shape=out_shape,
        compiler_params=_params(("parallel", "parallel")), name="inproj",
    )(x, *consts)


def _dsa_kernel(qT_ref, qiT_ref, wT_ref, kidx_ref, c_ref, cT_ref, nb_ref, wuvT_ref, o_ref,
                sc_ref, scb_ref, mm_ref, pref_ref, lg_ref, m_ref, acc_ref):
    i = pl.program_id(1)
    nk = sc_ref.shape[0]
    kf = float(TOPK_MAX)
    G = TK // SUBLANES
    RB = TK // BF16_ROWS

    def rep(fn, a):
        return jnp.broadcast_to(fn(a, axis=0, keepdims=True), (SUBLANES, TQ))

    def full(v, dt=F32):
        return jnp.full((SUBLANES, TQ), v, dt)

    def key_rows(j):
        return pl.ds(pl.multiple_of(j * TK, TK), TK)

    def score(j):
        ks = kidx_ref[key_rows(j), :]
        acc = None
        for h in range(IDX_HEADS):
            d = jnp.dot(ks, qiT_ref[h * IDX_DIM:(h + 1) * IDX_DIM, :], preferred_element_type=F32)
            t = jnp.maximum(d, 0.0) * wT_ref[h:h + 1, :]
            acc = t if acc is None else acc + t
        return acc

    mm_ref[0] = full(jnp.inf)
    mm_ref[1] = full(-jnp.inf)

    def put_scores(j, s, s_for_min):
        sc_ref[j] = s
        scb_ref[j] = s.astype(BF16)
        mm_ref[0] = jnp.minimum(mm_ref[0], _tree(jnp.minimum, s_for_min.reshape(G, SUBLANES, TQ)))
        mm_ref[1] = jnp.maximum(mm_ref[1], _tree(jnp.maximum, s.reshape(G, SUBLANES, TQ)))

    def score_pair(p, carry):
        for j in (2 * p, 2 * p + 1):
            s = score(j)
            put_scores(j, s, s)
        return carry

    lax.fori_loop(0, lax.shift_right_logical(i, 1), score_pair, 0)

    @pl.when((i & 1) == 1)
    def _():
        s = score(i - 1)
        put_scores(i - 1, s, s)

    key_t = lax.broadcasted_iota(jnp.int32, (TK, TQ), 0)
    qry_t = lax.broadcasted_iota(jnp.int32, (TK, TQ), 1)
    causal = key_t <= qry_t
    s_diag = score(i)
    put_scores(i, jnp.where(causal, s_diag, -jnp.inf), jnp.where(causal, s_diag, jnp.inf))

    @pl.when(i == 0)
    def _():
        sc_ref[0] = jnp.where(causal, 0.0, NEG)

    @pl.when(i > 0)
    def _():
        nt = i + 1

        def tile3(j):
            return sc_ref[j].reshape(G, SUBLANES, TQ)

        lo, hi = rep(jnp.min, mm_ref[0]), rep(jnp.max, mm_ref[1])

        def count_ge(thr):
            def body(j, acc):
                return acc + _tree(jnp.add, jnp.where(tile3(j) >= thr[None], 1.0, 0.0))
            return rep(jnp.sum, lax.fori_loop(0, nt, body, full(0.0)))

        bf_step = 2.0 ** -7
        tiny = 1e-30

        def as_bf16_value(x):
            return x.astype(BF16).astype(F32)

        def count_ge_coarse(thr):
            thr16 = jnp.concatenate([thr, thr], axis=0).astype(BF16)
            one, zero = jnp.ones((), BF16), jnp.zeros((), BF16)

            def body(j, acc):
                t = scb_ref[j].reshape(RB, BF16_ROWS, TQ)
                return acc + _tree(jnp.add, jnp.where(t >= thr16[None], one, zero))
            acc = lax.fori_loop(0, nt, body, jnp.zeros((BF16_ROWS, TQ), BF16))
            return rep(jnp.sum, acc.astype(F32))

        def coarse(_, carry):
            lo_c, hi_c = carry
            mid = as_bf16_value(0.5 * (lo_c + hi_c))
            ge = count_ge_coarse(mid) >= kf
            return jnp.where(ge, mid, lo_c), jnp.where(ge, hi_c, mid)

        lo_c = as_bf16_value(lo - jnp.abs(lo) * bf_step - tiny)
        hi_c = as_bf16_value(hi + jnp.abs(hi) * bf_step + tiny)
        lo_c, hi_c = lax.fori_loop(0, N_COARSE, coarse, (lo_c, hi_c))
        lo = lo_c - jnp.abs(lo_c) * bf_step - tiny
        hi = hi_c

        def bisect(_, carry):
            lo, hi, clo = carry
            mid = 0.5 * (lo + hi)
            cnt = count_ge(mid)
            ge = cnt >= kf
            return jnp.where(ge, mid, lo), jnp.where(ge, hi, mid), jnp.where(ge, cnt, clo)

        lo, hi, clo = lax.fori_loop(0, N_BISECT, bisect, (lo, hi, count_ge(lo)))

        def snap_body(j, am):
            s = tile3(j)
            return jnp.minimum(am, _tree(jnp.minimum, jnp.where(s >= lo[None], s, jnp.inf)))

        cur = rep(jnp.min, lax.fori_loop(0, nt, snap_body, full(jnp.inf)))

        def walk(cur):
            def body(j, carry):
                ac, am = carry
                s = tile3(j)
                g = s > cur[None]
                ac = ac + _tree(jnp.add, jnp.where(g, 1.0, 0.0))
                am = jnp.minimum(am, _tree(jnp.minimum, jnp.where(g, s, jnp.inf)))
                return ac, am
            ac, am = lax.fori_loop(0, nt, body, (full(0.0), full(jnp.inf)))
            return rep(jnp.sum, ac), rep(jnp.min, am)

        def walk_cond(carry):
            _, _, _, go, it = carry
            return jnp.logical_and(go > 0, it < nk * TK + 2)

        def walk_body(carry):
            cur, cge, _, _, it = carry
            c, nxt = walk(cur)
            move = c >= kf
            go = (jnp.max(jnp.where(move, 1.0, 0.0)) > 0.5).astype(jnp.int32)
            return jnp.where(move, nxt, cur), jnp.where(move, c, cge), c, go, it + 1

        kth, cge, cgt, _, _ = lax.while_loop(
            walk_cond, walk_body, (cur, clo, full(0.0), jnp.int32(1), jnp.int32(0)))
        need = kf - cgt
        has_excess = jnp.max(jnp.where(cge > kf, 1.0, 0.0)) > 0.5

        @pl.when(jnp.logical_not(has_excess))
        def _():
            def body(j, carry):
                sc_ref[j] = jnp.where(tile3(j) >= kth[None], 0.0, NEG).reshape(TK, TQ)
                return carry
            lax.fori_loop(0, nt, body, 0)

        @pl.when(has_excess)
        def _():
            tril = jnp.where(key_t >= qry_t, 1.0, 0.0).astype(BF16)

            def pref_body(p, carry):
                for j in (2 * p, jnp.minimum(2 * p + 1, nt - 1)):
                    e01 = jnp.where(tile3(j) == kth[None], 1.0, 0.0).reshape(TK, TQ).astype(BF16)
                    pref_ref[j] = jnp.dot(tril, e01, preferred_element_type=F32)
                return carry

            lax.fori_loop(0, lax.shift_right_logical(nt + 1, 1), pref_body, 0)

            def mask_body(j, offset):
                s = tile3(j)
                pref = pref_ref[j]
                rank = pref.reshape(G, SUBLANES, TQ) + offset[None]
                tie = jnp.where(rank <= need[None], 0.0, NEG)
                mb = jnp.where(s > kth[None], 0.0, jnp.where(s == kth[None], tie, NEG))
                sc_ref[j] = mb.reshape(TK, TQ)
                return offset + jnp.broadcast_to(pref[TK - 1:TK, :], (SUBLANES, TQ))

            lax.fori_loop(0, nt, mask_body, full(0.0))

    m_ref[...] = jnp.full(m_ref.shape, NEG, F32)
    acc_ref[...] = jnp.zeros(acc_ref.shape, F32)
    LG = CT_ROWS // SUBLANES

    def near_bias(kind, h):
        z = jnp.zeros((NB, NB), F32)
        b0, b1 = nb_ref[0, h], nb_ref[1, h]
        blocks = [[b0, b1], [z, b0]] if kind == 0 else [[z, z], [b1, z]]
        return jnp.concatenate([jnp.concatenate(r, axis=1) for r in blocks], axis=0)

    def attend(tiles):
        m_run = [m_ref[h] for h in range(DSA_HEADS)]
        stats = {}

        def phase_a(slot, h, ct, mbb, near):
            lg = jnp.dot(ct, qT_ref[h * DSA_LATENT:(h + 1) * DSA_LATENT, :],
                         preferred_element_type=F32)
            if near is not None:
                lg = lg + near_bias(near, h)
            lgb = lg.astype(BF16) + mbb
            lg_ref[slot, h] = lgb
            tmax = _tree(jnp.maximum, lgb.reshape(RB, BF16_ROWS, TQ)).astype(F32)
            m_new = jnp.maximum(m_run[h], rep(jnp.max, tmax))
            stats[slot, h] = (m_run[h], m_new)
            m_run[h] = m_new

        def phase_b(slot, h, ctT):
            m_old, m_new = stats[slot, h]
            alpha = jnp.exp2(m_old - m_new)
            m16 = jnp.concatenate([m_new, m_new], axis=0).astype(BF16)
            x = lg_ref[slot, h].reshape(RB, BF16_ROWS, TQ) - m16[None]
            pv = jnp.dot(ctT, jnp.exp2(x).reshape(TK, TQ), preferred_element_type=F32)
            acc3 = acc_ref[h].reshape(LG, SUBLANES, TQ) * alpha[None]
            acc_ref[h] = acc3.reshape(CT_ROWS, TQ) + pv

        ops = [(c_ref[key_rows(j), :], sc_ref[j].astype(BF16), near, cT_ref[j]) for j, near in tiles]
        units = [(slot, h) for slot in range(len(tiles)) for h in range(DSA_HEADS)]
        for k in range(len(units) + ATT_LAG):
            if k < len(units):
                slot, h = units[k]
                phase_a(slot, h, ops[slot][0], ops[slot][1], ops[slot][2])
            if k >= ATT_LAG:
                slot, h = units[k - ATT_LAG]
                phase_b(slot, h, ops[slot][3])
        for h in range(DSA_HEADS):
            m_ref[h] = m_run[h]

    nfar = jnp.maximum(i - 1, 0)

    def far_group(p, carry):
        attend([(FAR_GROUP * p + t, None) for t in range(FAR_GROUP)])
        return carry

    nfull = lax.shift_right_logical(nfar, FAR_GROUP.bit_length() - 1)
    lax.fori_loop(0, nfull, far_group, 0)
    done = nfull * FAR_GROUP
    size = FAR_GROUP // 2
    while size >= 1:
        @pl.when((nfar & size) != 0)
        def _(done=done, size=size):
            attend([(done + t, None) for t in range(size)])
        done = done + (nfar & size)
        size //= 2

    @pl.when(i > 0)
    def _():
        attend([(i - 1, 1), (i, 0)])

    @pl.when(i == 0)
    def _():
        attend([(0, 0)])

    outs = []
    for h in range(DSA_HEADS):
        rl = 1.0 / acc_ref[h, DSA_LATENT:DSA_LATENT + SUBLANES, :]
        o3 = acc_ref[h, :DSA_LATENT, :].reshape(DSA_LATENT // SUBLANES, SUBLANES, TQ) * rl[None]
        o = o3.reshape(DSA_LATENT, TQ).astype(BF16)
        outs.append(jnp.dot(wuvT_ref[h], o, preferred_element_type=F32))
    o_ref[...] = jnp.concatenate(outs, axis=0).T.astype(BF16)


def _dsa(qT, qiT, wT, kidx, c, cT, nb, wuvT):
    B, S, _ = c.shape
    nk = S // TK
    grid = (B, S // TQ)
    tileT = lambda r: pl.BlockSpec((None, r, TQ), lambda b, i: (b, 0, i))
    seq = lambda w: pl.BlockSpec((None, S, w), lambda b, i: (b, 0, 0))
    in_specs = [tileT(DSA_HEADS * DSA_LATENT), tileT(IDX_HEADS * IDX_DIM), tileT(BF16_ROWS),
                seq(IDX_DIM), seq(DSA_LATENT),
                pl.BlockSpec((None, nk, CT_ROWS, TK), lambda b, i: (b, 0, 0, 0)),
                _const_spec(nb.shape), _const_spec(wuvT.shape)]
    scratch = [
        pltpu.VMEM((nk, TK, TQ), F32),
        pltpu.VMEM((nk, TK, TQ), BF16),
        pltpu.VMEM((2, SUBLANES, TQ), F32),
        pltpu.VMEM((nk, TK, TQ), F32),
        pltpu.VMEM((FAR_GROUP, DSA_HEADS, TK, TQ), BF16),
        pltpu.VMEM((DSA_HEADS, SUBLANES, TQ), F32),
        pltpu.VMEM((DSA_HEADS, CT_ROWS, TQ), F32),
    ]
    return pl.pallas_call(
        _dsa_kernel, grid=grid, in_specs=in_specs,
        out_specs=pl.BlockSpec((None, TQ, DSA_WIDTH), lambda b, i: (b, i, 0)),
        out_shape=jax.ShapeDtypeStruct((B, S, DSA_WIDTH), BF16), scratch_shapes=scratch,
        compiler_params=_params(("parallel", "arbitrary")), name="dsa",
    )(qT, qiT, wT, kidx, c, cT, nb, wuvT)


def _s5_kernel(u_ref, bm_ref, cm_ref, lre_ref, lim_ref, dsk_ref, wglu_ref, o_ref,
               uslab_ref, uil_ref, oslab_ref, hre_ref, him_ref, *st_refs):
    nstate = S5_GROUPS * S5_STATE
    nb = hre_ref.shape[0]
    nslab = S5_WIDTH // LANES

    @pl.when(pl.program_id(0) == 0)
    def _():
        hre_ref[...] = jnp.zeros(hre_ref.shape, F32)
        him_ref[...] = jnp.zeros(him_ref.shape, F32)

    for b in range(nb):
        for k in range(nslab):
            uslab_ref[k, b * S5_PITCH:b * S5_PITCH + T_S5, :] = u_ref[b, :, k * LANES:(k + 1) * LANES]

    tsub = T_S5 // S5_SUB
    rsub = tsub * nb
    st_of = lambda c: st_refs[c % len(st_refs)]

    def gather(c):
        for t in range(c * tsub, (c + 1) * tsub):
            for k in range(nslab):
                uil_ref[t * nb:(t + 1) * nb, k * LANES:(k + 1) * LANES] = (
                    uslab_ref[k, pl.ds(t, nb, stride=S5_PITCH), :])

    def in_matmul(c):
        gather(c)
        ub = uil_ref[c * rsub:(c + 1) * rsub, :].astype(BF16)
        cin = S5_WIDTH // S5_IN_BLOCKS
        sin = nstate // S5_IN_BLOCKS
        for k in range(S5_IN_BLOCKS):
            uk = ub[:, k * cin:(k + 1) * cin]
            for part, off in enumerate((0, nstate)):
                cols = slice(off + k * sin, off + (k + 1) * sin)
                bk = bm_ref[k * cin:(k + 1) * cin, part * sin:(part + 1) * sin]
                st_of(c)[:, cols] = jnp.dot(uk, bk, preferred_element_type=F32)

    def scan(c):
        st_ref = st_of(c)
        half = nstate // 2
        for part in range(2):
            re_sl = slice(part * half, (part + 1) * half)
            im_sl = slice(nstate + part * half, nstate + (part + 1) * half)
            lre = lre_ref[:, re_sl]
            lim = lim_ref[:, re_sl]
            hr, hi = hre_ref[:, re_sl], him_ref[:, re_sl]
            for t in range(tsub):
                r = slice(t * nb, (t + 1) * nb)
                hr, hi = (lre * hr - lim * hi + st_ref[r, re_sl],
                          lre * hi + lim * hr + st_ref[r, im_sl])
                st_ref[r, re_sl] = hr
                st_ref[r, im_sl] = hi
            hre_ref[:, re_sl] = hr
            him_ref[:, re_sl] = hi

    def out_matmul(c):
        st_ref = st_of(c)
        sout = nstate // S5_OUT_BLOCKS
        ys = []
        for k in range(S5_OUT_BLOCKS):
            acc = None
            for off in (0, nstate):
                rows = slice(off + k * sout, off + (k + 1) * sout)
                part = jnp.dot(st_ref[:, rows].astype(BF16), cm_ref[rows, :],
                               preferred_element_type=F32)
                acc = part if acc is None else acc + part
            ys.append(acc)
        y = jnp.concatenate(ys, axis=1)
        y = jax.nn.gelu(y + dsk_ref[...] * uil_ref[c * rsub:(c + 1) * rsub, :])
        z = jnp.dot(y.astype(BF16), wglu_ref[...], preferred_element_type=F32)
        o = y * jax.nn.sigmoid(z)
        for k in range(nslab):
            oslab_ref[k, c * rsub:(c + 1) * rsub, :] = o[:, k * LANES:(k + 1) * LANES]
        for b in range(nb):
            for k in range(nslab):
                o_ref[b, c * tsub:(c + 1) * tsub, k * LANES:(k + 1) * LANES] = (
                    oslab_ref[k, pl.ds(c * rsub + b, tsub, stride=nb), :].astype(BF16))

    nsub = S5_SUB
    for c in range(nsub + 2):
        if c < nsub:
            in_matmul(c)
        if 1 <= c <= nsub:
            scan(c - 1)
        if c >= 2:
            out_matmul(c - 2)


def _s5(u, bm, cm, lre, lim, dsk, wglu):
    B, S, W = u.shape
    tb = T_S5 * B
    nstate = S5_GROUPS * S5_STATE
    tok = pl.BlockSpec((B, T_S5, W), lambda t: (0, t, 0))
    in_specs = [tok, _const_spec(bm.shape), _const_spec(cm.shape), _const_spec(lre.shape),
                _const_spec(lim.shape), _const_spec(dsk.shape), _const_spec(wglu.shape)]
    scratch = [pltpu.VMEM((W // LANES, B * S5_PITCH, LANES), F32),
               pltpu.VMEM((tb, W), F32),
               pltpu.VMEM((W // LANES, tb, LANES), F32),
               pltpu.VMEM((B, nstate), F32), pltpu.VMEM((B, nstate), F32)]
    scratch += [pltpu.VMEM((tb // S5_SUB, 2 * nstate), F32) for _ in range(S5_BUFS)]
    return pl.pallas_call(
        _s5_kernel, grid=(S // T_S5,), in_specs=in_specs, out_specs=tok,
        out_shape=jax.ShapeDtypeStruct((B, S, W), BF16), scratch_shapes=scratch,
        compiler_params=_params(("arbitrary",)), name="s5",
    )(u, bm, cm, lre, lim, dsk, wglu)


def _merge_kernel(x_ref, odsa_ref, os5_ref, mem_ref, gmix_ref, wgT_ref, wqxT_ref, gqx_ref,
                  wb1_ref, wb2_ref, wb3_ref, wout_ref, gmem_ref, wkv_ref, gk_ref, y_ref,
                  k_ref, v_ref):
    @pl.when(pl.program_id(1) == 0)
    def _():
        mb = _rms(mem_ref[...], gmem_ref[...]).astype(BF16)
        kv = jnp.dot(mb, wkv_ref[...], preferred_element_type=F32)
        for h in range(X_HEADS):
            sl = slice(h * X_HEAD_DIM, (h + 1) * X_HEAD_DIM)
            k_ref[:, sl] = _rms(kv[:, sl], gk_ref[...]).astype(BF16)
        v_ref[...] = kv[:, X_WIDTH:].astype(BF16)

    x = x_ref[...]
    hb = _rms(x, gmix_ref[...]).astype(BF16)

    qx = lax.dot_general(hb, wqxT_ref[...], NT_DIMS, preferred_element_type=F32)
    gqx = gqx_ref[...] * (X_HEAD_DIM ** -0.5)
    ox = []
    for h in range(X_HEADS):
        sl = slice(h * X_HEAD_DIM, (h + 1) * X_HEAD_DIM)
        qh = _rms(qx[:, sl], gqx).astype(BF16)
        lg = lax.dot_general(qh, k_ref[:, sl], NT_DIMS, preferred_element_type=F32)
        p = jnp.exp(lg - jnp.max(lg, axis=-1, keepdims=True))
        pv = jnp.dot(p.astype(BF16), v_ref[:, sl], preferred_element_type=F32)
        ox.append((pv / jnp.sum(p, axis=-1, keepdims=True)).astype(BF16))
    ox = jnp.concatenate(ox, axis=1)

    merged = None
    for br, (o, wb) in enumerate(((odsa_ref[...], wb1_ref), (os5_ref[...], wb2_ref), (ox, wb3_ref))):
        gate = jax.nn.sigmoid(lax.dot_general(hb, wgT_ref[br * D_MODEL:(br + 1) * D_MODEL, :],
                                              NT_DIMS, preferred_element_type=F32))
        term = gate * jnp.dot(o, wb[...], preferred_element_type=F32)
        merged = term if merged is None else merged + term
    y_ref[...] = x + jnp.dot(merged.astype(BF16), wout_ref[...], preferred_element_type=F32)


def _merge(x, odsa, os5, mem, gmix, wgT, wqxT, gqx, wb1, wb2, wb3, wout, gmem, wkv, gk):
    B, S, D = x.shape
    ts = TS_MERGE
    tok = lambda w: pl.BlockSpec((None, ts, w), lambda b, s: (b, s, 0))
    consts = (gmix, wgT, wqxT, gqx, wb1, wb2, wb3, wout, gmem, wkv, gk)
    in_specs = [tok(D), tok(DSA_WIDTH), tok(S5_WIDTH),
                pl.BlockSpec((None, N_MEM, D), lambda b, s: (b, 0, 0))
                ] + [_const_spec(a.shape) for a in consts]
    scratch = [pltpu.VMEM((N_MEM, X_WIDTH), BF16), pltpu.VMEM((N_MEM, X_WIDTH), BF16)]
    return pl.pallas_call(
        _merge_kernel, grid=(B, S // ts), in_specs=in_specs, out_specs=tok(D),
        out_shape=jax.ShapeDtypeStruct((B, S, D), F32), scratch_shapes=scratch,
        compiler_params=_params(("parallel", "arbitrary")), name="merge",
    )(x, odsa, os5, mem, *consts)


def _ffn_kernel(x_ref, g_ref, wg_ref, wu_ref, wd_ref, y_ref):
    x = x_ref[...]
    hb = _rms(x, g_ref[...]).astype(BF16)
    a = jnp.dot(hb, wg_ref[...], preferred_element_type=F32)
    b = jnp.dot(hb, wu_ref[...], preferred_element_type=F32)
    act = (jax.nn.silu(a) * b).astype(BF16)
    y_ref[...] = x + jnp.dot(act, wd_ref[...], preferred_element_type=F32)


def _ffn(x2, g, wg, wu, wd):
    n, D = x2.shape
    tok = pl.BlockSpec((TS_FFN, D), lambda t: (t, 0))
    return pl.pallas_call(
        _ffn_kernel, grid=(n // TS_FFN,),
        in_specs=[tok] + [_const_spec(a.shape) for a in (g, wg, wu, wd)],
        out_specs=tok, out_shape=jax.ShapeDtypeStruct((n, D), F32),
        compiler_params=_params(("parallel",)), name="ffn",
    )(x2, g, wg, wu, wd)


def _t5_bucket(n):
    max_exact = REL_BUCKETS // 2
    nf = jnp.maximum(n, 1).astype(F32)
    large = max_exact + (jnp.log(nf / max_exact) / math.log(REL_MAX_DIST / max_exact)
                         * (REL_BUCKETS - max_exact)).astype(jnp.int32)
    large = jnp.minimum(large, REL_BUCKETS - 1)
    return jnp.where(n < max_exact, n, large)


def _toeplitz(w, rows, cols):
    H, L = w.shape
    flat = jnp.tile(w, (1, rows))[:, :rows * (L - 1)]
    return flat.reshape(H, rows, L - 1)[:, :, :cols]


def _near_bias(rel_bias):
    n = jnp.arange(2 * NB, dtype=jnp.int32)
    f = (rel_bias[_t5_bucket(n)] - rel_bias[REL_BUCKETS - 1][None, :]).T * LOG2E
    w0 = jnp.concatenate([f[:, :NB], jnp.broadcast_to(f[:, :1], (f.shape[0], NB - 1))], axis=1)
    w1 = jnp.concatenate([f[:, NB:2 * NB], f[:, 1:NB]], axis=1)
    return jnp.stack([_toeplitz(w0, NB, NB), _toeplitz(w1, NB, NB)], axis=0).astype(F32)


def _s5_mats(a_re, a_im, log_dt, b_re, b_im, c_re, c_im):
    lam = lax.complex(a_re.astype(F32), a_im.astype(F32))
    dt = jnp.exp(log_dt.astype(F32))[:, None]
    lam_bar = jnp.exp(lam * dt)
    b_bar = ((lam_bar - 1.0) / lam)[..., None] * lax.complex(b_re.astype(F32), b_im.astype(F32))
    nstate = S5_GROUPS * S5_STATE
    gin, gout = S5_GROUPS // S5_IN_BLOCKS, S5_GROUPS // S5_OUT_BLOCKS

    def blockdiag_in(w):
        t = jnp.transpose(w, (0, 2, 1)).reshape(S5_WIDTH, S5_STATE)
        mask = ((jnp.arange(S5_WIDTH)[:, None] // S5_GROUP) % gin
                == jnp.arange(gin * S5_STATE)[None, :] // S5_STATE)
        return jnp.where(mask, jnp.tile(t, (1, gin)), 0.0)

    def blockdiag_out(w):
        t = jnp.transpose(w, (0, 2, 1)).reshape(nstate, S5_GROUP)
        mask = ((jnp.arange(nstate)[:, None] // S5_STATE) % gout
                == jnp.arange(gout * S5_GROUP)[None, :] // S5_GROUP)
        return jnp.where(mask, jnp.tile(t, (1, gout)), 0.0)

    bm = jnp.concatenate([blockdiag_in(jnp.real(b_bar)), blockdiag_in(jnp.imag(b_bar))], axis=1)
    cm = jnp.concatenate([blockdiag_out(c_re.astype(F32)), blockdiag_out(-c_im.astype(F32))], axis=0)
    return bm.astype(BF16), cm.astype(BF16), jnp.real(lam_bar).reshape(1, -1), jnp.imag(lam_bar).reshape(1, -1)


def kernel(x, mem, rel_bias, w_in, g_mix_norm, g_q_dsa, g_kv_dsa, w_uv_dsa, a_re, a_im, log_dt, b_re, b_im, c_re, c_im, d_skip, w_glu, g_mem_norm, w_mem_kv, g_q_cross, g_k_cross, w_br_dsa, w_br_s5, w_br_cross, w_out, g_ffn_norm, w_ffn_gate, w_ffn_up, w_ffn_down):
    B, S, D = x.shape
    depth = w_in.shape[0]
    assert D == D_MODEL and w_in.shape[2] == sum(IN_SPLITS) and mem.shape[1] == N_MEM
    assert TQ == TK == 2 * NB and min(TOPK_MAX, S // 4) == TOPK_MAX and TQ <= TOPK_MAX
    assert S % TS_IN == 0 and S % T_S5 == 0 and S % TS_MERGE == 0 and (B * S) % TS_FFN == 0
    assert T_S5 % S5_SUB == 0 and S5_PITCH % (2 * SUBLANES) == SUBLANES
    offs = [0] + [int(o) for o in np.cumsum(IN_SPLITS)]
    nb = _near_bias(rel_bias)
    row = lambda v: v.reshape(1, -1).astype(F32)
    col = lambda v: v.reshape(-1, 1).astype(F32)
    bf = lambda a: a.astype(BF16)
    for l in range(depth):
        w_inT = jnp.transpose(w_in[l])
        wqT, wcT, wqiT, wkT, wwT, wuT, wqxT, wgT = [bf(w_inT[offs[k]:offs[k + 1]])
                                                    for k in range(len(IN_SPLITS))]
        wckwT = jnp.pad(jnp.concatenate([wcT, wkT, wwT], axis=0),
                        ((0, 2 * LANES - DSA_LATENT - IDX_DIM - IDX_HEADS), (0, 0)))
        gqc = col(g_q_dsa[l]) * (DSA_LATENT ** -0.5 * LOG2E)

        qT, cT, c, qiT, wT, kidx, u = _inproj(
            x, row(g_mix_norm[l]), wqT, wckwT, wqiT, wuT, gqc, row(g_kv_dsa[l]))

        wuvT = bf(jnp.transpose(w_uv_dsa[l], (0, 2, 1)))
        o_dsa = _dsa(qT, qiT, wT, kidx, c, cT, nb, wuvT)

        bm, cm, lre, lim = _s5_mats(a_re[l], a_im[l], log_dt[l], b_re[l], b_im[l], c_re[l], c_im[l])
        lre = jnp.broadcast_to(lre, (B, lre.shape[1]))
        lim = jnp.broadcast_to(lim, (B, lim.shape[1]))
        o_s5 = _s5(u, bm, cm, lre, lim, row(d_skip[l]), bf(w_glu[l]))

        x1 = _merge(x, o_dsa, o_s5, mem, row(g_mix_norm[l]), wgT, wqxT, row(g_q_cross[l]),
                    bf(w_br_dsa[l]), bf(w_br_s5[l]), bf(w_br_cross[l]), bf(w_out[l]),
                    row(g_mem_norm[l]), bf(w_mem_kv[l]), row(g_k_cross[l]))

        x = _ffn(x1.reshape(B * S, D), row(g_ffn_norm[l]), bf(w_ffn_gate[l]), bf(w_ffn_up[l]),
                 bf(w_ffn_down[l])).reshape(B, S, D)
    return x
```

```python
import math

import jax
import jax.numpy as jnp
import numpy as np
from jax import lax
from jax.experimental import pallas as pl
from jax.experimental.pallas import tpu as pltpu

F32 = jnp.float32
BF16 = jnp.bfloat16

D_MODEL = 1024
N_MEM = 256
EPS = 1e-6
DSA_HEADS = 8
DSA_LATENT = 128
DSA_VDIM = 64
IDX_HEADS = 8
IDX_DIM = 64
TOPK_MAX = 256
REL_BUCKETS = 32
REL_MAX_DIST = 128
S5_WIDTH = 512
S5_GROUP = 16
S5_GROUPS = S5_WIDTH // S5_GROUP
S5_STATE = 64
X_HEADS = 4
X_HEAD_DIM = 128
DSA_WIDTH = DSA_HEADS * DSA_VDIM
X_WIDTH = X_HEADS * X_HEAD_DIM
N_BRANCH = 3
IN_SPLITS = (DSA_HEADS * DSA_LATENT, DSA_LATENT, IDX_HEADS * IDX_DIM, IDX_DIM,
             IDX_HEADS, S5_WIDTH, X_WIDTH, N_BRANCH * D_MODEL)

LANES = 128
SUBLANES = 8
BF16_ROWS = 16
CT_ROWS = DSA_LATENT + BF16_ROWS
VMEM_LIMIT = 56 * 1024 * 1024
LOG2E = math.log2(math.e)

TS_IN = 1024
TQ = 256
TK = 256
FAR_GROUP = 2
ATT_LAG = 8
NB = REL_MAX_DIST
N_COARSE = 8
N_BISECT = 8
T_S5 = 128
S5_SUB = 4
S5_IN_BLOCKS = 4
S5_OUT_BLOCKS = 2
S5_PITCH = T_S5 + SUBLANES
TS_MERGE = 1024
TS_FFN = 512
NEG = -(2.0 ** 100)

NT_DIMS = (((1,), (1,)), ((), ()))


def _rms(x, g):
    ms = jnp.mean(x * x, axis=-1, keepdims=True)
    return x * lax.rsqrt(ms + EPS) * g


def _tree(fn, x):
    while x.shape[0] > 1:
        half = x.shape[0] // 2
        x = fn(x[:half], x[half:])
    return x[0]


def _const_spec(shape):
    nd = len(shape)
    return pl.BlockSpec(shape, lambda *_: (0,) * nd, pipeline_mode=pl.Buffered(1))


def _params(sem):
    return pltpu.CompilerParams(dimension_semantics=sem, vmem_limit_bytes=VMEM_LIMIT)


IN_OFFS = tuple(int(o) for o in np.cumsum((0,) + IN_SPLITS))
IN_HEAD_ROWS = IN_OFFS[3] + LANES


def _inproj_kernel(x_ref, gmix_ref, w_ref, wuT_ref, gqc_ref, gkv_ref,
                   qT_ref, cT_ref, c_ref, qiT_ref, wT_ref, kidx_ref, u_ref):
    ts = x_ref.shape[0]
    hb = _rms(x_ref[...], gmix_ref[...]).astype(BF16)
    proj = lambda wT: lax.dot_general(hb, wT, NT_DIMS, preferred_element_type=F32)

    qT = proj(w_ref[IN_OFFS[0]:IN_OFFS[1], :]).T
    q3 = qT.reshape(DSA_HEADS, DSA_LATENT, ts)
    ms = jnp.mean(q3 * q3, axis=1, keepdims=True)
    qT_ref[...] = (q3 * lax.rsqrt(ms + EPS) * gqc_ref[...][None]).reshape(qT.shape).astype(BF16)

    ckw = proj(jnp.concatenate([w_ref[IN_OFFS[1]:IN_OFFS[2], :],
                                w_ref[IN_OFFS[3]:IN_OFFS[3] + LANES, :]], axis=0))
    kw = ckw[:, DSA_LATENT:]
    kidx_ref[...] = kw[:, :IDX_DIM].astype(BF16)
    wT_ref[...] = kw.T[IDX_DIM:IDX_DIM + BF16_ROWS, :]

    cn = _rms(ckw[:, :DSA_LATENT], gkv_ref[...])
    c_ref[...] = cn.astype(BF16)
    cTn = cn.T.astype(BF16)
    for k in range(ts // TK):
        cT_ref[k, :DSA_LATENT, :] = cTn[:, k * TK:(k + 1) * TK]
        cT_ref[k, DSA_LATENT:, :] = jnp.ones((BF16_ROWS, TK), BF16)

    qiT_ref[...] = proj(w_ref[IN_OFFS[2]:IN_OFFS[3], :]).T.astype(BF16)

    u_ref[...] = proj(wuT_ref[...])


def _inproj(x, gmix, w_inT, wuT, gqc, gkv):
    B, S, D = x.shape
    assert all(IN_OFFS[k] % BF16_ROWS == 0 for k in range(4)) and DSA_LATENT == LANES
    ts = TS_IN
    grid = (B, S // ts)
    tok = lambda w: pl.BlockSpec((None, ts, w), lambda b, s: (b, s, 0))
    tokT = lambda r: pl.BlockSpec((None, r, ts), lambda b, s: (b, 0, s))
    hq = DSA_HEADS * DSA_LATENT
    hi = IDX_HEADS * IDX_DIM
    out_shape = (
        jax.ShapeDtypeStruct((B, hq, S), BF16),
        jax.ShapeDtypeStruct((B, S // TK, CT_ROWS, TK), BF16),
        jax.ShapeDtypeStruct((B, S, DSA_LATENT), BF16),
        jax.ShapeDtypeStruct((B, hi, S), BF16),
        jax.ShapeDtypeStruct((B, BF16_ROWS, S), F32),
        jax.ShapeDtypeStruct((B, S, IDX_DIM), BF16),
        jax.ShapeDtypeStruct((B, S, S5_WIDTH), F32),
    )
    out_specs = (tokT(hq),
                 pl.BlockSpec((None, ts // TK, CT_ROWS, TK), lambda b, s: (b, s, 0, 0)),
                 tok(DSA_LATENT), tokT(hi), tokT(BF16_ROWS), tok(IDX_DIM),
                 tok(S5_WIDTH))
    in_specs = [tok(D), _const_spec(gmix.shape), _const_spec((IN_HEAD_ROWS, D)),
                _const_spec(wuT.shape), _const_spec(gqc.shape), _const_spec(gkv.shape)]
    return pl.pallas_call(
        _inproj_kernel, grid=grid, in_specs=in_specs, out_specs=out_specs, out_shape=out_shape,
        compiler_params=_params(("parallel", "parallel")), name="inproj",
    )(x, gmix, w_inT, wuT, gqc, gkv)


def _dsa_kernel(qT_ref, qiT_ref, wT_ref, kidx_ref, c_ref, cT_ref, nb_ref, wuvT_ref, o_ref,
                sc_ref, scb_ref, mm_ref, pref_ref, lg_ref, m_ref, acc_ref):
    i = pl.program_id(1)
    nk = sc_ref.shape[0]
    kf = float(TOPK_MAX)
    G = TK // SUBLANES
    RB = TK // BF16_ROWS

    def rep(fn, a):
        return jnp.broadcast_to(fn(a, axis=0, keepdims=True), (SUBLANES, TQ))

    def full(v, dt=F32):
        return jnp.full((SUBLANES, TQ), v, dt)

    def key_rows(j):
        return pl.ds(pl.multiple_of(j * TK, TK), TK)

    def score(j):
        ks = kidx_ref[key_rows(j), :]
        acc = None
        for h in range(IDX_HEADS):
            d = jnp.dot(ks, qiT_ref[h * IDX_DIM:(h + 1) * IDX_DIM, :], preferred_element_type=F32)
            t = jnp.maximum(d, 0.0) * wT_ref[h:h + 1, :]
            acc = t if acc is None else acc + t
        return acc

    mm_ref[0] = full(jnp.inf)
    mm_ref[1] = full(-jnp.inf)

    def put_scores(j, s, s_for_min):
        sc_ref[j] = s
        scb_ref[j] = s.astype(BF16)
        mm_ref[0] = jnp.minimum(mm_ref[0], _tree(jnp.minimum, s_for_min.reshape(G, SUBLANES, TQ)))
        mm_ref[1] = jnp.maximum(mm_ref[1], _tree(jnp.maximum, s.reshape(G, SUBLANES, TQ)))

    def score_pair(p, carry):
        for j in (2 * p, 2 * p + 1):
            s = score(j)
            put_scores(j, s, s)
        return carry

    lax.fori_loop(0, lax.shift_right_logical(i, 1), score_pair, 0)

    @pl.when((i & 1) == 1)
    def _():
        s = score(i - 1)
        put_scores(i - 1, s, s)

    key_t = lax.broadcasted_iota(jnp.int32, (TK, TQ), 0)
    qry_t = lax.broadcasted_iota(jnp.int32, (TK, TQ), 1)
    causal = key_t <= qry_t
    s_diag = score(i)
    put_scores(i, jnp.where(causal, s_diag, -jnp.inf), jnp.where(causal, s_diag, jnp.inf))

    @pl.when(i == 0)
    def _():
        sc_ref[0] = jnp.where(causal, 0.0, NEG)

    @pl.when(i > 0)
    def _():
        nt = i + 1

        def tile3(j):
            return sc_ref[j].reshape(G, SUBLANES, TQ)

        lo, hi = rep(jnp.min, mm_ref[0]), rep(jnp.max, mm_ref[1])

        def count_ge(thr):
            def body(j, acc):
                return acc + _tree(jnp.add, jnp.where(tile3(j) >= thr[None], 1.0, 0.0))
            return rep(jnp.sum, lax.fori_loop(0, nt, body, full(0.0)))

        bf_step = 2.0 ** -7
        tiny = 1e-30

        def as_bf16_value(x):
            return x.astype(BF16).astype(F32)

        def count_ge_coarse(thr):
            thr16 = jnp.concatenate([thr, thr], axis=0).astype(BF16)
            one, zero = jnp.ones((), BF16), jnp.zeros((), BF16)

            def body(j, acc):
                t = scb_ref[j].reshape(RB, BF16_ROWS, TQ)
                return acc + _tree(jnp.add, jnp.where(t >= thr16[None], one, zero))
            acc = lax.fori_loop(0, nt, body, jnp.zeros((BF16_ROWS, TQ), BF16))
            return rep(jnp.sum, acc.astype(F32))

        def coarse(_, carry):
            lo_c, hi_c = carry
            mid = as_bf16_value(0.5 * (lo_c + hi_c))
            ge = count_ge_coarse(mid) >= kf
            return jnp.where(ge, mid, lo_c), jnp.where(ge, hi_c, mid)

        lo_c = as_bf16_value(lo - jnp.abs(lo) * bf_step - tiny)
        hi_c = as_bf16_value(hi + jnp.abs(hi) * bf_step + tiny)
        lo_c, hi_c = lax.fori_loop(0, N_COARSE, coarse, (lo_c, hi_c))
        lo = lo_c - jnp.abs(lo_c) * bf_step - tiny
        hi = hi_c

        def bisect(_, carry):
            lo, hi, clo = carry
            mid = 0.5 * (lo + hi)
            cnt = count_ge(mid)
            ge = cnt >= kf
            return jnp.where(ge, mid, lo), jnp.where(ge, hi, mid), jnp.where(ge, cnt, clo)

        lo, hi, clo = lax.fori_loop(0, N_BISECT, bisect, (lo, hi, count_ge(lo)))

        def snap_body(j, am):
            s = tile3(j)
            return jnp.minimum(am, _tree(jnp.minimum, jnp.where(s >= lo[None], s, jnp.inf)))

        cur = rep(jnp.min, lax.fori_loop(0, nt, snap_body, full(jnp.inf)))

        def walk(cur):
            def body(j, carry):
                ac, am = carry
                s = tile3(j)
                g = s > cur[None]
                ac = ac + _tree(jnp.add, jnp.where(g, 1.0, 0.0))
                am = jnp.minimum(am, _tree(jnp.minimum, jnp.where(g, s, jnp.inf)))
                return ac, am
            ac, am = lax.fori_loop(0, nt, body, (full(0.0), full(jnp.inf)))
            return rep(jnp.sum, ac), rep(jnp.min, am)

        def walk_cond(carry):
            _, _, _, go, it = carry
            return jnp.logical_and(go > 0, it < nk * TK + 2)

        def walk_body(carry):
            cur, cge, _, _, it = carry
            c, nxt = walk(cur)
            move = c >= kf
            go = (jnp.max(jnp.where(move, 1.0, 0.0)) > 0.5).astype(jnp.int32)
            return jnp.where(move, nxt, cur), jnp.where(move, c, cge), c, go, it + 1

        kth, cge, cgt, _, _ = lax.while_loop(
            walk_cond, walk_body, (cur, clo, full(0.0), jnp.int32(1), jnp.int32(0)))
        need = kf - cgt
        has_excess = jnp.max(jnp.where(cge > kf, 1.0, 0.0)) > 0.5

        @pl.when(jnp.logical_not(has_excess))
        def _():
            def body(j, carry):
                sc_ref[j] = jnp.where(tile3(j) >= kth[None], 0.0, NEG).reshape(TK, TQ)
                return carry
            lax.fori_loop(0, nt, body, 0)

        @pl.when(has_excess)
        def _():
            tril = jnp.where(key_t >= qry_t, 1.0, 0.0).astype(BF16)

            def pref_body(p, carry):
                for j in (2 * p, jnp.minimum(2 * p + 1, nt - 1)):
                    e01 = jnp.where(tile3(j) == kth[None], 1.0, 0.0).reshape(TK, TQ).astype(BF16)
                    pref_ref[j] = jnp.dot(tril, e01, preferred_element_type=F32)
                return carry

            lax.fori_loop(0, lax.shift_right_logical(nt + 1, 1), pref_body, 0)

            def mask_body(j, offset):
                s = tile3(j)
                pref = pref_ref[j]
                rank = pref.reshape(G, SUBLANES, TQ) + offset[None]
                tie = jnp.where(rank <= need[None], 0.0, NEG)
                mb = jnp.where(s > kth[None], 0.0, jnp.where(s == kth[None], tie, NEG))
                sc_ref[j] = mb.reshape(TK, TQ)
                return offset + jnp.broadcast_to(pref[TK - 1:TK, :], (SUBLANES, TQ))

            lax.fori_loop(0, nt, mask_body, full(0.0))

    m_ref[...] = jnp.full(m_ref.shape, NEG, F32)
    acc_ref[...] = jnp.zeros(acc_ref.shape, F32)
    LG = CT_ROWS // SUBLANES

    def near_bias(kind, h):
        z = jnp.zeros((NB, NB), F32)
        b0, b1 = nb_ref[0, h], nb_ref[1, h]
        blocks = [[b0, b1], [z, b0]] if kind == 0 else [[z, z], [b1, z]]
        return jnp.concatenate([jnp.concatenate(r, axis=1) for r in blocks], axis=0)

    def attend(tiles):
        m_run = [m_ref[h] for h in range(DSA_HEADS)]
        stats = {}

        def phase_a(slot, h, ct, mbb, near):
            lg = jnp.dot(ct, qT_ref[h * DSA_LATENT:(h + 1) * DSA_LATENT, :],
                         preferred_element_type=F32)
            if near is not None:
                lg = lg + near_bias(near, h)
            lgb = lg.astype(BF16) + mbb
            lg_ref[slot, h] = lgb
            tmax = _tree(jnp.maximum, lgb.reshape(RB, BF16_ROWS, TQ)).astype(F32)
            m_new = jnp.maximum(m_run[h], rep(jnp.max, tmax))
            stats[slot, h] = (m_run[h], m_new)
            m_run[h] = m_new

        def phase_b(slot, h, ctT):
            m_old, m_new = stats[slot, h]
            alpha = jnp.exp2(m_old - m_new)
            m16 = jnp.concatenate([m_new, m_new], axis=0).astype(BF16)
            x = lg_ref[slot, h].reshape(RB, BF16_ROWS, TQ) - m16[None]
            pv = jnp.dot(ctT, jnp.exp2(x).reshape(TK, TQ), preferred_element_type=F32)
            acc3 = acc_ref[h].reshape(LG, SUBLANES, TQ) * alpha[None]
            acc_ref[h] = acc3.reshape(CT_ROWS, TQ) + pv

        ops = [(c_ref[key_rows(j), :], sc_ref[j].astype(BF16), near, cT_ref[j]) for j, near in tiles]
        units = [(slot, h) for slot in range(len(tiles)) for h in range(DSA_HEADS)]
        for k in range(len(units) + ATT_LAG):
            if k < len(units):
                slot, h = units[k]
                phase_a(slot, h, ops[slot][0], ops[slot][1], ops[slot][2])
            if k >= ATT_LAG:
                slot, h = units[k - ATT_LAG]
                phase_b(slot, h, ops[slot][3])
        for h in range(DSA_HEADS):
            m_ref[h] = m_run[h]

    nfar = jnp.maximum(i - 1, 0)

    def far_group(p, carry):
        attend([(FAR_GROUP * p + t, None) for t in range(FAR_GROUP)])
        return carry

    nfull = lax.shift_right_logical(nfar, FAR_GROUP.bit_length() - 1)
    lax.fori_loop(0, nfull, far_group, 0)
    done = nfull * FAR_GROUP
    size = FAR_GROUP // 2
    while size >= 1:
        @pl.when((nfar & size) != 0)
        def _(done=done, size=size):
            attend([(done + t, None) for t in range(size)])
        done = done + (nfar & size)
        size //= 2

    @pl.when(i > 0)
    def _():
        attend([(i - 1, 1), (i, 0)])

    @pl.when(i == 0)
    def _():
        attend([(0, 0)])

    outs = []
    for h in range(DSA_HEADS):
        rl = 1.0 / acc_ref[h, DSA_LATENT:DSA_LATENT + SUBLANES, :]
        o3 = acc_ref[h, :DSA_LATENT, :].reshape(DSA_LATENT // SUBLANES, SUBLANES, TQ) * rl[None]
        o = o3.reshape(DSA_LATENT, TQ).astype(BF16)
        outs.append(jnp.dot(wuvT_ref[h], o, preferred_element_type=F32))
    o_ref[...] = jnp.concatenate(outs, axis=0).T.astype(BF16)


def _dsa(qT, qiT, wT, kidx, c, cT, nb, wuvT):
    B, S, _ = c.shape
    nk = S // TK
    grid = (B, S // TQ)
    tileT = lambda r: pl.BlockSpec((None, r, TQ), lambda b, i: (b, 0, i))
    seq = lambda w: pl.BlockSpec((None, S, w), lambda b, i: (b, 0, 0))
    in_specs = [tileT(DSA_HEADS * DSA_LATENT), tileT(IDX_HEADS * IDX_DIM), tileT(BF16_ROWS),
                seq(IDX_DIM), seq(DSA_LATENT),
                pl.BlockSpec((None, nk, CT_ROWS, TK), lambda b, i: (b, 0, 0, 0)),
                _const_spec(nb.shape), _const_spec(wuvT.shape)]
    scratch = [
        pltpu.VMEM((nk, TK, TQ), F32),
        pltpu.VMEM((nk, TK, TQ), BF16),
        pltpu.VMEM((2, SUBLANES, TQ), F32),
        pltpu.VMEM((nk, TK, TQ), F32),
        pltpu.VMEM((FAR_GROUP, DSA_HEADS, TK, TQ), BF16),
        pltpu.VMEM((DSA_HEADS, SUBLANES, TQ), F32),
        pltpu.VMEM((DSA_HEADS, CT_ROWS, TQ), F32),
    ]
    return pl.pallas_call(
        _dsa_kernel, grid=grid, in_specs=in_specs,
        out_specs=pl.BlockSpec((None, TQ, DSA_WIDTH), lambda b, i: (b, i, 0)),
        out_shape=jax.ShapeDtypeStruct((B, S, DSA_WIDTH), BF16), scratch_shapes=scratch,
        compiler_params=_params(("parallel", "arbitrary")), name="dsa",
    )(qT, qiT, wT, kidx, c, cT, nb, wuvT)


def _s5_kernel(u_ref, bm_ref, cm_ref, lre_ref, lim_ref, dsk_ref, wglu_ref, o_ref,
               uslab_ref, uil_ref, oslab_ref, hre_ref, him_ref, *st_refs):
    nstate = S5_GROUPS * S5_STATE
    nb = hre_ref.shape[0]
    nslab = S5_WIDTH // LANES

    @pl.when(pl.program_id(0) == 0)
    def _():
        hre_ref[...] = jnp.zeros(hre_ref.shape, F32)
        him_ref[...] = jnp.zeros(him_ref.shape, F32)

    for b in range(nb):
        for k in range(nslab):
            uslab_ref[k, b * S5_PITCH:b * S5_PITCH + T_S5, :] = u_ref[b, :, k * LANES:(k + 1) * LANES]

    tsub = T_S5 // len(st_refs)
    rsub = tsub * nb

    def gather(c):
        for t in range(c * tsub, (c + 1) * tsub):
            for k in range(nslab):
                uil_ref[t * nb:(t + 1) * nb, k * LANES:(k + 1) * LANES] = (
                    uslab_ref[k, pl.ds(t, nb, stride=S5_PITCH), :])

    def in_matmul(c):
        gather(c)
        ub = uil_ref[c * rsub:(c + 1) * rsub, :].astype(BF16)
        cin = S5_WIDTH // S5_IN_BLOCKS
        sin = nstate // S5_IN_BLOCKS
        for k in range(S5_IN_BLOCKS):
            uk = ub[:, k * cin:(k + 1) * cin]
            for part, off in enumerate((0, nstate)):
                cols = slice(off + k * sin, off + (k + 1) * sin)
                bk = bm_ref[k * cin:(k + 1) * cin, part * sin:(part + 1) * sin]
                st_refs[c][:, cols] = jnp.dot(uk, bk, preferred_element_type=F32)

    def scan(c):
        st_ref = st_refs[c]
        half = nstate // 2
        for part in range(2):
            re_sl = slice(part * half, (part + 1) * half)
            im_sl = slice(nstate + part * half, nstate + (part + 1) * half)
            lre = lre_ref[:, re_sl]
            lim = lim_ref[:, re_sl]
            hr, hi = hre_ref[:, re_sl], him_ref[:, re_sl]
            for t in range(tsub):
                r = slice(t * nb, (t + 1) * nb)
                hr, hi = (lre * hr - lim * hi + st_ref[r, re_sl],
                          lre * hi + lim * hr + st_ref[r, im_sl])
                st_ref[r, re_sl] = hr
                st_ref[r, im_sl] = hi
            hre_ref[:, re_sl] = hr
            him_ref[:, re_sl] = hi

    def out_matmul(c):
        st_ref = st_refs[c]
        sout = nstate // S5_OUT_BLOCKS
        ys = []
        for k in range(S5_OUT_BLOCKS):
            acc = None
            for off in (0, nstate):
                rows = slice(off + k * sout, off + (k + 1) * sout)
                part = jnp.dot(st_ref[:, rows].astype(BF16), cm_ref[rows, :],
                               preferred_element_type=F32)
                acc = part if acc is None else acc + part
            ys.append(acc)
        y = jnp.concatenate(ys, axis=1)
        y = jax.nn.gelu(y + dsk_ref[...] * uil_ref[c * rsub:(c + 1) * rsub, :])
        z = jnp.dot(y.astype(BF16), wglu_ref[...], preferred_element_type=F32)
        o = y * jax.nn.sigmoid(z)
        for k in range(nslab):
            oslab_ref[k, c * rsub:(c + 1) * rsub, :] = o[:, k * LANES:(k + 1) * LANES]
        for b in range(nb):
            for k in range(nslab):
                o_ref[b, c * tsub:(c + 1) * tsub, k * LANES:(k + 1) * LANES] = (
                    oslab_ref[k, pl.ds(c * rsub + b, tsub, stride=nb), :].astype(BF16))

    nsub = len(st_refs)
    for c in range(nsub + 2):
        if c < nsub:
            in_matmul(c)
        if 1 <= c <= nsub:
            scan(c - 1)
        if c >= 2:
            out_matmul(c - 2)


def _s5(u, bm, cm, lre, lim, dsk, wglu):
    B, S, W = u.shape
    tb = T_S5 * B
    nstate = S5_GROUPS * S5_STATE
    tok = pl.BlockSpec((B, T_S5, W), lambda t: (0, t, 0))
    in_specs = [tok, _const_spec(bm.shape), _const_spec(cm.shape), _const_spec(lre.shape),
                _const_spec(lim.shape), _const_spec(dsk.shape), _const_spec(wglu.shape)]
    scratch = [pltpu.VMEM((W // LANES, B * S5_PITCH, LANES), F32),
               pltpu.VMEM((tb, W), F32),
               pltpu.VMEM((W // LANES, tb, LANES), F32),
               pltpu.VMEM((B, nstate), F32), pltpu.VMEM((B, nstate), F32)]
    scratch += [pltpu.VMEM((tb // S5_SUB, 2 * nstate), F32) for _ in range(S5_SUB)]
    return pl.pallas_call(
        _s5_kernel, grid=(S // T_S5,), in_specs=in_specs, out_specs=tok,
        out_shape=jax.ShapeDtypeStruct((B, S, W), BF16), scratch_shapes=scratch,
        compiler_params=_params(("arbitrary",)), name="s5",
    )(u, bm, cm, lre, lim, dsk, wglu)


def _merge_kernel(x_ref, odsa_ref, os5_ref, mem_ref, gmix_ref, wgT_ref, wqxT_ref, gqx_ref,
                  wb1_ref, wb2_ref, wb3_ref, wout_ref, gmem_ref, wkv_ref, gk_ref, y_ref,
                  k_ref, v_ref):
    @pl.when(pl.program_id(1) == 0)
    def _():
        mb = _rms(mem_ref[...], gmem_ref[...]).astype(BF16)
        kv = jnp.dot(mb, wkv_ref[...], preferred_element_type=F32)
        for h in range(X_HEADS):
            sl = slice(h * X_HEAD_DIM, (h + 1) * X_HEAD_DIM)
            k_ref[:, sl] = _rms(kv[:, sl], gk_ref[...]).astype(BF16)
        v_ref[...] = kv[:, X_WIDTH:].astype(BF16)

    x = x_ref[...]
    hb = _rms(x, gmix_ref[...]).astype(BF16)

    qx = lax.dot_general(hb, wqxT_ref[...], NT_DIMS, preferred_element_type=F32)
    gqx = gqx_ref[...] * (X_HEAD_DIM ** -0.5)
    ox = []
    for h in range(X_HEADS):
        sl = slice(h * X_HEAD_DIM, (h + 1) * X_HEAD_DIM)
        qh = _rms(qx[:, sl], gqx).astype(BF16)
        lg = lax.dot_general(qh, k_ref[:, sl], NT_DIMS, preferred_element_type=F32)
        p = jnp.exp(lg - jnp.max(lg, axis=-1, keepdims=True))
        pv = jnp.dot(p.astype(BF16), v_ref[:, sl], preferred_element_type=F32)
        ox.append((pv / jnp.sum(p, axis=-1, keepdims=True)).astype(BF16))
    ox = jnp.concatenate(ox, axis=1)

    merged = None
    for br, (o, wb) in enumerate(((odsa_ref[...], wb1_ref), (os5_ref[...], wb2_ref), (ox, wb3_ref))):
        gate = jax.nn.sigmoid(lax.dot_general(hb, wgT_ref[br * D_MODEL:(br + 1) * D_MODEL, :],
                                              NT_DIMS, preferred_element_type=F32))
        term = gate * jnp.dot(o, wb[...], preferred_element_type=F32)
        merged = term if merged is None else merged + term
    y_ref[...] = x + jnp.dot(merged.astype(BF16), wout_ref[...], preferred_element_type=F32)


def _merge(x, odsa, os5, mem, gmix, wgT, wqxT, gqx, wb1, wb2, wb3, wout, gmem, wkv, gk):
    B, S, D = x.shape
    ts = TS_MERGE
    tok = lambda w: pl.BlockSpec((None, ts, w), lambda b, s: (b, s, 0))
    consts = (gmix, wgT, wqxT, gqx, wb1, wb2, wb3, wout, gmem, wkv, gk)
    in_specs = [tok(D), tok(DSA_WIDTH), tok(S5_WIDTH),
                pl.BlockSpec((None, N_MEM, D), lambda b, s: (b, 0, 0))
                ] + [_const_spec(a.shape) for a in consts]
    scratch = [pltpu.VMEM((N_MEM, X_WIDTH), BF16), pltpu.VMEM((N_MEM, X_WIDTH), BF16)]
    return pl.pallas_call(
        _merge_kernel, grid=(B, S // ts), in_specs=in_specs, out_specs=tok(D),
        out_shape=jax.ShapeDtypeStruct((B, S, D), F32), scratch_shapes=scratch,
        compiler_params=_params(("parallel", "arbitrary")), name="merge",
    )(x, odsa, os5, mem, *consts)


def _ffn_kernel(x_ref, g_ref, wg_ref, wu_ref, wd_ref, y_ref):
    x = x_ref[...]
    hb = _rms(x, g_ref[...]).astype(BF16)
    a = jnp.dot(hb, wg_ref[...], preferred_element_type=F32)
    b = jnp.dot(hb, wu_ref[...], preferred_element_type=F32)
    act = (jax.nn.silu(a) * b).astype(BF16)
    y_ref[...] = x + jnp.dot(act, wd_ref[...], preferred_element_type=F32)


def _ffn(x2, g, wg, wu, wd):
    n, D = x2.shape
    tok = pl.BlockSpec((TS_FFN, D), lambda t: (t, 0))
    return pl.pallas_call(
        _ffn_kernel, grid=(n // TS_FFN,),
        in_specs=[tok] + [_const_spec(a.shape) for a in (g, wg, wu, wd)],
        out_specs=tok, out_shape=jax.ShapeDtypeStruct((n, D), F32),
        compiler_params=_params(("parallel",)), name="ffn",
    )(x2, g, wg, wu, wd)


def _t5_bucket(n):
    max_exact = REL_BUCKETS // 2
    nf = jnp.maximum(n, 1).astype(F32)
    large = max_exact + (jnp.log(nf / max_exact) / math.log(REL_MAX_DIST / max_exact)
                         * (REL_BUCKETS - max_exact)).astype(jnp.int32)
    large = jnp.minimum(large, REL_BUCKETS - 1)
    return jnp.where(n < max_exact, n, large)


def _toeplitz(w, rows, cols):
    H, L = w.shape
    flat = jnp.tile(w, (1, rows))[:, :rows * (L - 1)]
    return flat.reshape(H, rows, L - 1)[:, :, :cols]


def _near_bias(rel_bias):
    n = jnp.arange(2 * NB, dtype=jnp.int32)
    f = (rel_bias[_t5_bucket(n)] - rel_bias[REL_BUCKETS - 1][None, :]).T * LOG2E
    w0 = jnp.concatenate([f[:, :NB], jnp.broadcast_to(f[:, :1], (f.shape[0], NB - 1))], axis=1)
    w1 = jnp.concatenate([f[:, NB:2 * NB], f[:, 1:NB]], axis=1)
    return jnp.stack([_toeplitz(w0, NB, NB), _toeplitz(w1, NB, NB)], axis=0).astype(F32)


def _s5_mats(a_re, a_im, log_dt, b_re, b_im, c_re, c_im):
    lam = lax.complex(a_re.astype(F32), a_im.astype(F32))
    dt = jnp.exp(log_dt.astype(F32))[:, None]
    lam_bar = jnp.exp(lam * dt)
    b_bar = ((lam_bar - 1.0) / lam)[..., None] * lax.complex(b_re.astype(F32), b_im.astype(F32))
    nstate = S5_GROUPS * S5_STATE
    gin, gout = S5_GROUPS // S5_IN_BLOCKS, S5_GROUPS // S5_OUT_BLOCKS

    def blockdiag_in(w):
        t = jnp.transpose(w, (0, 2, 1)).reshape(S5_WIDTH, S5_STATE)
        mask = ((jnp.arange(S5_WIDTH)[:, None] // S5_GROUP) % gin
                == jnp.arange(gin * S5_STATE)[None, :] // S5_STATE)
        return jnp.where(mask, jnp.tile(t, (1, gin)), 0.0)

    def blockdiag_out(w):
        t = jnp.transpose(w, (0, 2, 1)).reshape(nstate, S5_GROUP)
        mask = ((jnp.arange(nstate)[:, None] // S5_STATE) % gout
                == jnp.arange(gout * S5_GROUP)[None, :] // S5_GROUP)
        return jnp.where(mask, jnp.tile(t, (1, gout)), 0.0)

    bm = jnp.concatenate([blockdiag_in(jnp.real(b_bar)), blockdiag_in(jnp.imag(b_bar))], axis=1)
    cm = jnp.concatenate([blockdiag_out(c_re.astype(F32)), blockdiag_out(-c_im.astype(F32))], axis=0)
    return bm.astype(BF16), cm.astype(BF16), jnp.real(lam_bar).reshape(1, -1), jnp.imag(lam_bar).reshape(1, -1)


def kernel(x, mem, rel_bias, w_in, g_mix_norm, g_q_dsa, g_kv_dsa, w_uv_dsa, a_re, a_im, log_dt, b_re, b_im, c_re, c_im, d_skip, w_glu, g_mem_norm, w_mem_kv, g_q_cross, g_k_cross, w_br_dsa, w_br_s5, w_br_cross, w_out, g_ffn_norm, w_ffn_gate, w_ffn_up, w_ffn_down):
    B, S, D = x.shape
    depth = w_in.shape[0]
    assert D == D_MODEL and w_in.shape[2] == sum(IN_SPLITS) and mem.shape[1] == N_MEM
    assert TQ == TK == 2 * NB and min(TOPK_MAX, S // 4) == TOPK_MAX and TQ <= TOPK_MAX
    assert S % TS_IN == 0 and S % T_S5 == 0 and S % TS_MERGE == 0 and (B * S) % TS_FFN == 0
    assert T_S5 % S5_SUB == 0 and S5_PITCH % (2 * SUBLANES) == SUBLANES
    offs = [0] + [int(o) for o in np.cumsum(IN_SPLITS)]
    nb = _near_bias(rel_bias)
    row = lambda v: v.reshape(1, -1).astype(F32)
    col = lambda v: v.reshape(-1, 1).astype(F32)
    bf = lambda a: a.astype(BF16)
    for l in range(depth):
        w_inT = bf(jnp.transpose(w_in[l]))
        wuT, wqxT, wgT = [w_inT[offs[k]:offs[k + 1]] for k in (5, 6, 7)]
        gqc = col(g_q_dsa[l]) * (DSA_LATENT ** -0.5 * LOG2E)

        qT, cT, c, qiT, wT, kidx, u = _inproj(
            x, row(g_mix_norm[l]), w_inT, wuT, gqc, row(g_kv_dsa[l]))

        wuvT = bf(jnp.transpose(w_uv_dsa[l], (0, 2, 1)))
        o_dsa = _dsa(qT, qiT, wT, kidx, c, cT, nb, wuvT)

        bm, cm, lre, lim = _s5_mats(a_re[l], a_im[l], log_dt[l], b_re[l], b_im[l], c_re[l], c_im[l])
        lre = jnp.broadcast_to(lre, (B, lre.shape[1]))
        lim = jnp.broadcast_to(lim, (B, lim.shape[1]))
        o_s5 = _s5(u, bm, cm, lre, lim, row(d_skip[l]), bf(w_glu[l]))

        x1 = _merge(x, o_dsa, o_s5, mem, row(g_mix_norm[l]), wgT, wqxT, row(g_q_cross[l]),
                    bf(w_br_dsa[l]), bf(w_br_s5[l]), bf(w_br_cross[l]), bf(w_out[l]),
                    row(g_mem_norm[l]), bf(w_mem_kv[l]), row(g_k_cross[l]))

        x = _ffn(x1.reshape(B * S, D), row(g_ffn_norm[l]), bf(w_ffn_gate[l]), bf(w_ffn_up[l]),
                 bf(w_ffn_down[l])).reshape(B, S, D)
    return x
```
